```python
import math
import jax, jax.numpy as jnp
from jax import lax
import numpy as np

D_MODEL = 2048
BATCH = 16
SEQ = 2048
DEPTH = 2

N_EVEN = (DEPTH + 1) // 2
N_ODD = DEPTH // 2
D_FF = 4 * D_MODEL
MIX_WIDTH = D_MODEL
EPS = 1e-6
NEG = -1e30

GLA_HEADS = 4
GLA_WIDTH = MIX_WIDTH // 2
GLA_DV = GLA_WIDTH // GLA_HEADS
GLA_DK = GLA_DV // 2
GLA_RANK = 16
GLA_GATE_TAU = 16.0
GLA_CHUNK = 64

NSA_HEADS = 16
NSA_GROUPS = 4
NSA_HPG = NSA_HEADS // NSA_GROUPS
NSA_DH = (MIX_WIDTH - GLA_WIDTH) // NSA_HEADS
CMP_LEN = 32
CMP_STRIDE = 16
SEL_BLOCK = 64
N_SELECT = 16
WINDOW = 512
Q_BLOCK = 128
SEL_Q_CHUNK = 16

ML_HEADS = 4
ML_DV = MIX_WIDTH // ML_HEADS
ML_DQK = ML_DV // 2
ML_CONV = 4
ML_CHUNK = 64

GLA_COLS = 2 * GLA_HEADS * GLA_DK + 2 * GLA_WIDTH + GLA_RANK
NSA_COLS = NSA_HEADS * NSA_DH + 6 * NSA_GROUPS * NSA_DH + 3 * NSA_HEADS
AB_COLS = GLA_COLS + NSA_COLS
ML_COLS = 2 * ML_HEADS * ML_DQK + 2 * MIX_WIDTH + 2 * ML_HEADS

kernel_name = "hybrid_gla_nsa_mlstm_adaln"


def split_last(t, sizes):
    cuts = [int(s) for s in np.cumsum(sizes)[:-1]]
    return jnp.split(t, cuts, axis=-1)


def rmsnorm(x, g):
    xf = x.astype(jnp.float32)
    y = xf * lax.rsqrt(jnp.mean(xf * xf, axis=-1, keepdims=True) + EPS)
    return (y * g.astype(jnp.float32)).astype(x.dtype)


def ada_mod(c, w, b):
    m = jnp.einsum('bd,de->be', jax.nn.silu(c), w) + b
    shift, scale, gate = jnp.split(m[:, None, :], 3, axis=-1)
    return shift, scale, gate


def gla_chunked(q, k, v, log_a):
    B, S, H, DK = q.shape
    DV = v.shape[-1]
    L = GLA_CHUNK
    N = S // L

    def chunks(t):
        return t.reshape(B, N, L, H, t.shape[-1]).transpose(1, 0, 3, 2, 4)

    qc, kc, vc, ac = chunks(q * DK ** -0.5), chunks(k), chunks(v), chunks(log_a)
    causal = jnp.tril(jnp.ones((L, L), bool))

    def step(state, inp):
        qq, kk, vv, aa = inp
        cum = jnp.cumsum(aa.astype(jnp.float32), axis=2)
        q_dec = qq * jnp.exp(cum)
        k_dec = kk * jnp.exp(-cum)
        att = jnp.where(causal, jnp.einsum('bhtd,bhsd->bhts', q_dec, k_dec), 0.0)
        o = jnp.einsum('bhts,bhsv->bhtv', att, vv) + jnp.einsum('bhtd,bhdv->bhtv', q_dec, state)
        total = cum[:, :, -1:, :]
        k_end = kk * jnp.exp(total - cum)
        state = state * jnp.exp(total[:, :, 0, :, None]) + jnp.einsum('bhsd,bhsv->bhdv', k_end, vv)
        return state, o

    state0 = jnp.zeros((B, H, DK, DV), jnp.float32)
    _, o = lax.scan(step, state0, (qc, kc, vc, ac))
    return o.transpose(1, 0, 3, 2, 4).reshape(B, S, H, DV)


def gla_branch(p, w_gate, b_gate, norm_g):
    B, S, _ = p.shape
    q, k, v, r, z = split_last(p, [GLA_HEADS * GLA_DK, GLA_HEADS * GLA_DK, GLA_WIDTH, GLA_WIDTH, GLA_RANK])
    log_a = jax.nn.log_sigmoid((z @ w_gate + b_gate).astype(jnp.float32)) / GLA_GATE_TAU
    heads = lambda t, d: t.reshape(B, S, GLA_HEADS, d)
    o = gla_chunked(heads(q, GLA_DK), heads(k, GLA_DK), heads(v, GLA_DV), heads(log_a, GLA_DK))
    o = rmsnorm(o, norm_g) * jax.nn.silu(heads(r, GLA_DV))
    return o.reshape(B, S, GLA_WIDTH).astype(p.dtype)


def compress_tokens(t, pos, w1, w2):
    B, S, G, Dh = t.shape
    sub = t.reshape(B, S // CMP_STRIDE, CMP_STRIDE, G, Dh)
    blk = jnp.concatenate([sub[:, :-1], sub[:, 1:]], axis=2)
    blk = blk + pos[None, None, :, None, :]
    nc = blk.shape[1]
    flat = blk.transpose(0, 1, 3, 2, 4).reshape(B, nc, G, CMP_LEN * Dh)
    hid = jax.nn.silu(jnp.einsum('bngf,fe->bnge', flat, w1))
    return jnp.einsum('bnge,ed->bngd', hid, w2)


def compressed_attention(q, k_cmp, v_cmp):
    S = q.shape[1]
    nc = k_cmp.shape[1]
    logits = jnp.einsum('btghd,bngd->bghtn', q, k_cmp).astype(jnp.float32) * NSA_DH ** -0.5
    t = jnp.arange(S)
    blk_end = jnp.arange(nc) * CMP_STRIDE + CMP_LEN - 1
    valid = blk_end[None, :] <= t[:, None]
    p = jax.nn.softmax(jnp.where(valid, logits, NEG), axis=-1) * valid
    o = jnp.einsum('bghtn,bngd->btghd', p.astype(v_cmp.dtype), v_cmp)
    return o, p


def select_blocks(p_cmp, S):
    ns = S // SEL_BLOCK
    nc = p_cmp.shape[-1]
    c_start = jnp.arange(nc) * CMP_STRIDE
    s_start = jnp.arange(ns) * SEL_BLOCK
    overlap = ((c_start[:, None] <= s_start[None, :] + SEL_BLOCK - 1)
               & (c_start[:, None] + CMP_LEN - 1 >= s_start[None, :])).astype(jnp.float32)
    imp = jnp.einsum('bghtn,nj->bgtj', p_cmp, overlap)
    cur = jnp.arange(S) // SEL_BLOCK
    j = jnp.arange(ns)
    causal = j[None, :] <= cur[:, None]
    forced = (j[None, :] == 0) | (j[None, :] == cur[:, None]) | (j[None, :] == cur[:, None] - 1)
    imp = jnp.where(forced, jnp.inf, jnp.where(causal, imp, -jnp.inf))
    _, idx = lax.top_k(imp, min(N_SELECT, ns))
    return idx


def selected_attention(q, k, v, idx):
    B, S, G, HPG, Dh = q.shape
    ns = S // SEL_BLOCK
    n_sel = idx.shape[-1]
    kb = k.reshape(B, ns, SEL_BLOCK, G, Dh).transpose(0, 3, 1, 2, 4)
    vb = v.reshape(B, ns, SEL_BLOCK, G, Dh).transpose(0, 3, 1, 2, 4)
    nq = S // SEL_Q_CHUNK
    q_ch = q.reshape(B, nq, SEL_Q_CHUNK, G, HPG, Dh).transpose(1, 0, 2, 3, 4, 5)
    idx_ch = idx.reshape(B, G, nq, SEL_Q_CHUNK, n_sel).transpose(2, 0, 1, 3, 4)
    pos_ch = jnp.arange(S).reshape(nq, SEL_Q_CHUNK)
    bi = jnp.arange(B)[:, None, None, None]
    gi = jnp.arange(G)[None, :, None, None]
    offs = jnp.arange(SEL_BLOCK)

    def one_chunk(args):
        qq, ii, tt = args
        kk = kb[bi, gi, ii]
        vv = vb[bi, gi, ii]
        logits = jnp.einsum('bqghd,bgqnkd->bghqnk', qq, kk).astype(jnp.float32) * Dh ** -0.5
        key_pos = ii[..., None] * SEL_BLOCK + offs
        ok = key_pos <= tt[None, None, :, None, None]
        logits = jnp.where(ok[:, :, None], logits, NEG)
        shp = logits.shape
        p = jax.nn.softmax(logits.reshape(shp[:-2] + (-1,)), axis=-1).reshape(shp)
        return jnp.einsum('bghqnk,bgqnkd->bqghd', p.astype(vv.dtype), vv)

    o = lax.map(one_chunk, (q_ch, idx_ch, pos_ch))
    return o.transpose(1, 0, 2, 3, 4, 5).reshape(B, S, G, HPG, Dh)


def window_attention(q, k, v):
    B, S, G, HPG, Dh = q.shape
    kp = jnp.pad(k, ((0, 0), (WINDOW, 0), (0, 0), (0, 0)))
    vp = jnp.pad(v, ((0, 0), (WINDOW, 0), (0, 0), (0, 0)))
    nqb = S // Q_BLOCK
    q_bl = q.reshape(B, nqb, Q_BLOCK, G, HPG, Dh).transpose(1, 0, 2, 3, 4, 5)
    span = WINDOW + Q_BLOCK

    def one_block(args):
        qq, i = args
        start = i * Q_BLOCK
        kk = lax.dynamic_slice_in_dim(kp, start, span, axis=1)
        vv = lax.dynamic_slice_in_dim(vp, start, span, axis=1)
        tq = start + jnp.arange(Q_BLOCK)
        sk = start - WINDOW + jnp.arange(span)
        diff = tq[:, None] - sk[None, :]
        ok = (diff >= 0) & (diff < WINDOW) & (sk[None, :] >= 0)
        logits = jnp.einsum('bqghd,bkgd->bghqk', qq, kk).astype(jnp.float32) * Dh ** -0.5
        p = jax.nn.softmax(jnp.where(ok, logits, NEG), axis=-1)
        return jnp.einsum('bghqk,bkgd->bqghd', p.astype(vv.dtype), vv)

    o = lax.map(one_block, (q_bl, jnp.arange(nqb)))
    return o.transpose(1, 0, 2, 3, 4, 5).reshape(B, S, G, HPG, Dh)


def nsa_branch(p, q_g, k_g, pos_k, pos_v, ck_w1, ck_w2, cv_w1, cv_w2):
    B, S, _ = p.shape
    G, HPG, Dh = NSA_GROUPS, NSA_HPG, NSA_DH
    kv = G * Dh
    q, kc, vc, ks, vs, kw, vw, gt = split_last(p, [NSA_HEADS * Dh, kv, kv, kv, kv, kv, kv, 3 * NSA_HEADS])
    grp = lambda t: t.reshape(B, S, G, Dh)
    q = rmsnorm(q.reshape(B, S, G, HPG, Dh), q_g)
    k_cmp = rmsnorm(compress_tokens(grp(kc), pos_k, ck_w1, ck_w2), k_g)
    v_cmp = compress_tokens(grp(vc), pos_v, cv_w1, cv_w2)
    k_slc, v_slc = rmsnorm(grp(ks), k_g), grp(vs)
    k_win, v_win = rmsnorm(grp(kw), k_g), grp(vw)
    o_cmp, p_cmp = compressed_attention(q, k_cmp, v_cmp)
    idx = select_blocks(p_cmp, S)
    o_slc = selected_attention(q, k_slc, v_slc, idx)
    o_win = window_attention(q, k_win, v_win)
    g = jax.nn.sigmoid(gt.astype(jnp.float32)).reshape(B, S, G, HPG, 3)
    o = g[..., 0:1] * o_cmp + g[..., 1:2] * o_slc + g[..., 2:3] * o_win
    return o.reshape(B, S, NSA_HEADS * Dh).astype(p.dtype)


def gla_nsa_mixer(h, w_in, w_out, gla_w_gate, gla_b_gate, gla_norm_g, nsa_q_g, nsa_k_g,
                  pos_k, pos_v, ck_w1, ck_w2, cv_w1, cv_w2):
    proj = h @ w_in
    p_gla, p_nsa = split_last(proj, [GLA_COLS, NSA_COLS])
    o_gla = gla_branch(p_gla, gla_w_gate, gla_b_gate, gla_norm_g)
    o_nsa = nsa_branch(p_nsa, nsa_q_g, nsa_k_g, pos_k, pos_v, ck_w1, ck_w2, cv_w1, cv_w2)
    return jnp.concatenate([o_gla, o_nsa], axis=-1) @ w_out


def causal_conv(x, w):
    K = w.shape[0]
    S = x.shape[1]
    xp = jnp.pad(x, ((0, 0), (K - 1, 0), (0, 0)))
    y = xp[:, 0:S] * w[0]
    for j in range(1, K):
        y = y + xp[:, j:j + S] * w[j]
    return y


def mlstm_chunked(q, k, v, log_i, log_f):
    B, S, H, DK = q.shape
    DV = v.shape[-1]
    L = ML_CHUNK
    N = S // L

    def chunks(t):
        return t.reshape(B, N, L, H, t.shape[-1]).transpose(1, 0, 3, 2, 4)

    qc, kc, vc = chunks(q * DK ** -0.5), chunks(k), chunks(v)
    ic = chunks(log_i[..., None])[..., 0]
    fc = chunks(log_f[..., None])[..., 0]
    causal = jnp.tril(jnp.ones((L, L), bool))

    def step(carry, inp):
        C, n, m = carry
        qq, kk, vv, li, lf = inp
        li = li.astype(jnp.float32)
        b = jnp.cumsum(lf.astype(jnp.float32), axis=-1)
        log_d = jnp.where(causal, b[..., :, None] - b[..., None, :] + li[..., None, :], -jnp.inf)
        log_inter = b + m[..., None]
        m_comb = jnp.maximum(log_inter, jnp.max(log_d, axis=-1))
        d = jnp.exp(log_d - m_comb[..., None])
        w_inter = jnp.exp(log_inter - m_comb)
        s = jnp.einsum('bhtd,bhsd->bhts', qq, kk) * d
        num = jnp.einsum('bhts,bhsv->bhtv', s, vv) + w_inter[..., None] * jnp.einsum('bhtd,bhdv->bhtv', qq, C)
        den = jnp.sum(s, axis=-1) + w_inter * jnp.einsum('bhtd,bhd->bht', qq, n)
        h = num / jnp.maximum(jnp.abs(den), jnp.exp(-m_comb))[..., None]
        total = b[..., -1]
        log_w = total[..., None] - b + li
        m_new = jnp.maximum(total + m, jnp.max(log_w, axis=-1))
        w = jnp.exp(log_w - m_new[..., None])
        decay = jnp.exp(total + m - m_new)
        C = decay[..., None, None] * C + jnp.einsum('bhs,bhsd,bhsv->bhdv', w, kk, vv)
        n = decay[..., None] * n + jnp.einsum('bhs,bhsd->bhd', w, kk)
        return (C, n, m_new), h

    carry0 = (jnp.zeros((B, H, DK, DV), jnp.float32), jnp.zeros((B, H, DK), jnp.float32),
              jnp.zeros((B, H), jnp.float32))
    _, h = lax.scan(step, carry0, (qc, kc, vc, ic, fc))
    return h.transpose(1, 0, 3, 2, 4).reshape(B, S, H, DV)


def mlstm_mixer(h, w_in, w_out, conv_w, b_i, b_f, norm_g):
    B, S, _ = h.shape
    proj = h @ w_in
    qk, v, og, gi, gf = split_last(proj, [2 * ML_HEADS * ML_DQK, MIX_WIDTH, MIX_WIDTH, ML_HEADS, ML_HEADS])
    qk = jax.nn.silu(causal_conv(qk, conv_w))
    q, k = jnp.split(qk, 2, axis=-1)
    log_i = (gi + b_i).astype(jnp.float32)
    log_f = jax.nn.log_sigmoid((gf + b_f).astype(jnp.float32))
    hh = mlstm_chunked(q.reshape(B, S, ML_HEADS, ML_DQK), k.reshape(B, S, ML_HEADS, ML_DQK),
                       v.reshape(B, S, ML_HEADS, ML_DV), log_i, log_f)
    hh = rmsnorm(hh, norm_g) * jax.nn.sigmoid(og.reshape(B, S, ML_HEADS, ML_DV))
    return hh.reshape(B, S, MIX_WIDTH).astype(h.dtype) @ w_out


def sqrelu_mlp(h, w1, w2):
    return jnp.square(jax.nn.relu(h @ w1)) @ w2


def setup_inputs(seed: int = 0) -> dict:
    key = jax.random.key(seed)
    ks = iter(jax.random.split(key, 40))

    def nrm(shape, scale):
        return jax.random.normal(next(ks), shape, jnp.float32) * scale

    def gain(shape):
        return 1.0 + nrm(shape, 0.1)

    D = D_MODEL
    return {
        "x": nrm((BATCH, SEQ, D), 1.0),
        "c": nrm((BATCH, D), 1.0),
        "norm_mix_g": gain((DEPTH, D)),
        "mod_mix_w": nrm((DEPTH, D, 3 * D), 0.5 * D ** -0.5),
        "mod_mix_b": nrm((DEPTH, 3 * D), 0.02),
        "norm_mlp_g": gain((DEPTH, D)),
        "mod_mlp_w": nrm((DEPTH, D, 3 * D), 0.5 * D ** -0.5),
        "mod_mlp_b": nrm((DEPTH, 3 * D), 0.02),
        "mlp_w1": nrm((DEPTH, D, D_FF), D ** -0.5),
        "mlp_w2": nrm((DEPTH, D_FF, D), D_FF ** -0.5),
        "ab_w_in": nrm((N_EVEN, D, AB_COLS), D ** -0.5),
        "ab_w_out": nrm((N_EVEN, MIX_WIDTH, D), MIX_WIDTH ** -0.5),
        "gla_w_gate": nrm((N_EVEN, GLA_RANK, GLA_HEADS * GLA_DK), GLA_RANK ** -0.5),
        "gla_b_gate": nrm((N_EVEN, GLA_HEADS * GLA_DK), 0.02),
        "gla_norm_g": gain((N_EVEN, GLA_DV)),
        "nsa_q_norm_g": gain((N_EVEN, NSA_DH)),
        "nsa_k_norm_g": gain((N_EVEN, NSA_DH)),
        "nsa_cmp_pos_k": nrm((N_EVEN, CMP_LEN, NSA_DH), 0.1),
        "nsa_cmp_pos_v": nrm((N_EVEN, CMP_LEN, NSA_DH), 0.1),
        "nsa_cmp_k_w1": nrm((N_EVEN, CMP_LEN * NSA_DH, NSA_DH), (CMP_LEN * NSA_DH) ** -0.5),
        "nsa_cmp_k_w2": nrm((N_EVEN, NSA_DH, NSA_DH), NSA_DH ** -0.5),
        "nsa_cmp_v_w1": nrm((N_EVEN, CMP_LEN * NSA_DH, NSA_DH), (CMP_LEN * NSA_DH) ** -0.5),
        "nsa_cmp_v_w2": nrm((N_EVEN, NSA_DH, NSA_DH), NSA_DH ** -0.5),
        "ml_w_in": nrm((N_ODD, D, ML_COLS), D ** -0.5),
        "ml_w_out": nrm((N_ODD, MIX_WIDTH, D), MIX_WIDTH ** -0.5),
        "ml_conv_w": nrm((N_ODD, ML_CONV, 2 * ML_HEADS * ML_DQK), ML_CONV ** -0.5),
        "ml_b_i": nrm((N_ODD, ML_HEADS), 0.1),
        "ml_b_f": 3.0 + 3.0 * jax.random.uniform(next(ks), (N_ODD, ML_HEADS), jnp.float32),
        "ml_norm_g": gain((N_ODD, ML_DV)),
    }


def reference(x, c, norm_mix_g, mod_mix_w, mod_mix_b, norm_mlp_g, mod_mlp_w, mod_mlp_b,
              mlp_w1, mlp_w2, ab_w_in, ab_w_out, gla_w_gate, gla_b_gate, gla_norm_g,
              nsa_q_norm_g, nsa_k_norm_g, nsa_cmp_pos_k, nsa_cmp_pos_v, nsa_cmp_k_w1, nsa_cmp_k_w2,
              nsa_cmp_v_w1, nsa_cmp_v_w2, ml_w_in, ml_w_out, ml_conv_w, ml_b_i, ml_b_f, ml_norm_g):
    for layer in range(DEPTH):
        shift, scale, gate = ada_mod(c, mod_mix_w[layer], mod_mix_b[layer])
        h = rmsnorm(x, norm_mix_g[layer]) * (1.0 + scale) + shift
        if layer % 2 == 0:
            e = layer // 2
            y = gla_nsa_mixer(h, ab_w_in[e], ab_w_out[e], gla_w_gate[e], gla_b_gate[e], gla_norm_g[e],
                              nsa_q_norm_g[e], nsa_k_norm_g[e], nsa_cmp_pos_k[e], nsa_cmp_pos_v[e],
                              nsa_cmp_k_w1[e], nsa_cmp_k_w2[e], nsa_cmp_v_w1[e], nsa_cmp_v_w2[e])
        else:
            o = layer // 2
            y = mlstm_mixer(h, ml_w_in[o], ml_w_out[o], ml_conv_w[o], ml_b_i[o], ml_b_f[o], ml_norm_g[o])
        x = x + gate * y
        shift, scale, gate = ada_mod(c, mod_mlp_w[layer], mod_mlp_b[layer])
        h = rmsnorm(x, norm_mlp_g[layer]) * (1.0 + scale) + shift
        x = x + gate * sqrelu_mlp(h, mlp_w1[layer], mlp_w2[layer])
    return x
```

```python
import functools

import numpy as np
import jax
import jax.numpy as jnp
from jax import lax
from jax.experimental import pallas as pl
from jax.experimental.pallas import tpu as pltpu

F32 = jnp.float32
BF16 = jnp.bfloat16
HI = lax.Precision.HIGHEST

EPS = 1e-6
NEG = -1e30

VMEM_LIMIT_BYTES = 56 * 1024 * 1024
LANES = 128

GLA_HEADS = 4
GLA_DK = 128
GLA_DV = 256
GLA_RANK = 16
GLA_GATE_TAU = 16.0
CHUNK = 64
REC_BLOCK = 256

NSA_HEADS = 16
NSA_GROUPS = 4
NSA_HPG = 4
NSA_DH = 64
CMP_LEN = 32
CMP_STRIDE = 16
SEL_BLOCK = 64
N_SELECT = 16
WINDOW = 512
NSA_TQ = 128
NSA_TK = 256

ML_HEADS = 4
ML_DQK = 256
ML_DV = 512
ML_CONV = 4


def _cparams(sem):
    return pltpu.CompilerParams(dimension_semantics=sem, vmem_limit_bytes=VMEM_LIMIT_BYTES)


def _sigmoid(x):
    return 1.0 / (1.0 + jnp.exp(-x))


def _silu(x):
    return x * _sigmoid(x)


def _log_sigmoid(x):
    return jnp.minimum(x, 0.0) - jnp.log(1.0 + jnp.exp(-jnp.abs(x)))


def _dot(a, b, precision=None):
    return jnp.dot(a, b, preferred_element_type=F32, precision=precision)


def _dot_nt(a, b, precision=None):
    return lax.dot_general(a, b, (((1,), (1,)), ((), ())), preferred_element_type=F32, precision=precision)


def _rms(x):
    return x * lax.rsqrt(jnp.mean(x * x, axis=-1, keepdims=True) + EPS)


def _mod_kernel(c_ref, w_ref, b_ref, o_ref):
    a = _silu(c_ref[...]).astype(BF16)
    o_ref[...] = _dot(a, w_ref[...].astype(BF16)) + b_ref[...]


def _ada_mod(c, w, b):
    nl, d, d3 = w.shape
    bsz = c.shape[0]
    tn = 768
    return pl.pallas_call(
        _mod_kernel,
        grid=(nl, d3 // tn),
        in_specs=[pl.BlockSpec((bsz, d), lambda l, j: (0, 0)),
                  pl.BlockSpec((None, d, tn), lambda l, j: (l, 0, j)),
                  pl.BlockSpec((None, 1, tn), lambda l, j: (l, 0, j))],
        out_specs=pl.BlockSpec((None, bsz, tn), lambda l, j: (l, 0, j)),
        out_shape=jax.ShapeDtypeStruct((nl, bsz, d3), F32),
        compiler_params=_cparams(("parallel", "parallel")),
        name="ada_mod",
    )(c, w, b.reshape(nl, 1, d3))


def _modulated_norm(x_ref, g_ref, sc_ref, sh_ref):
    return (_rms(x_ref[...]) * g_ref[...]) * (1.0 + sc_ref[...]) + sh_ref[...]


def _proj_kernel(x_ref, g_ref, sc_ref, sh_ref, w_ref, o_ref, h_ref):
    @pl.when(pl.program_id(2) == 0)
    def _():
        h_ref[...] = _modulated_norm(x_ref, g_ref, sc_ref, sh_ref).astype(BF16)

    o_ref[...] = _dot(h_ref[...], w_ref[...])


def _norm_proj(x, g, scale, shift, w, tm, tn):
    bsz, s, d = x.shape
    n = w.shape[1]
    return pl.pallas_call(
        _proj_kernel,
        grid=(bsz, s // tm, n // tn),
        in_specs=[pl.BlockSpec((None, tm, d), lambda b, i, j: (b, i, 0)),
                  pl.BlockSpec((1, d), lambda b, i, j: (0, 0)),
                  pl.BlockSpec((None, 1, d), lambda b, i, j: (b, 0, 0)),
                  pl.BlockSpec((None, 1, d), lambda b, i, j: (b, 0, 0)),
                  pl.BlockSpec((d, tn), lambda b, i, j: (0, j))],
        out_specs=pl.BlockSpec((None, tm, tn), lambda b, i, j: (b, i, j)),
        out_shape=jax.ShapeDtypeStruct((bsz, s, n), F32),
        scratch_shapes=[pltpu.VMEM((tm, d), BF16)],
        compiler_params=_cparams(("parallel", "parallel", "arbitrary")),
        name="norm_proj",
    )(x, g.reshape(1, d), scale, shift, w)


def _mlp_kernel(x_ref, g_ref, sc_ref, sh_ref, gate_ref, w1_ref, w2_ref, o_ref, h_ref, acc_ref):
    f = pl.program_id(2)

    @pl.when(f == 0)
    def _():
        h_ref[...] = _modulated_norm(x_ref, g_ref, sc_ref, sh_ref).astype(BF16)
        acc_ref[...] = jnp.zeros_like(acc_ref)

    u = jnp.maximum(_dot(h_ref[...], w1_ref[...]), 0.0)
    acc_ref[...] += _dot((u * u).astype(BF16), w2_ref[...])

    @pl.when(f == pl.num_programs(2) - 1)
    def _():
        o_ref[...] = x_ref[...] + gate_ref[...] * acc_ref[...]


def _mlp_sublayer(x, g, scale, shift, gate, w1, w2, tm=512, tf=512):
    bsz, s, d = x.shape
    ff = w1.shape[1]
    vec = pl.BlockSpec((None, 1, d), lambda b, i, f: (b, 0, 0))
    return pl.pallas_call(
        _mlp_kernel,
        grid=(bsz, s // tm, ff // tf),
        in_specs=[pl.BlockSpec((None, tm, d), lambda b, i, f: (b, i, 0)),
                  pl.BlockSpec((1, d), lambda b, i, f: (0, 0)),
                  vec, vec, vec,
                  pl.BlockSpec((d, tf), lambda b, i, f: (0, f)),
                  pl.BlockSpec((tf, d), lambda b, i, f: (f, 0))],
        out_specs=pl.BlockSpec((None, tm, d), lambda b, i, f: (b, i, 0)),
        out_shape=jax.ShapeDtypeStruct((bsz, s, d), F32),
        scratch_shapes=[pltpu.VMEM((tm, d), BF16), pltpu.VMEM((tm, d), F32)],
        compiler_params=_cparams(("parallel", "parallel", "arbitrary")),
        name="mlp_sublayer",
    )(x, g.reshape(1, d), scale, shift, gate, w1, w2)


def _outproj_kernel(n_in, *refs):
    a_refs, w_refs = refs[:n_in], refs[n_in:2 * n_in]
    x_ref, gate_ref, o_ref = refs[2 * n_in:]
    y = _dot(a_refs[0][...], w_refs[0][...])
    for a_ref, w_ref in zip(a_refs[1:], w_refs[1:]):
        y += _dot(a_ref[...], w_ref[...])
    o_ref[...] = x_ref[...] + gate_ref[...] * y


def _out_proj(acts, ws, x, gate, tm=512, tn=1024):
    bsz, s, d = x.shape
    n_in = len(acts)
    in_specs = [pl.BlockSpec((None, tm, a.shape[-1]), lambda b, i, j: (b, i, 0)) for a in acts]
    in_specs += [pl.BlockSpec((w.shape[0], tn), lambda b, i, j: (0, j)) for w in ws]
    in_specs += [pl.BlockSpec((None, tm, tn), lambda b, i, j: (b, i, j)),
                 pl.BlockSpec((None, 1, tn), lambda b, i, j: (b, 0, j))]
    return pl.pallas_call(
        functools.partial(_outproj_kernel, n_in),
        grid=(bsz, s // tm, d // tn),
        in_specs=in_specs,
        out_specs=pl.BlockSpec((None, tm, tn), lambda b, i, j: (b, i, j)),
        out_shape=jax.ShapeDtypeStruct((bsz, s, d), F32),
        compiler_params=_cparams(("parallel", "parallel", "parallel")),
        name="out_proj",
    )(*acts, *ws, x, gate)


def _tri(n, upper=False):
    r = lax.broadcasted_iota(jnp.int32, (n, n), 0)
    c = lax.broadcasted_iota(jnp.int32, (n, n), 1)
    return (c >= r) if upper else (c <= r)


def _gla_kernel(q_ref, k_ref, v_ref, r_ref, zg_ref, wg_ref, bg_ref, ng_ref, o_ref, st_ref):
    @pl.when(pl.program_id(1) == 0)
    def _():
        st_ref[...] = jnp.zeros_like(st_ref)

    tril = _tri(CHUNK)
    tril_f = tril.astype(F32)
    for c in range(REC_BLOCK // CHUNK):
        rows = slice(c * CHUNK, (c + 1) * CHUNK)
        pre = _dot(zg_ref[rows, :].astype(BF16), wg_ref[...]) + bg_ref[...]
        log_a = _log_sigmoid(pre) / GLA_GATE_TAU
        cum = _dot(tril_f, log_a, precision=HI)
        total = cum[CHUNK - 1:CHUNK, :]
        q_dec = (q_ref[rows, :] * GLA_DK ** -0.5) * jnp.exp(cum)
        kk = k_ref[rows, :]
        k_dec = kk * jnp.exp(-cum)
        k_end = kk * jnp.exp(total - cum)
        decay = jnp.exp(total)
        for h in range(GLA_HEADS):
            ks = slice(h * GLA_DK, (h + 1) * GLA_DK)
            vs = slice(h * GLA_DV, (h + 1) * GLA_DV)
            qh = q_dec[:, ks].astype(BF16)
            vh = v_ref[rows, vs]
            att = jnp.where(tril, _dot_nt(qh, k_dec[:, ks].astype(BF16)), 0.0)
            st = st_ref[h]
            o = _dot(att.astype(BF16), vh.astype(BF16)) + _dot_nt(qh, st.astype(BF16))
            st_ref[h] = st * decay[:, ks] + _dot(vh.T.astype(BF16), k_end[:, ks].astype(BF16))
            rh = r_ref[rows, vs]
            o_ref[rows, vs] = ((_rms(o) * ng_ref[...]) * _silu(rh)).astype(o_ref.dtype)


def _gla(p, w_gate, b_gate, norm_g, col):
    bsz, s, _ = p.shape
    t = REC_BLOCK

    def piece(name):
        width, idx = col[name]
        return pl.BlockSpec((None, t, width), lambda b, i, idx=idx: (b, i, idx))

    const = lambda shape: pl.BlockSpec(shape, lambda b, i: (0,) * len(shape))
    hk = GLA_HEADS * GLA_DK
    return pl.pallas_call(
        _gla_kernel,
        grid=(bsz, s // t),
        in_specs=[piece("gla_q"), piece("gla_k"), piece("gla_v"), piece("gla_r"), piece("gates"),
                  const((LANES, hk)), const((1, hk)), const((1, GLA_DV))],
        out_specs=pl.BlockSpec((None, t, GLA_HEADS * GLA_DV), lambda b, i: (b, i, 0)),
        out_shape=jax.ShapeDtypeStruct((bsz, s, GLA_HEADS * GLA_DV), BF16),
        scratch_shapes=[pltpu.VMEM((GLA_HEADS, GLA_DV, GLA_DK), F32)],
        compiler_params=_cparams(("parallel", "arbitrary")),
        name="gla",
    )(p, p, p, p, p, w_gate, b_gate.reshape(1, hk), norm_g.reshape(1, GLA_DV))


def _mlstm_kernel(q_ref, k_ref, v_ref, og_ref, gc_ref, gr_ref, wq_ref, wk_ref, bc_ref, br_ref, ng_ref,
                  o_ref, c_ref, n_ref, m_ref, qx_ref, kx_ref):
    t = REC_BLOCK

    @pl.when(pl.program_id(1) == 0)
    def _():
        c_ref[...] = jnp.zeros_like(c_ref)
        n_ref[...] = jnp.zeros_like(n_ref)
        m_ref[...] = jnp.zeros_like(m_ref)
        qx_ref[0:8, :] = jnp.zeros((8, qx_ref.shape[1]), F32)
        kx_ref[0:8, :] = jnp.zeros((8, kx_ref.shape[1]), F32)

    def conv_silu(x_ref, xx_ref, w_ref):
        xx_ref[8:8 + t, :] = x_ref[...]
        y = xx_ref[8:8 + t, :] * w_ref[ML_CONV - 1:ML_CONV, :]
        for back in range(1, ML_CONV):
            y += xx_ref[8 - back:8 - back + t, :] * w_ref[ML_CONV - 1 - back:ML_CONV - back, :]
        xx_ref[0:8, :] = xx_ref[t:t + 8, :]
        return _silu(y)

    q_all = conv_silu(q_ref, qx_ref, wq_ref) * ML_DQK ** -0.5
    k_all = conv_silu(k_ref, kx_ref, wk_ref)

    tril = _tri(CHUNK)
    tril_f = tril.astype(F32)
    triu_f = _tri(CHUNK, upper=True).astype(F32)
    lane = lax.broadcasted_iota(jnp.int32, (CHUNK, LANES), 1)
    is_f_col = (lane >= ML_HEADS) & (lane < 2 * ML_HEADS)
    sub = lax.broadcasted_iota(jnp.int32, (8, CHUNK), 0)
    is_f_row = sub >= ML_HEADS

    for c in range(t // CHUNK):
        rows = slice(c * CHUNK, (c + 1) * CHUNK)
        gcol = gc_ref[rows, :] + bc_ref[...]
        gcol = jnp.where(is_f_col, _log_sigmoid(gcol), gcol)
        bcol = _dot(tril_f, gcol, precision=HI)
        grow = gr_ref[:, rows] + br_ref[:, 0:1]
        grow = jnp.where(is_f_row, _log_sigmoid(grow), grow)
        brow = _dot(grow, triu_f, precision=HI)
        for h in range(ML_HEADS):
            qs = slice(h * ML_DQK, (h + 1) * ML_DQK)
            vs = slice(h * ML_DV, (h + 1) * ML_DV)
            b_col = bcol[:, ML_HEADS + h:ML_HEADS + h + 1]
            b_row = brow[ML_HEADS + h:ML_HEADS + h + 1, :]
            li_col = gcol[:, h:h + 1]
            li_row = grow[h:h + 1, :]
            m_prev = m_ref[h:h + 1, 0:1]
            log_d = jnp.where(tril, b_col - b_row + li_row, -jnp.inf)
            log_inter = b_col + m_prev
            m_comb = jnp.maximum(log_inter, jnp.max(log_d, axis=-1, keepdims=True))
            d = jnp.exp(log_d - m_comb)
            w_inter = jnp.exp(log_inter - m_comb)
            qh = q_all[rows, qs]
            kh = k_all[rows, qs]
            qb = qh.astype(BF16)
            vb = v_ref[rows, vs].astype(BF16)
            sc = _dot_nt(qb, kh.astype(BF16)) * d
            cm = c_ref[h]
            nv = n_ref[h]
            num = _dot(sc.astype(BF16), vb) + w_inter * _dot(qb, cm.astype(BF16))
            den = jnp.sum(sc, axis=-1, keepdims=True) + w_inter * jnp.sum(qh * nv, axis=-1, keepdims=True)
            hh = num / jnp.maximum(jnp.abs(den), jnp.exp(-m_comb))
            total = b_col[CHUNK - 1:CHUNK, :]
            log_w = total - b_col + li_col
            m_new = jnp.maximum(total + m_prev, jnp.max(log_w, axis=0, keepdims=True))
            decay = jnp.exp(total + m_prev - m_new)
            kw = kh * jnp.exp(log_w - m_new)
            c_ref[h] = decay * cm + _dot(kw.T.astype(BF16), vb)
            n_ref[h] = decay * nv + jnp.sum(kw, axis=0, keepdims=True)
            m_ref[h:h + 1, :] = jnp.broadcast_to(m_new, (1, LANES))
            og = og_ref[rows, vs]
            o_ref[rows, vs] = ((_rms(hh) * ng_ref[...]) * _sigmoid(og)).astype(o_ref.dtype)


def _mlstm(p, gates_row, conv_w, bias_col, bias_row, norm_g, col):
    bsz, s, _ = p.shape
    t = REC_BLOCK
    hq = ML_HEADS * ML_DQK
    hv = ML_HEADS * ML_DV

    def piece(name):
        width, idx = col[name]
        return pl.BlockSpec((None, t, width), lambda b, i, idx=idx: (b, i, idx))

    const = lambda shape: pl.BlockSpec(shape, lambda b, i: (0,) * len(shape))
    return pl.pallas_call(
        _mlstm_kernel,
        grid=(bsz, s // t),
        in_specs=[piece("ml_q"), piece("ml_k"), piece("ml_v"), piece("ml_og"), piece("gates"),
                  pl.BlockSpec((None, 8, t), lambda b, i: (b, 0, i)),
                  pl.BlockSpec((ML_CONV, hq), lambda b, i: (0, 0)),
                  pl.BlockSpec((ML_CONV, hq), lambda b, i: (0, 1)),
                  const((1, LANES)), const((8, LANES)), const((1, ML_DV))],
        out_specs=pl.BlockSpec((None, t, hv), lambda b, i: (b, i, 0)),
        out_shape=jax.ShapeDtypeStruct((bsz, s, hv), BF16),
        scratch_shapes=[pltpu.VMEM((ML_HEADS, ML_DQK, ML_DV), F32),
                        pltpu.VMEM((ML_HEADS, 1, ML_DQK), F32),
                        pltpu.VMEM((8, LANES), F32),
                        pltpu.VMEM((t + 8, hq), F32),
                        pltpu.VMEM((t + 8, hq), F32)],
        compiler_params=_cparams(("parallel", "arbitrary")),
        name="mlstm",
    )(p, p, p, p, p, gates_row, conv_w, conv_w, bias_col, bias_row, norm_g.reshape(1, ML_DV))


def _seg_norm(x, seg_ref, segt_ref):
    ss = _dot(x * x, seg_ref[...], precision=HI) * (1.0 / NSA_DH)
    return x * _dot(lax.rsqrt(ss + EPS), segt_ref[...], precision=HI)


def _nsa_prep_kernel(q_ref, kc_ref, vc_ref, ks_ref, vs_ref, kw_ref, vw_ref, qg_ref, kg_ref,
                     segq_ref, segqt_ref, segk_ref, segkt_ref,
                     qn_ref, kcc_ref, vcc_ref, kst_ref, vsb_ref, kwt_ref, vwb_ref):
    qn_ref[...] = (_seg_norm(q_ref[...], segq_ref, segqt_ref) * qg_ref[...]).astype(BF16)
    kcc_ref[...] = kc_ref[...]
    vcc_ref[...] = vc_ref[...]
    kst_ref[...] = (_seg_norm(ks_ref[...], segk_ref, segkt_ref) * kg_ref[...]).T.astype(BF16)
    kwt_ref[...] = (_seg_norm(kw_ref[...], segk_ref, segkt_ref) * kg_ref[...]).T.astype(BF16)
    vsb_ref[...] = vs_ref[...].astype(BF16)
    vwb_ref[...] = vw_ref[...].astype(BF16)


def _seg_matrices(width):
    seg = (np.arange(width)[:, None] // NSA_DH == np.arange(LANES)[None, :]).astype(np.float32)
    return jnp.asarray(seg), jnp.asarray(seg.T)


def _nsa_prep(p, q_gain, k_gain, col, t=512):
    bsz, s, _ = p.shape
    hq = NSA_HEADS * NSA_DH
    kv = NSA_GROUPS * NSA_DH

    def piece(name):
        width, idx = col[name]
        return pl.BlockSpec((None, t, width), lambda b, i, idx=idx: (b, i, idx))

    const = lambda shape: pl.BlockSpec(shape, lambda b, i: (0,) * len(shape))
    segq, segqt = _seg_matrices(hq)
    segk, segkt = _seg_matrices(kv)
    row = lambda width: pl.BlockSpec((None, t, width), lambda b, i: (b, i, 0))
    colmajor = pl.BlockSpec((None, kv, t), lambda b, i: (b, 0, i))
    shp = lambda *dims, dt=BF16: jax.ShapeDtypeStruct(dims, dt)
    return pl.pallas_call(
        _nsa_prep_kernel,
        grid=(bsz, s // t),
        in_specs=[piece("nsa_q"), piece("nsa_kc"), piece("nsa_vc"), piece("nsa_ks"), piece("nsa_vs"),
                  piece("nsa_kw"), piece("nsa_vw"), const((1, hq)), const((1, kv)),
                  const((hq, LANES)), const((LANES, hq)), const((kv, LANES)), const((LANES, kv))],
        out_specs=[row(hq), row(kv), row(kv), colmajor, row(kv), colmajor, row(kv)],
        out_shape=[shp(bsz, s, hq), shp(bsz, s, kv, dt=F32), shp(bsz, s, kv, dt=F32),
                   shp(bsz, kv, s), shp(bsz, s, kv), shp(bsz, kv, s), shp(bsz, s, kv)],
        compiler_params=_cparams(("parallel", "parallel")),
        name="nsa_prep",
    )(p, p, p, p, p, p, p, q_gain, k_gain, segq, segqt, segk, segkt)


def _compress_kernel(kc_ref, vc_ref, posk_ref, posv_ref, kw1a_ref, kw1b_ref, kw2_ref, vw1a_ref, vw1b_ref,
                     vw2_ref, kg_ref, segk_ref, segkt_ref, kct_ref, vcm_ref):
    def mlp(x_ref, pos_ref, w1a_ref, w1b_ref, w2_ref):
        x = x_ref[...]
        nsub, width = x.shape
        first = _dot((x + pos_ref[0:1, :]).astype(BF16), w1a_ref[...])
        second = _dot((x + pos_ref[1:2, :]).astype(BF16), w1b_ref[...])
        row = lax.broadcasted_iota(jnp.int32, second.shape, 0)
        nxt = jnp.where(row == nsub - 1, 0.0, pltpu.roll(second, nsub - 1, 0))
        return _dot(_silu(first + nxt).astype(BF16), w2_ref[...])

    kc = mlp(kc_ref, posk_ref, kw1a_ref, kw1b_ref, kw2_ref)
    kct_ref[...] = (_seg_norm(kc, segk_ref, segkt_ref) * kg_ref[...]).T.astype(BF16)
    vcm_ref[...] = mlp(vc_ref, posv_ref, vw1a_ref, vw1b_ref, vw2_ref).astype(BF16)


def _expand_w1(w1):
    w = w1.reshape(2, CMP_STRIDE, NSA_DH, NSA_DH)
    eye = jnp.eye(NSA_GROUPS, dtype=w1.dtype)
    big = jnp.einsum('hlde,gk->hlgdke', w, eye)
    big = big.reshape(2, CMP_STRIDE * NSA_GROUPS * NSA_DH, NSA_GROUPS * NSA_DH).astype(BF16)
    return big[0], big[1]


def _expand_pos(pos):
    pp = pos.reshape(2, CMP_STRIDE, 1, NSA_DH)
    return jnp.broadcast_to(pp, (2, CMP_STRIDE, NSA_GROUPS, NSA_DH)).reshape(2, -1)


def _block_diag(w2):
    return jnp.kron(jnp.eye(NSA_GROUPS, dtype=w2.dtype), w2).astype(BF16)


def _compress(kc, vc, pos_k, pos_v, ck_w1, ck_w2, cv_w1, cv_w2, k_gain):
    bsz, s, kv = kc.shape
    nsub = s // CMP_STRIDE
    flat = CMP_STRIDE * kv
    segk, segkt = _seg_matrices(kv)
    kw1a, kw1b = _expand_w1(ck_w1)
    vw1a, vw1b = _expand_w1(cv_w1)
    const = lambda shape: pl.BlockSpec(shape, lambda b: (0,) * len(shape))
    tok = pl.BlockSpec((None, nsub, flat), lambda b: (b, 0, 0))
    return pl.pallas_call(
        _compress_kernel,
        grid=(bsz,),
        in_specs=[tok, tok, const((2, flat)), const((2, flat)),
                  const((flat, kv)), const((flat, kv)), const((kv, kv)),
                  const((flat, kv)), const((flat, kv)), const((kv, kv)),
                  const((1, kv)), const((kv, LANES)), const((LANES, kv))],
        out_specs=[pl.BlockSpec((None, kv, nsub), lambda b: (b, 0, 0)),
                   pl.BlockSpec((None, nsub, kv), lambda b: (b, 0, 0))],
        out_shape=[jax.ShapeDtypeStruct((bsz, kv, nsub), BF16), jax.ShapeDtypeStruct((bsz, nsub, kv), BF16)],
        compiler_params=_cparams(("parallel",)),
        name="nsa_compress",
    )(kc.reshape(bsz, nsub, flat), vc.reshape(bsz, nsub, flat), _expand_pos(pos_k), _expand_pos(pos_v),
      kw1a, kw1b, _block_diag(ck_w2), vw1a, vw1b, _block_diag(cv_w2), k_gain, segk, segkt)


def _nsa_attn_kernel(q_ref, gt_ref, kct_ref, vcm_ref, kst_ref, vs_ref, kwt_ref, vw_ref, ovt_ref, e_ref,
                     o_ref, acc_ref):
    tq, tk = NSA_TQ, NSA_TK
    q0 = pl.program_id(1) * tq
    ncmp = kct_ref.shape[1]
    nsel = ovt_ref.shape[0]
    rows4 = NSA_HPG * tq

    t_cmp = q0 + lax.broadcasted_iota(jnp.int32, (rows4, ncmp), 0) % tq
    n_idx = lax.broadcasted_iota(jnp.int32, (rows4, ncmp), 1)
    cmp_valid = n_idx * CMP_STRIDE + (CMP_LEN - 1) <= t_cmp

    j_idx = lax.broadcasted_iota(jnp.int32, (nsel, tq), 0)
    cur = (q0 + lax.broadcasted_iota(jnp.int32, (nsel, tq), 1)) // SEL_BLOCK
    forced = (j_idx == 0) | (j_idx == cur) | (j_idx == cur - 1)
    causal_blk = j_idx <= cur

    t_q = q0 + lax.broadcasted_iota(jnp.int32, (tq, tk), 0)
    lane_k = lax.broadcasted_iota(jnp.int32, (tq, tk), 1)

    gates = _sigmoid(gt_ref[...])

    def attend(q4, kt_ref, v_ref, g, first_tile, n_tiles, mask_fn):
        gs = slice(g * NSA_DH, (g + 1) * NSA_DH)

        def body(i, carry):
            m, l, acc = carry
            k0 = pl.multiple_of((first_tile + i) * tk, tk)
            s = _dot(q4, kt_ref[gs, pl.ds(k0, tk)])
            ok = mask_fn(k0)
            s = jnp.where(ok[None], s.reshape(NSA_HPG, tq, tk), NEG).reshape(rows4, tk)
            m_new = jnp.maximum(m, jnp.max(s, axis=-1, keepdims=True))
            alpha = jnp.exp(m - m_new)
            p = jnp.exp(s - m_new)
            l = alpha * l + jnp.sum(p, axis=-1, keepdims=True)
            v = v_ref[pl.ds(k0, tk), :][:, gs]
            acc = alpha * acc + _dot(p.astype(BF16), v)
            return m_new, l, acc

        init = (jnp.full((rows4, 1), NEG, F32), jnp.zeros((rows4, 1), F32), jnp.zeros((rows4, NSA_DH), F32))
        _, l, acc = lax.fori_loop(0, n_tiles, body, init)
        return acc / l

    last_tile = (q0 + tq - 1) // tk
    win_first = jnp.maximum(q0 - (WINDOW - 1), 0) // tk

    def win_mask(k0):
        diff = t_q - (k0 + lane_k)
        return (diff >= 0) & (diff < WINDOW)

    def add_gated(g, branch, o4):
        for i in range(NSA_HPG):
            h = g * NSA_HPG + i
            gc = GLA_RANK + 3 * h + branch
            term = gates[:, gc:gc + 1] * o4[i * tq:(i + 1) * tq]
            acc_ref[h] = term if branch == 0 else acc_ref[h] + term

    for g in range(NSA_GROUPS):
        gs = slice(g * NSA_DH, (g + 1) * NSA_DH)
        q4 = jnp.concatenate([q_ref[:, (g * NSA_HPG + i) * NSA_DH:(g * NSA_HPG + i + 1) * NSA_DH]
                              for i in range(NSA_HPG)], axis=0)

        lg = jnp.where(cmp_valid, _dot(q4, kct_ref[gs, :]), NEG)
        ex = jnp.exp(lg - jnp.max(lg, axis=-1, keepdims=True))
        p_cmp = jnp.where(cmp_valid, ex / jnp.sum(ex, axis=-1, keepdims=True), 0.0)
        add_gated(g, 0, _dot(p_cmp.astype(BF16), vcm_ref[...][:, gs]))

        p_grp = p_cmp[0:tq] + p_cmp[tq:2 * tq] + p_cmp[2 * tq:3 * tq] + p_cmp[3 * tq:4 * tq]
        imp = _dot_nt(ovt_ref[...], p_grp, precision=HI)
        val = jnp.where(forced, jnp.inf, jnp.where(causal_blk, imp, -jnp.inf))
        rank = jnp.zeros((nsel, tq), F32)
        for i in range(nsel):
            vi = val[i:i + 1, :]
            rank += ((vi > val) | ((vi == val) & (j_idx > i))).astype(F32)
        sel_t = (rank < N_SELECT).astype(F32)
        sel = jnp.concatenate([sel_t, jnp.zeros((LANES - nsel, tq), F32)], axis=0).T.astype(BF16)

        def sel_mask(k0, sel=sel):
            picked = _dot(sel, e_ref[:, pl.ds(k0, tk)]) > 0.5
            return picked & (k0 + lane_k <= t_q)

        add_gated(g, 1, attend(q4, kst_ref, vs_ref, g, 0, last_tile + 1, sel_mask))
        add_gated(g, 2, attend(q4, kwt_ref, vw_ref, g, win_first, last_tile - win_first + 1, win_mask))
    o_ref[...] = jnp.concatenate([acc_ref[h] for h in range(NSA_HEADS)], axis=-1).astype(o_ref.dtype)


def _selection_constants(s):
    ncmp = s // CMP_STRIDE
    nsel = s // SEL_BLOCK
    c_start = np.arange(ncmp) * CMP_STRIDE
    s_start = np.arange(nsel) * SEL_BLOCK
    overlap_t = ((c_start[None, :] <= s_start[:, None] + SEL_BLOCK - 1)
                 & (c_start[None, :] + CMP_LEN - 1 >= s_start[:, None])).astype(np.float32)
    overlap_t[:, ncmp - 1] = 0.0
    onehot = (np.arange(LANES)[:, None] == (np.arange(s)[None, :] // SEL_BLOCK)).astype(np.float32)
    return jnp.asarray(overlap_t), jnp.asarray(onehot, dtype=BF16)


def _nsa_attention(p, qn, kct, vcm, kst, vsb, kwt, vwb, col):
    bsz, s, hq = qn.shape
    kv = NSA_GROUPS * NSA_DH
    ncmp = s // CMP_STRIDE
    nsel = s // SEL_BLOCK
    ovt, onehot = _selection_constants(s)
    gw, gidx = col["gates"]
    per_row = lambda shape: pl.BlockSpec((None,) + shape, lambda b, i: (b, 0, 0))
    const = lambda shape: pl.BlockSpec(shape, lambda b, i: (0,) * len(shape))
    return pl.pallas_call(
        _nsa_attn_kernel,
        grid=(bsz, s // NSA_TQ),
        in_specs=[pl.BlockSpec((None, NSA_TQ, hq), lambda b, i: (b, i, 0)),
                  pl.BlockSpec((None, NSA_TQ, gw), lambda b, i: (b, i, gidx)),
                  per_row((kv, ncmp)), per_row((ncmp, kv)),
                  per_row((kv, s)), per_row((s, kv)), per_row((kv, s)), per_row((s, kv)),
                  const((nsel, ncmp)), const((LANES, s))],
        out_specs=pl.BlockSpec((None, NSA_TQ, hq), lambda b, i: (b, i, 0)),
        out_shape=jax.ShapeDtypeStruct((bsz, s, hq), BF16),
        scratch_shapes=[pltpu.VMEM((NSA_HEADS, NSA_TQ, NSA_DH), F32)],
        compiler_params=_cparams(("parallel", "arbitrary")),
        name="nsa_attention",
    )(qn, p, kct, vcm, kst, vsb, kwt, vwb, ovt, onehot)


def _layout(pieces, tile):
    col, off, mats = {}, 0, []
    for name, w in pieces:
        width = w.shape[1]
        assert off % width == 0, (name, off, width)
        col[name] = (width, off // width)
        mats.append(w)
        off += width
    pad = -off % tile
    if pad:
        mats.append(jnp.zeros((mats[0].shape[0], pad), mats[0].dtype))
    return jnp.concatenate(mats, axis=1).astype(BF16), col


def _pad_cols(w, width):
    return jnp.pad(w, ((0, 0), (0, width - w.shape[1])))


AB_TILE = 1920
ML_TILE = 1280


def _ab_layout(w_in):
    hk = GLA_HEADS * GLA_DK
    hv = GLA_HEADS * GLA_DV
    kv = NSA_GROUPS * NSA_DH
    sizes = [hk, hk, hv, hv, GLA_RANK, NSA_HEADS * NSA_DH] + [kv] * 6 + [3 * NSA_HEADS]
    cuts = np.cumsum(sizes)[:-1].tolist()
    gq, gk, gv, gr, gz, nq, kc, vc, ks, vs, kw, vw, gt = jnp.split(w_in, cuts, axis=1)
    gates = _pad_cols(jnp.concatenate([gz, gt], axis=1), LANES)
    return _layout([("gla_q", gq), ("gla_k", gk), ("gla_v", gv), ("gla_r", gr), ("nsa_q", nq),
                    ("nsa_kc", kc), ("nsa_vc", vc), ("nsa_ks", ks), ("nsa_vs", vs), ("nsa_kw", kw),
                    ("nsa_vw", vw), ("gates", gates)], AB_TILE)


def _ml_layout(w_in):
    hq = ML_HEADS * ML_DQK
    hv = ML_HEADS * ML_DV
    cuts = np.cumsum([hq, hq, hv, hv])[:].tolist()
    q, k, v, og, gates = jnp.split(w_in, cuts, axis=1)
    return _layout([("ml_q", q), ("ml_k", k), ("ml_v", v), ("ml_og", og),
                    ("gates", _pad_cols(gates, LANES))], ML_TILE)


def kernel(x, c, norm_mix_g, mod_mix_w, mod_mix_b, norm_mlp_g, mod_mlp_w, mod_mlp_b, mlp_w1, mlp_w2, ab_w_in, ab_w_out, gla_w_gate, gla_b_gate, gla_norm_g, nsa_q_norm_g, nsa_k_norm_g, nsa_cmp_pos_k, nsa_cmp_pos_v, nsa_cmp_k_w1, nsa_cmp_k_w2, nsa_cmp_v_w1, nsa_cmp_v_w2, ml_w_in, ml_w_out, ml_conv_w, ml_b_i, ml_b_f, ml_norm_g):
    bsz, s, d = x.shape
    depth = norm_mix_g.shape[0]
    mod_mix = _ada_mod(c, mod_mix_w, mod_mix_b)
    mod_mlp = _ada_mod(c, mod_mlp_w, mod_mlp_b)

    def split_mod(m):
        return [m[:, None, i * d:(i + 1) * d] for i in range(3)]

    for layer in range(depth):
        shift, scale, gate = split_mod(mod_mix[layer])
        if layer % 2 == 0:
            e = layer // 2
            w_in, col = _ab_layout(ab_w_in[e])
            p = _norm_proj(x, norm_mix_g[layer], scale, shift, w_in, tm=512, tn=AB_TILE)
            w_gate = jnp.pad(gla_w_gate[e], ((0, LANES - GLA_RANK), (0, 0))).astype(BF16)
            o_gla = _gla(p, w_gate, gla_b_gate[e], gla_norm_g[e], col)
            q_gain = (jnp.tile(nsa_q_norm_g[e], NSA_HEADS) * NSA_DH ** -0.5).reshape(1, -1)
            k_gain = jnp.tile(nsa_k_norm_g[e], NSA_GROUPS).reshape(1, -1)
            qn, kcc, vcc, kst, vsb, kwt, vwb = _nsa_prep(p, q_gain, k_gain, col)
            kct, vcm = _compress(kcc, vcc, nsa_cmp_pos_k[e], nsa_cmp_pos_v[e], nsa_cmp_k_w1[e],
                                 nsa_cmp_k_w2[e], nsa_cmp_v_w1[e], nsa_cmp_v_w2[e], k_gain)
            o_nsa = _nsa_attention(p, qn, kct, vcm, kst, vsb, kwt, vwb, col)
            w_out = ab_w_out[e].astype(BF16)
            half = GLA_HEADS * GLA_DV
            x = _out_proj([o_gla, o_nsa], [w_out[:half], w_out[half:]], x, gate)
        else:
            o = layer // 2
            w_in, col = _ml_layout(ml_w_in[o])
            p = _norm_proj(x, norm_mix_g[layer], scale, shift, w_in, tm=512, tn=ML_TILE)
            gw, gidx = col["gates"]
            gates_row = jnp.swapaxes(p[:, :, gidx * gw:gidx * gw + 2 * ML_HEADS], 1, 2)
            bias = jnp.concatenate([ml_b_i[o], ml_b_f[o]])
            bias_col = _pad_cols(bias.reshape(1, -1), LANES)
            bias_row = jnp.broadcast_to(bias.reshape(-1, 1), (2 * ML_HEADS, LANES))
            hh = _mlstm(p, gates_row, ml_conv_w[o], bias_col, bias_row, ml_norm_g[o], col)
            x = _out_proj([hh], [ml_w_out[o].astype(BF16)], x, gate)
        shift, scale, gate = split_mod(mod_mlp[layer])
        x = _mlp_sublayer(x, norm_mlp_g[layer], scale, shift, gate,
                          mlp_w1[layer].astype(BF16), mlp_w2[layer].astype(BF16))
    return x
```

```python
import functools

import numpy as np
import jax
import jax.numpy as jnp
from jax import lax
from jax.experimental import pallas as pl
from jax.experimental.pallas import tpu as pltpu

F32 = jnp.float32
BF16 = jnp.bfloat16
HI = lax.Precision.HIGHEST

EPS = 1e-6
NEG = -1e30

VMEM_LIMIT_BYTES = 56 * 1024 * 1024
LANES = 128

GLA_HEADS = 4
GLA_DK = 128
GLA_DV = 256
GLA_RANK = 16
GLA_GATE_TAU = 16.0
CHUNK = 64
REC_BLOCK = 256

NSA_HEADS = 16
NSA_GROUPS = 4
NSA_HPG = 4
NSA_DH = 64
CMP_LEN = 32
CMP_STRIDE = 16
SEL_BLOCK = 64
N_SELECT = 16
WINDOW = 512
NSA_TQ = 128
NSA_TK = 256
NSA_VPAD = 80
PAST_END = 1 << 24

ML_HEADS = 4
ML_DQK = 256
ML_DV = 512
ML_CONV = 4


def _cparams(sem):
    return pltpu.CompilerParams(dimension_semantics=sem, vmem_limit_bytes=VMEM_LIMIT_BYTES)


def _sigmoid(x):
    return 1.0 / (1.0 + jnp.exp(-x))


def _silu(x):
    return x * _sigmoid(x)


def _log_sigmoid(x):
    return jnp.minimum(x, 0.0) - jnp.log(1.0 + jnp.exp(-jnp.abs(x)))


def _dot(a, b, precision=None):
    return jnp.dot(a, b, preferred_element_type=F32, precision=precision)


def _dot_nt(a, b, precision=None):
    return lax.dot_general(a, b, (((1,), (1,)), ((), ())), preferred_element_type=F32, precision=precision)


def _rms(x):
    return x * lax.rsqrt(jnp.mean(x * x, axis=-1, keepdims=True) + EPS)


def _mod_kernel(c_ref, w_ref, b_ref, o_ref):
    a = _silu(c_ref[...]).astype(BF16)
    o_ref[...] = _dot(a, w_ref[...].astype(BF16)) + b_ref[...]


def _ada_mod(c, w, b):
    nl, d, d3 = w.shape
    bsz = c.shape[0]
    tn = 768
    return pl.pallas_call(
        _mod_kernel,
        grid=(nl, d3 // tn),
        in_specs=[pl.BlockSpec((bsz, d), lambda l, j: (0, 0)),
                  pl.BlockSpec((None, d, tn), lambda l, j: (l, 0, j)),
                  pl.BlockSpec((None, 1, tn), lambda l, j: (l, 0, j))],
        out_specs=pl.BlockSpec((None, bsz, tn), lambda l, j: (l, 0, j)),
        out_shape=jax.ShapeDtypeStruct((nl, bsz, d3), F32),
        compiler_params=_cparams(("parallel", "parallel")),
        name="ada_mod",
    )(c, w, b.reshape(nl, 1, d3))


def _modulated_norm(x_ref, g_ref, sc_ref, sh_ref):
    return (_rms(x_ref[...]) * g_ref[...]) * (1.0 + sc_ref[...]) + sh_ref[...]


def _proj_kernel(x_ref, g_ref, sc_ref, sh_ref, w_ref, o_ref, h_ref):
    i = pl.program_id(2)

    @pl.when(pl.program_id(1) == 0)
    def _():
        h_ref[i] = _modulated_norm(x_ref, g_ref, sc_ref, sh_ref).astype(BF16)

    o_ref[...] = _dot(h_ref[i], w_ref[...])


def _norm_proj(x, g, scale, shift, w, tm, tn):
    bsz, s, d = x.shape
    n = w.shape[1]
    rows = s // tm
    x_spec = pl.BlockSpec((None, tm, d), lambda b, j, i: (b, jnp.where(j == 0, i, rows - 1), 0))
    return pl.pallas_call(
        _proj_kernel,
        grid=(bsz, n // tn, rows),
        in_specs=[x_spec,
                  pl.BlockSpec((1, d), lambda b, j, i: (0, 0)),
                  pl.BlockSpec((None, 1, d), lambda b, j, i: (b, 0, 0)),
                  pl.BlockSpec((None, 1, d), lambda b, j, i: (b, 0, 0)),
                  pl.BlockSpec((d, tn), lambda b, j, i: (0, j))],
        out_specs=pl.BlockSpec((None, tm, tn), lambda b, j, i: (b, i, j)),
        out_shape=jax.ShapeDtypeStruct((bsz, s, n), F32),
        scratch_shapes=[pltpu.VMEM((rows, tm, d), BF16)],
        compiler_params=_cparams(("parallel", "arbitrary", "arbitrary")),
        name="norm_proj",
    )(x, g.reshape(1, d), scale, shift, w)


def _mlp_kernel(x_ref, g_ref, sc_ref, sh_ref, gate_ref, w1_ref, w2_ref, o_ref, h_ref, acc_ref):
    f = pl.program_id(2)

    @pl.when(f == 0)
    def _():
        h_ref[...] = _modulated_norm(x_ref, g_ref, sc_ref, sh_ref).astype(BF16)
        acc_ref[...] = jnp.zeros_like(acc_ref)

    u = jnp.maximum(_dot(h_ref[...], w1_ref[...]), 0.0)
    acc_ref[...] += _dot((u * u).astype(BF16), w2_ref[...])

    @pl.when(f == pl.num_programs(2) - 1)
    def _():
        o_ref[...] = x_ref[...] + gate_ref[...] * acc_ref[...]


def _mlp_sublayer(x, g, scale, shift, gate, w1, w2, tm=512, tf=1024):
    bsz, s, d = x.shape
    ff = w1.shape[1]
    vec = pl.BlockSpec((None, 1, d), lambda b, i, f: (b, 0, 0))
    return pl.pallas_call(
        _mlp_kernel,
        grid=(bsz, s // tm, ff // tf),
        in_specs=[pl.BlockSpec((None, tm, d), lambda b, i, f: (b, i, 0)),
                  pl.BlockSpec((1, d), lambda b, i, f: (0, 0)),
                  vec, vec, vec,
                  pl.BlockSpec((d, tf), lambda b, i, f: (0, f)),
                  pl.BlockSpec((tf, d), lambda b, i, f: (f, 0))],
        out_specs=pl.BlockSpec((None, tm, d), lambda b, i, f: (b, i, 0)),
        out_shape=jax.ShapeDtypeStruct((bsz, s, d), F32),
        scratch_shapes=[pltpu.VMEM((tm, d), BF16), pltpu.VMEM((tm, d), F32)],
        compiler_params=_cparams(("parallel", "parallel", "arbitrary")),
        name="mlp_sublayer",
    )(x, g.reshape(1, d), scale, shift, gate, w1, w2)


def _outproj_kernel(n_in, *refs):
    a_refs, w_refs = refs[:n_in], refs[n_in:2 * n_in]
    x_ref, gate_ref, o_ref = refs[2 * n_in:]
    y = _dot(a_refs[0][...], w_refs[0][...])
    for a_ref, w_ref in zip(a_refs[1:], w_refs[1:]):
        y += _dot(a_ref[...], w_ref[...])
    o_ref[...] = x_ref[...] + gate_ref[...] * y


def _out_proj(acts, ws, x, gate, tm=512):
    bsz, s, d = x.shape
    n_in = len(acts)
    in_specs = [pl.BlockSpec((None, tm, a.shape[-1]), lambda b, i: (b, i, 0)) for a in acts]
    in_specs += [pl.BlockSpec(w.shape, lambda b, i: (0, 0)) for w in ws]
    in_specs += [pl.BlockSpec((None, tm, d), lambda b, i: (b, i, 0)),
                 pl.BlockSpec((None, 1, d), lambda b, i: (b, 0, 0))]
    return pl.pallas_call(
        functools.partial(_outproj_kernel, n_in),
        grid=(bsz, s // tm),
        in_specs=in_specs,
        out_specs=pl.BlockSpec((None, tm, d), lambda b, i: (b, i, 0)),
        out_shape=jax.ShapeDtypeStruct((bsz, s, d), F32),
        compiler_params=_cparams(("parallel", "parallel")),
        name="out_proj",
    )(*acts, *ws, x, gate)


def _tri(n, upper=False):
    r = lax.broadcasted_iota(jnp.int32, (n, n), 0)
    c = lax.broadcasted_iota(jnp.int32, (n, n), 1)
    return (c >= r) if upper else (c <= r)


def _gla_kernel(q_ref, k_ref, v_ref, r_ref, zg_ref, wg_ref, bg_ref, ng_ref, o_ref, st_ref):
    @pl.when(pl.program_id(1) == 0)
    def _():
        st_ref[...] = jnp.zeros_like(st_ref)

    tril = _tri(CHUNK)
    tril_f = tril.astype(F32)
    for c in range(REC_BLOCK // CHUNK):
        rows = slice(c * CHUNK, (c + 1) * CHUNK)
        pre = _dot(zg_ref[rows, :].astype(BF16), wg_ref[...]) + bg_ref[...]
        log_a = _log_sigmoid(pre) / GLA_GATE_TAU
        cum = _dot(tril_f, log_a, precision=HI)
        total = cum[CHUNK - 1:CHUNK, :]
        q_dec = (q_ref[rows, :] * GLA_DK ** -0.5) * jnp.exp(cum)
        kk = k_ref[rows, :]
        k_dec = kk * jnp.exp(-cum)
        k_end = kk * jnp.exp(total - cum)
        decay = jnp.exp(total)
        for h in range(GLA_HEADS):
            ks = slice(h * GLA_DK, (h + 1) * GLA_DK)
            vs = slice(h * GLA_DV, (h + 1) * GLA_DV)
            qh = q_dec[:, ks].astype(BF16)
            vh = v_ref[rows, vs]
            att = jnp.where(tril, _dot_nt(qh, k_dec[:, ks].astype(BF16)), 0.0)
            st = st_ref[h]
            o = _dot(att.astype(BF16), vh.astype(BF16)) + _dot_nt(qh, st.astype(BF16))
            st_ref[h] = st * decay[:, ks] + _dot(vh.T.astype(BF16), k_end[:, ks].astype(BF16))
            rh = r_ref[rows, vs]
            o_ref[rows, vs] = ((_rms(o) * ng_ref[...]) * _silu(rh)).astype(o_ref.dtype)


def _gla(p, w_gate, b_gate, norm_g, col):
    bsz, s, _ = p.shape
    t = REC_BLOCK

    def piece(name):
        width, idx = col[name]
        return pl.BlockSpec((None, t, width), lambda b, i, idx=idx: (b, i, idx))

    const = lambda shape: pl.BlockSpec(shape, lambda b, i: (0,) * len(shape))
    hk = GLA_HEADS * GLA_DK
    return pl.pallas_call(
        _gla_kernel,
        grid=(bsz, s // t),
        in_specs=[piece("gla_q"), piece("gla_k"), piece("gla_v"), piece("gla_r"), piece("gates"),
                  const((LANES, hk)), const((1, hk)), const((1, GLA_DV))],
        out_specs=pl.BlockSpec((None, t, GLA_HEADS * GLA_DV), lambda b, i: (b, i, 0)),
        out_shape=jax.ShapeDtypeStruct((bsz, s, GLA_HEADS * GLA_DV), BF16),
        scratch_shapes=[pltpu.VMEM((GLA_HEADS, GLA_DV, GLA_DK), F32)],
        compiler_params=_cparams(("parallel", "arbitrary")),
        name="gla",
    )(p, p, p, p, p, w_gate, b_gate.reshape(1, hk), norm_g.reshape(1, GLA_DV))


def _mlstm_kernel(q_ref, k_ref, v_ref, og_ref, gc_ref, gr_ref, wq_ref, wk_ref, bc_ref, br_ref, ng_ref,
                  o_ref, c_ref, n_ref, m_ref, qx_ref, kx_ref):
    t = REC_BLOCK

    @pl.when(pl.program_id(1) == 0)
    def _():
        c_ref[...] = jnp.zeros_like(c_ref)
        n_ref[...] = jnp.zeros_like(n_ref)
        m_ref[...] = jnp.zeros_like(m_ref)
        qx_ref[0:8, :] = jnp.zeros((8, qx_ref.shape[1]), F32)
        kx_ref[0:8, :] = jnp.zeros((8, kx_ref.shape[1]), F32)

    def conv_silu(x_ref, xx_ref, w_ref):
        xx_ref[8:8 + t, :] = x_ref[...]
        y = xx_ref[8:8 + t, :] * w_ref[ML_CONV - 1:ML_CONV, :]
        for back in range(1, ML_CONV):
            y += xx_ref[8 - back:8 - back + t, :] * w_ref[ML_CONV - 1 - back:ML_CONV - back, :]
        xx_ref[0:8, :] = xx_ref[t:t + 8, :]
        return _silu(y)

    q_all = conv_silu(q_ref, qx_ref, wq_ref) * ML_DQK ** -0.5
    k_all = conv_silu(k_ref, kx_ref, wk_ref)

    tril = _tri(CHUNK)
    tril_f = tril.astype(F32)
    triu_f = _tri(CHUNK, upper=True).astype(F32)
    lane = lax.broadcasted_iota(jnp.int32, (CHUNK, LANES), 1)
    is_f_col = (lane >= ML_HEADS) & (lane < 2 * ML_HEADS)
    sub = lax.broadcasted_iota(jnp.int32, (8, CHUNK), 0)
    is_f_row = sub >= ML_HEADS

    for c in range(t // CHUNK):
        rows = slice(c * CHUNK, (c + 1) * CHUNK)
        gcol = gc_ref[rows, :] + bc_ref[...]
        gcol = jnp.where(is_f_col, _log_sigmoid(gcol), gcol)
        bcol = _dot(tril_f, gcol, precision=HI)
        grow = gr_ref[:, rows] + br_ref[:, 0:1]
        grow = jnp.where(is_f_row, _log_sigmoid(grow), grow)
        brow = _dot(grow, triu_f, precision=HI)
        for h in range(ML_HEADS):
            qs = slice(h * ML_DQK, (h + 1) * ML_DQK)
            vs = slice(h * ML_DV, (h + 1) * ML_DV)
            b_col = bcol[:, ML_HEADS + h:ML_HEADS + h + 1]
            b_row = brow[ML_HEADS + h:ML_HEADS + h + 1, :]
            li_col = gcol[:, h:h + 1]
            li_row = grow[h:h + 1, :]
            m_prev = m_ref[h:h + 1, 0:1]
            log_d = jnp.where(tril, b_col - b_row + li_row, -jnp.inf)
            log_inter = b_col + m_prev
            m_comb = jnp.maximum(log_inter, jnp.max(log_d, axis=-1, keepdims=True))
            d = jnp.exp(log_d - m_comb)
            w_inter = jnp.exp(log_inter - m_comb)
            qh = q_all[rows, qs]
            kh = k_all[rows, qs]
            qb = qh.astype(BF16)
            vb = v_ref[rows, vs].astype(BF16)
            sc = _dot_nt(qb, kh.astype(BF16)) * d
            cm = c_ref[h]
            nv = n_ref[h]
            num = _dot(sc.astype(BF16), vb) + w_inter * _dot(qb, cm.astype(BF16))
            den = jnp.sum(sc, axis=-1, keepdims=True) + w_inter * jnp.sum(qh * nv, axis=-1, keepdims=True)
            hh = num / jnp.maximum(jnp.abs(den), jnp.exp(-m_comb))
            total = b_col[CHUNK - 1:CHUNK, :]
            log_w = total - b_col + li_col
            m_new = jnp.maximum(total + m_prev, jnp.max(log_w, axis=0, keepdims=True))
            decay = jnp.exp(total + m_prev - m_new)
            kw = kh * jnp.exp(log_w - m_new)
            c_ref[h] = decay * cm + _dot(kw.T.astype(BF16), vb)
            n_ref[h] = decay * nv + jnp.sum(kw, axis=0, keepdims=True)
            m_ref[h:h + 1, :] = jnp.broadcast_to(m_new, (1, LANES))
            og = og_ref[rows, vs]
            o_ref[rows, vs] = ((_rms(hh) * ng_ref[...]) * _sigmoid(og)).astype(o_ref.dtype)


def _mlstm(p, gates_row, conv_w, bias_col, bias_row, norm_g, col):
    bsz, s, _ = p.shape
    t = REC_BLOCK
    hq = ML_HEADS * ML_DQK
    hv = ML_HEADS * ML_DV

    def piece(name):
        width, idx = col[name]
        return pl.BlockSpec((None, t, width), lambda b, i, idx=idx: (b, i, idx))

    const = lambda shape: pl.BlockSpec(shape, lambda b, i: (0,) * len(shape))
    return pl.pallas_call(
        _mlstm_kernel,
        grid=(bsz, s // t),
        in_specs=[piece("ml_q"), piece("ml_k"), piece("ml_v"), piece("ml_og"), piece("gates"),
                  pl.BlockSpec((None, 8, t), lambda b, i: (b, 0, i)),
                  pl.BlockSpec((ML_CONV, hq), lambda b, i: (0, 0)),
                  pl.BlockSpec((ML_CONV, hq), lambda b, i: (0, 1)),
                  const((1, LANES)), const((8, LANES)), const((1, ML_DV))],
        out_specs=pl.BlockSpec((None, t, hv), lambda b, i: (b, i, 0)),
        out_shape=jax.ShapeDtypeStruct((bsz, s, hv), BF16),
        scratch_shapes=[pltpu.VMEM((ML_HEADS, ML_DQK, ML_DV), F32),
                        pltpu.VMEM((ML_HEADS, 1, ML_DQK), F32),
                        pltpu.VMEM((8, LANES), F32),
                        pltpu.VMEM((t + 8, hq), F32),
                        pltpu.VMEM((t + 8, hq), F32)],
        compiler_params=_cparams(("parallel", "arbitrary")),
        name="mlstm",
    )(p, p, p, p, p, gates_row, conv_w, conv_w, bias_col, bias_row, norm_g.reshape(1, ML_DV))


def _split_bf16(x):
    hi = x.astype(BF16)
    return hi, (x - hi.astype(F32)).astype(BF16)


def _seg_norm(x, seg_ref, segt_ref):
    if seg_ref.dtype == BF16:
        seg_sum = lambda v, m_ref: sum(_dot(part, m_ref[...]) for part in _split_bf16(v))
    else:
        seg_sum = lambda v, m_ref: _dot(v, m_ref[...], precision=HI)
    ss = seg_sum(x * x, seg_ref) * (1.0 / NSA_DH)
    return x * seg_sum(lax.rsqrt(ss + EPS), segt_ref)


def _nsa_prep_kernel(q_ref, kc_ref, vc_ref, ks_ref, vs_ref, kw_ref, vw_ref, qg_ref, kg_ref, segk_ref, segkt_ref,
                     qnt_ref, kcc_ref, vcc_ref, ksg_ref, vst_ref, kwg_ref, vwt_ref):
    t = q_ref.shape[0]
    qt = q_ref[...].T
    for h in range(NSA_HEADS):
        hs = slice(h * NSA_DH, (h + 1) * NSA_DH)
        blk = qt[hs, :]
        inv = lax.rsqrt(jnp.mean(blk * blk, axis=0, keepdims=True) + EPS)
        qnt_ref[hs, :] = ((blk * inv) * qg_ref[hs, :]).astype(BF16)
    kcc_ref[...] = kc_ref[...]
    vcc_ref[...] = vc_ref[...]
    ones = jnp.ones((NSA_VPAD - NSA_DH, t), BF16)
    for k_ref, v_ref, kg_out, vt_out in ((ks_ref, vs_ref, ksg_ref, vst_ref), (kw_ref, vw_ref, kwg_ref, vwt_ref)):
        kn = (_seg_norm(k_ref[...], segk_ref, segkt_ref) * kg_ref[...]).astype(BF16)
        vt = v_ref[...].T.astype(BF16)
        for g in range(NSA_GROUPS):
            gs = slice(g * NSA_DH, (g + 1) * NSA_DH)
            kg_out[g] = kn[:, gs]
            vt_out[g, 0:NSA_DH, :] = vt[gs, :]
            vt_out[g, NSA_DH:NSA_VPAD, :] = ones


def _seg_matrices(width, dtype):
    seg = (np.arange(width)[:, None] // NSA_DH == np.arange(LANES)[None, :]).astype(np.float32)
    return jnp.asarray(seg, dtype=dtype), jnp.asarray(seg.T, dtype=dtype)


def _nsa_prep(p, q_gain, k_gain, col, t=512):
    bsz, s, _ = p.shape
    hq = NSA_HEADS * NSA_DH
    kv = NSA_GROUPS * NSA_DH

    def piece(name):
        width, idx = col[name]
        return pl.BlockSpec((None, t, width), lambda b, i, idx=idx: (b, i, idx))

    const = lambda shape: pl.BlockSpec(shape, lambda b, i: (0,) * len(shape))
    segk, segkt = _seg_matrices(kv, BF16)
    row = pl.BlockSpec((None, t, kv), lambda b, i: (b, i, 0))
    keys = pl.BlockSpec((None, NSA_GROUPS, t, NSA_DH), lambda b, i: (b, 0, i, 0))
    vals = pl.BlockSpec((None, NSA_GROUPS, NSA_VPAD, t), lambda b, i: (b, 0, 0, i))
    shp = lambda *dims, dt=BF16: jax.ShapeDtypeStruct(dims, dt)
    keys_shape = shp(bsz, NSA_GROUPS, s, NSA_DH)
    vals_shape = shp(bsz, NSA_GROUPS, NSA_VPAD, s)
    return pl.pallas_call(
        _nsa_prep_kernel,
        grid=(bsz, s // t),
        in_specs=[piece("nsa_q"), piece("nsa_kc"), piece("nsa_vc"), piece("nsa_ks"), piece("nsa_vs"),
                  piece("nsa_kw"), piece("nsa_vw"), const((hq, 1)), const((1, kv)),
                  const((kv, LANES)), const((LANES, kv))],
        out_specs=[pl.BlockSpec((None, hq, t), lambda b, i: (b, 0, i)), row, row, keys, vals, keys, vals],
        out_shape=[shp(bsz, hq, s), shp(bsz, s, kv, dt=F32), shp(bsz, s, kv, dt=F32),
                   keys_shape, vals_shape, keys_shape, vals_shape],
        compiler_params=_cparams(("parallel", "parallel")),
        name="nsa_prep",
    )(p, p, p, p, p, p, p, q_gain, k_gain, segk, segkt)


def _compress_kernel(kc_ref, vc_ref, posk_ref, posv_ref, kw1a_ref, kw1b_ref, kw2_ref, vw1a_ref, vw1b_ref,
                     vw2_ref, kg_ref, segk_ref, segkt_ref, kcg_ref, vct_ref):
    def hidden(x_ref, pos_ref, w1a_ref, w1b_ref):
        x = x_ref[...]
        nsub = x.shape[0]
        first = _dot((x + pos_ref[0:1, :]).astype(BF16), w1a_ref[...])
        second = _dot((x + pos_ref[1:2, :]).astype(BF16), w1b_ref[...])
        row = lax.broadcasted_iota(jnp.int32, second.shape, 0)
        nxt = jnp.where(row == nsub - 1, 0.0, pltpu.roll(second, nsub - 1, 0))
        return _silu(first + nxt)

    kc = _dot(hidden(kc_ref, posk_ref, kw1a_ref, kw1b_ref).astype(BF16), kw2_ref[...])
    kn = (_seg_norm(kc, segk_ref, segkt_ref) * kg_ref[...]).astype(BF16)
    vt = _dot(vw2_ref[...], hidden(vc_ref, posv_ref, vw1a_ref, vw1b_ref).T.astype(BF16)).astype(BF16)
    for g in range(NSA_GROUPS):
        gs = slice(g * NSA_DH, (g + 1) * NSA_DH)
        kcg_ref[g] = kn[:, gs]
        vct_ref[g] = vt[gs, :]


def _expand_w1(w1):
    w = w1.reshape(2, CMP_STRIDE, NSA_DH, NSA_DH)
    eye = jnp.eye(NSA_GROUPS, dtype=w1.dtype)
    big = jnp.einsum('hlde,gk->hlgdke', w, eye)
    big = big.reshape(2, CMP_STRIDE * NSA_GROUPS * NSA_DH, NSA_GROUPS * NSA_DH).astype(BF16)
    return big[0], big[1]


def _expand_pos(pos):
    pp = pos.reshape(2, CMP_STRIDE, 1, NSA_DH)
    return jnp.broadcast_to(pp, (2, CMP_STRIDE, NSA_GROUPS, NSA_DH)).reshape(2, -1)


def _block_diag(w2):
    return jnp.kron(jnp.eye(NSA_GROUPS, dtype=w2.dtype), w2).astype(BF16)


def _compress(kc, vc, pos_k, pos_v, ck_w1, ck_w2, cv_w1, cv_w2, k_gain):
    bsz, s, kv = kc.shape
    nsub = s // CMP_STRIDE
    flat = CMP_STRIDE * kv
    segk, segkt = _seg_matrices(kv, F32)
    kw1a, kw1b = _expand_w1(ck_w1)
    vw1a, vw1b = _expand_w1(cv_w1)
    const = lambda shape: pl.BlockSpec(shape, lambda b: (0,) * len(shape))
    tok = pl.BlockSpec((None, nsub, flat), lambda b: (b, 0, 0))
    return pl.pallas_call(
        _compress_kernel,
        grid=(bsz,),
        in_specs=[tok, tok, const((2, flat)), const((2, flat)),
                  const((flat, kv)), const((flat, kv)), const((kv, kv)),
                  const((flat, kv)), const((flat, kv)), const((kv, kv)),
                  const((1, kv)), const((kv, LANES)), const((LANES, kv))],
        out_specs=[pl.BlockSpec((None, NSA_GROUPS, nsub, NSA_DH), lambda b: (b, 0, 0, 0)),
                   pl.BlockSpec((None, NSA_GROUPS, NSA_DH, nsub), lambda b: (b, 0, 0, 0))],
        out_shape=[jax.ShapeDtypeStruct((bsz, NSA_GROUPS, nsub, NSA_DH), BF16),
                   jax.ShapeDtypeStruct((bsz, NSA_GROUPS, NSA_DH, nsub), BF16)],
        compiler_params=_cparams(("parallel",)),
        name="nsa_compress",
    )(kc.reshape(bsz, nsub, flat), vc.reshape(bsz, nsub, flat), _expand_pos(pos_k), _expand_pos(pos_v),
      kw1a, kw1b, _block_diag(ck_w2), vw1a, vw1b, _block_diag(cv_w2).T, k_gain, segk, segkt)


def _nsa_attn_kernel(qt_ref, gt_ref, kcg_ref, vct_ref, ksg_ref, vst_ref, kwg_ref, vwt_ref, ovt_ref,
                     o_ref, acc_ref, sel_ref, s_ref, p_ref, pv_ref):
    tq, tk = NSA_TQ, NSA_TK
    q0 = pl.program_id(1) * tq
    ncmp = kcg_ref.shape[1]
    nsel = ovt_ref.shape[0]
    cols4 = NSA_HPG * tq

    t_cmp = q0 + lax.broadcasted_iota(jnp.int32, (ncmp, cols4), 1) % tq
    n_idx = lax.broadcasted_iota(jnp.int32, (ncmp, cols4), 0)
    cmp_valid = n_idx * CMP_STRIDE + (CMP_LEN - 1) <= t_cmp

    j_idx = lax.broadcasted_iota(jnp.int32, (nsel, tq), 0)
    cur = (q0 + lax.broadcasted_iota(jnp.int32, (nsel, tq), 1)) // SEL_BLOCK
    forced = (j_idx == 0) | (j_idx == cur) | (j_idx == cur - 1)
    causal_blk = j_idx <= cur

    row_k = lax.broadcasted_iota(jnp.int32, (tk, tq), 0)
    t_q = q0 + lax.broadcasted_iota(jnp.int32, (tk, tq), 1)

    gates_t = _sigmoid(gt_ref[...]).T

    def add_gated(g, branch, o4):
        for i in range(NSA_HPG):
            h = g * NSA_HPG + i
            hs = slice(h * NSA_DH, (h + 1) * NSA_DH)
            gc = GLA_RANK + 3 * h + branch
            term = gates_t[gc:gc + 1, :] * o4[:, i * tq:(i + 1) * tq]
            acc_ref[hs, :] = term if branch == 0 else acc_ref[hs, :] + term

    def q_group(g):
        return jnp.concatenate([qt_ref[(g * NSA_HPG + i) * NSA_DH:(g * NSA_HPG + i + 1) * NSA_DH, :]
                                for i in range(NSA_HPG)], axis=1)

    def attend(branch, g, k_ref, vt_ref, first_tile, n_tiles, mask_fn):
        qt4 = q_group(g)
        last = first_tile + n_tiles - 1

        def scores(kt):
            return _dot(k_ref[g, pl.ds(pl.multiple_of(kt * tk, tk), tk), :], qt4)

        def weighted_values(kt, p):
            return _dot(vt_ref[g, :, pl.ds(pl.multiple_of(kt * tk, tk), tk)], p)

        def step(kt, cur, m):
            nxt = 1 - cur
            s_ref[nxt] = scores(jnp.minimum(kt + 1, last))
            acc = pv_ref[...] + weighted_values(jnp.clip(kt - 1, first_tile, last), p_ref[nxt])
            k0 = jnp.where(kt <= last, kt * tk, PAST_END)
            ok = mask_fn(g, jnp.minimum(kt, last), k0)
            m_new = []
            for c in range(NSA_HPG):
                cs = slice(c * tq, (c + 1) * tq)
                s = jnp.where(ok, s_ref[cur, :, cs], NEG)
                m_c = jnp.maximum(m[:, cs], jnp.max(s, axis=0, keepdims=True))
                p_ref[cur, :, cs] = jnp.exp(s - m_c).astype(BF16)
                m_new.append(m_c)
            m_new = jnp.concatenate(m_new, axis=1)
            pv_ref[...] = acc * jnp.exp(m - m_new)
            return m_new

        s_ref[0] = scores(first_tile)
        p_ref[1] = jnp.zeros(p_ref.shape[1:], BF16)
        pv_ref[...] = jnp.zeros_like(pv_ref)
        n_pairs = (n_tiles + 1) // 2

        def body(j, m):
            kt = first_tile + 2 * j
            return step(kt + 1, 1, step(kt, 0, m))

        lax.fori_loop(0, n_pairs, body, jnp.full((1, cols4), NEG, F32))
        final_tile = jnp.minimum(first_tile + 2 * n_pairs - 1, last)
        acc = pv_ref[...] + weighted_values(final_tile, p_ref[1])
        add_gated(g, branch, acc[0:NSA_DH, :] / acc[NSA_DH:NSA_DH + 1, :])

    last_tile = (q0 + tq - 1) // tk
    win_first = jnp.maximum(q0 - (WINDOW - 1), 0) // tk

    def sel_mask(g, kt, k0):
        per_tile = tk // SEL_BLOCK
        picked = jnp.concatenate(
            [jnp.broadcast_to(sel_ref[g, pl.ds(kt * per_tile + j, 1), :], (SEL_BLOCK, tq))
             for j in range(per_tile)], axis=0)
        return (picked > 0.5) & (k0 + row_k <= t_q)

    def win_mask(g, kt, k0):
        diff = t_q - (k0 + row_k)
        return (diff >= 0) & (diff < WINDOW)

    for g in range(NSA_GROUPS):
        lg = jnp.where(cmp_valid, _dot(kcg_ref[g], q_group(g)), NEG)
        ex = jnp.exp(lg - jnp.max(lg, axis=0, keepdims=True))
        p_cmp = jnp.where(cmp_valid, ex / jnp.sum(ex, axis=0, keepdims=True), 0.0)
        add_gated(g, 0, _dot(vct_ref[g], p_cmp.astype(BF16)))

        p_grp = p_cmp[:, 0:tq] + p_cmp[:, tq:2 * tq] + p_cmp[:, 2 * tq:3 * tq] + p_cmp[:, 3 * tq:4 * tq]
        imp = _dot(ovt_ref[...], p_grp, precision=HI)
        val = jnp.where(forced, jnp.inf, jnp.where(causal_blk, imp, -jnp.inf))
        rank = jnp.zeros((nsel, tq), F32)
        for i in range(nsel):
            vi = val[i:i + 1, :]
            rank += ((vi > val) | ((vi == val) & (j_idx > i))).astype(F32)
        sel_ref[g] = (rank < N_SELECT).astype(F32)

    for g in range(NSA_GROUPS):
        attend(1, g, ksg_ref, vst_ref, 0, last_tile + 1, sel_mask)
        attend(2, g, kwg_ref, vwt_ref, win_first, last_tile - win_first + 1, win_mask)
    o_ref[...] = acc_ref[...].T.astype(o_ref.dtype)


def _overlap_t(s):
    ncmp = s // CMP_STRIDE
    nsel = s // SEL_BLOCK
    c_start = np.arange(ncmp) * CMP_STRIDE
    s_start = np.arange(nsel) * SEL_BLOCK
    overlap_t = ((c_start[None, :] <= s_start[:, None] + SEL_BLOCK - 1)
                 & (c_start[None, :] + CMP_LEN - 1 >= s_start[:, None])).astype(np.float32)
    overlap_t[:, ncmp - 1] = 0.0
    return jnp.asarray(overlap_t)


def _nsa_attention(p, qnt, kcg, vct, ksg, vst, kwg, vwt, col):
    bsz, hq, s = qnt.shape
    ncmp = s // CMP_STRIDE
    nsel = s // SEL_BLOCK
    gw, gidx = col["gates"]
    per_row = lambda shape: pl.BlockSpec((None,) + shape, lambda b, i: (b,) + (0,) * len(shape))
    keys = per_row((NSA_GROUPS, s, NSA_DH))
    vals = per_row((NSA_GROUPS, NSA_VPAD, s))
    return pl.pallas_call(
        _nsa_attn_kernel,
        grid=(bsz, s // NSA_TQ),
        in_specs=[pl.BlockSpec((None, hq, NSA_TQ), lambda b, i: (b, 0, i)),
                  pl.BlockSpec((None, NSA_TQ, gw), lambda b, i: (b, i, gidx)),
                  per_row((NSA_GROUPS, ncmp, NSA_DH)), per_row((NSA_GROUPS, NSA_DH, ncmp)),
                  keys, vals, keys, vals,
                  pl.BlockSpec((nsel, ncmp), lambda b, i: (0, 0))],
        out_specs=pl.BlockSpec((None, NSA_TQ, hq), lambda b, i: (b, i, 0)),
        out_shape=jax.ShapeDtypeStruct((bsz, s, hq), BF16),
        scratch_shapes=[pltpu.VMEM((hq, NSA_TQ), F32), pltpu.VMEM((NSA_GROUPS, nsel, NSA_TQ), F32),
                        pltpu.VMEM((2, NSA_TK, NSA_HPG * NSA_TQ), F32),
                        pltpu.VMEM((2, NSA_TK, NSA_HPG * NSA_TQ), BF16),
                        pltpu.VMEM((NSA_VPAD, NSA_HPG * NSA_TQ), F32)],
        compiler_params=_cparams(("parallel", "arbitrary")),
        name="nsa_attention",
    )(qnt, p, kcg, vct, ksg, vst, kwg, vwt, _overlap_t(s))


def _layout(pieces, tile):
    col, off, mats = {}, 0, []
    for name, w in pieces:
        width = w.shape[1]
        assert off % width == 0, (name, off, width)
        col[name] = (width, off // width)
        mats.append(w)
        off += width
    pad = -off % tile
    if pad:
        mats.append(jnp.zeros((mats[0].shape[0], pad), mats[0].dtype))
    return jnp.concatenate(mats, axis=1).astype(BF16), col


def _pad_cols(w, width):
    return jnp.pad(w, ((0, 0), (0, width - w.shape[1])))


AB_TILE = 1920
ML_TILE = 1280


def _ab_layout(w_in):
    hk = GLA_HEADS * GLA_DK
    hv = GLA_HEADS * GLA_DV
    kv = NSA_GROUPS * NSA_DH
    sizes = [hk, hk, hv, hv, GLA_RANK, NSA_HEADS * NSA_DH] + [kv] * 6 + [3 * NSA_HEADS]
    cuts = np.cumsum(sizes)[:-1].tolist()
    gq, gk, gv, gr, gz, nq, kc, vc, ks, vs, kw, vw, gt = jnp.split(w_in, cuts, axis=1)
    gates = _pad_cols(jnp.concatenate([gz, gt], axis=1), LANES)
    return _layout([("gla_q", gq), ("gla_k", gk), ("gla_v", gv), ("gla_r", gr), ("nsa_q", nq),
                    ("nsa_kc", kc), ("nsa_vc", vc), ("nsa_ks", ks), ("nsa_vs", vs), ("nsa_kw", kw),
                    ("nsa_vw", vw), ("gates", gates)], AB_TILE)


def _ml_layout(w_in):
    hq = ML_HEADS * ML_DQK
    hv = ML_HEADS * ML_DV
    cuts = np.cumsum([hq, hq, hv, hv])[:].tolist()
    q, k, v, og, gates = jnp.split(w_in, cuts, axis=1)
    return _layout([("ml_q", q), ("ml_k", k), ("ml_v", v), ("ml_og", og),
                    ("gates", _pad_cols(gates, LANES))], ML_TILE)


def kernel(x, c, norm_mix_g, mod_mix_w, mod_mix_b, norm_mlp_g, mod_mlp_w, mod_mlp_b, mlp_w1, mlp_w2, ab_w_in, ab_w_out, gla_w_gate, gla_b_gate, gla_norm_g, nsa_q_norm_g, nsa_k_norm_g, nsa_cmp_pos_k, nsa_cmp_pos_v, nsa_cmp_k_w1, nsa_cmp_k_w2, nsa_cmp_v_w1, nsa_cmp_v_w2, ml_w_in, ml_w_out, ml_conv_w, ml_b_i, ml_b_f, ml_norm_g):
    bsz, s, d = x.shape
    depth = norm_mix_g.shape[0]
    mod_mix = _ada_mod(c, mod_mix_w, mod_mix_b)
    mod_mlp = _ada_mod(c, mod_mlp_w, mod_mlp_b)

    def split_mod(m):
        return [m[:, None, i * d:(i + 1) * d] for i in range(3)]

    for layer in range(depth):
        shift, scale, gate = split_mod(mod_mix[layer])
        if layer % 2 == 0:
            e = layer // 2
            w_in, col = _ab_layout(ab_w_in[e])
            p = _norm_proj(x, norm_mix_g[layer], scale, shift, w_in, tm=512, tn=AB_TILE)
            w_gate = jnp.pad(gla_w_gate[e], ((0, LANES - GLA_RANK), (0, 0))).astype(BF16)
            o_gla = _gla(p, w_gate, gla_b_gate[e], gla_norm_g[e], col)
            q_gain = (jnp.tile(nsa_q_norm_g[e], NSA_HEADS) * NSA_DH ** -0.5).reshape(-1, 1)
            k_gain = jnp.tile(nsa_k_norm_g[e], NSA_GROUPS).reshape(1, -1)
            qnt, kcc, vcc, ksg, vst, kwg, vwt = _nsa_prep(p, q_gain, k_gain, col)
            kcg, vct = _compress(kcc, vcc, nsa_cmp_pos_k[e], nsa_cmp_pos_v[e], nsa_cmp_k_w1[e],
                                 nsa_cmp_k_w2[e], nsa_cmp_v_w1[e], nsa_cmp_v_w2[e], k_gain)
            o_nsa = _nsa_attention(p, qnt, kcg, vct, ksg, vst, kwg, vwt, col)
            w_out = ab_w_out[e].astype(BF16)
            half = GLA_HEADS * GLA_DV
            x = _out_proj([o_gla, o_nsa], [w_out[:half], w_out[half:]], x, gate)
        else:
            o = layer // 2
            w_in, col = _ml_layout(ml_w_in[o])
            p = _norm_proj(x, norm_mix_g[layer], scale, shift, w_in, tm=512, tn=ML_TILE)
            gw, gidx = col["gates"]
            gates_row = jnp.swapaxes(p[:, :, gidx * gw:gidx * gw + 2 * ML_HEADS], 1, 2)
            bias = jnp.concatenate([ml_b_i[o], ml_b_f[o]])
            bias_col = _pad_cols(bias.reshape(1, -1), LANES)
            bias_row = jnp.broadcast_to(bias.reshape(-1, 1), (2 * ML_HEADS, LANES))
            hh = _mlstm(p, gates_row, ml_conv_w[o], bias_col, bias_row, ml_norm_g[o], col)
            x = _out_proj([hh], [ml_w_out[o].astype(BF16)], x, gate)
        shift, scale, gate = split_mod(mod_mlp[layer])
        x = _mlp_sublayer(x, norm_mlp_g[layer], scale, shift, gate,
                          mlp_w1[layer].astype(BF16), mlp_w2[layer].astype(BF16))
    return x
```

```python
import functools

import numpy as np
import jax
import jax.numpy as jnp
from jax import lax
from jax.experimental import pallas as pl
from jax.experimental.pallas import tpu as pltpu

F32 = jnp.float32
BF16 = jnp.bfloat16
HI = lax.Precision.HIGHEST

EPS = 1e-6
NEG = -1e30
LOG2E = 1.4426950408889634

VMEM_LIMIT_BYTES = 56 * 1024 * 1024
LANES = 128

GLA_HEADS = 4
GLA_DK = 128
GLA_DV = 256
GLA_RANK = 16
GLA_GATE_TAU = 16.0
CHUNK = 64
ML_CHUNK = 128
REC_BLOCK = 256

NSA_HEADS = 16
NSA_GROUPS = 4
NSA_HPG = 4
NSA_DH = 64
CMP_LEN = 32
CMP_STRIDE = 16
SEL_BLOCK = 64
N_SELECT = 16
WINDOW = 512
NSA_TQ = 128
NSA_TK = 256
NSA_VPAD = 80
PAST_END = 1 << 24

ML_HEADS = 4
ML_DQK = 256
ML_DV = 512
ML_CONV = 4


def _cparams(sem):
    return pltpu.CompilerParams(dimension_semantics=sem, vmem_limit_bytes=VMEM_LIMIT_BYTES)


def _sigmoid(x):
    return 1.0 / (1.0 + jnp.exp(-x))


def _silu(x):
    return x * _sigmoid(x)


def _log_sigmoid(x):
    return jnp.minimum(x, 0.0) - jnp.log(1.0 + jnp.exp(-jnp.abs(x)))


def _dot(a, b, precision=None):
    return jnp.dot(a, b, preferred_element_type=F32, precision=precision)


def _dot_nt(a, b, precision=None):
    return lax.dot_general(a, b, (((1,), (1,)), ((), ())), preferred_element_type=F32, precision=precision)


def _rms(x):
    return x * lax.rsqrt(jnp.mean(x * x, axis=-1, keepdims=True) + EPS)


def _mod_kernel(c_ref, w_ref, b_ref, o_ref):
    a = _silu(c_ref[...]).astype(BF16)
    o_ref[...] = _dot(a, w_ref[...].astype(BF16)) + b_ref[...]


def _ada_mod(c, w, b):
    nl, d, d3 = w.shape
    bsz = c.shape[0]
    tn = 768
    return pl.pallas_call(
        _mod_kernel,
        grid=(nl, d3 // tn),
        in_specs=[pl.BlockSpec((bsz, d), lambda l, j: (0, 0)),
                  pl.BlockSpec((None, d, tn), lambda l, j: (l, 0, j)),
                  pl.BlockSpec((None, 1, tn), lambda l, j: (l, 0, j))],
        out_specs=pl.BlockSpec((None, bsz, tn), lambda l, j: (l, 0, j)),
        out_shape=jax.ShapeDtypeStruct((nl, bsz, d3), F32),
        compiler_params=_cparams(("parallel", "parallel")),
        name="ada_mod",
    )(c, w, b.reshape(nl, 1, d3))


def _modulated_norm(x, g_ref, sc_ref, sh_ref):
    return ((_rms(x) * g_ref[...]) * (1.0 + sc_ref[...]) + sh_ref[...]).astype(BF16)


def _proj_kernel(x0_ref, xn_ref, g_ref, sc_ref, sh_ref, w_ref, o_ref, h_ref, hn_ref):
    j, i = pl.program_id(1), pl.program_id(2)

    @pl.when((j == 0) & (i == 0))
    def _():
        h_ref[0] = _modulated_norm(x0_ref[...], g_ref, sc_ref, sh_ref)

    @pl.when((j == 0) & (i > 0))
    def _():
        h_ref[i] = hn_ref[...]

    @pl.when(j == 0)
    def _():
        hn_ref[...] = _modulated_norm(xn_ref[...], g_ref, sc_ref, sh_ref)
        o_ref[...] = _dot(h_ref[i], w_ref[...])

    @pl.when(j > 0)
    def _():
        o_ref[...] = _dot(h_ref[i], w_ref[...])


def _norm_proj(x, g, scale, shift, w, tm, tn):
    bsz, s, d = x.shape
    n = w.shape[1]
    rows = s // tm
    xn_spec = pl.BlockSpec((None, tm, d), lambda b, j, i: (b, jnp.where(j == 0, jnp.minimum(i + 1, rows - 1),
                                                                         rows - 1), 0))
    return pl.pallas_call(
        _proj_kernel,
        grid=(bsz, n // tn, rows),
        in_specs=[pl.BlockSpec((None, tm, d), lambda b, j, i: (b, 0, 0)),
                  xn_spec,
                  pl.BlockSpec((1, d), lambda b, j, i: (0, 0)),
                  pl.BlockSpec((None, 1, d), lambda b, j, i: (b, 0, 0)),
                  pl.BlockSpec((None, 1, d), lambda b, j, i: (b, 0, 0)),
                  pl.BlockSpec((d, tn), lambda b, j, i: (0, j))],
        out_specs=pl.BlockSpec((None, tm, tn), lambda b, j, i: (b, i, j)),
        out_shape=jax.ShapeDtypeStruct((bsz, s, n), F32),
        scratch_shapes=[pltpu.VMEM((rows, tm, d), BF16), pltpu.VMEM((tm, d), BF16)],
        compiler_params=_cparams(("parallel", "arbitrary", "arbitrary")),
        name="norm_proj",
    )(x, x, g.reshape(1, d), scale, shift, w)


def _mlp_kernel(x_ref, g_ref, sc_ref, sh_ref, gate_ref, w1_ref, w2_ref, o_ref, h_ref, acc_ref):
    f = pl.program_id(2)

    @pl.when(f == 0)
    def _():
        h_ref[...] = _modulated_norm(x_ref[...], g_ref, sc_ref, sh_ref)
        acc_ref[...] = jnp.zeros_like(acc_ref)

    u = jnp.maximum(_dot(h_ref[...], w1_ref[...]), 0.0)
    acc_ref[...] += _dot((u * u).astype(BF16), w2_ref[...])

    @pl.when(f == pl.num_programs(2) - 1)
    def _():
        o_ref[...] = x_ref[...] + gate_ref[...] * acc_ref[...]


def _mlp_sublayer(x, g, scale, shift, gate, w1, w2, tm=512, tf=1024):
    bsz, s, d = x.shape
    ff = w1.shape[1]
    vec = pl.BlockSpec((None, 1, d), lambda b, i, f: (b, 0, 0))
    return pl.pallas_call(
        _mlp_kernel,
        grid=(bsz, s // tm, ff // tf),
        in_specs=[pl.BlockSpec((None, tm, d), lambda b, i, f: (b, i, 0)),
                  pl.BlockSpec((1, d), lambda b, i, f: (0, 0)),
                  vec, vec, vec,
                  pl.BlockSpec((d, tf), lambda b, i, f: (0, f)),
                  pl.BlockSpec((tf, d), lambda b, i, f: (f, 0))],
        out_specs=pl.BlockSpec((None, tm, d), lambda b, i, f: (b, i, 0)),
        out_shape=jax.ShapeDtypeStruct((bsz, s, d), F32),
        scratch_shapes=[pltpu.VMEM((tm, d), BF16), pltpu.VMEM((tm, d), F32)],
        compiler_params=_cparams(("parallel", "parallel", "arbitrary")),
        name="mlp_sublayer",
    )(x, g.reshape(1, d), scale, shift, gate, w1, w2)


def _outproj_kernel(n_in, *refs):
    a_refs, w_refs = refs[:n_in], refs[n_in:2 * n_in]
    x_ref, gate_ref, o_ref = refs[2 * n_in:]
    y = _dot(a_refs[0][...], w_refs[0][...])
    for a_ref, w_ref in zip(a_refs[1:], w_refs[1:]):
        y += _dot(a_ref[...], w_ref[...])
    o_ref[...] = x_ref[...] + gate_ref[...] * y


def _out_proj(acts, ws, x, gate, tm=512):
    bsz, s, d = x.shape
    n_in = len(acts)
    in_specs = [pl.BlockSpec((None, tm, a.shape[-1]), lambda b, i: (b, i, 0)) for a in acts]
    in_specs += [pl.BlockSpec(w.shape, lambda b, i: (0, 0)) for w in ws]
    in_specs += [pl.BlockSpec((None, tm, d), lambda b, i: (b, i, 0)),
                 pl.BlockSpec((None, 1, d), lambda b, i: (b, 0, 0))]
    return pl.pallas_call(
        functools.partial(_outproj_kernel, n_in),
        grid=(bsz, s // tm),
        in_specs=in_specs,
        out_specs=pl.BlockSpec((None, tm, d), lambda b, i: (b, i, 0)),
        out_shape=jax.ShapeDtypeStruct((bsz, s, d), F32),
        compiler_params=_cparams(("parallel", "parallel")),
        name="out_proj",
    )(*acts, *ws, x, gate)


def _tri(n, upper=False):
    r = lax.broadcasted_iota(jnp.int32, (n, n), 0)
    c = lax.broadcasted_iota(jnp.int32, (n, n), 1)
    return (c >= r) if upper else (c <= r)


def _gla_kernel(q_ref, k_ref, v_ref, r_ref, zg_ref, wg_ref, bg_ref, ng_ref, o_ref, st_ref):
    @pl.when(pl.program_id(1) == 0)
    def _():
        st_ref[...] = jnp.zeros_like(st_ref)

    tril = _tri(CHUNK)
    tril_f = tril.astype(F32)
    for c in range(REC_BLOCK // CHUNK):
        rows = slice(c * CHUNK, (c + 1) * CHUNK)
        pre = _dot(zg_ref[rows, :].astype(BF16), wg_ref[...]) + bg_ref[...]
        log_a = _log_sigmoid(pre) / GLA_GATE_TAU
        cum = _dot(tril_f, log_a, precision=HI)
        total = cum[CHUNK - 1:CHUNK, :]
        q_dec = (q_ref[rows, :] * GLA_DK ** -0.5) * jnp.exp(cum)
        kk = k_ref[rows, :]
        k_dec = kk * jnp.exp(-cum)
        k_end = kk * jnp.exp(total - cum)
        decay = jnp.exp(total)
        for h in range(GLA_HEADS):
            ks = slice(h * GLA_DK, (h + 1) * GLA_DK)
            vs = slice(h * GLA_DV, (h + 1) * GLA_DV)
            qh = q_dec[:, ks].astype(BF16)
            vh = v_ref[rows, vs]
            att = jnp.where(tril, _dot_nt(qh, k_dec[:, ks].astype(BF16)), 0.0)
            st = st_ref[h]
            o = _dot(att.astype(BF16), vh.astype(BF16)) + _dot_nt(qh, st.astype(BF16))
            st_ref[h] = st * decay[:, ks] + _dot(vh.T.astype(BF16), k_end[:, ks].astype(BF16))
            rh = r_ref[rows, vs]
            o_ref[rows, vs] = ((_rms(o) * ng_ref[...]) * _silu(rh)).astype(o_ref.dtype)


def _gla(p, w_gate, b_gate, norm_g, col):
    bsz, s, _ = p.shape
    t = REC_BLOCK

    def piece(name):
        width, idx = col[name]
        return pl.BlockSpec((None, t, width), lambda b, i, idx=idx: (b, i, idx))

    const = lambda shape: pl.BlockSpec(shape, lambda b, i: (0,) * len(shape))
    hk = GLA_HEADS * GLA_DK
    return pl.pallas_call(
        _gla_kernel,
        grid=(bsz, s // t),
        in_specs=[piece("gla_q"), piece("gla_k"), piece("gla_v"), piece("gla_r"), piece("gates"),
                  const((LANES, hk)), const((1, hk)), const((1, GLA_DV))],
        out_specs=pl.BlockSpec((None, t, GLA_HEADS * GLA_DV), lambda b, i: (b, i, 0)),
        out_shape=jax.ShapeDtypeStruct((bsz, s, GLA_HEADS * GLA_DV), BF16),
        scratch_shapes=[pltpu.VMEM((GLA_HEADS, GLA_DV, GLA_DK), F32)],
        compiler_params=_cparams(("parallel", "arbitrary")),
        name="gla",
    )(p, p, p, p, p, w_gate, b_gate.reshape(1, hk), norm_g.reshape(1, GLA_DV))


def _mlstm_kernel(q_ref, k_ref, v_ref, og_ref, gc_ref, gr_ref, wq_ref, wk_ref, bc_ref, br_ref, ng_ref,
                  o_ref, c_ref, n_ref, m_ref, qx_ref, kx_ref):
    t = REC_BLOCK

    @pl.when(pl.program_id(1) == 0)
    def _():
        c_ref[...] = jnp.zeros_like(c_ref)
        n_ref[...] = jnp.zeros_like(n_ref)
        m_ref[...] = jnp.zeros_like(m_ref)
        qx_ref[0:8, :] = jnp.zeros((8, qx_ref.shape[1]), F32)
        kx_ref[0:8, :] = jnp.zeros((8, kx_ref.shape[1]), F32)

    def conv_silu(x_ref, xx_ref, w_ref):
        xx_ref[8:8 + t, :] = x_ref[...]
        y = xx_ref[8:8 + t, :] * w_ref[ML_CONV - 1:ML_CONV, :]
        for back in range(1, ML_CONV):
            y += xx_ref[8 - back:8 - back + t, :] * w_ref[ML_CONV - 1 - back:ML_CONV - back, :]
        xx_ref[0:8, :] = xx_ref[t:t + 8, :]
        return _silu(y)

    q_all = conv_silu(q_ref, qx_ref, wq_ref) * ML_DQK ** -0.5
    k_all = conv_silu(k_ref, kx_ref, wk_ref)

    tril = _tri(ML_CHUNK)
    tril_f = tril.astype(F32)
    triu_f = _tri(ML_CHUNK, upper=True).astype(F32)
    lane = lax.broadcasted_iota(jnp.int32, (ML_CHUNK, LANES), 1)
    is_f_col = (lane >= ML_HEADS) & (lane < 2 * ML_HEADS)
    sub = lax.broadcasted_iota(jnp.int32, (8, ML_CHUNK), 0)
    is_f_row = sub >= ML_HEADS

    for c in range(t // ML_CHUNK):
        rows = slice(c * ML_CHUNK, (c + 1) * ML_CHUNK)
        gcol = gc_ref[rows, :] + bc_ref[...]
        gcol = jnp.where(is_f_col, _log_sigmoid(gcol), gcol)
        bcol = _dot(tril_f, gcol, precision=HI)
        grow = gr_ref[:, rows] + br_ref[:, 0:1]
        grow = jnp.where(is_f_row, _log_sigmoid(grow), grow)
        brow = _dot(grow, triu_f, precision=HI)
        for h in range(ML_HEADS):
            qs = slice(h * ML_DQK, (h + 1) * ML_DQK)
            vs = slice(h * ML_DV, (h + 1) * ML_DV)
            b_col = bcol[:, ML_HEADS + h:ML_HEADS + h + 1]
            b_row = brow[ML_HEADS + h:ML_HEADS + h + 1, :]
            li_col = gcol[:, h:h + 1]
            li_row = grow[h:h + 1, :]
            m_prev = m_ref[h:h + 1, 0:1]
            log_d = jnp.where(tril, b_col - b_row + li_row, -jnp.inf)
            log_inter = b_col + m_prev
            m_comb = jnp.maximum(log_inter, jnp.max(log_d, axis=-1, keepdims=True))
            d = jnp.exp(log_d - m_comb)
            w_inter = jnp.exp(log_inter - m_comb)
            qh = q_all[rows, qs]
            kh = k_all[rows, qs]
            qb = qh.astype(BF16)
            vb = v_ref[rows, vs].astype(BF16)
            sc = _dot_nt(qb, kh.astype(BF16)) * d
            cm = c_ref[h]
            nv = n_ref[h]
            num = _dot(sc.astype(BF16), vb) + w_inter * _dot(qb, cm.astype(BF16))
            den = jnp.sum(sc, axis=-1, keepdims=True) + w_inter * jnp.sum(qh * nv, axis=-1, keepdims=True)
            hh = num / jnp.maximum(jnp.abs(den), jnp.exp(-m_comb))
            total = b_col[ML_CHUNK - 1:ML_CHUNK, :]
            log_w = total - b_col + li_col
            m_new = jnp.maximum(total + m_prev, jnp.max(log_w, axis=0, keepdims=True))
            decay = jnp.exp(total + m_prev - m_new)
            kw = kh * jnp.exp(log_w - m_new)
            c_ref[h] = decay * cm + _dot(kw.T.astype(BF16), vb)
            n_ref[h] = decay * nv + jnp.sum(kw, axis=0, keepdims=True)
            m_ref[h:h + 1, :] = jnp.broadcast_to(m_new, (1, LANES))
            og = og_ref[rows, vs]
            o_ref[rows, vs] = ((_rms(hh) * ng_ref[...]) * _sigmoid(og)).astype(o_ref.dtype)


def _mlstm(p, gates_row, conv_w, bias_col, bias_row, norm_g, col):
    bsz, s, _ = p.shape
    t = REC_BLOCK
    hq = ML_HEADS * ML_DQK
    hv = ML_HEADS * ML_DV

    def piece(name):
        width, idx = col[name]
        return pl.BlockSpec((None, t, width), lambda b, i, idx=idx: (b, i, idx))

    const = lambda shape: pl.BlockSpec(shape, lambda b, i: (0,) * len(shape))
    return pl.pallas_call(
        _mlstm_kernel,
        grid=(bsz, s // t),
        in_specs=[piece("ml_q"), piece("ml_k"), piece("ml_v"), piece("ml_og"), piece("gates"),
                  pl.BlockSpec((None, 8, t), lambda b, i: (b, 0, i)),
                  pl.BlockSpec((ML_CONV, hq), lambda b, i: (0, 0)),
                  pl.BlockSpec((ML_CONV, hq), lambda b, i: (0, 1)),
                  const((1, LANES)), const((8, LANES)), const((1, ML_DV))],
        out_specs=pl.BlockSpec((None, t, hv), lambda b, i: (b, i, 0)),
        out_shape=jax.ShapeDtypeStruct((bsz, s, hv), BF16),
        scratch_shapes=[pltpu.VMEM((ML_HEADS, ML_DQK, ML_DV), F32),
                        pltpu.VMEM((ML_HEADS, 1, ML_DQK), F32),
                        pltpu.VMEM((8, LANES), F32),
                        pltpu.VMEM((t + 8, hq), F32),
                        pltpu.VMEM((t + 8, hq), F32)],
        compiler_params=_cparams(("parallel", "arbitrary")),
        name="mlstm",
    )(p, p, p, p, p, gates_row, conv_w, conv_w, bias_col, bias_row, norm_g.reshape(1, ML_DV))


def _split_bf16(x):
    hi = x.astype(BF16)
    return hi, (x - hi.astype(F32)).astype(BF16)


def _seg_norm(x, seg_ref, segt_ref):
    if seg_ref.dtype == BF16:
        seg_sum = lambda v, m_ref: sum(_dot(part, m_ref[...]) for part in _split_bf16(v))
    else:
        seg_sum = lambda v, m_ref: _dot(v, m_ref[...], precision=HI)
    ss = seg_sum(x * x, seg_ref) * (1.0 / NSA_DH)
    return x * seg_sum(lax.rsqrt(ss + EPS), segt_ref)


def _nsa_prep_kernel(q_ref, kc_ref, vc_ref, ks_ref, vs_ref, kw_ref, vw_ref, qg_ref, kg_ref, segk_ref, segkt_ref,
                     qnt_ref, kcc_ref, vcc_ref, ksg_ref, vst_ref, kwg_ref, vwt_ref):
    t = q_ref.shape[0]
    qt = q_ref[...].T
    for h in range(NSA_HEADS):
        hs = slice(h * NSA_DH, (h + 1) * NSA_DH)
        blk = qt[hs, :]
        inv = lax.rsqrt(jnp.mean(blk * blk, axis=0, keepdims=True) + EPS)
        qnt_ref[hs, :] = ((blk * inv) * qg_ref[hs, :]).astype(BF16)
    kcc_ref[...] = kc_ref[...]
    vcc_ref[...] = vc_ref[...]
    ones = jnp.ones((NSA_VPAD - NSA_DH, t), BF16)
    for k_ref, v_ref, kg_out, vt_out in ((ks_ref, vs_ref, ksg_ref, vst_ref), (kw_ref, vw_ref, kwg_ref, vwt_ref)):
        kn = (_seg_norm(k_ref[...], segk_ref, segkt_ref) * kg_ref[...]).astype(BF16)
        vt = v_ref[...].T.astype(BF16)
        for g in range(NSA_GROUPS):
            gs = slice(g * NSA_DH, (g + 1) * NSA_DH)
            kg_out[g] = kn[:, gs]
            vt_out[g, 0:NSA_DH, :] = vt[gs, :]
            vt_out[g, NSA_DH:NSA_VPAD, :] = ones


def _seg_matrices(width, dtype):
    seg = (np.arange(width)[:, None] // NSA_DH == np.arange(LANES)[None, :]).astype(np.float32)
    return jnp.asarray(seg, dtype=dtype), jnp.asarray(seg.T, dtype=dtype)


def _nsa_prep(p, q_gain, k_gain, col, t=512):
    bsz, s, _ = p.shape
    hq = NSA_HEADS * NSA_DH
    kv = NSA_GROUPS * NSA_DH

    def piece(name):
        width, idx = col[name]
        return pl.BlockSpec((None, t, width), lambda b, i, idx=idx: (b, i, idx))

    const = lambda shape: pl.BlockSpec(shape, lambda b, i: (0,) * len(shape))
    segk, segkt = _seg_matrices(kv, BF16)
    row = pl.BlockSpec((None, t, kv), lambda b, i: (b, i, 0))
    keys = pl.BlockSpec((None, NSA_GROUPS, t, NSA_DH), lambda b, i: (b, 0, i, 0))
    vals = pl.BlockSpec((None, NSA_GROUPS, NSA_VPAD, t), lambda b, i: (b, 0, 0, i))
    shp = lambda *dims, dt=BF16: jax.ShapeDtypeStruct(dims, dt)
    keys_shape = shp(bsz, NSA_GROUPS, s, NSA_DH)
    vals_shape = shp(bsz, NSA_GROUPS, NSA_VPAD, s)
    return pl.pallas_call(
        _nsa_prep_kernel,
        grid=(bsz, s // t),
        in_specs=[piece("nsa_q"), piece("nsa_kc"), piece("nsa_vc"), piece("nsa_ks"), piece("nsa_vs"),
                  piece("nsa_kw"), piece("nsa_vw"), const((hq, 1)), const((1, kv)),
                  const((kv, LANES)), const((LANES, kv))],
        out_specs=[pl.BlockSpec((None, hq, t), lambda b, i: (b, 0, i)), row, row, keys, vals, keys, vals],
        out_shape=[shp(bsz, hq, s), shp(bsz, s, kv, dt=F32), shp(bsz, s, kv, dt=F32),
                   keys_shape, vals_shape, keys_shape, vals_shape],
        compiler_params=_cparams(("parallel", "parallel")),
        name="nsa_prep",
    )(p, p, p, p, p, p, p, q_gain, k_gain, segk, segkt)


def _compress_kernel(kc_ref, vc_ref, posk_ref, posv_ref, kw1a_ref, kw1b_ref, kw2_ref, vw1a_ref, vw1b_ref,
                     vw2_ref, kg_ref, segk_ref, segkt_ref, kcg_ref, vct_ref):
    def hidden(x_ref, pos_ref, w1a_ref, w1b_ref):
        x = x_ref[...]
        nsub = x.shape[0]
        first = _dot((x + pos_ref[0:1, :]).astype(BF16), w1a_ref[...])
        second = _dot((x + pos_ref[1:2, :]).astype(BF16), w1b_ref[...])
        row = lax.broadcasted_iota(jnp.int32, second.shape, 0)
        nxt = jnp.where(row == nsub - 1, 0.0, pltpu.roll(second, nsub - 1, 0))
        return _silu(first + nxt)

    kc = _dot(hidden(kc_ref, posk_ref, kw1a_ref, kw1b_ref).astype(BF16), kw2_ref[...])
    kn = (_seg_norm(kc, segk_ref, segkt_ref) * kg_ref[...]).astype(BF16)
    vt = _dot(vw2_ref[...], hidden(vc_ref, posv_ref, vw1a_ref, vw1b_ref).T.astype(BF16)).astype(BF16)
    for g in range(NSA_GROUPS):
        gs = slice(g * NSA_DH, (g + 1) * NSA_DH)
        kcg_ref[g] = kn[:, gs]
        vct_ref[g] = vt[gs, :]


def _expand_w1(w1):
    w = w1.reshape(2, CMP_STRIDE, NSA_DH, NSA_DH)
    eye = jnp.eye(NSA_GROUPS, dtype=w1.dtype)
    big = jnp.einsum('hlde,gk->hlgdke', w, eye)
    big = big.reshape(2, CMP_STRIDE * NSA_GROUPS * NSA_DH, NSA_GROUPS * NSA_DH).astype(BF16)
    return big[0], big[1]


def _expand_pos(pos):
    pp = pos.reshape(2, CMP_STRIDE, 1, NSA_DH)
    return jnp.broadcast_to(pp, (2, CMP_STRIDE, NSA_GROUPS, NSA_DH)).reshape(2, -1)


def _block_diag(w2):
    return jnp.kron(jnp.eye(NSA_GROUPS, dtype=w2.dtype), w2).astype(BF16)


def _compress(kc, vc, pos_k, pos_v, ck_w1, ck_w2, cv_w1, cv_w2, k_gain):
    bsz, s, kv = kc.shape
    nsub = s // CMP_STRIDE
    flat = CMP_STRIDE * kv
    segk, segkt = _seg_matrices(kv, F32)
    kw1a, kw1b = _expand_w1(ck_w1)
    vw1a, vw1b = _expand_w1(cv_w1)
    const = lambda shape: pl.BlockSpec(shape, lambda b: (0,) * len(shape))
    tok = pl.BlockSpec((None, nsub, flat), lambda b: (b, 0, 0))
    return pl.pallas_call(
        _compress_kernel,
        grid=(bsz,),
        in_specs=[tok, tok, const((2, flat)), const((2, flat)),
                  const((flat, kv)), const((flat, kv)), const((kv, kv)),
                  const((flat, kv)), const((flat, kv)), const((kv, kv)),
                  const((1, kv)), const((kv, LANES)), const((LANES, kv))],
        out_specs=[pl.BlockSpec((None, NSA_GROUPS, nsub, NSA_DH), lambda b: (b, 0, 0, 0)),
                   pl.BlockSpec((None, NSA_GROUPS, NSA_DH, nsub), lambda b: (b, 0, 0, 0))],
        out_shape=[jax.ShapeDtypeStruct((bsz, NSA_GROUPS, nsub, NSA_DH), BF16),
                   jax.ShapeDtypeStruct((bsz, NSA_GROUPS, NSA_DH, nsub), BF16)],
        compiler_params=_cparams(("parallel",)),
        name="nsa_compress",
    )(kc.reshape(bsz, nsub, flat), vc.reshape(bsz, nsub, flat), _expand_pos(pos_k), _expand_pos(pos_v),
      kw1a, kw1b, _block_diag(ck_w2), vw1a, vw1b, _block_diag(cv_w2).T, k_gain, segk, segkt)


def _nsa_attn_kernel(qt_ref, gt_ref, kcg_ref, vct_ref, ksg_ref, vst_ref, kwg_ref, vwt_ref, ovt_ref,
                     o_ref, acc_ref, sel_ref, s_ref, p_ref, pv_ref):
    tq, tk = NSA_TQ, NSA_TK
    q0 = pl.program_id(1) * tq
    ncmp = kcg_ref.shape[1]
    nsel = ovt_ref.shape[0]
    cols4 = NSA_HPG * tq

    t_cmp = q0 + lax.broadcasted_iota(jnp.int32, (ncmp, cols4), 1) % tq
    n_idx = lax.broadcasted_iota(jnp.int32, (ncmp, cols4), 0)
    cmp_valid = n_idx * CMP_STRIDE + (CMP_LEN - 1) <= t_cmp

    j_idx = lax.broadcasted_iota(jnp.int32, (nsel, tq), 0)
    cur = (q0 + lax.broadcasted_iota(jnp.int32, (nsel, tq), 1)) // SEL_BLOCK
    forced = (j_idx == 0) | (j_idx == cur) | (j_idx == cur - 1)
    causal_blk = j_idx <= cur

    row_blk = lax.broadcasted_iota(jnp.int32, (SEL_BLOCK, tq), 0)
    t_row = q0 + lax.broadcasted_iota(jnp.int32, (1, tq), 1)

    gates_t = _sigmoid(gt_ref[...]).T

    def add_gated(g, branch, o4):
        for i in range(NSA_HPG):
            h = g * NSA_HPG + i
            hs = slice(h * NSA_DH, (h + 1) * NSA_DH)
            gc = GLA_RANK + 3 * h + branch
            term = gates_t[gc:gc + 1, :] * o4[:, i * tq:(i + 1) * tq]
            acc_ref[hs, :] = term if branch == 0 else acc_ref[hs, :] + term

    def q_group(g):
        return jnp.concatenate([qt_ref[(g * NSA_HPG + i) * NSA_DH:(g * NSA_HPG + i + 1) * NSA_DH, :]
                                for i in range(NSA_HPG)], axis=1)

    def attend(branch, g, k_ref, vt_ref, first_tile, n_tiles, mask_fn):
        qt4 = q_group(g)
        last = first_tile + n_tiles - 1

        def scores(kt):
            return _dot(k_ref[g, pl.ds(pl.multiple_of(kt * tk, tk), tk), :], qt4)

        def weighted_values(kt, p):
            return _dot(vt_ref[g, :, pl.ds(pl.multiple_of(kt * tk, tk), tk)], p)

        def step(kt, cur, m):
            nxt = 1 - cur
            s_ref[nxt] = scores(jnp.minimum(kt + 1, last))
            acc = pv_ref[...] + weighted_values(jnp.clip(kt - 1, first_tile, last), p_ref[nxt])
            k0 = jnp.where(kt <= last, kt * tk, PAST_END)
            ok = mask_fn(g, jnp.minimum(kt, last), k0)
            m_new = []
            for c in range(NSA_HPG):
                cs = slice(c * tq, (c + 1) * tq)
                s = jnp.where(ok, s_ref[cur, :, cs], NEG)
                m_c = jnp.maximum(m[:, cs], jnp.max(s, axis=0, keepdims=True))
                p_ref[cur, :, cs] = jnp.exp2(s - m_c).astype(BF16)
                m_new.append(m_c)
            m_new = jnp.concatenate(m_new, axis=1)
            pv_ref[...] = acc * jnp.exp2(m - m_new)
            return m_new

        s_ref[0] = scores(first_tile)
        p_ref[1] = jnp.zeros(p_ref.shape[1:], BF16)
        pv_ref[...] = jnp.zeros_like(pv_ref)
        n_pairs = (n_tiles + 1) // 2

        def body(j, m):
            kt = first_tile + 2 * j
            return step(kt + 1, 1, step(kt, 0, m))

        lax.fori_loop(0, n_pairs, body, jnp.full((1, cols4), NEG, F32))
        final_tile = jnp.minimum(first_tile + 2 * n_pairs - 1, last)
        acc = pv_ref[...] + weighted_values(final_tile, p_ref[1])
        add_gated(g, branch, acc[0:NSA_DH, :] / acc[NSA_DH:NSA_DH + 1, :])

    last_tile = (q0 + tq - 1) // tk
    last_blk = (q0 + tq - 1) // SEL_BLOCK

    def sel_mask(g, kt, k0):
        per_tile = tk // SEL_BLOCK
        blocks = []
        for j in range(per_tile):
            picked = sel_ref[g, pl.ds(kt * per_tile + j, 1), :] > 0.5
            limit = jnp.where(picked, t_row, -1) - (k0 + j * SEL_BLOCK)
            blocks.append(row_blk <= limit)
        return jnp.concatenate(blocks, axis=0)

    def window_branch(g):
        span = WINDOW + tk
        k_start = pl.multiple_of(jnp.maximum(last_tile - WINDOW // tk, 0) * tk, tk)
        s = _dot(kwg_ref[g, pl.ds(k_start, span), :], q_group(g))
        row_w = lax.broadcasted_iota(jnp.int32, (span, tq), 0)
        ahead = t_row - k_start
        ok = (row_w <= ahead) & (row_w > ahead - WINDOW)
        probs = []
        for c in range(NSA_HPG):
            sc = jnp.where(ok, s[:, c * tq:(c + 1) * tq], NEG)
            probs.append(jnp.exp2(sc - jnp.max(sc, axis=0, keepdims=True)).astype(BF16))
        acc = _dot(vwt_ref[g, :, pl.ds(k_start, span)], jnp.concatenate(probs, axis=1))
        add_gated(g, 2, acc[0:NSA_DH, :] / acc[NSA_DH:NSA_DH + 1, :])

    importance = []
    for g in range(NSA_GROUPS):
        lg = jnp.where(cmp_valid, _dot(kcg_ref[g], q_group(g)), NEG)
        ex = jnp.exp2(lg - jnp.max(lg, axis=0, keepdims=True))
        p_cmp = jnp.where(cmp_valid, ex / jnp.sum(ex, axis=0, keepdims=True), 0.0)
        add_gated(g, 0, _dot(vct_ref[g], p_cmp.astype(BF16)))
        p_grp = p_cmp[:, 0:tq] + p_cmp[:, tq:2 * tq] + p_cmp[:, 2 * tq:3 * tq] + p_cmp[:, 3 * tq:4 * tq]
        importance.append(sum(_dot(ovt_ref[...], part) for part in _split_bf16(p_grp)))
        window_branch(g)

    @pl.when(last_blk < N_SELECT)
    def _():
        for g in range(NSA_GROUPS):
            sel_ref[g] = causal_blk.astype(F32)

    @pl.when(last_blk >= N_SELECT)
    def _():
        for g in range(NSA_GROUPS):
            val = jnp.where(forced, jnp.inf, jnp.where(causal_blk, importance[g], -jnp.inf))
            rank = jnp.zeros((nsel, tq), F32)
            for i in range(nsel):
                vi = val[i:i + 1, :]
                rank += ((vi > val) | ((vi == val) & (j_idx > i))).astype(F32)
            sel_ref[g] = (rank < N_SELECT).astype(F32)

    for g in range(NSA_GROUPS):
        attend(1, g, ksg_ref, vst_ref, 0, last_tile + 1, sel_mask)
    o_ref[...] = acc_ref[...].T.astype(o_ref.dtype)


def _overlap_t(s):
    ncmp = s // CMP_STRIDE
    nsel = s // SEL_BLOCK
    c_start = np.arange(ncmp) * CMP_STRIDE
    s_start = np.arange(nsel) * SEL_BLOCK
    overlap_t = ((c_start[None, :] <= s_start[:, None] + SEL_BLOCK - 1)
                 & (c_start[None, :] + CMP_LEN - 1 >= s_start[:, None])).astype(np.float32)
    overlap_t[:, ncmp - 1] = 0.0
    return jnp.asarray(overlap_t, dtype=BF16)


def _nsa_attention(p, qnt, kcg, vct, ksg, vst, kwg, vwt, col):
    bsz, hq, s = qnt.shape
    ncmp = s // CMP_STRIDE
    nsel = s // SEL_BLOCK
    gw, gidx = col["gates"]
    per_row = lambda shape: pl.BlockSpec((None,) + shape, lambda b, i: (b,) + (0,) * len(shape))
    keys = per_row((NSA_GROUPS, s, NSA_DH))
    vals = per_row((NSA_GROUPS, NSA_VPAD, s))
    return pl.pallas_call(
        _nsa_attn_kernel,
        grid=(bsz, s // NSA_TQ),
        in_specs=[pl.BlockSpec((None, hq, NSA_TQ), lambda b, i: (b, 0, i)),
                  pl.BlockSpec((None, NSA_TQ, gw), lambda b, i: (b, i, gidx)),
                  per_row((NSA_GROUPS, ncmp, NSA_DH)), per_row((NSA_GROUPS, NSA_DH, ncmp)),
                  keys, vals, keys, vals,
                  pl.BlockSpec((nsel, ncmp), lambda b, i: (0, 0))],
        out_specs=pl.BlockSpec((None, NSA_TQ, hq), lambda b, i: (b, i, 0)),
        out_shape=jax.ShapeDtypeStruct((bsz, s, hq), BF16),
        scratch_shapes=[pltpu.VMEM((hq, NSA_TQ), F32), pltpu.VMEM((NSA_GROUPS, nsel, NSA_TQ), F32),
                        pltpu.VMEM((2, NSA_TK, NSA_HPG * NSA_TQ), F32),
                        pltpu.VMEM((2, NSA_TK, NSA_HPG * NSA_TQ), BF16),
                        pltpu.VMEM((NSA_VPAD, NSA_HPG * NSA_TQ), F32)],
        compiler_params=_cparams(("parallel", "arbitrary")),
        name="nsa_attention",
    )(qnt, p, kcg, vct, ksg, vst, kwg, vwt, _overlap_t(s))


def _layout(pieces, tile):
    col, off, mats = {}, 0, []
    for name, w in pieces:
        width = w.shape[1]
        assert off % width == 0, (name, off, width)
        col[name] = (width, off // width)
        mats.append(w)
        off += width
    pad = -off % tile
    if pad:
        mats.append(jnp.zeros((mats[0].shape[0], pad), mats[0].dtype))
    return jnp.concatenate(mats, axis=1).astype(BF16), col


def _pad_cols(w, width):
    return jnp.pad(w, ((0, 0), (0, width - w.shape[1])))


AB_TILE = 1920
ML_TILE = 1280


def _ab_layout(w_in):
    hk = GLA_HEADS * GLA_DK
    hv = GLA_HEADS * GLA_DV
    kv = NSA_GROUPS * NSA_DH
    sizes = [hk, hk, hv, hv, GLA_RANK, NSA_HEADS * NSA_DH] + [kv] * 6 + [3 * NSA_HEADS]
    cuts = np.cumsum(sizes)[:-1].tolist()
    gq, gk, gv, gr, gz, nq, kc, vc, ks, vs, kw, vw, gt = jnp.split(w_in, cuts, axis=1)
    gates = _pad_cols(jnp.concatenate([gz, gt], axis=1), LANES)
    return _layout([("gla_q", gq), ("gla_k", gk), ("gla_v", gv), ("gla_r", gr), ("nsa_q", nq),
                    ("nsa_kc", kc), ("nsa_vc", vc), ("nsa_ks", ks), ("nsa_vs", vs), ("nsa_kw", kw),
                    ("nsa_vw", vw), ("gates", gates)], AB_TILE)


def _ml_layout(w_in):
    hq = ML_HEADS * ML_DQK
    hv = ML_HEADS * ML_DV
    cuts = np.cumsum([hq, hq, hv, hv])[:].tolist()
    q, k, v, og, gates = jnp.split(w_in, cuts, axis=1)
    return _layout([("ml_q", q), ("ml_k", k), ("ml_v", v), ("ml_og", og),
                    ("gates", _pad_cols(gates, LANES))], ML_TILE)


def kernel(x, c, norm_mix_g, mod_mix_w, mod_mix_b, norm_mlp_g, mod_mlp_w, mod_mlp_b, mlp_w1, mlp_w2, ab_w_in, ab_w_out, gla_w_gate, gla_b_gate, gla_norm_g, nsa_q_norm_g, nsa_k_norm_g, nsa_cmp_pos_k, nsa_cmp_pos_v, nsa_cmp_k_w1, nsa_cmp_k_w2, nsa_cmp_v_w1, nsa_cmp_v_w2, ml_w_in, ml_w_out, ml_conv_w, ml_b_i, ml_b_f, ml_norm_g):
    bsz, s, d = x.shape
    depth = norm_mix_g.shape[0]
    mod_mix = _ada_mod(c, mod_mix_w, mod_mix_b)
    mod_mlp = _ada_mod(c, mod_mlp_w, mod_mlp_b)

    def split_mod(m):
        return [m[:, None, i * d:(i + 1) * d] for i in range(3)]

    for layer in range(depth):
        shift, scale, gate = split_mod(mod_mix[layer])
        if layer % 2 == 0:
            e = layer // 2
            w_in, col = _ab_layout(ab_w_in[e])
            p = _norm_proj(x, norm_mix_g[layer], scale, shift, w_in, tm=512, tn=AB_TILE)
            w_gate = jnp.pad(gla_w_gate[e], ((0, LANES - GLA_RANK), (0, 0))).astype(BF16)
            o_gla = _gla(p, w_gate, gla_b_gate[e], gla_norm_g[e], col)
            q_gain = (jnp.tile(nsa_q_norm_g[e], NSA_HEADS) * (NSA_DH ** -0.5 * LOG2E)).reshape(-1, 1)
            k_gain = jnp.tile(nsa_k_norm_g[e], NSA_GROUPS).reshape(1, -1)
            qnt, kcc, vcc, ksg, vst, kwg, vwt = _nsa_prep(p, q_gain, k_gain, col)
            kcg, vct = _compress(kcc, vcc, nsa_cmp_pos_k[e], nsa_cmp_pos_v[e], nsa_cmp_k_w1[e],
                                 nsa_cmp_k_w2[e], nsa_cmp_v_w1[e], nsa_cmp_v_w2[e], k_gain)
            o_nsa = _nsa_attention(p, qnt, kcg, vct, ksg, vst, kwg, vwt, col)
            w_out = ab_w_out[e].astype(BF16)
            half = GLA_HEADS * GLA_DV
            x = _out_proj([o_gla, o_nsa], [w_out[:half], w_out[half:]], x, gate)
        else:
            o = layer // 2
            w_in, col = _ml_layout(ml_w_in[o])
            p = _norm_proj(x, norm_mix_g[layer], scale, shift, w_in, tm=512, tn=ML_TILE)
            gw, gidx = col["gates"]
            gates_row = jnp.swapaxes(p[:, :, gidx * gw:gidx * gw + 2 * ML_HEADS], 1, 2)
            bias = jnp.concatenate([ml_b_i[o], ml_b_f[o]])
            bias_col = _pad_cols(bias.reshape(1, -1), LANES)
            bias_row = jnp.broadcast_to(bias.reshape(-1, 1), (2 * ML_HEADS, LANES))
            hh = _mlstm(p, gates_row, ml_conv_w[o], bias_col, bias_row, ml_norm_g[o], col)
            x = _out_proj([hh], [ml_w_out[o].astype(BF16)], x, gate)
        shift, scale, gate = split_mod(mod_mlp[layer])
        x = _mlp_sublayer(x, norm_mlp_g[layer], scale, shift, gate,
                          mlp_w1[layer].astype(BF16), mlp_w2[layer].astype(BF16))
    return x
```

```python
import functools

import numpy as np
import jax
import jax.numpy as jnp
from jax import lax
from jax.experimental import pallas as pl
from jax.experimental.pallas import tpu as pltpu

F32 = jnp.float32
BF16 = jnp.bfloat16
HI = lax.Precision.HIGHEST

EPS = 1e-6
NEG = -1e30
LOG2E = 1.4426950408889634

VMEM_LIMIT_BYTES = 56 * 1024 * 1024
LANES = 128

GLA_HEADS = 4
GLA_DK = 128
GLA_DV = 256
GLA_RANK = 16
GLA_GATE_TAU = 16.0
CHUNK = 64
ML_CHUNK = 128
REC_BLOCK = 256

NSA_HEADS = 16
NSA_GROUPS = 4
NSA_HPG = 4
NSA_DH = 64
CMP_LEN = 32
CMP_STRIDE = 16
SEL_BLOCK = 64
N_SELECT = 16
WINDOW = 512
NSA_TQ = 256
NSA_TK = 256
NSA_VPAD = 80
PAST_END = 1 << 24

ML_HEADS = 4
ML_DQK = 256
ML_DV = 512
ML_CONV = 4


def _cparams(sem):
    return pltpu.CompilerParams(dimension_semantics=sem, vmem_limit_bytes=VMEM_LIMIT_BYTES)


def _sigmoid(x):
    return 1.0 / (1.0 + jnp.exp(-x))


def _silu(x):
    return x * _sigmoid(x)


def _log_sigmoid(x):
    return jnp.minimum(x, 0.0) - jnp.log(1.0 + jnp.exp(-jnp.abs(x)))


def _dot(a, b, precision=None):
    return jnp.dot(a, b, preferred_element_type=F32, precision=precision)


def _dot_nt(a, b, precision=None):
    return lax.dot_general(a, b, (((1,), (1,)), ((), ())), preferred_element_type=F32, precision=precision)


def _rms(x):
    return x * lax.rsqrt(jnp.mean(x * x, axis=-1, keepdims=True) + EPS)


def _mod_kernel(c_ref, w_ref, b_ref, o_ref):
    a = _silu(c_ref[...]).astype(BF16)
    o_ref[...] = _dot(a, w_ref[...].astype(BF16)) + b_ref[...]


def _ada_mod(c, w, b):
    nl, d, d3 = w.shape
    bsz = c.shape[0]
    tn = 768
    return pl.pallas_call(
        _mod_kernel,
        grid=(nl, d3 // tn),
        in_specs=[pl.BlockSpec((bsz, d), lambda l, j: (0, 0)),
                  pl.BlockSpec((None, d, tn), lambda l, j: (l, 0, j)),
                  pl.BlockSpec((None, 1, tn), lambda l, j: (l, 0, j))],
        out_specs=pl.BlockSpec((None, bsz, tn), lambda l, j: (l, 0, j)),
        out_shape=jax.ShapeDtypeStruct((nl, bsz, d3), F32),
        compiler_params=_cparams(("parallel", "parallel")),
        name="ada_mod",
    )(c, w, b.reshape(nl, 1, d3))


def _modulated_norm(x_ref, g_ref, sc_ref, sh_ref):
    return ((_rms(x_ref[...]) * g_ref[...]) * (1.0 + sc_ref[...]) + sh_ref[...]).astype(BF16)


def _proj_kernel(x_ref, g_ref, sc_ref, sh_ref, w_ref, o_ref, h_ref):
    i = pl.program_id(2)

    @pl.when(pl.program_id(1) == 0)
    def _():
        h_ref[i] = _modulated_norm(x_ref, g_ref, sc_ref, sh_ref)

    o_ref[...] = _dot(h_ref[i], w_ref[...])


def _norm_proj(x, g, scale, shift, w, tm, tn):
    bsz, s, d = x.shape
    n = w.shape[1]
    rows = s // tm
    x_spec = pl.BlockSpec((None, tm, d), lambda b, j, i: (b, jnp.where(j == 0, i, rows - 1), 0))
    return pl.pallas_call(
        _proj_kernel,
        grid=(bsz, n // tn, rows),
        in_specs=[x_spec,
                  pl.BlockSpec((1, d), lambda b, j, i: (0, 0)),
                  pl.BlockSpec((None, 1, d), lambda b, j, i: (b, 0, 0)),
                  pl.BlockSpec((None, 1, d), lambda b, j, i: (b, 0, 0)),
                  pl.BlockSpec((d, tn), lambda b, j, i: (0, j))],
        out_specs=pl.BlockSpec((None, tm, tn), lambda b, j, i: (b, i, j)),
        out_shape=jax.ShapeDtypeStruct((bsz, s, n), F32),
        scratch_shapes=[pltpu.VMEM((rows, tm, d), BF16)],
        compiler_params=_cparams(("parallel", "arbitrary", "arbitrary")),
        name="norm_proj",
    )(x, g.reshape(1, d), scale, shift, w)


def _mlp_kernel(x_ref, g_ref, sc_ref, sh_ref, gate_ref, w1_ref, w2_ref, o_ref, h_ref, acc_ref):
    f = pl.program_id(2)

    @pl.when(f == 0)
    def _():
        h_ref[...] = _modulated_norm(x_ref, g_ref, sc_ref, sh_ref)
        acc_ref[...] = jnp.zeros_like(acc_ref)

    u = jnp.maximum(_dot(h_ref[...], w1_ref[...]), 0.0)
    acc_ref[...] += _dot((u * u).astype(BF16), w2_ref[...])

    @pl.when(f == pl.num_programs(2) - 1)
    def _():
        o_ref[...] = x_ref[...] + gate_ref[...] * acc_ref[...]


def _mlp_sublayer(x, g, scale, shift, gate, w1, w2, tm=512, tf=1024):
    bsz, s, d = x.shape
    ff = w1.shape[1]
    vec = pl.BlockSpec((None, 1, d), lambda b, i, f: (b, 0, 0))
    return pl.pallas_call(
        _mlp_kernel,
        grid=(bsz, s // tm, ff // tf),
        in_specs=[pl.BlockSpec((None, tm, d), lambda b, i, f: (b, i, 0)),
                  pl.BlockSpec((1, d), lambda b, i, f: (0, 0)),
                  vec, vec, vec,
                  pl.BlockSpec((d, tf), lambda b, i, f: (0, f)),
                  pl.BlockSpec((tf, d), lambda b, i, f: (f, 0))],
        out_specs=pl.BlockSpec((None, tm, d), lambda b, i, f: (b, i, 0)),
        out_shape=jax.ShapeDtypeStruct((bsz, s, d), F32),
        scratch_shapes=[pltpu.VMEM((tm, d), BF16), pltpu.VMEM((tm, d), F32)],
        compiler_params=_cparams(("parallel", "parallel", "arbitrary")),
        name="mlp_sublayer",
    )(x, g.reshape(1, d), scale, shift, gate, w1, w2)


def _outproj_kernel(n_in, *refs):
    a_refs, w_refs = refs[:n_in], refs[n_in:2 * n_in]
    x_ref, gate_ref, o_ref = refs[2 * n_in:]
    y = _dot(a_refs[0][...], w_refs[0][...])
    for a_ref, w_ref in zip(a_refs[1:], w_refs[1:]):
        y += _dot(a_ref[...], w_ref[...])
    o_ref[...] = x_ref[...] + gate_ref[...] * y


def _out_proj(acts, ws, x, gate, tm=512):
    bsz, s, d = x.shape
    n_in = len(acts)
    in_specs = [pl.BlockSpec((None, tm, a.shape[-1]), lambda b, i: (b, i, 0)) for a in acts]
    in_specs += [pl.BlockSpec(w.shape, lambda b, i: (0, 0)) for w in ws]
    in_specs += [pl.BlockSpec((None, tm, d), lambda b, i: (b, i, 0)),
                 pl.BlockSpec((None, 1, d), lambda b, i: (b, 0, 0))]
    return pl.pallas_call(
        functools.partial(_outproj_kernel, n_in),
        grid=(bsz, s // tm),
        in_specs=in_specs,
        out_specs=pl.BlockSpec((None, tm, d), lambda b, i: (b, i, 0)),
        out_shape=jax.ShapeDtypeStruct((bsz, s, d), F32),
        compiler_params=_cparams(("parallel", "parallel")),
        name="out_proj",
    )(*acts, *ws, x, gate)


def _tri(n, upper=False):
    r = lax.broadcasted_iota(jnp.int32, (n, n), 0)
    c = lax.broadcasted_iota(jnp.int32, (n, n), 1)
    return (c >= r) if upper else (c <= r)


def _gla_kernel(q_ref, k_ref, v_ref, r_ref, zg_ref, wg_ref, bg_ref, ng_ref, o_ref, st_ref):
    @pl.when(pl.program_id(1) == 0)
    def _():
        st_ref[...] = jnp.zeros_like(st_ref)

    tril = _tri(CHUNK)
    tril_f = tril.astype(F32)
    for c in range(REC_BLOCK // CHUNK):
        rows = slice(c * CHUNK, (c + 1) * CHUNK)
        pre = _dot(zg_ref[rows, :].astype(BF16), wg_ref[...]) + bg_ref[...]
        log_a = _log_sigmoid(pre) / GLA_GATE_TAU
        cum = _dot(tril_f, log_a, precision=HI)
        total = cum[CHUNK - 1:CHUNK, :]
        q_dec = (q_ref[rows, :] * GLA_DK ** -0.5) * jnp.exp(cum)
        kk = k_ref[rows, :]
        k_dec = kk * jnp.exp(-cum)
        k_end = kk * jnp.exp(total - cum)
        decay = jnp.exp(total)
        for h in range(GLA_HEADS):
            ks = slice(h * GLA_DK, (h + 1) * GLA_DK)
            vs = slice(h * GLA_DV, (h + 1) * GLA_DV)
            qh = q_dec[:, ks].astype(BF16)
            vh = v_ref[rows, vs]
            att = jnp.where(tril, _dot_nt(qh, k_dec[:, ks].astype(BF16)), 0.0)
            st = st_ref[h]
            o = _dot(att.astype(BF16), vh.astype(BF16)) + _dot_nt(qh, st.astype(BF16))
            st_ref[h] = st * decay[:, ks] + _dot(vh.T.astype(BF16), k_end[:, ks].astype(BF16))
            rh = r_ref[rows, vs]
            o_ref[rows, vs] = ((_rms(o) * ng_ref[...]) * _silu(rh)).astype(o_ref.dtype)


def _gla(p, w_gate, b_gate, norm_g, col):
    bsz, s, _ = p.shape
    t = REC_BLOCK

    def piece(name):
        width, idx = col[name]
        return pl.BlockSpec((None, t, width), lambda b, i, idx=idx: (b, i, idx))

    const = lambda shape: pl.BlockSpec(shape, lambda b, i: (0,) * len(shape))
    hk = GLA_HEADS * GLA_DK
    return pl.pallas_call(
        _gla_kernel,
        grid=(bsz, s // t),
        in_specs=[piece("gla_q"), piece("gla_k"), piece("gla_v"), piece("gla_r"), piece("gates"),
                  const((LANES, hk)), const((1, hk)), const((1, GLA_DV))],
        out_specs=pl.BlockSpec((None, t, GLA_HEADS * GLA_DV), lambda b, i: (b, i, 0)),
        out_shape=jax.ShapeDtypeStruct((bsz, s, GLA_HEADS * GLA_DV), BF16),
        scratch_shapes=[pltpu.VMEM((GLA_HEADS, GLA_DV, GLA_DK), F32)],
        compiler_params=_cparams(("parallel", "arbitrary")),
        name="gla",
    )(p, p, p, p, p, w_gate, b_gate.reshape(1, hk), norm_g.reshape(1, GLA_DV))


def _mlstm_kernel(q_ref, k_ref, v_ref, og_ref, gc_ref, wq_ref, wk_ref, bc_ref, br_ref, ng_ref,
                  o_ref, c_ref, n_ref, m_ref, qx_ref, kx_ref):
    t = REC_BLOCK

    @pl.when(pl.program_id(1) == 0)
    def _():
        c_ref[...] = jnp.zeros_like(c_ref)
        n_ref[...] = jnp.zeros_like(n_ref)
        m_ref[...] = jnp.zeros_like(m_ref)
        qx_ref[0:8, :] = jnp.zeros((8, qx_ref.shape[1]), F32)
        kx_ref[0:8, :] = jnp.zeros((8, kx_ref.shape[1]), F32)

    def conv_silu(x_ref, xx_ref, w_ref):
        xx_ref[8:8 + t, :] = x_ref[...]
        y = xx_ref[8:8 + t, :] * w_ref[ML_CONV - 1:ML_CONV, :]
        for back in range(1, ML_CONV):
            y += xx_ref[8 - back:8 - back + t, :] * w_ref[ML_CONV - 1 - back:ML_CONV - back, :]
        xx_ref[0:8, :] = xx_ref[t:t + 8, :]
        return _silu(y)

    q_all = conv_silu(q_ref, qx_ref, wq_ref) * ML_DQK ** -0.5
    k_all = conv_silu(k_ref, kx_ref, wk_ref)

    tril = _tri(ML_CHUNK)
    tril_f = tril.astype(F32)
    triu_f = _tri(ML_CHUNK, upper=True).astype(F32)
    lane = lax.broadcasted_iota(jnp.int32, (ML_CHUNK, LANES), 1)
    is_f_col = (lane >= ML_HEADS) & (lane < 2 * ML_HEADS)
    sub = lax.broadcasted_iota(jnp.int32, (8, ML_CHUNK), 0)
    is_f_row = sub >= ML_HEADS
    gates_by_row = gc_ref[...].T[0:2 * ML_HEADS, :]

    for c in range(t // ML_CHUNK):
        rows = slice(c * ML_CHUNK, (c + 1) * ML_CHUNK)
        gcol = gc_ref[rows, :] + bc_ref[...]
        gcol = jnp.where(is_f_col, _log_sigmoid(gcol), gcol)
        bcol = _dot(tril_f, gcol, precision=HI)
        grow = gates_by_row[:, rows] + br_ref[:, 0:1]
        grow = jnp.where(is_f_row, _log_sigmoid(grow), grow)
        brow = _dot(grow, triu_f, precision=HI)
        for h in range(ML_HEADS):
            qs = slice(h * ML_DQK, (h + 1) * ML_DQK)
            vs = slice(h * ML_DV, (h + 1) * ML_DV)
            b_col = bcol[:, ML_HEADS + h:ML_HEADS + h + 1]
            b_row = brow[ML_HEADS + h:ML_HEADS + h + 1, :]
            li_col = gcol[:, h:h + 1]
            li_row = grow[h:h + 1, :]
            m_prev = m_ref[h:h + 1, 0:1]
            log_d = jnp.where(tril, b_col - b_row + li_row, -jnp.inf)
            log_inter = b_col + m_prev
            m_comb = jnp.maximum(log_inter, jnp.max(log_d, axis=-1, keepdims=True))
            d = jnp.exp(log_d - m_comb)
            w_inter = jnp.exp(log_inter - m_comb)
            qh = q_all[rows, qs]
            kh = k_all[rows, qs]
            qb = qh.astype(BF16)
            vb = v_ref[rows, vs].astype(BF16)
            sc = _dot_nt(qb, kh.astype(BF16)) * d
            cm = c_ref[h]
            nv = n_ref[h]
            num = _dot(sc.astype(BF16), vb) + w_inter * _dot(qb, cm.astype(BF16))
            den = jnp.sum(sc, axis=-1, keepdims=True) + w_inter * jnp.sum(qh * nv, axis=-1, keepdims=True)
            hh = num / jnp.maximum(jnp.abs(den), jnp.exp(-m_comb))
            total = b_col[ML_CHUNK - 1:ML_CHUNK, :]
            log_w = total - b_col + li_col
            m_new = jnp.maximum(total + m_prev, jnp.max(log_w, axis=0, keepdims=True))
            decay = jnp.exp(total + m_prev - m_new)
            kw = kh * jnp.exp(log_w - m_new)
            c_ref[h] = decay * cm + _dot(kw.T.astype(BF16), vb)
            n_ref[h] = decay * nv + jnp.sum(kw, axis=0, keepdims=True)
            m_ref[h:h + 1, :] = jnp.broadcast_to(m_new, (1, LANES))
            og = og_ref[rows, vs]
            o_ref[rows, vs] = ((_rms(hh) * ng_ref[...]) * _sigmoid(og)).astype(o_ref.dtype)


def _mlstm(p, conv_w, bias_col, bias_row, norm_g, col):
    bsz, s, _ = p.shape
    t = REC_BLOCK
    hq = ML_HEADS * ML_DQK
    hv = ML_HEADS * ML_DV

    def piece(name):
        width, idx = col[name]
        return pl.BlockSpec((None, t, width), lambda b, i, idx=idx: (b, i, idx))

    const = lambda shape: pl.BlockSpec(shape, lambda b, i: (0,) * len(shape))
    return pl.pallas_call(
        _mlstm_kernel,
        grid=(bsz, s // t),
        in_specs=[piece("ml_q"), piece("ml_k"), piece("ml_v"), piece("ml_og"), piece("gates"),
                  pl.BlockSpec((ML_CONV, hq), lambda b, i: (0, 0)),
                  pl.BlockSpec((ML_CONV, hq), lambda b, i: (0, 1)),
                  const((1, LANES)), const((8, LANES)), const((1, ML_DV))],
        out_specs=pl.BlockSpec((None, t, hv), lambda b, i: (b, i, 0)),
        out_shape=jax.ShapeDtypeStruct((bsz, s, hv), BF16),
        scratch_shapes=[pltpu.VMEM((ML_HEADS, ML_DQK, ML_DV), F32),
                        pltpu.VMEM((ML_HEADS, 1, ML_DQK), F32),
                        pltpu.VMEM((8, LANES), F32),
                        pltpu.VMEM((t + 8, hq), F32),
                        pltpu.VMEM((t + 8, hq), F32)],
        compiler_params=_cparams(("parallel", "arbitrary")),
        name="mlstm",
    )(p, p, p, p, p, conv_w, conv_w, bias_col, bias_row, norm_g.reshape(1, ML_DV))


def _split_bf16(x):
    hi = x.astype(BF16)
    return hi, (x - hi.astype(F32)).astype(BF16)


def _seg_norm(x, seg_ref, segt_ref):
    if seg_ref.dtype == BF16:
        seg_sum = lambda v, m_ref: sum(_dot(part, m_ref[...]) for part in _split_bf16(v))
    else:
        seg_sum = lambda v, m_ref: _dot(v, m_ref[...], precision=HI)
    ss = seg_sum(x * x, seg_ref) * (1.0 / NSA_DH)
    return x * seg_sum(lax.rsqrt(ss + EPS), segt_ref)


def _nsa_prep_kernel(q_ref, kc_ref, vc_ref, ks_ref, vs_ref, kw_ref, vw_ref, qg_ref, kg_ref, segk_ref, segkt_ref,
                     qnt_ref, kcc_ref, vcc_ref, ksg_ref, vst_ref, kwg_ref, vwt_ref):
    t = q_ref.shape[0]
    qt = q_ref[...].T
    for h in range(NSA_HEADS):
        hs = slice(h * NSA_DH, (h + 1) * NSA_DH)
        blk = qt[hs, :]
        inv = lax.rsqrt(jnp.mean(blk * blk, axis=0, keepdims=True) + EPS)
        qnt_ref[hs, :] = ((blk * inv) * qg_ref[hs, :]).astype(BF16)
    kcc_ref[...] = kc_ref[...]
    vcc_ref[...] = vc_ref[...]
    ones = jnp.ones((NSA_VPAD - NSA_DH, t), BF16)
    for k_ref, v_ref, kg_out, vt_out in ((ks_ref, vs_ref, ksg_ref, vst_ref), (kw_ref, vw_ref, kwg_ref, vwt_ref)):
        kn = (_seg_norm(k_ref[...], segk_ref, segkt_ref) * kg_ref[...]).astype(BF16)
        vt = v_ref[...].T.astype(BF16)
        for g in range(NSA_GROUPS):
            gs = slice(g * NSA_DH, (g + 1) * NSA_DH)
            kg_out[g] = kn[:, gs]
            vt_out[g, 0:NSA_DH, :] = vt[gs, :]
            vt_out[g, NSA_DH:NSA_VPAD, :] = ones


def _seg_matrices(width, dtype):
    seg = (np.arange(width)[:, None] // NSA_DH == np.arange(LANES)[None, :]).astype(np.float32)
    return jnp.asarray(seg, dtype=dtype), jnp.asarray(seg.T, dtype=dtype)


def _nsa_prep(p, q_gain, k_gain, col, t=512):
    bsz, s, _ = p.shape
    hq = NSA_HEADS * NSA_DH
    kv = NSA_GROUPS * NSA_DH

    def piece(name):
        width, idx = col[name]
        return pl.BlockSpec((None, t, width), lambda b, i, idx=idx: (b, i, idx))

    const = lambda shape: pl.BlockSpec(shape, lambda b, i: (0,) * len(shape))
    segk, segkt = _seg_matrices(kv, BF16)
    row = pl.BlockSpec((None, t, kv), lambda b, i: (b, i, 0))
    keys = pl.BlockSpec((None, NSA_GROUPS, t, NSA_DH), lambda b, i: (b, 0, i, 0))
    vals = pl.BlockSpec((None, NSA_GROUPS, NSA_VPAD, t), lambda b, i: (b, 0, 0, i))
    shp = lambda *dims, dt=BF16: jax.ShapeDtypeStruct(dims, dt)
    keys_shape = shp(bsz, NSA_GROUPS, s, NSA_DH)
    vals_shape = shp(bsz, NSA_GROUPS, NSA_VPAD, s)
    return pl.pallas_call(
        _nsa_prep_kernel,
        grid=(bsz, s // t),
        in_specs=[piece("nsa_q"), piece("nsa_kc"), piece("nsa_vc"), piece("nsa_ks"), piece("nsa_vs"),
                  piece("nsa_kw"), piece("nsa_vw"), const((hq, 1)), const((1, kv)),
                  const((kv, LANES)), const((LANES, kv))],
        out_specs=[pl.BlockSpec((None, hq, t), lambda b, i: (b, 0, i)), row, row, keys, vals, keys, vals],
        out_shape=[shp(bsz, hq, s), shp(bsz, s, kv, dt=F32), shp(bsz, s, kv, dt=F32),
                   keys_shape, vals_shape, keys_shape, vals_shape],
        compiler_params=_cparams(("parallel", "parallel")),
        name="nsa_prep",
    )(p, p, p, p, p, p, p, q_gain, k_gain, segk, segkt)


def _compress_kernel(kc_ref, vc_ref, posk_ref, posv_ref, kw1a_ref, kw1b_ref, kw2_ref, vw1a_ref, vw1b_ref,
                     vw2_ref, kg_ref, segk_ref, segkt_ref, kcg_ref, vct_ref):
    def hidden(x_ref, pos_ref, w1a_ref, w1b_ref):
        x = x_ref[...]
        nsub = x.shape[0]
        first = _dot((x + pos_ref[0:1, :]).astype(BF16), w1a_ref[...])
        second = _dot((x + pos_ref[1:2, :]).astype(BF16), w1b_ref[...])
        row = lax.broadcasted_iota(jnp.int32, second.shape, 0)
        nxt = jnp.where(row == nsub - 1, 0.0, pltpu.roll(second, nsub - 1, 0))
        return _silu(first + nxt)

    kc = _dot(hidden(kc_ref, posk_ref, kw1a_ref, kw1b_ref).astype(BF16), kw2_ref[...])
    kn = (_seg_norm(kc, segk_ref, segkt_ref) * kg_ref[...]).astype(BF16)
    vt = _dot(vw2_ref[...], hidden(vc_ref, posv_ref, vw1a_ref, vw1b_ref).T.astype(BF16)).astype(BF16)
    for g in range(NSA_GROUPS):
        gs = slice(g * NSA_DH, (g + 1) * NSA_DH)
        kcg_ref[g] = kn[:, gs]
        vct_ref[g] = vt[gs, :]


def _expand_w1(w1):
    w = w1.reshape(2, CMP_STRIDE, NSA_DH, NSA_DH)
    eye = jnp.eye(NSA_GROUPS, dtype=w1.dtype)
    big = jnp.einsum('hlde,gk->hlgdke', w, eye)
    big = big.reshape(2, CMP_STRIDE * NSA_GROUPS * NSA_DH, NSA_GROUPS * NSA_DH).astype(BF16)
    return big[0], big[1]


def _expand_pos(pos):
    pp = pos.reshape(2, CMP_STRIDE, 1, NSA_DH)
    return jnp.broadcast_to(pp, (2, CMP_STRIDE, NSA_GROUPS, NSA_DH)).reshape(2, -1)


def _block_diag(w2):
    return jnp.kron(jnp.eye(NSA_GROUPS, dtype=w2.dtype), w2).astype(BF16)


def _compress(kc, vc, pos_k, pos_v, ck_w1, ck_w2, cv_w1, cv_w2, k_gain):
    bsz, s, kv = kc.shape
    nsub = s // CMP_STRIDE
    flat = CMP_STRIDE * kv
    segk, segkt = _seg_matrices(kv, F32)
    kw1a, kw1b = _expand_w1(ck_w1)
    vw1a, vw1b = _expand_w1(cv_w1)
    const = lambda shape: pl.BlockSpec(shape, lambda b: (0,) * len(shape))
    tok = pl.BlockSpec((None, nsub, flat), lambda b: (b, 0, 0))
    return pl.pallas_call(
        _compress_kernel,
        grid=(bsz,),
        in_specs=[tok, tok, const((2, flat)), const((2, flat)),
                  const((flat, kv)), const((flat, kv)), const((kv, kv)),
                  const((flat, kv)), const((flat, kv)), const((kv, kv)),
                  const((1, kv)), const((kv, LANES)), const((LANES, kv))],
        out_specs=[pl.BlockSpec((None, NSA_GROUPS, nsub, NSA_DH), lambda b: (b, 0, 0, 0)),
                   pl.BlockSpec((None, NSA_GROUPS, NSA_DH, nsub), lambda b: (b, 0, 0, 0))],
        out_shape=[jax.ShapeDtypeStruct((bsz, NSA_GROUPS, nsub, NSA_DH), BF16),
                   jax.ShapeDtypeStruct((bsz, NSA_GROUPS, NSA_DH, nsub), BF16)],
        compiler_params=_cparams(("parallel",)),
        name="nsa_compress",
    )(kc.reshape(bsz, nsub, flat), vc.reshape(bsz, nsub, flat), _expand_pos(pos_k), _expand_pos(pos_v),
      kw1a, kw1b, _block_diag(ck_w2), vw1a, vw1b, _block_diag(cv_w2).T, k_gain, segk, segkt)


def _nsa_attn_kernel(qt_ref, gt_ref, kcg_ref, vct_ref, ksg_ref, vst_ref, kwg_ref, vwt_ref, ovt_ref,
                     o_ref, acc_ref, sel_ref, s_ref, p_ref, pv_ref):
    tq, tk = NSA_TQ, NSA_TK
    q0 = pl.program_id(1) * tq
    ncmp = kcg_ref.shape[1]
    nsel = ovt_ref.shape[0]
    cols4 = NSA_HPG * tq

    t_cmp = q0 + lax.broadcasted_iota(jnp.int32, (ncmp, cols4), 1) % tq
    n_idx = lax.broadcasted_iota(jnp.int32, (ncmp, cols4), 0)
    cmp_valid = n_idx * CMP_STRIDE + (CMP_LEN - 1) <= t_cmp

    j_idx = lax.broadcasted_iota(jnp.int32, (nsel, tq), 0)
    cur = (q0 + lax.broadcasted_iota(jnp.int32, (nsel, tq), 1)) // SEL_BLOCK
    forced = (j_idx == 0) | (j_idx == cur) | (j_idx == cur - 1)
    causal_blk = j_idx <= cur

    row_blk = lax.broadcasted_iota(jnp.int32, (SEL_BLOCK, tq), 0)
    t_row = q0 + lax.broadcasted_iota(jnp.int32, (1, tq), 1)

    gates_t = _sigmoid(gt_ref[...]).T

    def add_gated(g, branch, o4):
        for i in range(NSA_HPG):
            h = g * NSA_HPG + i
            hs = slice(h * NSA_DH, (h + 1) * NSA_DH)
            gc = GLA_RANK + 3 * h + branch
            term = gates_t[gc:gc + 1, :] * o4[:, i * tq:(i + 1) * tq]
            acc_ref[hs, :] = term if branch == 0 else acc_ref[hs, :] + term

    def q_group(g):
        return jnp.concatenate([qt_ref[(g * NSA_HPG + i) * NSA_DH:(g * NSA_HPG + i + 1) * NSA_DH, :]
                                for i in range(NSA_HPG)], axis=1)

    def attend(branch, g, k_ref, vt_ref, first_tile, n_tiles, mask_fn):
        qt4 = q_group(g)
        last = first_tile + n_tiles - 1

        def scores(kt):
            return _dot(k_ref[g, pl.ds(pl.multiple_of(kt * tk, tk), tk), :], qt4)

        def weighted_values(kt, p):
            return _dot(vt_ref[g, :, pl.ds(pl.multiple_of(kt * tk, tk), tk)], p)

        def step(kt, cur, m):
            nxt = 1 - cur
            s_ref[nxt] = scores(jnp.minimum(kt + 1, last))
            acc = pv_ref[...] + weighted_values(jnp.clip(kt - 1, first_tile, last), p_ref[nxt])
            k0 = jnp.where(kt <= last, kt * tk, PAST_END)
            ok = mask_fn(g, jnp.minimum(kt, last), k0)
            m_new = []
            for c in range(cols4 // LANES):
                cs = slice(c * LANES, (c + 1) * LANES)
                qs = slice((c % (tq // LANES)) * LANES, (c % (tq // LANES) + 1) * LANES)
                s = jnp.where(ok[:, qs], s_ref[cur, :, cs], NEG)
                m_c = jnp.maximum(m[:, cs], jnp.max(s, axis=0, keepdims=True))
                p_ref[cur, :, cs] = jnp.exp2(s - m_c).astype(BF16)
                m_new.append(m_c)
            m_new = jnp.concatenate(m_new, axis=1)
            pv_ref[...] = acc * jnp.exp2(m - m_new)
            return m_new

        s_ref[0] = scores(first_tile)
        p_ref[1] = jnp.zeros(p_ref.shape[1:], BF16)
        pv_ref[...] = jnp.zeros_like(pv_ref)
        n_pairs = (n_tiles + 1) // 2

        def body(j, m):
            kt = first_tile + 2 * j
            return step(kt + 1, 1, step(kt, 0, m))

        lax.fori_loop(0, n_pairs, body, jnp.full((1, cols4), NEG, F32))
        final_tile = jnp.minimum(first_tile + 2 * n_pairs - 1, last)
        acc = pv_ref[...] + weighted_values(final_tile, p_ref[1])
        add_gated(g, branch, acc[0:NSA_DH, :] / acc[NSA_DH:NSA_DH + 1, :])

    last_tile = (q0 + tq - 1) // tk
    last_blk = (q0 + tq - 1) // SEL_BLOCK

    def sel_mask(g, kt, k0):
        per_tile = tk // SEL_BLOCK
        blocks = []
        for j in range(per_tile):
            picked = sel_ref[g, pl.ds(kt * per_tile + j, 1), :] > 0.5
            limit = jnp.where(picked, t_row, -1) - (k0 + j * SEL_BLOCK)
            blocks.append(row_blk <= limit)
        return jnp.concatenate(blocks, axis=0)

    def window_branch(g):
        span = WINDOW + tk
        k_start = pl.multiple_of(jnp.maximum(last_tile - WINDOW // tk, 0) * tk, tk)
        s = _dot(kwg_ref[g, pl.ds(k_start, span), :], q_group(g))
        row_w = lax.broadcasted_iota(jnp.int32, (span, tq), 0)
        ahead = t_row - k_start
        ok = (row_w <= ahead) & (row_w > ahead - WINDOW)
        probs = []
        for c in range(cols4 // LANES):
            qs = slice((c % (tq // LANES)) * LANES, (c % (tq // LANES) + 1) * LANES)
            sc = jnp.where(ok[:, qs], s[:, c * LANES:(c + 1) * LANES], NEG)
            probs.append(jnp.exp2(sc - jnp.max(sc, axis=0, keepdims=True)).astype(BF16))
        acc = _dot(vwt_ref[g, :, pl.ds(k_start, span)], jnp.concatenate(probs, axis=1))
        add_gated(g, 2, acc[0:NSA_DH, :] / acc[NSA_DH:NSA_DH + 1, :])

    importance = []
    for g in range(NSA_GROUPS):
        lg = jnp.where(cmp_valid, _dot(kcg_ref[g], q_group(g)), NEG)
        ex = jnp.exp2(lg - jnp.max(lg, axis=0, keepdims=True))
        p_cmp = jnp.where(cmp_valid, ex / jnp.sum(ex, axis=0, keepdims=True), 0.0)
        add_gated(g, 0, _dot(vct_ref[g], p_cmp.astype(BF16)))
        p_grp = p_cmp[:, 0:tq] + p_cmp[:, tq:2 * tq] + p_cmp[:, 2 * tq:3 * tq] + p_cmp[:, 3 * tq:4 * tq]
        importance.append(sum(_dot(ovt_ref[...], part) for part in _split_bf16(p_grp)))
        window_branch(g)

    @pl.when(last_blk < N_SELECT)
    def _():
        for g in range(NSA_GROUPS):
            sel_ref[g] = causal_blk.astype(F32)

    @pl.when(last_blk >= N_SELECT)
    def _():
        for g in range(NSA_GROUPS):
            val = jnp.where(forced, jnp.inf, jnp.where(causal_blk, importance[g], -jnp.inf))
            rank = jnp.zeros((nsel, tq), F32)
            for i in range(nsel):
                vi = val[i:i + 1, :]
                rank += ((vi > val) | ((vi == val) & (j_idx > i))).astype(F32)
            sel_ref[g] = (rank < N_SELECT).astype(F32)

    for g in range(NSA_GROUPS):
        attend(1, g, ksg_ref, vst_ref, 0, last_tile + 1, sel_mask)
    o_ref[...] = acc_ref[...].T.astype(o_ref.dtype)


def _overlap_t(s):
    ncmp = s // CMP_STRIDE
    nsel = s // SEL_BLOCK
    c_start = np.arange(ncmp) * CMP_STRIDE
    s_start = np.arange(nsel) * SEL_BLOCK
    overlap_t = ((c_start[None, :] <= s_start[:, None] + SEL_BLOCK - 1)
                 & (c_start[None, :] + CMP_LEN - 1 >= s_start[:, None])).astype(np.float32)
    overlap_t[:, ncmp - 1] = 0.0
    return jnp.asarray(overlap_t, dtype=BF16)


def _nsa_attention(p, qnt, kcg, vct, ksg, vst, kwg, vwt, col):
    bsz, hq, s = qnt.shape
    ncmp = s // CMP_STRIDE
    nsel = s // SEL_BLOCK
    gw, gidx = col["gates"]
    per_row = lambda shape: pl.BlockSpec((None,) + shape, lambda b, i: (b,) + (0,) * len(shape))
    keys = per_row((NSA_GROUPS, s, NSA_DH))
    vals = per_row((NSA_GROUPS, NSA_VPAD, s))
    return pl.pallas_call(
        _nsa_attn_kernel,
        grid=(bsz, s // NSA_TQ),
        in_specs=[pl.BlockSpec((None, hq, NSA_TQ), lambda b, i: (b, 0, i)),
                  pl.BlockSpec((None, NSA_TQ, gw), lambda b, i: (b, i, gidx)),
                  per_row((NSA_GROUPS, ncmp, NSA_DH)), per_row((NSA_GROUPS, NSA_DH, ncmp)),
                  keys, vals, keys, vals,
                  pl.BlockSpec((nsel, ncmp), lambda b, i: (0, 0))],
        out_specs=pl.BlockSpec((None, NSA_TQ, hq), lambda b, i: (b, i, 0)),
        out_shape=jax.ShapeDtypeStruct((bsz, s, hq), BF16),
        scratch_shapes=[pltpu.VMEM((hq, NSA_TQ), F32), pltpu.VMEM((NSA_GROUPS, nsel, NSA_TQ), F32),
                        pltpu.VMEM((2, NSA_TK, NSA_HPG * NSA_TQ), F32),
                        pltpu.VMEM((2, NSA_TK, NSA_HPG * NSA_TQ), BF16),
                        pltpu.VMEM((NSA_VPAD, NSA_HPG * NSA_TQ), F32)],
        compiler_params=_cparams(("parallel", "arbitrary")),
        name="nsa_attention",
    )(qnt, p, kcg, vct, ksg, vst, kwg, vwt, _overlap_t(s))


def _layout(pieces, tile):
    col, off, mats = {}, 0, []
    for name, w in pieces:
        width = w.shape[1]
        assert off % width == 0, (name, off, width)
        col[name] = (width, off // width)
        mats.append(w)
        off += width
    pad = -off % tile
    if pad:
        mats.append(jnp.zeros((mats[0].shape[0], pad), mats[0].dtype))
    return jnp.concatenate(mats, axis=1).astype(BF16), col


def _pad_cols(w, width):
    return jnp.pad(w, ((0, 0), (0, width - w.shape[1])))


AB_TILE = 1920
ML_TILE = 1280


def _ab_layout(w_in):
    hk = GLA_HEADS * GLA_DK
    hv = GLA_HEADS * GLA_DV
    kv = NSA_GROUPS * NSA_DH
    sizes = [hk, hk, hv, hv, GLA_RANK, NSA_HEADS * NSA_DH] + [kv] * 6 + [3 * NSA_HEADS]
    cuts = np.cumsum(sizes)[:-1].tolist()
    gq, gk, gv, gr, gz, nq, kc, vc, ks, vs, kw, vw, gt = jnp.split(w_in, cuts, axis=1)
    gates = _pad_cols(jnp.concatenate([gz, gt], axis=1), LANES)
    return _layout([("gla_q", gq), ("gla_k", gk), ("gla_v", gv), ("gla_r", gr), ("nsa_q", nq),
                    ("nsa_kc", kc), ("nsa_vc", vc), ("nsa_ks", ks), ("nsa_vs", vs), ("nsa_kw", kw),
                    ("nsa_vw", vw), ("gates", gates)], AB_TILE)


def _ml_layout(w_in):
    hq = ML_HEADS * ML_DQK
    hv = ML_HEADS * ML_DV
    cuts = np.cumsum([hq, hq, hv, hv])[:].tolist()
    q, k, v, og, gates = jnp.split(w_in, cuts, axis=1)
    return _layout([("ml_q", q), ("ml_k", k), ("ml_v", v), ("ml_og", og),
                    ("gates", _pad_cols(gates, LANES))], ML_TILE)


def kernel(x, c, norm_mix_g, mod_mix_w, mod_mix_b, norm_mlp_g, mod_mlp_w, mod_mlp_b, mlp_w1, mlp_w2, ab_w_in, ab_w_out, gla_w_gate, gla_b_gate, gla_norm_g, nsa_q_norm_g, nsa_k_norm_g, nsa_cmp_pos_k, nsa_cmp_pos_v, nsa_cmp_k_w1, nsa_cmp_k_w2, nsa_cmp_v_w1, nsa_cmp_v_w2, ml_w_in, ml_w_out, ml_conv_w, ml_b_i, ml_b_f, ml_norm_g):
    bsz, s, d = x.shape
    depth = norm_mix_g.shape[0]
    mod_mix = _ada_mod(c, mod_mix_w, mod_mix_b)
    mod_mlp = _ada_mod(c, mod_mlp_w, mod_mlp_b)

    def split_mod(m):
        return [m[:, None, i * d:(i + 1) * d] for i in range(3)]

    for layer in range(depth):
        shift, scale, gate = split_mod(mod_mix[layer])
        if layer % 2 == 0:
            e = layer // 2
            w_in, col = _ab_layout(ab_w_in[e])
            p = _norm_proj(x, norm_mix_g[layer], scale, shift, w_in, tm=512, tn=AB_TILE)
            w_gate = jnp.pad(gla_w_gate[e], ((0, LANES - GLA_RANK), (0, 0))).astype(BF16)
            o_gla = _gla(p, w_gate, gla_b_gate[e], gla_norm_g[e], col)
            q_gain = (jnp.tile(nsa_q_norm_g[e], NSA_HEADS) * (NSA_DH ** -0.5 * LOG2E)).reshape(-1, 1)
            k_gain = jnp.tile(nsa_k_norm_g[e], NSA_GROUPS).reshape(1, -1)
            qnt, kcc, vcc, ksg, vst, kwg, vwt = _nsa_prep(p, q_gain, k_gain, col)
            kcg, vct = _compress(kcc, vcc, nsa_cmp_pos_k[e], nsa_cmp_pos_v[e], nsa_cmp_k_w1[e],
                                 nsa_cmp_k_w2[e], nsa_cmp_v_w1[e], nsa_cmp_v_w2[e], k_gain)
            o_nsa = _nsa_attention(p, qnt, kcg, vct, ksg, vst, kwg, vwt, col)
            w_out = ab_w_out[e].astype(BF16)
            half = GLA_HEADS * GLA_DV
            x = _out_proj([o_gla, o_nsa], [w_out[:half], w_out[half:]], x, gate)
        else:
            o = layer // 2
            w_in, col = _ml_layout(ml_w_in[o])
            p = _norm_proj(x, norm_mix_g[layer], scale, shift, w_in, tm=512, tn=ML_TILE)
            bias = jnp.concatenate([ml_b_i[o], ml_b_f[o]])
            bias_col = _pad_cols(bias.reshape(1, -1), LANES)
            bias_row = jnp.broadcast_to(bias.reshape(-1, 1), (2 * ML_HEADS, LANES))
            hh = _mlstm(p, ml_conv_w[o], bias_col, bias_row, ml_norm_g[o], col)
            x = _out_proj([hh], [ml_w_out[o].astype(BF16)], x, gate)
        shift, scale, gate = split_mod(mod_mlp[layer])
        x = _mlp_sublayer(x, norm_mlp_g[layer], scale, shift, gate,
                          mlp_w1[layer].astype(BF16), mlp_w2[layer].astype(BF16))
    return x
```

```python
import functools

import numpy as np
import jax
import jax.numpy as jnp
from jax import lax
from jax.experimental import pallas as pl
from jax.experimental.pallas import tpu as pltpu

F32 = jnp.float32
BF16 = jnp.bfloat16
HI = lax.Precision.HIGHEST

EPS = 1e-6
NEG = -1e30
LOG2E = 1.4426950408889634

VMEM_LIMIT_BYTES = 56 * 1024 * 1024
LANES = 128

GLA_HEADS = 4
GLA_DK = 128
GLA_DV = 256
GLA_RANK = 16
GLA_GATE_TAU = 16.0
CHUNK = 64
ML_CHUNK = 128
REC_BLOCK = 256

NSA_HEADS = 16
NSA_GROUPS = 4
NSA_HPG = 4
NSA_DH = 64
CMP_LEN = 32
CMP_STRIDE = 16
SEL_BLOCK = 64
N_SELECT = 16
WINDOW = 512
NSA_TQ = 256
NSA_TK = 256
NSA_VPAD = 80
PAST_END = 1 << 24

ML_HEADS = 4
ML_DQK = 256
ML_DV = 512
ML_CONV = 4


def _cparams(sem):
    return pltpu.CompilerParams(dimension_semantics=sem, vmem_limit_bytes=VMEM_LIMIT_BYTES)


def _sigmoid(x):
    return 1.0 / (1.0 + jnp.exp(-x))


def _silu(x):
    return x * _sigmoid(x)


def _log_sigmoid(x):
    return jnp.minimum(x, 0.0) - jnp.log(1.0 + jnp.exp(-jnp.abs(x)))


def _dot(a, b, precision=None):
    return jnp.dot(a, b, preferred_element_type=F32, precision=precision)


def _dot_nt(a, b, precision=None):
    return lax.dot_general(a, b, (((1,), (1,)), ((), ())), preferred_element_type=F32, precision=precision)


def _rms(x):
    return x * lax.rsqrt(jnp.mean(x * x, axis=-1, keepdims=True) + EPS)


def _mod_kernel(c_ref, w_ref, b_ref, o_ref):
    a = _silu(c_ref[...]).astype(BF16)
    o_ref[...] = _dot(a, w_ref[...].astype(BF16)) + b_ref[...]


def _ada_mod(c, w, b):
    nl, d, d3 = w.shape
    bsz = c.shape[0]
    tn = 768
    return pl.pallas_call(
        _mod_kernel,
        grid=(nl, d3 // tn),
        in_specs=[pl.BlockSpec((bsz, d), lambda l, j: (0, 0)),
                  pl.BlockSpec((None, d, tn), lambda l, j: (l, 0, j)),
                  pl.BlockSpec((None, 1, tn), lambda l, j: (l, 0, j))],
        out_specs=pl.BlockSpec((None, bsz, tn), lambda l, j: (l, 0, j)),
        out_shape=jax.ShapeDtypeStruct((nl, bsz, d3), F32),
        compiler_params=_cparams(("parallel", "parallel")),
        name="ada_mod",
    )(c, w, b.reshape(nl, 1, d3))


def _modulated_norm(x_ref, g_ref, sc_ref, sh_ref):
    x = x_ref[...]
    gain = g_ref[...] * (1.0 + sc_ref[...])
    inv = lax.rsqrt(jnp.mean(x * x, axis=-1, keepdims=True) + EPS)
    return ((x * inv) * gain + sh_ref[...]).astype(BF16)


def _proj_kernel(x_ref, g_ref, sc_ref, sh_ref, w_ref, o_ref, h_ref):
    i = pl.program_id(2)

    @pl.when(pl.program_id(1) == 0)
    def _():
        h_ref[i] = _modulated_norm(x_ref, g_ref, sc_ref, sh_ref)

    o_ref[...] = _dot(h_ref[i], w_ref[...])


def _norm_proj(x, g, scale, shift, w, tm, tn):
    bsz, s, d = x.shape
    n = w.shape[1]
    rows = s // tm
    x_spec = pl.BlockSpec((None, tm, d), lambda b, j, i: (b, jnp.where(j == 0, i, rows - 1), 0))
    return pl.pallas_call(
        _proj_kernel,
        grid=(bsz, n // tn, rows),
        in_specs=[x_spec,
                  pl.BlockSpec((1, d), lambda b, j, i: (0, 0)),
                  pl.BlockSpec((None, 1, d), lambda b, j, i: (b, 0, 0)),
                  pl.BlockSpec((None, 1, d), lambda b, j, i: (b, 0, 0)),
                  pl.BlockSpec((d, tn), lambda b, j, i: (0, j))],
        out_specs=pl.BlockSpec((None, tm, tn), lambda b, j, i: (b, i, j)),
        out_shape=jax.ShapeDtypeStruct((bsz, s, n), F32),
        scratch_shapes=[pltpu.VMEM((rows, tm, d), BF16)],
        compiler_params=_cparams(("parallel", "arbitrary", "arbitrary")),
        name="norm_proj",
    )(x, g.reshape(1, d), scale, shift, w)


def _mlp_kernel(x_ref, g_ref, sc_ref, sh_ref, gate_ref, w1_ref, w2_ref, o_ref, h_ref, acc_ref):
    f = pl.program_id(2)

    @pl.when(f == 0)
    def _():
        h_ref[...] = _modulated_norm(x_ref, g_ref, sc_ref, sh_ref)
        acc_ref[...] = jnp.zeros_like(acc_ref)

    u = jnp.maximum(_dot(h_ref[...], w1_ref[...]), 0.0)
    acc_ref[...] += _dot((u * u).astype(BF16), w2_ref[...])

    @pl.when(f == pl.num_programs(2) - 1)
    def _():
        o_ref[...] = x_ref[...] + gate_ref[...] * acc_ref[...]


def _mlp_sublayer(x, g, scale, shift, gate, w1, w2, tm=512, tf=1024):
    bsz, s, d = x.shape
    ff = w1.shape[1]
    vec = pl.BlockSpec((None, 1, d), lambda b, i, f: (b, 0, 0))
    return pl.pallas_call(
        _mlp_kernel,
        grid=(bsz, s // tm, ff // tf),
        in_specs=[pl.BlockSpec((None, tm, d), lambda b, i, f: (b, i, 0)),
                  pl.BlockSpec((1, d), lambda b, i, f: (0, 0)),
                  vec, vec, vec,
                  pl.BlockSpec((d, tf), lambda b, i, f: (0, f)),
                  pl.BlockSpec((tf, d), lambda b, i, f: (f, 0))],
        out_specs=pl.BlockSpec((None, tm, d), lambda b, i, f: (b, i, 0)),
        out_shape=jax.ShapeDtypeStruct((bsz, s, d), F32),
        scratch_shapes=[pltpu.VMEM((tm, d), BF16), pltpu.VMEM((tm, d), F32)],
        compiler_params=_cparams(("parallel", "parallel", "arbitrary")),
        name="mlp_sublayer",
    )(x, g.reshape(1, d), scale, shift, gate, w1, w2)


def _outproj_kernel(n_in, *refs):
    a_refs, w_refs = refs[:n_in], refs[n_in:2 * n_in]
    x_ref, gate_ref, o_ref = refs[2 * n_in:]
    y = _dot(a_refs[0][...], w_refs[0][...])
    for a_ref, w_ref in zip(a_refs[1:], w_refs[1:]):
        y += _dot(a_ref[...], w_ref[...])
    o_ref[...] = x_ref[...] + gate_ref[...] * y


def _out_proj(acts, ws, x, gate, tm=512):
    bsz, s, d = x.shape
    n_in = len(acts)
    in_specs = [pl.BlockSpec((None, tm, a.shape[-1]), lambda b, i: (b, i, 0)) for a in acts]
    in_specs += [pl.BlockSpec(w.shape, lambda b, i: (0, 0)) for w in ws]
    in_specs += [pl.BlockSpec((None, tm, d), lambda b, i: (b, i, 0)),
                 pl.BlockSpec((None, 1, d), lambda b, i: (b, 0, 0))]
    return pl.pallas_call(
        functools.partial(_outproj_kernel, n_in),
        grid=(bsz, s // tm),
        in_specs=in_specs,
        out_specs=pl.BlockSpec((None, tm, d), lambda b, i: (b, i, 0)),
        out_shape=jax.ShapeDtypeStruct((bsz, s, d), F32),
        compiler_params=_cparams(("parallel", "parallel")),
        name="out_proj",
    )(*acts, *ws, x, gate)


def _tri(n, upper=False):
    r = lax.broadcasted_iota(jnp.int32, (n, n), 0)
    c = lax.broadcasted_iota(jnp.int32, (n, n), 1)
    return (c >= r) if upper else (c <= r)


def _gla_kernel(q_ref, k_ref, v_ref, r_ref, zg_ref, wg_ref, bg_ref, ng_ref, o_ref, st_ref):
    @pl.when(pl.program_id(1) == 0)
    def _():
        st_ref[...] = jnp.zeros_like(st_ref)

    tril = _tri(CHUNK)
    tril_f = tril.astype(F32)
    for c in range(REC_BLOCK // CHUNK):
        rows = slice(c * CHUNK, (c + 1) * CHUNK)
        pre = _dot(zg_ref[rows, :].astype(BF16), wg_ref[...]) + bg_ref[...]
        log_a = _log_sigmoid(pre) / GLA_GATE_TAU
        cum = _dot(tril_f, log_a, precision=HI)
        total = cum[CHUNK - 1:CHUNK, :]
        q_dec = (q_ref[rows, :] * GLA_DK ** -0.5) * jnp.exp(cum)
        kk = k_ref[rows, :]
        k_dec = kk * jnp.exp(-cum)
        k_end = kk * jnp.exp(total - cum)
        decay = jnp.exp(total)
        for h in range(GLA_HEADS):
            ks = slice(h * GLA_DK, (h + 1) * GLA_DK)
            vs = slice(h * GLA_DV, (h + 1) * GLA_DV)
            qh = q_dec[:, ks].astype(BF16)
            vh = v_ref[rows, vs]
            att = jnp.where(tril, _dot_nt(qh, k_dec[:, ks].astype(BF16)), 0.0)
            st = st_ref[h]
            o = _dot(att.astype(BF16), vh.astype(BF16)) + _dot_nt(qh, st.astype(BF16))
            st_ref[h] = st * decay[:, ks] + _dot(vh.T.astype(BF16), k_end[:, ks].astype(BF16))
            rh = r_ref[rows, vs]
            o_ref[rows, vs] = ((_rms(o) * ng_ref[...]) * _silu(rh)).astype(o_ref.dtype)


def _gla(p, w_gate, b_gate, norm_g, col):
    bsz, s, _ = p.shape
    t = REC_BLOCK

    def piece(name):
        width, idx = col[name]
        return pl.BlockSpec((None, t, width), lambda b, i, idx=idx: (b, i, idx))

    const = lambda shape: pl.BlockSpec(shape, lambda b, i: (0,) * len(shape))
    hk = GLA_HEADS * GLA_DK
    return pl.pallas_call(
        _gla_kernel,
        grid=(bsz, s // t),
        in_specs=[piece("gla_q"), piece("gla_k"), piece("gla_v"), piece("gla_r"), piece("gates"),
                  const((LANES, hk)), const((1, hk)), const((1, GLA_DV))],
        out_specs=pl.BlockSpec((None, t, GLA_HEADS * GLA_DV), lambda b, i: (b, i, 0)),
        out_shape=jax.ShapeDtypeStruct((bsz, s, GLA_HEADS * GLA_DV), BF16),
        scratch_shapes=[pltpu.VMEM((GLA_HEADS, GLA_DV, GLA_DK), F32)],
        compiler_params=_cparams(("parallel", "arbitrary")),
        name="gla",
    )(p, p, p, p, p, w_gate, b_gate.reshape(1, hk), norm_g.reshape(1, GLA_DV))


def _mlstm_kernel(q_ref, k_ref, v_ref, og_ref, gc_ref, wq_ref, wk_ref, bc_ref, br_ref, ng_ref,
                  o_ref, c_ref, n_ref, m_ref, qx_ref, kx_ref):
    t = REC_BLOCK

    @pl.when(pl.program_id(1) == 0)
    def _():
        c_ref[...] = jnp.zeros_like(c_ref)
        n_ref[...] = jnp.zeros_like(n_ref)
        m_ref[...] = jnp.zeros_like(m_ref)
        qx_ref[0:8, :] = jnp.zeros((8, qx_ref.shape[1]), F32)
        kx_ref[0:8, :] = jnp.zeros((8, kx_ref.shape[1]), F32)

    def conv_silu(x_ref, xx_ref, w_ref):
        xx_ref[8:8 + t, :] = x_ref[...]
        y = xx_ref[8:8 + t, :] * w_ref[ML_CONV - 1:ML_CONV, :]
        for back in range(1, ML_CONV):
            y += xx_ref[8 - back:8 - back + t, :] * w_ref[ML_CONV - 1 - back:ML_CONV - back, :]
        xx_ref[0:8, :] = xx_ref[t:t + 8, :]
        return _silu(y)

    q_all = conv_silu(q_ref, qx_ref, wq_ref) * ML_DQK ** -0.5
    k_all = conv_silu(k_ref, kx_ref, wk_ref)

    tril = _tri(ML_CHUNK)
    tril_f = tril.astype(F32)
    triu_f = _tri(ML_CHUNK, upper=True).astype(F32)
    lane = lax.broadcasted_iota(jnp.int32, (ML_CHUNK, LANES), 1)
    is_f_col = (lane >= ML_HEADS) & (lane < 2 * ML_HEADS)
    sub = lax.broadcasted_iota(jnp.int32, (8, ML_CHUNK), 0)
    is_f_row = sub >= ML_HEADS
    gates_by_row = gc_ref[...].T[0:2 * ML_HEADS, :]

    for c in range(t // ML_CHUNK):
        rows = slice(c * ML_CHUNK, (c + 1) * ML_CHUNK)
        gcol = gc_ref[rows, :] + bc_ref[...]
        gcol = jnp.where(is_f_col, _log_sigmoid(gcol), gcol)
        bcol = _dot(tril_f, gcol, precision=HI)
        grow = gates_by_row[:, rows] + br_ref[:, 0:1]
        grow = jnp.where(is_f_row, _log_sigmoid(grow), grow)
        brow = _dot(grow, triu_f, precision=HI)
        for h in range(ML_HEADS):
            qs = slice(h * ML_DQK, (h + 1) * ML_DQK)
            vs = slice(h * ML_DV, (h + 1) * ML_DV)
            b_col = bcol[:, ML_HEADS + h:ML_HEADS + h + 1]
            b_row = brow[ML_HEADS + h:ML_HEADS + h + 1, :]
            li_col = gcol[:, h:h + 1]
            li_row = grow[h:h + 1, :]
            m_prev = m_ref[h:h + 1, 0:1]
            log_d = jnp.where(tril, b_col - b_row + li_row, -jnp.inf)
            log_inter = b_col + m_prev
            m_comb = jnp.maximum(log_inter, jnp.max(log_d, axis=-1, keepdims=True))
            d = jnp.exp(log_d - m_comb)
            w_inter = jnp.exp(log_inter - m_comb)
            qh = q_all[rows, qs]
            kh = k_all[rows, qs]
            qb = qh.astype(BF16)
            vb = v_ref[rows, vs].astype(BF16)
            sc = _dot_nt(qb, kh.astype(BF16)) * d
            cm = c_ref[h]
            nv = n_ref[h]
            num = _dot(sc.astype(BF16), vb) + w_inter * _dot(qb, cm.astype(BF16))
            den = jnp.sum(sc, axis=-1, keepdims=True) + w_inter * jnp.sum(qh * nv, axis=-1, keepdims=True)
            hh = num / jnp.maximum(jnp.abs(den), jnp.exp(-m_comb))
            total = b_col[ML_CHUNK - 1:ML_CHUNK, :]
            log_w = total - b_col + li_col
            m_new = jnp.maximum(total + m_prev, jnp.max(log_w, axis=0, keepdims=True))
            decay = jnp.exp(total + m_prev - m_new)
            kw = kh * jnp.exp(log_w - m_new)
            c_ref[h] = decay * cm + _dot(kw.T.astype(BF16), vb)
            n_ref[h] = decay * nv + jnp.sum(kw, axis=0, keepdims=True)
            m_ref[h:h + 1, :] = jnp.broadcast_to(m_new, (1, LANES))
            og = og_ref[rows, vs]
            o_ref[rows, vs] = ((_rms(hh) * ng_ref[...]) * _sigmoid(og)).astype(o_ref.dtype)


def _mlstm(p, conv_w, bias_col, bias_row, norm_g, col):
    bsz, s, _ = p.shape
    t = REC_BLOCK
    hq = ML_HEADS * ML_DQK
    hv = ML_HEADS * ML_DV

    def piece(name):
        width, idx = col[name]
        return pl.BlockSpec((None, t, width), lambda b, i, idx=idx: (b, i, idx))

    const = lambda shape: pl.BlockSpec(shape, lambda b, i: (0,) * len(shape))
    return pl.pallas_call(
        _mlstm_kernel,
        grid=(bsz, s // t),
        in_specs=[piece("ml_q"), piece("ml_k"), piece("ml_v"), piece("ml_og"), piece("gates"),
                  pl.BlockSpec((ML_CONV, hq), lambda b, i: (0, 0)),
                  pl.BlockSpec((ML_CONV, hq), lambda b, i: (0, 1)),
                  const((1, LANES)), const((8, LANES)), const((1, ML_DV))],
        out_specs=pl.BlockSpec((None, t, hv), lambda b, i: (b, i, 0)),
        out_shape=jax.ShapeDtypeStruct((bsz, s, hv), BF16),
        scratch_shapes=[pltpu.VMEM((ML_HEADS, ML_DQK, ML_DV), F32),
                        pltpu.VMEM((ML_HEADS, 1, ML_DQK), F32),
                        pltpu.VMEM((8, LANES), F32),
                        pltpu.VMEM((t + 8, hq), F32),
                        pltpu.VMEM((t + 8, hq), F32)],
        compiler_params=_cparams(("parallel", "arbitrary")),
        name="mlstm",
    )(p, p, p, p, p, conv_w, conv_w, bias_col, bias_row, norm_g.reshape(1, ML_DV))


def _split_bf16(x):
    hi = x.astype(BF16)
    return hi, (x - hi.astype(F32)).astype(BF16)


def _seg_norm(x, seg_ref, segt_ref):
    if seg_ref.dtype == BF16:
        seg_sum = lambda v, m_ref: sum(_dot(part, m_ref[...]) for part in _split_bf16(v))
    else:
        seg_sum = lambda v, m_ref: _dot(v, m_ref[...], precision=HI)
    ss = seg_sum(x * x, seg_ref) * (1.0 / NSA_DH)
    return x * seg_sum(lax.rsqrt(ss + EPS), segt_ref)


def _nsa_prep_kernel(q_ref, ks_ref, vs_ref, kw_ref, vw_ref, qg_ref, kg_ref, segk_ref, segkt_ref,
                     qnt_ref, ksg_ref, vst_ref, kwg_ref, vwt_ref):
    t = q_ref.shape[0]
    qt = q_ref[...].T
    for h in range(NSA_HEADS):
        hs = slice(h * NSA_DH, (h + 1) * NSA_DH)
        blk = qt[hs, :]
        inv = lax.rsqrt(jnp.mean(blk * blk, axis=0, keepdims=True) + EPS)
        qnt_ref[hs, :] = ((blk * inv) * qg_ref[hs, :]).astype(BF16)
    ones = jnp.ones((NSA_VPAD - NSA_DH, t), BF16)
    for k_ref, v_ref, kg_out, vt_out in ((ks_ref, vs_ref, ksg_ref, vst_ref), (kw_ref, vw_ref, kwg_ref, vwt_ref)):
        kn = (_seg_norm(k_ref[...], segk_ref, segkt_ref) * kg_ref[...]).astype(BF16)
        vt = v_ref[...].T.astype(BF16)
        for g in range(NSA_GROUPS):
            gs = slice(g * NSA_DH, (g + 1) * NSA_DH)
            kg_out[g] = kn[:, gs]
            vt_out[g, 0:NSA_DH, :] = vt[gs, :]
            vt_out[g, NSA_DH:NSA_VPAD, :] = ones


def _seg_matrices(width, dtype):
    seg = (np.arange(width)[:, None] // NSA_DH == np.arange(LANES)[None, :]).astype(np.float32)
    return jnp.asarray(seg, dtype=dtype), jnp.asarray(seg.T, dtype=dtype)


def _nsa_prep(p, q_gain, k_gain, col, t=512):
    bsz, s, _ = p.shape
    hq = NSA_HEADS * NSA_DH
    kv = NSA_GROUPS * NSA_DH

    def piece(name):
        width, idx = col[name]
        return pl.BlockSpec((None, t, width), lambda b, i, idx=idx: (b, i, idx))

    const = lambda shape: pl.BlockSpec(shape, lambda b, i: (0,) * len(shape))
    segk, segkt = _seg_matrices(kv, BF16)
    keys = pl.BlockSpec((None, NSA_GROUPS, t, NSA_DH), lambda b, i: (b, 0, i, 0))
    vals = pl.BlockSpec((None, NSA_GROUPS, NSA_VPAD, t), lambda b, i: (b, 0, 0, i))
    shp = lambda *dims, dt=BF16: jax.ShapeDtypeStruct(dims, dt)
    keys_shape = shp(bsz, NSA_GROUPS, s, NSA_DH)
    vals_shape = shp(bsz, NSA_GROUPS, NSA_VPAD, s)
    return pl.pallas_call(
        _nsa_prep_kernel,
        grid=(bsz, s // t),
        in_specs=[piece("nsa_q"), piece("nsa_ks"), piece("nsa_vs"), piece("nsa_kw"), piece("nsa_vw"),
                  const((hq, 1)), const((1, kv)), const((kv, LANES)), const((LANES, kv))],
        out_specs=[pl.BlockSpec((None, hq, t), lambda b, i: (b, 0, i)), keys, vals, keys, vals],
        out_shape=[shp(bsz, hq, s), keys_shape, vals_shape, keys_shape, vals_shape],
        compiler_params=_cparams(("parallel", "parallel")),
        name="nsa_prep",
    )(p, p, p, p, p, q_gain, k_gain, segk, segkt)


def _compress_kernel(kc0_ref, kc1_ref, vc0_ref, vc1_ref, posk_ref, posv_ref, kw1a_ref, kw1b_ref, kw2_ref,
                     vw1a_ref, vw1b_ref, vw2_ref, kg_ref, segk_ref, segkt_ref, kcg_ref, vct_ref):
    def hidden(x_refs, pos_ref, w1a_ref, w1b_ref):
        nsub = x_refs[0].shape[0] // CMP_STRIDE
        first = second = None
        for l in range(CMP_STRIDE):
            x = jnp.concatenate([r[pl.ds(l, nsub, stride=CMP_STRIDE), :] for r in x_refs], axis=1)
            a = _dot((x + pos_ref[l:l + 1, :]).astype(BF16), w1a_ref[l])
            b = _dot((x + pos_ref[CMP_STRIDE + l:CMP_STRIDE + l + 1, :]).astype(BF16), w1b_ref[l])
            first = a if first is None else first + a
            second = b if second is None else second + b
        row = lax.broadcasted_iota(jnp.int32, second.shape, 0)
        nxt = jnp.where(row == nsub - 1, 0.0, pltpu.roll(second, nsub - 1, 0))
        return _silu(first + nxt)

    kc = _dot(hidden((kc0_ref, kc1_ref), posk_ref, kw1a_ref, kw1b_ref).astype(BF16), kw2_ref[...])
    kn = (_seg_norm(kc, segk_ref, segkt_ref) * kg_ref[...]).astype(BF16)
    vt = _dot(vw2_ref[...], hidden((vc0_ref, vc1_ref), posv_ref, vw1a_ref, vw1b_ref).T.astype(BF16)).astype(BF16)
    for g in range(NSA_GROUPS):
        gs = slice(g * NSA_DH, (g + 1) * NSA_DH)
        kcg_ref[g] = kn[:, gs]
        vct_ref[g] = vt[gs, :]


def _expand_w1(w1):
    w = w1.reshape(2, CMP_STRIDE, NSA_DH, NSA_DH)
    eye = jnp.eye(NSA_GROUPS, dtype=w1.dtype)
    big = jnp.einsum('hlde,gk->hlgdke', w, eye)
    big = big.reshape(2, CMP_STRIDE, NSA_GROUPS * NSA_DH, NSA_GROUPS * NSA_DH).astype(BF16)
    return big[0], big[1]


def _expand_pos(pos):
    return jnp.tile(pos, (1, NSA_GROUPS))


def _block_diag(w2):
    return jnp.kron(jnp.eye(NSA_GROUPS, dtype=w2.dtype), w2).astype(BF16)


def _compress(p, pos_k, pos_v, ck_w1, ck_w2, cv_w1, cv_w2, k_gain, col):
    bsz, s, _ = p.shape
    kv = NSA_GROUPS * NSA_DH
    nsub = s // CMP_STRIDE
    segk, segkt = _seg_matrices(kv, F32)
    kw1a, kw1b = _expand_w1(ck_w1)
    vw1a, vw1b = _expand_w1(cv_w1)
    const = lambda shape: pl.BlockSpec(shape, lambda b: (0,) * len(shape))

    def halves(name):
        width, idx = col[name]
        assert width == 2 * LANES
        return [pl.BlockSpec((None, s, LANES), lambda b, j=2 * idx + h: (b, 0, j)) for h in range(2)]

    w1_spec = const((CMP_STRIDE, kv, kv))
    return pl.pallas_call(
        _compress_kernel,
        grid=(bsz,),
        in_specs=[*halves("nsa_kc"), *halves("nsa_vc"), const((CMP_LEN, kv)), const((CMP_LEN, kv)),
                  w1_spec, w1_spec, const((kv, kv)),
                  w1_spec, w1_spec, const((kv, kv)),
                  const((1, kv)), const((kv, LANES)), const((LANES, kv))],
        out_specs=[pl.BlockSpec((None, NSA_GROUPS, nsub, NSA_DH), lambda b: (b, 0, 0, 0)),
                   pl.BlockSpec((None, NSA_GROUPS, NSA_DH, nsub), lambda b: (b, 0, 0, 0))],
        out_shape=[jax.ShapeDtypeStruct((bsz, NSA_GROUPS, nsub, NSA_DH), BF16),
                   jax.ShapeDtypeStruct((bsz, NSA_GROUPS, NSA_DH, nsub), BF16)],
        compiler_params=_cparams(("parallel",)),
        name="nsa_compress",
    )(p, p, p, p, _expand_pos(pos_k), _expand_pos(pos_v),
      kw1a, kw1b, _block_diag(ck_w2), vw1a, vw1b, _block_diag(cv_w2).T, k_gain, segk, segkt)


def _nsa_attn_kernel(qt_ref, gt_ref, kcg_ref, vct_ref, ksg_ref, vst_ref, kwg_ref, vwt_ref, ovt_ref,
                     o_ref, acc_ref, sel_ref, s_ref, p_ref, pv_ref):
    tq, tk = NSA_TQ, NSA_TK
    q0 = pl.program_id(1) * tq
    ncmp = kcg_ref.shape[1]
    nsel = ovt_ref.shape[0]
    cols4 = NSA_HPG * tq

    t_cmp = q0 + lax.broadcasted_iota(jnp.int32, (ncmp, cols4), 1) % tq
    n_idx = lax.broadcasted_iota(jnp.int32, (ncmp, cols4), 0)
    cmp_valid = n_idx * CMP_STRIDE + (CMP_LEN - 1) <= t_cmp

    j_idx = lax.broadcasted_iota(jnp.int32, (nsel, tq), 0)
    cur = (q0 + lax.broadcasted_iota(jnp.int32, (nsel, tq), 1)) // SEL_BLOCK
    forced = (j_idx == 0) | (j_idx == cur) | (j_idx == cur - 1)
    causal_blk = j_idx <= cur

    row_blk = lax.broadcasted_iota(jnp.int32, (SEL_BLOCK, tq), 0)
    t_row = q0 + lax.broadcasted_iota(jnp.int32, (1, tq), 1)

    gates_t = _sigmoid(gt_ref[...]).T

    def add_gated(g, branch, o4):
        for i in range(NSA_HPG):
            h = g * NSA_HPG + i
            hs = slice(h * NSA_DH, (h + 1) * NSA_DH)
            gc = GLA_RANK + 3 * h + branch
            term = gates_t[gc:gc + 1, :] * o4[:, i * tq:(i + 1) * tq]
            acc_ref[hs, :] = term if branch == 0 else acc_ref[hs, :] + term

    def q_group(g):
        return jnp.concatenate([qt_ref[(g * NSA_HPG + i) * NSA_DH:(g * NSA_HPG + i + 1) * NSA_DH, :]
                                for i in range(NSA_HPG)], axis=1)

    def attend(branch, g, k_ref, vt_ref, first_tile, n_tiles, mask_fn):
        qt4 = q_group(g)
        last = first_tile + n_tiles - 1

        def scores(kt):
            return _dot(k_ref[g, pl.ds(pl.multiple_of(kt * tk, tk), tk), :], qt4)

        def weighted_values(kt, p):
            return _dot(vt_ref[g, :, pl.ds(pl.multiple_of(kt * tk, tk), tk)], p)

        def step(kt, cur, m):
            nxt = 1 - cur
            s_ref[nxt] = scores(jnp.minimum(kt + 1, last))
            acc = pv_ref[...] + weighted_values(jnp.clip(kt - 1, first_tile, last), p_ref[nxt])
            k0 = jnp.where(kt <= last, kt * tk, PAST_END)
            ok = mask_fn(g, jnp.minimum(kt, last), k0)
            m_new = []
            for c in range(cols4 // LANES):
                cs = slice(c * LANES, (c + 1) * LANES)
                qs = slice((c % (tq // LANES)) * LANES, (c % (tq // LANES) + 1) * LANES)
                s = jnp.where(ok[:, qs], s_ref[cur, :, cs], NEG)
                m_c = jnp.maximum(m[:, cs], jnp.max(s, axis=0, keepdims=True))
                p_ref[cur, :, cs] = jnp.exp2(s - m_c).astype(BF16)
                m_new.append(m_c)
            m_new = jnp.concatenate(m_new, axis=1)
            pv_ref[...] = acc * jnp.exp2(m - m_new)
            return m_new

        s_ref[0] = scores(first_tile)
        p_ref[1] = jnp.zeros(p_ref.shape[1:], BF16)
        pv_ref[...] = jnp.zeros_like(pv_ref)
        n_pairs = (n_tiles + 1) // 2

        def body(j, m):
            kt = first_tile + 2 * j
            return step(kt + 1, 1, step(kt, 0, m))

        lax.fori_loop(0, n_pairs, body, jnp.full((1, cols4), NEG, F32))
        final_tile = jnp.minimum(first_tile + 2 * n_pairs - 1, last)
        acc = pv_ref[...] + weighted_values(final_tile, p_ref[1])
        add_gated(g, branch, acc[0:NSA_DH, :] / acc[NSA_DH:NSA_DH + 1, :])

    last_tile = (q0 + tq - 1) // tk
    last_blk = (q0 + tq - 1) // SEL_BLOCK

    def sel_mask(g, kt, k0):
        per_tile = tk // SEL_BLOCK
        blocks = []
        for j in range(per_tile):
            picked = sel_ref[g, pl.ds(kt * per_tile + j, 1), :] > 0.5
            limit = jnp.where(picked, t_row, -1) - (k0 + j * SEL_BLOCK)
            blocks.append(row_blk <= limit)
        return jnp.concatenate(blocks, axis=0)

    def window_branch(g):
        span = WINDOW + tk
        k_start = pl.multiple_of(jnp.maximum(last_tile - WINDOW // tk, 0) * tk, tk)
        s = _dot(kwg_ref[g, pl.ds(k_start, span), :], q_group(g))
        row_w = lax.broadcasted_iota(jnp.int32, (span, tq), 0)
        ahead = t_row - k_start
        ok = (row_w <= ahead) & (row_w > ahead - WINDOW)
        probs = []
        for c in range(cols4 // LANES):
            qs = slice((c % (tq // LANES)) * LANES, (c % (tq // LANES) + 1) * LANES)
            sc = jnp.where(ok[:, qs], s[:, c * LANES:(c + 1) * LANES], NEG)
            probs.append(jnp.exp2(sc - jnp.max(sc, axis=0, keepdims=True)).astype(BF16))
        acc = _dot(vwt_ref[g, :, pl.ds(k_start, span)], jnp.concatenate(probs, axis=1))
        add_gated(g, 2, acc[0:NSA_DH, :] / acc[NSA_DH:NSA_DH + 1, :])

    importance = []
    for g in range(NSA_GROUPS):
        lg = jnp.where(cmp_valid, _dot(kcg_ref[g], q_group(g)), NEG)
        ex = jnp.exp2(lg - jnp.max(lg, axis=0, keepdims=True))
        p_cmp = jnp.where(cmp_valid, ex / jnp.sum(ex, axis=0, keepdims=True), 0.0)
        add_gated(g, 0, _dot(vct_ref[g], p_cmp.astype(BF16)))
        p_grp = p_cmp[:, 0:tq] + p_cmp[:, tq:2 * tq] + p_cmp[:, 2 * tq:3 * tq] + p_cmp[:, 3 * tq:4 * tq]
        importance.append(sum(_dot(ovt_ref[...], part) for part in _split_bf16(p_grp)))
        window_branch(g)

    @pl.when(last_blk < N_SELECT)
    def _():
        for g in range(NSA_GROUPS):
            sel_ref[g] = causal_blk.astype(F32)

    @pl.when(last_blk >= N_SELECT)
    def _():
        for g in range(NSA_GROUPS):
            val = jnp.where(forced, jnp.inf, jnp.where(causal_blk, importance[g], -jnp.inf))
            rank = jnp.zeros((nsel, tq), F32)
            for i in range(nsel):
                vi = val[i:i + 1, :]
                rank += ((vi > val) | ((vi == val) & (j_idx > i))).astype(F32)
            sel_ref[g] = (rank < N_SELECT).astype(F32)

    for g in range(NSA_GROUPS):
        attend(1, g, ksg_ref, vst_ref, 0, last_tile + 1, sel_mask)
    o_ref[...] = acc_ref[...].T.astype(o_ref.dtype)


def _overlap_t(s):
    ncmp = s // CMP_STRIDE
    nsel = s // SEL_BLOCK
    c_start = np.arange(ncmp) * CMP_STRIDE
    s_start = np.arange(nsel) * SEL_BLOCK
    overlap_t = ((c_start[None, :] <= s_start[:, None] + SEL_BLOCK - 1)
                 & (c_start[None, :] + CMP_LEN - 1 >= s_start[:, None])).astype(np.float32)
    overlap_t[:, ncmp - 1] = 0.0
    return jnp.asarray(overlap_t, dtype=BF16)


def _nsa_attention(p, qnt, kcg, vct, ksg, vst, kwg, vwt, col):
    bsz, hq, s = qnt.shape
    ncmp = s // CMP_STRIDE
    nsel = s // SEL_BLOCK
    gw, gidx = col["gates"]
    per_row = lambda shape: pl.BlockSpec((None,) + shape, lambda b, i: (b,) + (0,) * len(shape))
    keys = per_row((NSA_GROUPS, s, NSA_DH))
    vals = per_row((NSA_GROUPS, NSA_VPAD, s))
    return pl.pallas_call(
        _nsa_attn_kernel,
        grid=(bsz, s // NSA_TQ),
        in_specs=[pl.BlockSpec((None, hq, NSA_TQ), lambda b, i: (b, 0, i)),
                  pl.BlockSpec((None, NSA_TQ, gw), lambda b, i: (b, i, gidx)),
                  per_row((NSA_GROUPS, ncmp, NSA_DH)), per_row((NSA_GROUPS, NSA_DH, ncmp)),
                  keys, vals, keys, vals,
                  pl.BlockSpec((nsel, ncmp), lambda b, i: (0, 0))],
        out_specs=pl.BlockSpec((None, NSA_TQ, hq), lambda b, i: (b, i, 0)),
        out_shape=jax.ShapeDtypeStruct((bsz, s, hq), BF16),
        scratch_shapes=[pltpu.VMEM((hq, NSA_TQ), F32), pltpu.VMEM((NSA_GROUPS, nsel, NSA_TQ), F32),
                        pltpu.VMEM((2, NSA_TK, NSA_HPG * NSA_TQ), F32),
                        pltpu.VMEM((2, NSA_TK, NSA_HPG * NSA_TQ), BF16),
                        pltpu.VMEM((NSA_VPAD, NSA_HPG * NSA_TQ), F32)],
        compiler_params=_cparams(("parallel", "arbitrary")),
        name="nsa_attention",
    )(qnt, p, kcg, vct, ksg, vst, kwg, vwt, _overlap_t(s))


def _layout(pieces, tile):
    col, off, mats = {}, 0, []
    for name, w in pieces:
        width = w.shape[1]
        assert off % width == 0, (name, off, width)
        col[name] = (width, off // width)
        mats.append(w)
        off += width
    pad = -off % tile
    if pad:
        mats.append(jnp.zeros((mats[0].shape[0], pad), mats[0].dtype))
    return jnp.concatenate(mats, axis=1).astype(BF16), col


def _pad_cols(w, width):
    return jnp.pad(w, ((0, 0), (0, width - w.shape[1])))


AB_TILE = 1920
ML_TILE = 1280


def _ab_layout(w_in):
    hk = GLA_HEADS * GLA_DK
    hv = GLA_HEADS * GLA_DV
    kv = NSA_GROUPS * NSA_DH
    sizes = [hk, hk, hv, hv, GLA_RANK, NSA_HEADS * NSA_DH] + [kv] * 6 + [3 * NSA_HEADS]
    cuts = np.cumsum(sizes)[:-1].tolist()
    gq, gk, gv, gr, gz, nq, kc, vc, ks, vs, kw, vw, gt = jnp.split(w_in, cuts, axis=1)
    gates = _pad_cols(jnp.concatenate([gz, gt], axis=1), LANES)
    return _layout([("gla_q", gq), ("gla_k", gk), ("gla_v", gv), ("gla_r", gr), ("nsa_q", nq),
                    ("nsa_kc", kc), ("nsa_vc", vc), ("nsa_ks", ks), ("nsa_vs", vs), ("nsa_kw", kw),
                    ("nsa_vw", vw), ("gates", gates)], AB_TILE)


def _ml_layout(w_in):
    hq = ML_HEADS * ML_DQK
    hv = ML_HEADS * ML_DV
    cuts = np.cumsum([hq, hq, hv, hv])[:].tolist()
    q, k, v, og, gates = jnp.split(w_in, cuts, axis=1)
    return _layout([("ml_q", q), ("ml_k", k), ("ml_v", v), ("ml_og", og),
                    ("gates", _pad_cols(gates, LANES))], ML_TILE)


def kernel(x, c, norm_mix_g, mod_mix_w, mod_mix_b, norm_mlp_g, mod_mlp_w, mod_mlp_b, mlp_w1, mlp_w2, ab_w_in, ab_w_out, gla_w_gate, gla_b_gate, gla_norm_g, nsa_q_norm_g, nsa_k_norm_g, nsa_cmp_pos_k, nsa_cmp_pos_v, nsa_cmp_k_w1, nsa_cmp_k_w2, nsa_cmp_v_w1, nsa_cmp_v_w2, ml_w_in, ml_w_out, ml_conv_w, ml_b_i, ml_b_f, ml_norm_g):
    bsz, s, d = x.shape
    depth = norm_mix_g.shape[0]
    mod_mix = _ada_mod(c, mod_mix_w, mod_mix_b)
    mod_mlp = _ada_mod(c, mod_mlp_w, mod_mlp_b)

    def split_mod(m):
        return [m[:, None, i * d:(i + 1) * d] for i in range(3)]

    for layer in range(depth):
        shift, scale, gate = split_mod(mod_mix[layer])
        if layer % 2 == 0:
            e = layer // 2
            w_in, col = _ab_layout(ab_w_in[e])
            p = _norm_proj(x, norm_mix_g[layer], scale, shift, w_in, tm=512, tn=AB_TILE)
            w_gate = jnp.pad(gla_w_gate[e], ((0, LANES - GLA_RANK), (0, 0))).astype(BF16)
            o_gla = _gla(p, w_gate, gla_b_gate[e], gla_norm_g[e], col)
            q_gain = (jnp.tile(nsa_q_norm_g[e], NSA_HEADS) * (NSA_DH ** -0.5 * LOG2E)).reshape(-1, 1)
            k_gain = jnp.tile(nsa_k_norm_g[e], NSA_GROUPS).reshape(1, -1)
            qnt, ksg, vst, kwg, vwt = _nsa_prep(p, q_gain, k_gain, col)
            kcg, vct = _compress(p, nsa_cmp_pos_k[e], nsa_cmp_pos_v[e], nsa_cmp_k_w1[e],
                                 nsa_cmp_k_w2[e], nsa_cmp_v_w1[e], nsa_cmp_v_w2[e], k_gain, col)
            o_nsa = _nsa_attention(p, qnt, kcg, vct, ksg, vst, kwg, vwt, col)
            w_out = ab_w_out[e].astype(BF16)
            half = GLA_HEADS * GLA_DV
            x = _out_proj([o_gla, o_nsa], [w_out[:half], w_out[half:]], x, gate)
        else:
            o = layer // 2
            w_in, col = _ml_layout(ml_w_in[o])
            p = _norm_proj(x, norm_mix_g[layer], scale, shift, w_in, tm=512, tn=ML_TILE)
            bias = jnp.concatenate([ml_b_i[o], ml_b_f[o]])
            bias_col = _pad_cols(bias.reshape(1, -1), LANES)
            bias_row = jnp.broadcast_to(bias.reshape(-1, 1), (2 * ML_HEADS, LANES))
            hh = _mlstm(p, ml_conv_w[o], bias_col, bias_row, ml_norm_g[o], col)
            x = _out_proj([hh], [ml_w_out[o].astype(BF16)], x, gate)
        shift, scale, gate = split_mod(mod_mlp[layer])
        x = _mlp_sublayer(x, norm_mlp_g[layer], scale, shift, gate,
                          mlp_w1[layer].astype(BF16), mlp_w2[layer].astype(BF16))
    return x
```

```python
import functools

import numpy as np
import jax
import jax.numpy as jnp
from jax import lax
from jax.experimental import pallas as pl
from jax.experimental.pallas import tpu as pltpu

F32 = jnp.float32
BF16 = jnp.bfloat16
HI = lax.Precision.HIGHEST

EPS = 1e-6
NEG = -1e30
LOG2E = 1.4426950408889634

VMEM_LIMIT_BYTES = 56 * 1024 * 1024
LANES = 128

GLA_HEADS = 4
GLA_DK = 128
GLA_DV = 256
GLA_RANK = 16
GLA_GATE_TAU = 16.0
CHUNK = 64
ML_CHUNK = 128
REC_BLOCK = 256

NSA_HEADS = 16
NSA_GROUPS = 4
NSA_HPG = 4
NSA_DH = 64
CMP_LEN = 32
CMP_STRIDE = 16
SEL_BLOCK = 64
N_SELECT = 16
WINDOW = 512
NSA_TQ = 256
NSA_TK = 256
NSA_VPAD = 80
PAST_END = 1 << 24

ML_HEADS = 4
ML_DQK = 256
ML_DV = 512
ML_CONV = 4


def _cparams(sem):
    return pltpu.CompilerParams(dimension_semantics=sem, vmem_limit_bytes=VMEM_LIMIT_BYTES)


def _sigmoid(x):
    return 1.0 / (1.0 + jnp.exp(-x))


def _silu(x):
    return x * _sigmoid(x)


def _log_sigmoid(x):
    return jnp.minimum(x, 0.0) - jnp.log(1.0 + jnp.exp(-jnp.abs(x)))


def _dot(a, b, precision=None):
    return jnp.dot(a, b, preferred_element_type=F32, precision=precision)


def _dot_nt(a, b, precision=None):
    return lax.dot_general(a, b, (((1,), (1,)), ((), ())), preferred_element_type=F32, precision=precision)


def _rms(x):
    return x * lax.rsqrt(jnp.mean(x * x, axis=-1, keepdims=True) + EPS)


def _mod_kernel(c_ref, w_ref, b_ref, o_ref):
    a = _silu(c_ref[...]).astype(BF16)
    o_ref[...] = _dot(a, w_ref[...].astype(BF16)) + b_ref[...]


def _ada_mod(c, w, b):
    nl, d, d3 = w.shape
    bsz = c.shape[0]
    tn = 768
    return pl.pallas_call(
        _mod_kernel,
        grid=(nl, d3 // tn),
        in_specs=[pl.BlockSpec((bsz, d), lambda l, j: (0, 0)),
                  pl.BlockSpec((None, d, tn), lambda l, j: (l, 0, j)),
                  pl.BlockSpec((None, 1, tn), lambda l, j: (l, 0, j))],
        out_specs=pl.BlockSpec((None, bsz, tn), lambda l, j: (l, 0, j)),
        out_shape=jax.ShapeDtypeStruct((nl, bsz, d3), F32),
        compiler_params=_cparams(("parallel", "parallel")),
        name="ada_mod",
    )(c, w, b.reshape(nl, 1, d3))


def _modulated_norm(x_ref, g_ref, sc_ref, sh_ref):
    x = x_ref[...]
    gain = g_ref[...] * (1.0 + sc_ref[...])
    inv = lax.rsqrt(jnp.mean(x * x, axis=-1, keepdims=True) + EPS)
    return ((x * inv) * gain + sh_ref[...]).astype(BF16)


def _proj_kernel(x_ref, g_ref, sc_ref, sh_ref, w_ref, o_ref, h_ref):
    i = pl.program_id(2)

    @pl.when(pl.program_id(1) == 0)
    def _():
        h_ref[i] = _modulated_norm(x_ref, g_ref, sc_ref, sh_ref)

    o_ref[...] = _dot(h_ref[i], w_ref[...])


def _norm_proj(x, g, scale, shift, w, tm, tn):
    bsz, s, d = x.shape
    n = w.shape[1]
    rows = s // tm
    x_spec = pl.BlockSpec((None, tm, d), lambda b, j, i: (b, jnp.where(j == 0, i, rows - 1), 0))
    return pl.pallas_call(
        _proj_kernel,
        grid=(bsz, n // tn, rows),
        in_specs=[x_spec,
                  pl.BlockSpec((1, d), lambda b, j, i: (0, 0)),
                  pl.BlockSpec((None, 1, d), lambda b, j, i: (b, 0, 0)),
                  pl.BlockSpec((None, 1, d), lambda b, j, i: (b, 0, 0)),
                  pl.BlockSpec((d, tn), lambda b, j, i: (0, j))],
        out_specs=pl.BlockSpec((None, tm, tn), lambda b, j, i: (b, i, j)),
        out_shape=jax.ShapeDtypeStruct((bsz, s, n), F32),
        scratch_shapes=[pltpu.VMEM((rows, tm, d), BF16)],
        compiler_params=_cparams(("parallel", "arbitrary", "arbitrary")),
        name="norm_proj",
    )(x, g.reshape(1, d), scale, shift, w)


def _mlp_kernel(x_ref, g_ref, sc_ref, sh_ref, gate_ref, w1_ref, w2_ref, o_ref, h_ref, acc_ref):
    f = pl.program_id(2)

    @pl.when(f == 0)
    def _():
        h_ref[...] = _modulated_norm(x_ref, g_ref, sc_ref, sh_ref)
        acc_ref[...] = jnp.zeros_like(acc_ref)

    u = jnp.maximum(_dot(h_ref[...], w1_ref[...]), 0.0)
    acc_ref[...] += _dot((u * u).astype(BF16), w2_ref[...])

    @pl.when(f == pl.num_programs(2) - 1)
    def _():
        o_ref[...] = x_ref[...] + gate_ref[...] * acc_ref[...]


def _mlp_sublayer(x, g, scale, shift, gate, w1, w2, layer, tm=512, tf=1024):
    bsz, s, d = x.shape
    ff = w1.shape[2]
    vec = pl.BlockSpec((None, 1, d), lambda b, i, f: (b, 0, 0))
    return pl.pallas_call(
        _mlp_kernel,
        grid=(bsz, s // tm, ff // tf),
        in_specs=[pl.BlockSpec((None, tm, d), lambda b, i, f: (b, i, 0)),
                  pl.BlockSpec((1, d), lambda b, i, f: (0, 0)),
                  vec, vec, vec,
                  pl.BlockSpec((None, d, tf), lambda b, i, f: (layer, 0, f)),
                  pl.BlockSpec((None, tf, d), lambda b, i, f: (layer, f, 0))],
        out_specs=pl.BlockSpec((None, tm, d), lambda b, i, f: (b, i, 0)),
        out_shape=jax.ShapeDtypeStruct((bsz, s, d), F32),
        scratch_shapes=[pltpu.VMEM((tm, d), BF16), pltpu.VMEM((tm, d), F32)],
        compiler_params=_cparams(("parallel", "parallel", "arbitrary")),
        name="mlp_sublayer",
    )(x, g.reshape(1, d), scale, shift, gate, w1, w2)


def _outproj_kernel(n_in, *refs):
    a_refs, w_refs = refs[:n_in], refs[n_in:2 * n_in]
    x_ref, gate_ref, o_ref = refs[2 * n_in:]
    y = _dot(a_refs[0][...], w_refs[0][...])
    for a_ref, w_ref in zip(a_refs[1:], w_refs[1:]):
        y += _dot(a_ref[...], w_ref[...])
    o_ref[...] = x_ref[...] + gate_ref[...] * y


def _out_proj(acts, ws, x, gate, tm=512):
    bsz, s, d = x.shape
    n_in = len(acts)
    in_specs = [pl.BlockSpec((None, tm, a.shape[-1]), lambda b, i: (b, i, 0)) for a in acts]
    in_specs += [pl.BlockSpec(w.shape, lambda b, i: (0, 0)) for w in ws]
    in_specs += [pl.BlockSpec((None, tm, d), lambda b, i: (b, i, 0)),
                 pl.BlockSpec((None, 1, d), lambda b, i: (b, 0, 0))]
    return pl.pallas_call(
        functools.partial(_outproj_kernel, n_in),
        grid=(bsz, s // tm),
        in_specs=in_specs,
        out_specs=pl.BlockSpec((None, tm, d), lambda b, i: (b, i, 0)),
        out_shape=jax.ShapeDtypeStruct((bsz, s, d), F32),
        compiler_params=_cparams(("parallel", "parallel")),
        name="out_proj",
    )(*acts, *ws, x, gate)


def _tri(n, upper=False):
    r = lax.broadcasted_iota(jnp.int32, (n, n), 0)
    c = lax.broadcasted_iota(jnp.int32, (n, n), 1)
    return (c >= r) if upper else (c <= r)


def _gla_kernel(q_ref, k_ref, v_ref, r_ref, zg_ref, wg_ref, bg_ref, ng_ref, o_ref, st_ref):
    @pl.when(pl.program_id(1) == 0)
    def _():
        st_ref[...] = jnp.zeros_like(st_ref)

    tril = _tri(CHUNK)
    tril_f = tril.astype(F32)
    for c in range(REC_BLOCK // CHUNK):
        rows = slice(c * CHUNK, (c + 1) * CHUNK)
        pre = _dot(zg_ref[rows, :].astype(BF16), wg_ref[...]) + bg_ref[...]
        log_a = _log_sigmoid(pre) / GLA_GATE_TAU
        cum = _dot(tril_f, log_a, precision=HI)
        total = cum[CHUNK - 1:CHUNK, :]
        q_dec = (q_ref[rows, :] * GLA_DK ** -0.5) * jnp.exp(cum)
        kk = k_ref[rows, :]
        k_dec = kk * jnp.exp(-cum)
        k_end = kk * jnp.exp(total - cum)
        decay = jnp.exp(total)
        for h in range(GLA_HEADS):
            ks = slice(h * GLA_DK, (h + 1) * GLA_DK)
            vs = slice(h * GLA_DV, (h + 1) * GLA_DV)
            qh = q_dec[:, ks].astype(BF16)
            vh = v_ref[rows, vs]
            att = jnp.where(tril, _dot_nt(qh, k_dec[:, ks].astype(BF16)), 0.0)
            st = st_ref[h]
            o = _dot(att.astype(BF16), vh.astype(BF16)) + _dot_nt(qh, st.astype(BF16))
            st_ref[h] = st * decay[:, ks] + _dot(vh.T.astype(BF16), k_end[:, ks].astype(BF16))
            rh = r_ref[rows, vs]
            o_ref[rows, vs] = ((_rms(o) * ng_ref[...]) * _silu(rh)).astype(o_ref.dtype)


def _gla(p, w_gate, b_gate, norm_g, col):
    bsz, s, _ = p.shape
    t = REC_BLOCK

    def piece(name):
        width, idx = col[name]
        return pl.BlockSpec((None, t, width), lambda b, i, idx=idx: (b, i, idx))

    const = lambda shape: pl.BlockSpec(shape, lambda b, i: (0,) * len(shape))
    hk = GLA_HEADS * GLA_DK
    return pl.pallas_call(
        _gla_kernel,
        grid=(bsz, s // t),
        in_specs=[piece("gla_q"), piece("gla_k"), piece("gla_v"), piece("gla_r"), piece("gates"),
                  const((LANES, hk)), const((1, hk)), const((1, GLA_DV))],
        out_specs=pl.BlockSpec((None, t, GLA_HEADS * GLA_DV), lambda b, i: (b, i, 0)),
        out_shape=jax.ShapeDtypeStruct((bsz, s, GLA_HEADS * GLA_DV), BF16),
        scratch_shapes=[pltpu.VMEM((GLA_HEADS, GLA_DV, GLA_DK), F32)],
        compiler_params=_cparams(("parallel", "arbitrary")),
        name="gla",
    )(p, p, p, p, p, w_gate, b_gate.reshape(1, hk), norm_g.reshape(1, GLA_DV))


def _mlstm_kernel(q_ref, k_ref, v_ref, og_ref, gc_ref, wq_ref, wk_ref, bc_ref, br_ref, ng_ref,
                  o_ref, c_ref, n_ref, m_ref, qx_ref, kx_ref):
    t = REC_BLOCK

    @pl.when(pl.program_id(1) == 0)
    def _():
        c_ref[...] = jnp.zeros_like(c_ref)
        n_ref[...] = jnp.zeros_like(n_ref)
        m_ref[...] = jnp.zeros_like(m_ref)
        qx_ref[0:8, :] = jnp.zeros((8, qx_ref.shape[1]), F32)
        kx_ref[0:8, :] = jnp.zeros((8, kx_ref.shape[1]), F32)

    def conv_silu(x_ref, xx_ref, w_ref):
        xx_ref[8:8 + t, :] = x_ref[...]
        y = xx_ref[8:8 + t, :] * w_ref[ML_CONV - 1:ML_CONV, :]
        for back in range(1, ML_CONV):
            y += xx_ref[8 - back:8 - back + t, :] * w_ref[ML_CONV - 1 - back:ML_CONV - back, :]
        xx_ref[0:8, :] = xx_ref[t:t + 8, :]
        return _silu(y)

    q_all = conv_silu(q_ref, qx_ref, wq_ref) * ML_DQK ** -0.5
    k_all = conv_silu(k_ref, kx_ref, wk_ref)

    tril = _tri(ML_CHUNK)
    tril_f = tril.astype(F32)
    triu_f = _tri(ML_CHUNK, upper=True).astype(F32)
    lane = lax.broadcasted_iota(jnp.int32, (ML_CHUNK, LANES), 1)
    is_f_col = (lane >= ML_HEADS) & (lane < 2 * ML_HEADS)
    sub = lax.broadcasted_iota(jnp.int32, (8, ML_CHUNK), 0)
    is_f_row = sub >= ML_HEADS
    gates_by_row = gc_ref[...].T[0:2 * ML_HEADS, :]

    for c in range(t // ML_CHUNK):
        rows = slice(c * ML_CHUNK, (c + 1) * ML_CHUNK)
        gcol = gc_ref[rows, :] + bc_ref[...]
        gcol = jnp.where(is_f_col, _log_sigmoid(gcol), gcol)
        bcol = _dot(tril_f, gcol, precision=HI)
        grow = gates_by_row[:, rows] + br_ref[:, 0:1]
        grow = jnp.where(is_f_row, _log_sigmoid(grow), grow)
        brow = _dot(grow, triu_f, precision=HI)
        for h in range(ML_HEADS):
            qs = slice(h * ML_DQK, (h + 1) * ML_DQK)
            vs = slice(h * ML_DV, (h + 1) * ML_DV)
            b_col = bcol[:, ML_HEADS + h:ML_HEADS + h + 1]
            b_row = brow[ML_HEADS + h:ML_HEADS + h + 1, :]
            li_col = gcol[:, h:h + 1]
            li_row = grow[h:h + 1, :]
            m_prev = m_ref[h:h + 1, 0:1]
            log_d = jnp.where(tril, b_col - b_row + li_row, -jnp.inf)
            log_inter = b_col + m_prev
            m_comb = jnp.maximum(log_inter, jnp.max(log_d, axis=-1, keepdims=True))
            d = jnp.exp(log_d - m_comb)
            w_inter = jnp.exp(log_inter - m_comb)
            qh = q_all[rows, qs]
            kh = k_all[rows, qs]
            qb = qh.astype(BF16)
            vb = v_ref[rows, vs].astype(BF16)
            sc = _dot_nt(qb, kh.astype(BF16)) * d
            cm = c_ref[h]
            nv = n_ref[h]
            num = _dot(sc.astype(BF16), vb) + w_inter * _dot(qb, cm.astype(BF16))
            den = jnp.sum(sc, axis=-1, keepdims=True) + w_inter * jnp.sum(qh * nv, axis=-1, keepdims=True)
            hh = num / jnp.maximum(jnp.abs(den), jnp.exp(-m_comb))
            total = b_col[ML_CHUNK - 1:ML_CHUNK, :]
            log_w = total - b_col + li_col
            m_new = jnp.maximum(total + m_prev, jnp.max(log_w, axis=0, keepdims=True))
            decay = jnp.exp(total + m_prev - m_new)
            kw = kh * jnp.exp(log_w - m_new)
            c_ref[h] = decay * cm + _dot(kw.T.astype(BF16), vb)
            n_ref[h] = decay * nv + jnp.sum(kw, axis=0, keepdims=True)
            m_ref[h:h + 1, :] = jnp.broadcast_to(m_new, (1, LANES))
            og = og_ref[rows, vs]
            o_ref[rows, vs] = ((_rms(hh) * ng_ref[...]) * _sigmoid(og)).astype(o_ref.dtype)


def _mlstm(p, conv_w, bias_col, bias_row, norm_g, col):
    bsz, s, _ = p.shape
    t = REC_BLOCK
    hq = ML_HEADS * ML_DQK
    hv = ML_HEADS * ML_DV

    def piece(name):
        width, idx = col[name]
        return pl.BlockSpec((None, t, width), lambda b, i, idx=idx: (b, i, idx))

    const = lambda shape: pl.BlockSpec(shape, lambda b, i: (0,) * len(shape))
    return pl.pallas_call(
        _mlstm_kernel,
        grid=(bsz, s // t),
        in_specs=[piece("ml_q"), piece("ml_k"), piece("ml_v"), piece("ml_og"), piece("gates"),
                  pl.BlockSpec((ML_CONV, hq), lambda b, i: (0, 0)),
                  pl.BlockSpec((ML_CONV, hq), lambda b, i: (0, 1)),
                  const((1, LANES)), const((8, LANES)), const((1, ML_DV))],
        out_specs=pl.BlockSpec((None, t, hv), lambda b, i: (b, i, 0)),
        out_shape=jax.ShapeDtypeStruct((bsz, s, hv), BF16),
        scratch_shapes=[pltpu.VMEM((ML_HEADS, ML_DQK, ML_DV), F32),
                        pltpu.VMEM((ML_HEADS, 1, ML_DQK), F32),
                        pltpu.VMEM((8, LANES), F32),
                        pltpu.VMEM((t + 8, hq), F32),
                        pltpu.VMEM((t + 8, hq), F32)],
        compiler_params=_cparams(("parallel", "arbitrary")),
        name="mlstm",
    )(p, p, p, p, p, conv_w, conv_w, bias_col, bias_row, norm_g.reshape(1, ML_DV))


def _split_bf16(x):
    hi = x.astype(BF16)
    return hi, (x - hi.astype(F32)).astype(BF16)


def _seg_norm(x, seg_ref, segt_ref):
    if seg_ref.dtype == BF16:
        seg_sum = lambda v, m_ref: sum(_dot(part, m_ref[...]) for part in _split_bf16(v))
    else:
        seg_sum = lambda v, m_ref: _dot(v, m_ref[...], precision=HI)
    ss = seg_sum(x * x, seg_ref) * (1.0 / NSA_DH)
    return x * seg_sum(lax.rsqrt(ss + EPS), segt_ref)


def _nsa_prep_kernel(q_ref, ks_ref, vs_ref, kw_ref, vw_ref, qg_ref, kg_ref, segk_ref, segkt_ref,
                     qnt_ref, ksg_ref, vst_ref, kwg_ref, vwt_ref):
    t = q_ref.shape[0]
    qt = q_ref[...].T
    for h in range(NSA_HEADS):
        hs = slice(h * NSA_DH, (h + 1) * NSA_DH)
        blk = qt[hs, :]
        inv = lax.rsqrt(jnp.mean(blk * blk, axis=0, keepdims=True) + EPS)
        qnt_ref[hs, :] = ((blk * inv) * qg_ref[hs, :]).astype(BF16)
    ones = jnp.ones((NSA_VPAD - NSA_DH, t), BF16)
    for k_ref, v_ref, kg_out, vt_out in ((ks_ref, vs_ref, ksg_ref, vst_ref), (kw_ref, vw_ref, kwg_ref, vwt_ref)):
        kn = (_seg_norm(k_ref[...], segk_ref, segkt_ref) * kg_ref[...]).astype(BF16)
        vt = v_ref[...].T.astype(BF16)
        for g in range(NSA_GROUPS):
            gs = slice(g * NSA_DH, (g + 1) * NSA_DH)
            kg_out[g] = kn[:, gs]
            vt_out[g, 0:NSA_DH, :] = vt[gs, :]
            vt_out[g, NSA_DH:NSA_VPAD, :] = ones


def _seg_matrices(width, dtype):
    seg = (np.arange(width)[:, None] // NSA_DH == np.arange(LANES)[None, :]).astype(np.float32)
    return jnp.asarray(seg, dtype=dtype), jnp.asarray(seg.T, dtype=dtype)


def _nsa_prep(p, q_gain, k_gain, col, t=512):
    bsz, s, _ = p.shape
    hq = NSA_HEADS * NSA_DH
    kv = NSA_GROUPS * NSA_DH

    def piece(name):
        width, idx = col[name]
        return pl.BlockSpec((None, t, width), lambda b, i, idx=idx: (b, i, idx))

    const = lambda shape: pl.BlockSpec(shape, lambda b, i: (0,) * len(shape))
    segk, segkt = _seg_matrices(kv, BF16)
    keys = pl.BlockSpec((None, NSA_GROUPS, t, NSA_DH), lambda b, i: (b, 0, i, 0))
    vals = pl.BlockSpec((None, NSA_GROUPS, NSA_VPAD, t), lambda b, i: (b, 0, 0, i))
    shp = lambda *dims, dt=BF16: jax.ShapeDtypeStruct(dims, dt)
    keys_shape = shp(bsz, NSA_GROUPS, s, NSA_DH)
    vals_shape = shp(bsz, NSA_GROUPS, NSA_VPAD, s)
    return pl.pallas_call(
        _nsa_prep_kernel,
        grid=(bsz, s // t),
        in_specs=[piece("nsa_q"), piece("nsa_ks"), piece("nsa_vs"), piece("nsa_kw"), piece("nsa_vw"),
                  const((hq, 1)), const((1, kv)), const((kv, LANES)), const((LANES, kv))],
        out_specs=[pl.BlockSpec((None, hq, t), lambda b, i: (b, 0, i)), keys, vals, keys, vals],
        out_shape=[shp(bsz, hq, s), keys_shape, vals_shape, keys_shape, vals_shape],
        compiler_params=_cparams(("parallel", "parallel")),
        name="nsa_prep",
    )(p, p, p, p, p, q_gain, k_gain, segk, segkt)


def _compress_kernel(kc0_ref, kc1_ref, vc0_ref, vc1_ref, posk_ref, posv_ref, kw1a_ref, kw1b_ref, kw2_ref,
                     vw1a_ref, vw1b_ref, vw2_ref, kg_ref, segk_ref, segkt_ref, kcg_ref, vct_ref):
    def hidden(x_refs, pos_ref, w1a_ref, w1b_ref):
        nsub = x_refs[0].shape[0] // CMP_STRIDE
        first = second = None
        for l in range(CMP_STRIDE):
            x = jnp.concatenate([r[pl.ds(l, nsub, stride=CMP_STRIDE), :] for r in x_refs], axis=1)
            a = _dot((x + pos_ref[l:l + 1, :]).astype(BF16), w1a_ref[l])
            b = _dot((x + pos_ref[CMP_STRIDE + l:CMP_STRIDE + l + 1, :]).astype(BF16), w1b_ref[l])
            first = a if first is None else first + a
            second = b if second is None else second + b
        row = lax.broadcasted_iota(jnp.int32, second.shape, 0)
        nxt = jnp.where(row == nsub - 1, 0.0, pltpu.roll(second, nsub - 1, 0))
        return _silu(first + nxt)

    kc = _dot(hidden((kc0_ref, kc1_ref), posk_ref, kw1a_ref, kw1b_ref).astype(BF16), kw2_ref[...])
    kn = (_seg_norm(kc, segk_ref, segkt_ref) * kg_ref[...]).astype(BF16)
    vt = _dot(vw2_ref[...], hidden((vc0_ref, vc1_ref), posv_ref, vw1a_ref, vw1b_ref).T.astype(BF16)).astype(BF16)
    for g in range(NSA_GROUPS):
        gs = slice(g * NSA_DH, (g + 1) * NSA_DH)
        kcg_ref[g] = kn[:, gs]
        vct_ref[g] = vt[gs, :]


def _expand_w1(w1):
    w = w1.reshape(2, CMP_STRIDE, NSA_DH, NSA_DH)
    eye = jnp.eye(NSA_GROUPS, dtype=w1.dtype)
    big = jnp.einsum('hlde,gk->hlgdke', w, eye)
    big = big.reshape(2, CMP_STRIDE, NSA_GROUPS * NSA_DH, NSA_GROUPS * NSA_DH).astype(BF16)
    return big[0], big[1]


def _expand_pos(pos):
    return jnp.tile(pos, (1, NSA_GROUPS))


def _block_diag(w2):
    return jnp.kron(jnp.eye(NSA_GROUPS, dtype=w2.dtype), w2).astype(BF16)


def _compress(p, pos_k, pos_v, ck_w1, ck_w2, cv_w1, cv_w2, k_gain, col):
    bsz, s, _ = p.shape
    kv = NSA_GROUPS * NSA_DH
    nsub = s // CMP_STRIDE
    segk, segkt = _seg_matrices(kv, F32)
    kw1a, kw1b = _expand_w1(ck_w1)
    vw1a, vw1b = _expand_w1(cv_w1)
    const = lambda shape: pl.BlockSpec(shape, lambda b: (0,) * len(shape))

    def halves(name):
        width, idx = col[name]
        assert width == 2 * LANES
        return [pl.BlockSpec((None, s, LANES), lambda b, j=2 * idx + h: (b, 0, j)) for h in range(2)]

    w1_spec = const((CMP_STRIDE, kv, kv))
    return pl.pallas_call(
        _compress_kernel,
        grid=(bsz,),
        in_specs=[*halves("nsa_kc"), *halves("nsa_vc"), const((CMP_LEN, kv)), const((CMP_LEN, kv)),
                  w1_spec, w1_spec, const((kv, kv)),
                  w1_spec, w1_spec, const((kv, kv)),
                  const((1, kv)), const((kv, LANES)), const((LANES, kv))],
        out_specs=[pl.BlockSpec((None, NSA_GROUPS, nsub, NSA_DH), lambda b: (b, 0, 0, 0)),
                   pl.BlockSpec((None, NSA_GROUPS, NSA_DH, nsub), lambda b: (b, 0, 0, 0))],
        out_shape=[jax.ShapeDtypeStruct((bsz, NSA_GROUPS, nsub, NSA_DH), BF16),
                   jax.ShapeDtypeStruct((bsz, NSA_GROUPS, NSA_DH, nsub), BF16)],
        compiler_params=_cparams(("parallel",)),
        name="nsa_compress",
    )(p, p, p, p, _expand_pos(pos_k), _expand_pos(pos_v),
      kw1a, kw1b, _block_diag(ck_w2), vw1a, vw1b, _block_diag(cv_w2).T, k_gain, segk, segkt)


def _nsa_attn_kernel(qt_ref, gt_ref, kcg_ref, vct_ref, ksg_ref, vst_ref, kwg_ref, vwt_ref, ovt_ref,
                     o_ref, acc_ref, sel_ref, s_ref, p_ref, pv_ref):
    tq, tk = NSA_TQ, NSA_TK
    q0 = pl.program_id(1) * tq
    ncmp = kcg_ref.shape[1]
    nsel = ovt_ref.shape[0]
    cols4 = NSA_HPG * tq

    t_cmp = q0 + lax.broadcasted_iota(jnp.int32, (ncmp, cols4), 1) % tq
    n_idx = lax.broadcasted_iota(jnp.int32, (ncmp, cols4), 0)
    cmp_valid = n_idx * CMP_STRIDE + (CMP_LEN - 1) <= t_cmp

    j_idx = lax.broadcasted_iota(jnp.int32, (nsel, tq), 0)
    cur = (q0 + lax.broadcasted_iota(jnp.int32, (nsel, tq), 1)) // SEL_BLOCK
    forced = (j_idx == 0) | (j_idx == cur) | (j_idx == cur - 1)
    causal_blk = j_idx <= cur

    row_blk = lax.broadcasted_iota(jnp.int32, (SEL_BLOCK, tq), 0)
    t_row = q0 + lax.broadcasted_iota(jnp.int32, (1, tq), 1)

    gates_t = _sigmoid(gt_ref[...]).T

    def add_gated(g, branch, o4):
        for i in range(NSA_HPG):
            h = g * NSA_HPG + i
            hs = slice(h * NSA_DH, (h + 1) * NSA_DH)
            gc = GLA_RANK + 3 * h + branch
            term = gates_t[gc:gc + 1, :] * o4[:, i * tq:(i + 1) * tq]
            acc_ref[hs, :] = term if branch == 0 else acc_ref[hs, :] + term

    def q_group(g):
        return jnp.concatenate([qt_ref[(g * NSA_HPG + i) * NSA_DH:(g * NSA_HPG + i + 1) * NSA_DH, :]
                                for i in range(NSA_HPG)], axis=1)

    def attend(branch, k_ref, vt_ref, first_tile, n_tiles, mask_fn):
        last = first_tile + n_tiles - 1

        def scores(g, kt):
            return _dot(k_ref[g, pl.ds(pl.multiple_of(kt * tk, tk), tk), :], q_group(g))

        def weighted_values(g, kt, p):
            return _dot(vt_ref[g, :, pl.ds(pl.multiple_of(kt * tk, tk), tk)], p)

        def step(kt, cur, ms):
            nxt = 1 - cur
            k0 = jnp.where(kt <= last, kt * tk, PAST_END)
            out = []
            for g in range(NSA_GROUPS):
                m = ms[g]
                s_ref[g, nxt] = scores(g, jnp.minimum(kt + 1, last))
                acc = pv_ref[g] + weighted_values(g, jnp.clip(kt - 1, first_tile, last), p_ref[g, nxt])
                ok = mask_fn(g, jnp.minimum(kt, last), k0)
                m_new = []
                for c in range(cols4 // LANES):
                    cs = slice(c * LANES, (c + 1) * LANES)
                    qs = slice((c % (tq // LANES)) * LANES, (c % (tq // LANES) + 1) * LANES)
                    s = jnp.where(ok[:, qs], s_ref[g, cur, :, cs], NEG)
                    m_c = jnp.maximum(m[:, cs], jnp.max(s, axis=0, keepdims=True))
                    p_ref[g, cur, :, cs] = jnp.exp2(s - m_c).astype(BF16)
                    m_new.append(m_c)
                m_new = jnp.concatenate(m_new, axis=1)
                pv_ref[g] = acc * jnp.exp2(m - m_new)
                out.append(m_new)
            return tuple(out)

        for g in range(NSA_GROUPS):
            s_ref[g, 0] = scores(g, first_tile)
            p_ref[g, 1] = jnp.zeros(p_ref.shape[2:], BF16)
        pv_ref[...] = jnp.zeros_like(pv_ref)
        n_pairs = (n_tiles + 1) // 2

        def body(j, ms):
            kt = first_tile + 2 * j
            return step(kt + 1, 1, step(kt, 0, ms))

        lax.fori_loop(0, n_pairs, body, tuple(jnp.full((1, cols4), NEG, F32) for _ in range(NSA_GROUPS)))
        final_tile = jnp.minimum(first_tile + 2 * n_pairs - 1, last)
        for g in range(NSA_GROUPS):
            acc = pv_ref[g] + weighted_values(g, final_tile, p_ref[g, 1])
            add_gated(g, branch, acc[0:NSA_DH, :] / acc[NSA_DH:NSA_DH + 1, :])

    last_tile = (q0 + tq - 1) // tk
    last_blk = (q0 + tq - 1) // SEL_BLOCK

    def sel_mask(g, kt, k0):
        per_tile = tk // SEL_BLOCK
        blocks = []
        for j in range(per_tile):
            picked = sel_ref[g, pl.ds(kt * per_tile + j, 1), :] > 0.5
            limit = jnp.where(picked, t_row, -1) - (k0 + j * SEL_BLOCK)
            blocks.append(row_blk <= limit)
        return jnp.concatenate(blocks, axis=0)

    def window_branch(g):
        span = WINDOW + tk
        k_start = pl.multiple_of(jnp.maximum(last_tile - WINDOW // tk, 0) * tk, tk)
        s = _dot(kwg_ref[g, pl.ds(k_start, span), :], q_group(g))
        row_w = lax.broadcasted_iota(jnp.int32, (span, tq), 0)
        ahead = t_row - k_start
        ok = (row_w <= ahead) & (row_w > ahead - WINDOW)
        probs = []
        for c in range(cols4 // LANES):
            qs = slice((c % (tq // LANES)) * LANES, (c % (tq // LANES) + 1) * LANES)
            sc = jnp.where(ok[:, qs], s[:, c * LANES:(c + 1) * LANES], NEG)
            probs.append(jnp.exp2(sc - jnp.max(sc, axis=0, keepdims=True)).astype(BF16))
        acc = _dot(vwt_ref[g, :, pl.ds(k_start, span)], jnp.concatenate(probs, axis=1))
        add_gated(g, 2, acc[0:NSA_DH, :] / acc[NSA_DH:NSA_DH + 1, :])

    importance = []
    for g in range(NSA_GROUPS):
        lg = jnp.where(cmp_valid, _dot(kcg_ref[g], q_group(g)), NEG)
        ex = jnp.exp2(lg - jnp.max(lg, axis=0, keepdims=True))
        p_cmp = jnp.where(cmp_valid, ex / jnp.sum(ex, axis=0, keepdims=True), 0.0)
        add_gated(g, 0, _dot(vct_ref[g], p_cmp.astype(BF16)))
        p_grp = p_cmp[:, 0:tq] + p_cmp[:, tq:2 * tq] + p_cmp[:, 2 * tq:3 * tq] + p_cmp[:, 3 * tq:4 * tq]
        importance.append(sum(_dot(ovt_ref[...], part) for part in _split_bf16(p_grp)))
        window_branch(g)

    @pl.when(last_blk < N_SELECT)
    def _():
        for g in range(NSA_GROUPS):
            sel_ref[g] = causal_blk.astype(F32)

    @pl.when(last_blk >= N_SELECT)
    def _():
        for g in range(NSA_GROUPS):
            val = jnp.where(forced, jnp.inf, jnp.where(causal_blk, importance[g], -jnp.inf))
            rank = jnp.zeros((nsel, tq), F32)
            for i in range(nsel):
                vi = val[i:i + 1, :]
                rank += ((vi > val) | ((vi == val) & (j_idx > i))).astype(F32)
            sel_ref[g] = (rank < N_SELECT).astype(F32)

    attend(1, ksg_ref, vst_ref, 0, last_tile + 1, sel_mask)
    o_ref[...] = acc_ref[...].T.astype(o_ref.dtype)


def _overlap_t(s):
    ncmp = s // CMP_STRIDE
    nsel = s // SEL_BLOCK
    c_start = np.arange(ncmp) * CMP_STRIDE
    s_start = np.arange(nsel) * SEL_BLOCK
    overlap_t = ((c_start[None, :] <= s_start[:, None] + SEL_BLOCK - 1)
                 & (c_start[None, :] + CMP_LEN - 1 >= s_start[:, None])).astype(np.float32)
    overlap_t[:, ncmp - 1] = 0.0
    return jnp.asarray(overlap_t, dtype=BF16)


def _nsa_attention(p, qnt, kcg, vct, ksg, vst, kwg, vwt, col):
    bsz, hq, s = qnt.shape
    ncmp = s // CMP_STRIDE
    nsel = s // SEL_BLOCK
    gw, gidx = col["gates"]
    per_row = lambda shape: pl.BlockSpec((None,) + shape, lambda b, i: (b,) + (0,) * len(shape))
    keys = per_row((NSA_GROUPS, s, NSA_DH))
    vals = per_row((NSA_GROUPS, NSA_VPAD, s))
    return pl.pallas_call(
        _nsa_attn_kernel,
        grid=(bsz, s // NSA_TQ),
        in_specs=[pl.BlockSpec((None, hq, NSA_TQ), lambda b, i: (b, 0, i)),
                  pl.BlockSpec((None, NSA_TQ, gw), lambda b, i: (b, i, gidx)),
                  per_row((NSA_GROUPS, ncmp, NSA_DH)), per_row((NSA_GROUPS, NSA_DH, ncmp)),
                  keys, vals, keys, vals,
                  pl.BlockSpec((nsel, ncmp), lambda b, i: (0, 0))],
        out_specs=pl.BlockSpec((None, NSA_TQ, hq), lambda b, i: (b, i, 0)),
        out_shape=jax.ShapeDtypeStruct((bsz, s, hq), BF16),
        scratch_shapes=[pltpu.VMEM((hq, NSA_TQ), F32), pltpu.VMEM((NSA_GROUPS, nsel, NSA_TQ), F32),
                        pltpu.VMEM((NSA_GROUPS, 2, NSA_TK, NSA_HPG * NSA_TQ), F32),
                        pltpu.VMEM((NSA_GROUPS, 2, NSA_TK, NSA_HPG * NSA_TQ), BF16),
                        pltpu.VMEM((NSA_GROUPS, NSA_VPAD, NSA_HPG * NSA_TQ), F32)],
        compiler_params=_cparams(("parallel", "arbitrary")),
        name="nsa_attention",
    )(qnt, p, kcg, vct, ksg, vst, kwg, vwt, _overlap_t(s))


def _layout(pieces, tile):
    col, off, mats = {}, 0, []
    for name, w in pieces:
        width = w.shape[1]
        assert off % width == 0, (name, off, width)
        col[name] = (width, off // width)
        mats.append(w)
        off += width
    pad = -off % tile
    if pad:
        mats.append(jnp.zeros((mats[0].shape[0], pad), mats[0].dtype))
    return jnp.concatenate(mats, axis=1).astype(BF16), col


def _pad_cols(w, width):
    return jnp.pad(w, ((0, 0), (0, width - w.shape[1])))


AB_TILE = 1920
ML_TILE = 1280


def _ab_layout(w_in):
    hk = GLA_HEADS * GLA_DK
    hv = GLA_HEADS * GLA_DV
    kv = NSA_GROUPS * NSA_DH
    sizes = [hk, hk, hv, hv, GLA_RANK, NSA_HEADS * NSA_DH] + [kv] * 6 + [3 * NSA_HEADS]
    cuts = np.cumsum(sizes)[:-1].tolist()
    gq, gk, gv, gr, gz, nq, kc, vc, ks, vs, kw, vw, gt = jnp.split(w_in, cuts, axis=1)
    gates = _pad_cols(jnp.concatenate([gz, gt], axis=1), LANES)
    return _layout([("gla_q", gq), ("gla_k", gk), ("gla_v", gv), ("gla_r", gr), ("nsa_q", nq),
                    ("nsa_kc", kc), ("nsa_vc", vc), ("nsa_ks", ks), ("nsa_vs", vs), ("nsa_kw", kw),
                    ("nsa_vw", vw), ("gates", gates)], AB_TILE)


def _ml_layout(w_in):
    hq = ML_HEADS * ML_DQK
    hv = ML_HEADS * ML_DV
    cuts = np.cumsum([hq, hq, hv, hv])[:].tolist()
    q, k, v, og, gates = jnp.split(w_in, cuts, axis=1)
    return _layout([("ml_q", q), ("ml_k", k), ("ml_v", v), ("ml_og", og),
                    ("gates", _pad_cols(gates, LANES))], ML_TILE)


def kernel(x, c, norm_mix_g, mod_mix_w, mod_mix_b, norm_mlp_g, mod_mlp_w, mod_mlp_b, mlp_w1, mlp_w2, ab_w_in, ab_w_out, gla_w_gate, gla_b_gate, gla_norm_g, nsa_q_norm_g, nsa_k_norm_g, nsa_cmp_pos_k, nsa_cmp_pos_v, nsa_cmp_k_w1, nsa_cmp_k_w2, nsa_cmp_v_w1, nsa_cmp_v_w2, ml_w_in, ml_w_out, ml_conv_w, ml_b_i, ml_b_f, ml_norm_g):
    bsz, s, d = x.shape
    depth = norm_mix_g.shape[0]
    mod_mix = _ada_mod(c, mod_mix_w, mod_mix_b)
    mod_mlp = _ada_mod(c, mod_mlp_w, mod_mlp_b)
    mlp_w1_bf16, mlp_w2_bf16 = mlp_w1.astype(BF16), mlp_w2.astype(BF16)

    def split_mod(m):
        return [m[:, None, i * d:(i + 1) * d] for i in range(3)]

    for layer in range(depth):
        shift, scale, gate = split_mod(mod_mix[layer])
        if layer % 2 == 0:
            e = layer // 2
            w_in, col = _ab_layout(ab_w_in[e])
            p = _norm_proj(x, norm_mix_g[layer], scale, shift, w_in, tm=512, tn=AB_TILE)
            w_gate = jnp.pad(gla_w_gate[e], ((0, LANES - GLA_RANK), (0, 0))).astype(BF16)
            o_gla = _gla(p, w_gate, gla_b_gate[e], gla_norm_g[e], col)
            q_gain = (jnp.tile(nsa_q_norm_g[e], NSA_HEADS) * (NSA_DH ** -0.5 * LOG2E)).reshape(-1, 1)
            k_gain = jnp.tile(nsa_k_norm_g[e], NSA_GROUPS).reshape(1, -1)
            qnt, ksg, vst, kwg, vwt = _nsa_prep(p, q_gain, k_gain, col)
            kcg, vct = _compress(p, nsa_cmp_pos_k[e], nsa_cmp_pos_v[e], nsa_cmp_k_w1[e],
                                 nsa_cmp_k_w2[e], nsa_cmp_v_w1[e], nsa_cmp_v_w2[e], k_gain, col)
            o_nsa = _nsa_attention(p, qnt, kcg, vct, ksg, vst, kwg, vwt, col)
            w_out = ab_w_out[e].astype(BF16)
            half = GLA_HEADS * GLA_DV
            x = _out_proj([o_gla, o_nsa], [w_out[:half], w_out[half:]], x, gate)
        else:
            o = layer // 2
            w_in, col = _ml_layout(ml_w_in[o])
            p = _norm_proj(x, norm_mix_g[layer], scale, shift, w_in, tm=512, tn=ML_TILE)
            bias = jnp.concatenate([ml_b_i[o], ml_b_f[o]])
            bias_col = _pad_cols(bias.reshape(1, -1), LANES)
            bias_row = jnp.broadcast_to(bias.reshape(-1, 1), (2 * ML_HEADS, LANES))
            hh = _mlstm(p, ml_conv_w[o], bias_col, bias_row, ml_norm_g[o], col)
            x = _out_proj([hh], [ml_w_out[o].astype(BF16)], x, gate)
        shift, scale, gate = split_mod(mod_mlp[layer])
        x = _mlp_sublayer(x, norm_mlp_g[layer], scale, shift, gate, mlp_w1_bf16, mlp_w2_bf16, layer)
    return x
```

```python
import functools

import numpy as np
import jax
import jax.numpy as jnp
from jax import lax
from jax.experimental import pallas as pl
from jax.experimental.pallas import tpu as pltpu

F32 = jnp.float32
BF16 = jnp.bfloat16
HI = lax.Precision.HIGHEST

EPS = 1e-6
NEG = -1e30
LOG2E = 1.4426950408889634

VMEM_LIMIT_BYTES = 56 * 1024 * 1024
LANES = 128

GLA_HEADS = 4
GLA_DK = 128
GLA_DV = 256
GLA_RANK = 16
GLA_GATE_TAU = 16.0
CHUNK = 64
ML_CHUNK = 128
REC_BLOCK = 256

NSA_HEADS = 16
NSA_GROUPS = 4
NSA_HPG = 4
NSA_DH = 64
CMP_LEN = 32
CMP_STRIDE = 16
SEL_BLOCK = 64
N_SELECT = 16
WINDOW = 512
NSA_TQ = 256
NSA_TK = 256
NSA_VPAD = 80

ML_HEADS = 4
ML_DQK = 256
ML_DV = 512
ML_CONV = 4


def _cparams(sem):
    return pltpu.CompilerParams(dimension_semantics=sem, vmem_limit_bytes=VMEM_LIMIT_BYTES)


def _sigmoid(x):
    return 1.0 / (1.0 + jnp.exp(-x))


def _silu(x):
    return x * _sigmoid(x)


def _log_sigmoid(x):
    return jnp.minimum(x, 0.0) - jnp.log(1.0 + jnp.exp(-jnp.abs(x)))


def _dot(a, b, precision=None):
    return jnp.dot(a, b, preferred_element_type=F32, precision=precision)


def _dot_nt(a, b, precision=None):
    return lax.dot_general(a, b, (((1,), (1,)), ((), ())), preferred_element_type=F32, precision=precision)


def _rms(x):
    return x * lax.rsqrt(jnp.mean(x * x, axis=-1, keepdims=True) + EPS)


def _mod_kernel(c_ref, w_ref, b_ref, o_ref):
    a = _silu(c_ref[...]).astype(BF16)
    o_ref[...] = _dot(a, w_ref[...].astype(BF16)) + b_ref[...]


def _ada_mod(c, w, b):
    nl, d, d3 = w.shape
    bsz = c.shape[0]
    tn = 768
    return pl.pallas_call(
        _mod_kernel,
        grid=(nl, d3 // tn),
        in_specs=[pl.BlockSpec((bsz, d), lambda l, j: (0, 0)),
                  pl.BlockSpec((None, d, tn), lambda l, j: (l, 0, j)),
                  pl.BlockSpec((None, 1, tn), lambda l, j: (l, 0, j))],
        out_specs=pl.BlockSpec((None, bsz, tn), lambda l, j: (l, 0, j)),
        out_shape=jax.ShapeDtypeStruct((nl, bsz, d3), F32),
        compiler_params=_cparams(("parallel", "parallel")),
        name="ada_mod",
    )(c, w, b.reshape(nl, 1, d3))


def _modulated_norm(x_ref, g_ref, sc_ref, sh_ref):
    x = x_ref[...]
    gain = g_ref[...] * (1.0 + sc_ref[...])
    inv = lax.rsqrt(jnp.mean(x * x, axis=-1, keepdims=True) + EPS)
    return ((x * inv) * gain + sh_ref[...]).astype(BF16)


def _proj_kernel(x_ref, g_ref, sc_ref, sh_ref, w_ref, o_ref, h_ref):
    i = pl.program_id(2)

    @pl.when(pl.program_id(1) == 0)
    def _():
        h_ref[i] = _modulated_norm(x_ref, g_ref, sc_ref, sh_ref)

    o_ref[...] = _dot(h_ref[i], w_ref[...])


def _norm_proj(x, g, scale, shift, w, tm, tn):
    bsz, s, d = x.shape
    n = w.shape[1]
    rows = s // tm
    x_spec = pl.BlockSpec((None, tm, d), lambda b, j, i: (b, jnp.where(j == 0, i, rows - 1), 0))
    return pl.pallas_call(
        _proj_kernel,
        grid=(bsz, n // tn, rows),
        in_specs=[x_spec,
                  pl.BlockSpec((1, d), lambda b, j, i: (0, 0)),
                  pl.BlockSpec((None, 1, d), lambda b, j, i: (b, 0, 0)),
                  pl.BlockSpec((None, 1, d), lambda b, j, i: (b, 0, 0)),
                  pl.BlockSpec((d, tn), lambda b, j, i: (0, j))],
        out_specs=pl.BlockSpec((None, tm, tn), lambda b, j, i: (b, i, j)),
        out_shape=jax.ShapeDtypeStruct((bsz, s, n), F32),
        scratch_shapes=[pltpu.VMEM((rows, tm, d), BF16)],
        compiler_params=_cparams(("parallel", "arbitrary", "arbitrary")),
        name="norm_proj",
    )(x, g.reshape(1, d), scale, shift, w)


def _mlp_kernel(x_ref, g_ref, sc_ref, sh_ref, gate_ref, w1_ref, w2_ref, o_ref, h_ref, acc_ref):
    f = pl.program_id(2)

    @pl.when(f == 0)
    def _():
        h_ref[...] = _modulated_norm(x_ref, g_ref, sc_ref, sh_ref)
        acc_ref[...] = jnp.zeros_like(acc_ref)

    u = jnp.maximum(_dot(h_ref[...], w1_ref[...]), 0.0)
    acc_ref[...] += _dot((u * u).astype(BF16), w2_ref[...])

    @pl.when(f == pl.num_programs(2) - 1)
    def _():
        o_ref[...] = x_ref[...] + gate_ref[...] * acc_ref[...]


def _mlp_sublayer(x, g, scale, shift, gate, w1, w2, layer, tm=512, tf=1024):
    bsz, s, d = x.shape
    ff = w1.shape[2]
    vec = pl.BlockSpec((None, 1, d), lambda b, i, f: (b, 0, 0))
    return pl.pallas_call(
        _mlp_kernel,
        grid=(bsz, s // tm, ff // tf),
        in_specs=[pl.BlockSpec((None, tm, d), lambda b, i, f: (b, i, 0)),
                  pl.BlockSpec((1, d), lambda b, i, f: (0, 0)),
                  vec, vec, vec,
                  pl.BlockSpec((None, d, tf), lambda b, i, f: (layer, 0, f)),
                  pl.BlockSpec((None, tf, d), lambda b, i, f: (layer, f, 0))],
        out_specs=pl.BlockSpec((None, tm, d), lambda b, i, f: (b, i, 0)),
        out_shape=jax.ShapeDtypeStruct((bsz, s, d), F32),
        scratch_shapes=[pltpu.VMEM((tm, d), BF16), pltpu.VMEM((tm, d), F32)],
        compiler_params=_cparams(("parallel", "parallel", "arbitrary")),
        name="mlp_sublayer",
    )(x, g.reshape(1, d), scale, shift, gate, w1, w2)


def _outproj_kernel(n_in, *refs):
    a_refs, w_refs = refs[:n_in], refs[n_in:2 * n_in]
    x_ref, gate_ref, o_ref = refs[2 * n_in:]
    y = _dot(a_refs[0][...], w_refs[0][...])
    for a_ref, w_ref in zip(a_refs[1:], w_refs[1:]):
        y += _dot(a_ref[...], w_ref[...])
    o_ref[...] = x_ref[...] + gate_ref[...] * y


def _out_proj(acts, ws, x, gate, tm=512):
    bsz, s, d = x.shape
    n_in = len(acts)
    in_specs = [pl.BlockSpec((None, tm, a.shape[-1]), lambda b, i: (b, i, 0)) for a in acts]
    in_specs += [pl.BlockSpec(w.shape, lambda b, i: (0, 0)) for w in ws]
    in_specs += [pl.BlockSpec((None, tm, d), lambda b, i: (b, i, 0)),
                 pl.BlockSpec((None, 1, d), lambda b, i: (b, 0, 0))]
    return pl.pallas_call(
        functools.partial(_outproj_kernel, n_in),
        grid=(bsz, s // tm),
        in_specs=in_specs,
        out_specs=pl.BlockSpec((None, tm, d), lambda b, i: (b, i, 0)),
        out_shape=jax.ShapeDtypeStruct((bsz, s, d), F32),
        compiler_params=_cparams(("parallel", "parallel")),
        name="out_proj",
    )(*acts, *ws, x, gate)


def _tri(n, upper=False):
    r = lax.broadcasted_iota(jnp.int32, (n, n), 0)
    c = lax.broadcasted_iota(jnp.int32, (n, n), 1)
    return (c >= r) if upper else (c <= r)


def _gla_kernel(q_ref, k_ref, v_ref, r_ref, zg_ref, wg_ref, bg_ref, ng_ref, o_ref, st_ref):
    @pl.when(pl.program_id(1) == 0)
    def _():
        st_ref[...] = jnp.zeros_like(st_ref)

    tril = _tri(CHUNK)
    tril_f = tril.astype(F32)
    for c in range(REC_BLOCK // CHUNK):
        rows = slice(c * CHUNK, (c + 1) * CHUNK)
        pre = _dot(zg_ref[rows, :].astype(BF16), wg_ref[...]) + bg_ref[...]
        log_a = _log_sigmoid(pre) / GLA_GATE_TAU
        cum = _dot(tril_f, log_a, precision=HI)
        total = cum[CHUNK - 1:CHUNK, :]
        q_dec = (q_ref[rows, :] * GLA_DK ** -0.5) * jnp.exp(cum)
        kk = k_ref[rows, :]
        k_dec = kk * jnp.exp(-cum)
        k_end = kk * jnp.exp(total - cum)
        decay = jnp.exp(total)
        for h in range(GLA_HEADS):
            ks = slice(h * GLA_DK, (h + 1) * GLA_DK)
            vs = slice(h * GLA_DV, (h + 1) * GLA_DV)
            qh = q_dec[:, ks].astype(BF16)
            vh = v_ref[rows, vs]
            att = jnp.where(tril, _dot_nt(qh, k_dec[:, ks].astype(BF16)), 0.0)
            st = st_ref[h]
            o = _dot(att.astype(BF16), vh.astype(BF16)) + _dot_nt(qh, st.astype(BF16))
            st_ref[h] = st * decay[:, ks] + _dot(vh.T.astype(BF16), k_end[:, ks].astype(BF16))
            rh = r_ref[rows, vs]
            o_ref[rows, vs] = ((_rms(o) * ng_ref[...]) * _silu(rh)).astype(o_ref.dtype)


def _gla(p, w_gate, b_gate, norm_g, col):
    bsz, s, _ = p.shape
    t = REC_BLOCK

    def piece(name):
        width, idx = col[name]
        return pl.BlockSpec((None, t, width), lambda b, i, idx=idx: (b, i, idx))

    const = lambda shape: pl.BlockSpec(shape, lambda b, i: (0,) * len(shape))
    hk = GLA_HEADS * GLA_DK
    return pl.pallas_call(
        _gla_kernel,
        grid=(bsz, s // t),
        in_specs=[piece("gla_q"), piece("gla_k"), piece("gla_v"), piece("gla_r"), piece("gates"),
                  const((LANES, hk)), const((1, hk)), const((1, GLA_DV))],
        out_specs=pl.BlockSpec((None, t, GLA_HEADS * GLA_DV), lambda b, i: (b, i, 0)),
        out_shape=jax.ShapeDtypeStruct((bsz, s, GLA_HEADS * GLA_DV), BF16),
        scratch_shapes=[pltpu.VMEM((GLA_HEADS, GLA_DV, GLA_DK), F32)],
        compiler_params=_cparams(("parallel", "arbitrary")),
        name="gla",
    )(p, p, p, p, p, w_gate, b_gate.reshape(1, hk), norm_g.reshape(1, GLA_DV))


def _mlstm_kernel(q_ref, k_ref, v_ref, og_ref, gc_ref, wq_ref, wk_ref, bc_ref, br_ref, ng_ref,
                  o_ref, c_ref, n_ref, m_ref, qx_ref, kx_ref):
    t = REC_BLOCK

    @pl.when(pl.program_id(1) == 0)
    def _():
        c_ref[...] = jnp.zeros_like(c_ref)
        n_ref[...] = jnp.zeros_like(n_ref)
        m_ref[...] = jnp.zeros_like(m_ref)
        qx_ref[0:8, :] = jnp.zeros((8, qx_ref.shape[1]), F32)
        kx_ref[0:8, :] = jnp.zeros((8, kx_ref.shape[1]), F32)

    def conv_silu(x_ref, xx_ref, w_ref):
        xx_ref[8:8 + t, :] = x_ref[...]
        y = xx_ref[8:8 + t, :] * w_ref[ML_CONV - 1:ML_CONV, :]
        for back in range(1, ML_CONV):
            y += xx_ref[8 - back:8 - back + t, :] * w_ref[ML_CONV - 1 - back:ML_CONV - back, :]
        xx_ref[0:8, :] = xx_ref[t:t + 8, :]
        return _silu(y)

    q_all = conv_silu(q_ref, qx_ref, wq_ref) * ML_DQK ** -0.5
    k_all = conv_silu(k_ref, kx_ref, wk_ref)

    tril = _tri(ML_CHUNK)
    tril_f = tril.astype(F32)
    triu_f = _tri(ML_CHUNK, upper=True).astype(F32)
    lane = lax.broadcasted_iota(jnp.int32, (ML_CHUNK, LANES), 1)
    is_f_col = (lane >= ML_HEADS) & (lane < 2 * ML_HEADS)
    sub = lax.broadcasted_iota(jnp.int32, (8, ML_CHUNK), 0)
    is_f_row = sub >= ML_HEADS
    gates_by_row = gc_ref[...].T[0:2 * ML_HEADS, :]

    for c in range(t // ML_CHUNK):
        rows = slice(c * ML_CHUNK, (c + 1) * ML_CHUNK)
        gcol = gc_ref[rows, :] + bc_ref[...]
        gcol = jnp.where(is_f_col, _log_sigmoid(gcol), gcol)
        bcol = _dot(tril_f, gcol, precision=HI)
        grow = gates_by_row[:, rows] + br_ref[:, 0:1]
        grow = jnp.where(is_f_row, _log_sigmoid(grow), grow)
        brow = _dot(grow, triu_f, precision=HI)
        for h in range(ML_HEADS):
            qs = slice(h * ML_DQK, (h + 1) * ML_DQK)
            vs = slice(h * ML_DV, (h + 1) * ML_DV)
            b_col = bcol[:, ML_HEADS + h:ML_HEADS + h + 1]
            b_row = brow[ML_HEADS + h:ML_HEADS + h + 1, :]
            li_col = gcol[:, h:h + 1]
            li_row = grow[h:h + 1, :]
            m_prev = m_ref[h:h + 1, 0:1]
            log_d = jnp.where(tril, b_col - b_row + li_row, -jnp.inf)
            log_inter = b_col + m_prev
            m_comb = jnp.maximum(log_inter, jnp.max(log_d, axis=-1, keepdims=True))
            d = jnp.exp(log_d - m_comb)
            w_inter = jnp.exp(log_inter - m_comb)
            qh = q_all[rows, qs]
            kh = k_all[rows, qs]
            qb = qh.astype(BF16)
            vb = v_ref[rows, vs].astype(BF16)
            sc = _dot_nt(qb, kh.astype(BF16)) * d
            cm = c_ref[h]
            nv = n_ref[h]
            num = _dot(sc.astype(BF16), vb) + w_inter * _dot(qb, cm.astype(BF16))
            den = jnp.sum(sc, axis=-1, keepdims=True) + w_inter * jnp.sum(qh * nv, axis=-1, keepdims=True)
            hh = num / jnp.maximum(jnp.abs(den), jnp.exp(-m_comb))
            total = b_col[ML_CHUNK - 1:ML_CHUNK, :]
            log_w = total - b_col + li_col
            m_new = jnp.maximum(total + m_prev, jnp.max(log_w, axis=0, keepdims=True))
            decay = jnp.exp(total + m_prev - m_new)
            kw = kh * jnp.exp(log_w - m_new)
            c_ref[h] = decay * cm + _dot(kw.T.astype(BF16), vb)
            n_ref[h] = decay * nv + jnp.sum(kw, axis=0, keepdims=True)
            m_ref[h:h + 1, :] = jnp.broadcast_to(m_new, (1, LANES))
            og = og_ref[rows, vs]
            o_ref[rows, vs] = ((_rms(hh) * ng_ref[...]) * _sigmoid(og)).astype(o_ref.dtype)


def _mlstm(p, conv_w, bias_col, bias_row, norm_g, col):
    bsz, s, _ = p.shape
    t = REC_BLOCK
    hq = ML_HEADS * ML_DQK
    hv = ML_HEADS * ML_DV

    def piece(name):
        width, idx = col[name]
        return pl.BlockSpec((None, t, width), lambda b, i, idx=idx: (b, i, idx))

    const = lambda shape: pl.BlockSpec(shape, lambda b, i: (0,) * len(shape))
    return pl.pallas_call(
        _mlstm_kernel,
        grid=(bsz, s // t),
        in_specs=[piece("ml_q"), piece("ml_k"), piece("ml_v"), piece("ml_og"), piece("gates"),
                  pl.BlockSpec((ML_CONV, hq), lambda b, i: (0, 0)),
                  pl.BlockSpec((ML_CONV, hq), lambda b, i: (0, 1)),
                  const((1, LANES)), const((8, LANES)), const((1, ML_DV))],
        out_specs=pl.BlockSpec((None, t, hv), lambda b, i: (b, i, 0)),
        out_shape=jax.ShapeDtypeStruct((bsz, s, hv), BF16),
        scratch_shapes=[pltpu.VMEM((ML_HEADS, ML_DQK, ML_DV), F32),
                        pltpu.VMEM((ML_HEADS, 1, ML_DQK), F32),
                        pltpu.VMEM((8, LANES), F32),
                        pltpu.VMEM((t + 8, hq), F32),
                        pltpu.VMEM((t + 8, hq), F32)],
        compiler_params=_cparams(("parallel", "arbitrary")),
        name="mlstm",
    )(p, p, p, p, p, conv_w, conv_w, bias_col, bias_row, norm_g.reshape(1, ML_DV))


def _split_bf16(x):
    hi = x.astype(BF16)
    return hi, (x - hi.astype(F32)).astype(BF16)


def _seg_norm(x, seg_ref, segt_ref):
    if seg_ref.dtype == BF16:
        seg_sum = lambda v, m_ref: sum(_dot(part, m_ref[...]) for part in _split_bf16(v))
    else:
        seg_sum = lambda v, m_ref: _dot(v, m_ref[...], precision=HI)
    ss = seg_sum(x * x, seg_ref) * (1.0 / NSA_DH)
    return x * seg_sum(lax.rsqrt(ss + EPS), segt_ref)


def _nsa_prep_kernel(q_ref, ks_ref, vs_ref, kw_ref, vw_ref, qg_ref, kg_ref, segk_ref, segkt_ref,
                     qnt_ref, ksg_ref, vst_ref, kwg_ref, vwt_ref):
    t = q_ref.shape[0]
    qt = q_ref[...].T
    for h in range(NSA_HEADS):
        hs = slice(h * NSA_DH, (h + 1) * NSA_DH)
        blk = qt[hs, :]
        inv = lax.rsqrt(jnp.mean(blk * blk, axis=0, keepdims=True) + EPS)
        qnt_ref[hs, :] = ((blk * inv) * qg_ref[hs, :]).astype(BF16)
    ones = jnp.ones((NSA_VPAD - NSA_DH, t), BF16)
    for k_ref, v_ref, kg_out, vt_out in ((ks_ref, vs_ref, ksg_ref, vst_ref), (kw_ref, vw_ref, kwg_ref, vwt_ref)):
        kn = (_seg_norm(k_ref[...], segk_ref, segkt_ref) * kg_ref[...]).astype(BF16)
        vt = v_ref[...].T.astype(BF16)
        for g in range(NSA_GROUPS):
            gs = slice(g * NSA_DH, (g + 1) * NSA_DH)
            kg_out[g] = kn[:, gs]
            vt_out[g, 0:NSA_DH, :] = vt[gs, :]
            vt_out[g, NSA_DH:NSA_VPAD, :] = ones


def _seg_matrices(width, dtype):
    seg = (np.arange(width)[:, None] // NSA_DH == np.arange(LANES)[None, :]).astype(np.float32)
    return jnp.asarray(seg, dtype=dtype), jnp.asarray(seg.T, dtype=dtype)


def _nsa_prep(p, q_gain, k_gain, col, t=512):
    bsz, s, _ = p.shape
    hq = NSA_HEADS * NSA_DH
    kv = NSA_GROUPS * NSA_DH

    def piece(name):
        width, idx = col[name]
        return pl.BlockSpec((None, t, width), lambda b, i, idx=idx: (b, i, idx))

    const = lambda shape: pl.BlockSpec(shape, lambda b, i: (0,) * len(shape))
    segk, segkt = _seg_matrices(kv, BF16)
    keys = pl.BlockSpec((None, NSA_GROUPS, t, NSA_DH), lambda b, i: (b, 0, i, 0))
    vals = pl.BlockSpec((None, NSA_GROUPS, NSA_VPAD, t), lambda b, i: (b, 0, 0, i))
    shp = lambda *dims, dt=BF16: jax.ShapeDtypeStruct(dims, dt)
    keys_shape = shp(bsz, NSA_GROUPS, s, NSA_DH)
    vals_shape = shp(bsz, NSA_GROUPS, NSA_VPAD, s)
    return pl.pallas_call(
        _nsa_prep_kernel,
        grid=(bsz, s // t),
        in_specs=[piece("nsa_q"), piece("nsa_ks"), piece("nsa_vs"), piece("nsa_kw"), piece("nsa_vw"),
                  const((hq, 1)), const((1, kv)), const((kv, LANES)), const((LANES, kv))],
        out_specs=[pl.BlockSpec((None, hq, t), lambda b, i: (b, 0, i)), keys, vals, keys, vals],
        out_shape=[shp(bsz, hq, s), keys_shape, vals_shape, keys_shape, vals_shape],
        compiler_params=_cparams(("parallel", "parallel")),
        name="nsa_prep",
    )(p, p, p, p, p, q_gain, k_gain, segk, segkt)


def _compress_kernel(kc0_ref, kc1_ref, vc0_ref, vc1_ref, posk_ref, posv_ref, kw1a_ref, kw1b_ref, kw2_ref,
                     vw1a_ref, vw1b_ref, vw2_ref, kg_ref, segk_ref, segkt_ref, kcg_ref, vct_ref):
    def hidden(x_refs, pos_ref, w1a_ref, w1b_ref):
        nsub = x_refs[0].shape[0] // CMP_STRIDE
        first = second = None
        for l in range(CMP_STRIDE):
            x = jnp.concatenate([r[pl.ds(l, nsub, stride=CMP_STRIDE), :] for r in x_refs], axis=1)
            a = _dot((x + pos_ref[l:l + 1, :]).astype(BF16), w1a_ref[l])
            b = _dot((x + pos_ref[CMP_STRIDE + l:CMP_STRIDE + l + 1, :]).astype(BF16), w1b_ref[l])
            first = a if first is None else first + a
            second = b if second is None else second + b
        row = lax.broadcasted_iota(jnp.int32, second.shape, 0)
        nxt = jnp.where(row == nsub - 1, 0.0, pltpu.roll(second, nsub - 1, 0))
        return _silu(first + nxt)

    kc = _dot(hidden((kc0_ref, kc1_ref), posk_ref, kw1a_ref, kw1b_ref).astype(BF16), kw2_ref[...])
    kn = (_seg_norm(kc, segk_ref, segkt_ref) * kg_ref[...]).astype(BF16)
    vt = _dot(vw2_ref[...], hidden((vc0_ref, vc1_ref), posv_ref, vw1a_ref, vw1b_ref).T.astype(BF16)).astype(BF16)
    for g in range(NSA_GROUPS):
        gs = slice(g * NSA_DH, (g + 1) * NSA_DH)
        kcg_ref[g] = kn[:, gs]
        vct_ref[g] = vt[gs, :]


def _expand_w1(w1):
    w = w1.reshape(2, CMP_STRIDE, NSA_DH, NSA_DH)
    eye = jnp.eye(NSA_GROUPS, dtype=w1.dtype)
    big = jnp.einsum('hlde,gk->hlgdke', w, eye)
    big = big.reshape(2, CMP_STRIDE, NSA_GROUPS * NSA_DH, NSA_GROUPS * NSA_DH).astype(BF16)
    return big[0], big[1]


def _expand_pos(pos):
    return jnp.tile(pos, (1, NSA_GROUPS))


def _block_diag(w2):
    return jnp.kron(jnp.eye(NSA_GROUPS, dtype=w2.dtype), w2).astype(BF16)


def _compress(p, pos_k, pos_v, ck_w1, ck_w2, cv_w1, cv_w2, k_gain, col):
    bsz, s, _ = p.shape
    kv = NSA_GROUPS * NSA_DH
    nsub = s // CMP_STRIDE
    segk, segkt = _seg_matrices(kv, F32)
    kw1a, kw1b = _expand_w1(ck_w1)
    vw1a, vw1b = _expand_w1(cv_w1)
    const = lambda shape: pl.BlockSpec(shape, lambda b: (0,) * len(shape))

    def halves(name):
        width, idx = col[name]
        assert width == 2 * LANES
        return [pl.BlockSpec((None, s, LANES), lambda b, j=2 * idx + h: (b, 0, j)) for h in range(2)]

    w1_spec = const((CMP_STRIDE, kv, kv))
    return pl.pallas_call(
        _compress_kernel,
        grid=(bsz,),
        in_specs=[*halves("nsa_kc"), *halves("nsa_vc"), const((CMP_LEN, kv)), const((CMP_LEN, kv)),
                  w1_spec, w1_spec, const((kv, kv)),
                  w1_spec, w1_spec, const((kv, kv)),
                  const((1, kv)), const((kv, LANES)), const((LANES, kv))],
        out_specs=[pl.BlockSpec((None, NSA_GROUPS, nsub, NSA_DH), lambda b: (b, 0, 0, 0)),
                   pl.BlockSpec((None, NSA_GROUPS, NSA_DH, nsub), lambda b: (b, 0, 0, 0))],
        out_shape=[jax.ShapeDtypeStruct((bsz, NSA_GROUPS, nsub, NSA_DH), BF16),
                   jax.ShapeDtypeStruct((bsz, NSA_GROUPS, NSA_DH, nsub), BF16)],
        compiler_params=_cparams(("parallel",)),
        name="nsa_compress",
    )(p, p, p, p, _expand_pos(pos_k), _expand_pos(pos_v),
      kw1a, kw1b, _block_diag(ck_w2), vw1a, vw1b, _block_diag(cv_w2).T, k_gain, segk, segkt)


def _nsa_attn_kernel(qt_ref, gt_ref, kcg_ref, vct_ref, ksg_ref, vst_ref, kwg_ref, vwt_ref, ovt_ref,
                     o_ref, acc_ref, sel_ref, s_ref, p_ref, pv_ref):
    tq, tk = NSA_TQ, NSA_TK
    q0 = pl.program_id(1) * tq
    ncmp = kcg_ref.shape[1]
    nsel = ovt_ref.shape[0]
    cols4 = NSA_HPG * tq

    t_cmp = q0 + lax.broadcasted_iota(jnp.int32, (ncmp, cols4), 1) % tq
    n_idx = lax.broadcasted_iota(jnp.int32, (ncmp, cols4), 0)
    cmp_valid = n_idx * CMP_STRIDE + (CMP_LEN - 1) <= t_cmp

    j_idx = lax.broadcasted_iota(jnp.int32, (nsel, tq), 0)
    cur = (q0 + lax.broadcasted_iota(jnp.int32, (nsel, tq), 1)) // SEL_BLOCK
    forced = (j_idx == 0) | (j_idx == cur) | (j_idx == cur - 1)
    causal_blk = j_idx <= cur

    row_blk = lax.broadcasted_iota(jnp.int32, (SEL_BLOCK, tq), 0)
    t_row = q0 + lax.broadcasted_iota(jnp.int32, (1, tq), 1)

    gates_t = _sigmoid(gt_ref[...]).T

    def add_gated(g, branch, o4):
        for i in range(NSA_HPG):
            h = g * NSA_HPG + i
            hs = slice(h * NSA_DH, (h + 1) * NSA_DH)
            gc = GLA_RANK + 3 * h + branch
            term = gates_t[gc:gc + 1, :] * o4[:, i * tq:(i + 1) * tq]
            acc_ref[hs, :] = term if branch == 0 else acc_ref[hs, :] + term

    def q_group(g):
        return jnp.concatenate([qt_ref[(g * NSA_HPG + i) * NSA_DH:(g * NSA_HPG + i + 1) * NSA_DH, :]
                                for i in range(NSA_HPG)], axis=1)

    def attend(branch, k_ref, vt_ref, first_tile, n_tiles, mask_fn):
        last = first_tile + n_tiles - 1

        def scores(g, kt):
            return _dot(k_ref[g, pl.ds(pl.multiple_of(kt * tk, tk), tk), :], q_group(g))

        def weighted_values(g, kt, p):
            return _dot(vt_ref[g, :, pl.ds(pl.multiple_of(kt * tk, tk), tk)], p)

        def step(kt, cur, ms):
            nxt = 1 - cur
            out = []
            for g in range(NSA_GROUPS):
                m = ms[g]
                s_ref[g, nxt] = scores(g, jnp.minimum(kt + 1, last))
                acc = pv_ref[g] + weighted_values(g, jnp.maximum(kt - 1, first_tile), p_ref[g, nxt])
                ok = mask_fn(g, kt, kt * tk)
                m_new = []
                for c in range(cols4 // LANES):
                    cs = slice(c * LANES, (c + 1) * LANES)
                    qs = slice((c % (tq // LANES)) * LANES, (c % (tq // LANES) + 1) * LANES)
                    s = jnp.where(ok[:, qs], s_ref[g, cur, :, cs], NEG)
                    m_c = jnp.maximum(m[:, cs], jnp.max(s, axis=0, keepdims=True))
                    p_ref[g, cur, :, cs] = jnp.exp2(s - m_c).astype(BF16)
                    m_new.append(m_c)
                m_new = jnp.concatenate(m_new, axis=1)
                pv_ref[g] = acc * jnp.exp2(m - m_new)
                out.append(m_new)
            return tuple(out)

        for g in range(NSA_GROUPS):
            s_ref[g, 0] = scores(g, first_tile)
            p_ref[g, 1] = jnp.zeros(p_ref.shape[2:], BF16)
        pv_ref[...] = jnp.zeros_like(pv_ref)

        def body(j, ms):
            kt = first_tile + 2 * j
            return step(kt + 1, 1, step(kt, 0, ms))

        ms = lax.fori_loop(0, n_tiles // 2, body, tuple(jnp.full((1, cols4), NEG, F32) for _ in range(NSA_GROUPS)))

        @pl.when(n_tiles % 2 == 1)
        def _():
            step(last, 0, ms)

        for g in range(NSA_GROUPS):
            acc = pv_ref[g] + weighted_values(g, last, p_ref[g, (n_tiles - 1) % 2])
            add_gated(g, branch, acc[0:NSA_DH, :] / acc[NSA_DH:NSA_DH + 1, :])

    last_tile = (q0 + tq - 1) // tk
    last_blk = (q0 + tq - 1) // SEL_BLOCK

    def sel_mask(g, kt, k0):
        per_tile = tk // SEL_BLOCK
        blocks = []
        for j in range(per_tile):
            picked = sel_ref[g, pl.ds(kt * per_tile + j, 1), :] > 0.5
            limit = jnp.where(picked, t_row, -1) - (k0 + j * SEL_BLOCK)
            blocks.append(row_blk <= limit)
        return jnp.concatenate(blocks, axis=0)

    win_span = WINDOW + tk
    win_start = pl.multiple_of(jnp.maximum(last_tile - WINDOW // tk, 0) * tk, tk)
    row_w = lax.broadcasted_iota(jnp.int32, (win_span, tq), 0)
    ahead = t_row - win_start
    win_ok = (row_w <= ahead) & (row_w > ahead - WINDOW)

    def window_branch(g):
        s = _dot(kwg_ref[g, pl.ds(win_start, win_span), :], q_group(g))
        probs = []
        for c in range(cols4 // LANES):
            qs = slice((c % (tq // LANES)) * LANES, (c % (tq // LANES) + 1) * LANES)
            sc = jnp.where(win_ok[:, qs], s[:, c * LANES:(c + 1) * LANES], NEG)
            probs.append(jnp.exp2(sc - jnp.max(sc, axis=0, keepdims=True)).astype(BF16))
        acc = _dot(vwt_ref[g, :, pl.ds(win_start, win_span)], jnp.concatenate(probs, axis=1))
        add_gated(g, 2, acc[0:NSA_DH, :] / acc[NSA_DH:NSA_DH + 1, :])

    importance = []
    for g in range(NSA_GROUPS):
        lg = jnp.where(cmp_valid, _dot(kcg_ref[g], q_group(g)), NEG)
        ex = jnp.exp2(lg - jnp.max(lg, axis=0, keepdims=True))
        p_cmp = jnp.where(cmp_valid, ex / jnp.sum(ex, axis=0, keepdims=True), 0.0)
        add_gated(g, 0, _dot(vct_ref[g], p_cmp.astype(BF16)))
        p_grp = p_cmp[:, 0:tq] + p_cmp[:, tq:2 * tq] + p_cmp[:, 2 * tq:3 * tq] + p_cmp[:, 3 * tq:4 * tq]
        importance.append(sum(_dot(ovt_ref[...], part) for part in _split_bf16(p_grp)))
        window_branch(g)

    @pl.when(last_blk < N_SELECT)
    def _():
        for g in range(NSA_GROUPS):
            sel_ref[g] = causal_blk.astype(F32)

    @pl.when(last_blk >= N_SELECT)
    def _():
        for g in range(NSA_GROUPS):
            val = jnp.where(forced, jnp.inf, jnp.where(causal_blk, importance[g], -jnp.inf))
            rank = jnp.zeros((nsel, tq), F32)
            for i in range(nsel):
                vi = val[i:i + 1, :]
                rank += ((vi > val) | ((vi == val) & (j_idx > i))).astype(F32)
            sel_ref[g] = (rank < N_SELECT).astype(F32)

    attend(1, ksg_ref, vst_ref, 0, last_tile + 1, sel_mask)
    o_ref[...] = acc_ref[...].T.astype(o_ref.dtype)


def _overlap_t(s):
    ncmp = s // CMP_STRIDE
    nsel = s // SEL_BLOCK
    c_start = np.arange(ncmp) * CMP_STRIDE
    s_start = np.arange(nsel) * SEL_BLOCK
    overlap_t = ((c_start[None, :] <= s_start[:, None] + SEL_BLOCK - 1)
                 & (c_start[None, :] + CMP_LEN - 1 >= s_start[:, None])).astype(np.float32)
    overlap_t[:, ncmp - 1] = 0.0
    return jnp.asarray(overlap_t, dtype=BF16)


def _nsa_attention(p, qnt, kcg, vct, ksg, vst, kwg, vwt, col):
    bsz, hq, s = qnt.shape
    ncmp = s // CMP_STRIDE
    nsel = s // SEL_BLOCK
    gw, gidx = col["gates"]
    per_row = lambda shape: pl.BlockSpec((None,) + shape, lambda b, i: (b,) + (0,) * len(shape))
    keys = per_row((NSA_GROUPS, s, NSA_DH))
    vals = per_row((NSA_GROUPS, NSA_VPAD, s))
    return pl.pallas_call(
        _nsa_attn_kernel,
        grid=(bsz, s // NSA_TQ),
        in_specs=[pl.BlockSpec((None, hq, NSA_TQ), lambda b, i: (b, 0, i)),
                  pl.BlockSpec((None, NSA_TQ, gw), lambda b, i: (b, i, gidx)),
                  per_row((NSA_GROUPS, ncmp, NSA_DH)), per_row((NSA_GROUPS, NSA_DH, ncmp)),
                  keys, vals, keys, vals,
                  pl.BlockSpec((nsel, ncmp), lambda b, i: (0, 0))],
        out_specs=pl.BlockSpec((None, NSA_TQ, hq), lambda b, i: (b, i, 0)),
        out_shape=jax.ShapeDtypeStruct((bsz, s, hq), BF16),
        scratch_shapes=[pltpu.VMEM((hq, NSA_TQ), F32), pltpu.VMEM((NSA_GROUPS, nsel, NSA_TQ), F32),
                        pltpu.VMEM((NSA_GROUPS, 2, NSA_TK, NSA_HPG * NSA_TQ), F32),
                        pltpu.VMEM((NSA_GROUPS, 2, NSA_TK, NSA_HPG * NSA_TQ), BF16),
                        pltpu.VMEM((NSA_GROUPS, NSA_VPAD, NSA_HPG * NSA_TQ), F32)],
        compiler_params=_cparams(("parallel", "arbitrary")),
        name="nsa_attention",
    )(qnt, p, kcg, vct, ksg, vst, kwg, vwt, _overlap_t(s))


def _layout(pieces, tile):
    col, off, mats = {}, 0, []
    for name, w in pieces:
        width = w.shape[1]
        assert off % width == 0, (name, off, width)
        col[name] = (width, off // width)
        mats.append(w)
        off += width
    pad = -off % tile
    if pad:
        mats.append(jnp.zeros((mats[0].shape[0], pad), mats[0].dtype))
    return jnp.concatenate(mats, axis=1).astype(BF16), col


def _pad_cols(w, width):
    return jnp.pad(w, ((0, 0), (0, width - w.shape[1])))


AB_TILE = 1920
ML_TILE = 1280


def _ab_layout(w_in):
    hk = GLA_HEADS * GLA_DK
    hv = GLA_HEADS * GLA_DV
    kv = NSA_GROUPS * NSA_DH
    sizes = [hk, hk, hv, hv, GLA_RANK, NSA_HEADS * NSA_DH] + [kv] * 6 + [3 * NSA_HEADS]
    cuts = np.cumsum(sizes)[:-1].tolist()
    gq, gk, gv, gr, gz, nq, kc, vc, ks, vs, kw, vw, gt = jnp.split(w_in, cuts, axis=1)
    gates = _pad_cols(jnp.concatenate([gz, gt], axis=1), LANES)
    return _layout([("gla_q", gq), ("gla_k", gk), ("gla_v", gv), ("gla_r", gr), ("nsa_q", nq),
                    ("nsa_kc", kc), ("nsa_vc", vc), ("nsa_ks", ks), ("nsa_vs", vs), ("nsa_kw", kw),
                    ("nsa_vw", vw), ("gates", gates)], AB_TILE)


def _ml_layout(w_in):
    hq = ML_HEADS * ML_DQK
    hv = ML_HEADS * ML_DV
    wide = 2 * hq + 2 * hv
    col = {"ml_q": (hq, 0), "ml_k": (hq, 1), "ml_v": (hv, 1), "ml_og": (hv, 2), "gates": (LANES, wide // LANES)}
    total = -(-(wide + LANES) // ML_TILE) * ML_TILE
    return jnp.pad(w_in, ((0, 0), (0, total - w_in.shape[1]))).astype(BF16), col


def kernel(x, c, norm_mix_g, mod_mix_w, mod_mix_b, norm_mlp_g, mod_mlp_w, mod_mlp_b, mlp_w1, mlp_w2, ab_w_in, ab_w_out, gla_w_gate, gla_b_gate, gla_norm_g, nsa_q_norm_g, nsa_k_norm_g, nsa_cmp_pos_k, nsa_cmp_pos_v, nsa_cmp_k_w1, nsa_cmp_k_w2, nsa_cmp_v_w1, nsa_cmp_v_w2, ml_w_in, ml_w_out, ml_conv_w, ml_b_i, ml_b_f, ml_norm_g):
    bsz, s, d = x.shape
    depth = norm_mix_g.shape[0]
    mod_mix = _ada_mod(c, mod_mix_w, mod_mix_b)
    mod_mlp = _ada_mod(c, mod_mlp_w, mod_mlp_b)
    mlp_w1_bf16, mlp_w2_bf16 = mlp_w1.astype(BF16), mlp_w2.astype(BF16)

    def split_mod(m):
        return [m[:, None, i * d:(i + 1) * d] for i in range(3)]

    for layer in range(depth):
        shift, scale, gate = split_mod(mod_mix[layer])
        if layer % 2 == 0:
            e = layer // 2
            w_in, col = _ab_layout(ab_w_in[e])
            p = _norm_proj(x, norm_mix_g[layer], scale, shift, w_in, tm=512, tn=AB_TILE)
            w_gate = jnp.pad(gla_w_gate[e], ((0, LANES - GLA_RANK), (0, 0))).astype(BF16)
            o_gla = _gla(p, w_gate, gla_b_gate[e], gla_norm_g[e], col)
            q_gain = (jnp.tile(nsa_q_norm_g[e], NSA_HEADS) * (NSA_DH ** -0.5 * LOG2E)).reshape(-1, 1)
            k_gain = jnp.tile(nsa_k_norm_g[e], NSA_GROUPS).reshape(1, -1)
            qnt, ksg, vst, kwg, vwt = _nsa_prep(p, q_gain, k_gain, col)
            kcg, vct = _compress(p, nsa_cmp_pos_k[e], nsa_cmp_pos_v[e], nsa_cmp_k_w1[e],
                                 nsa_cmp_k_w2[e], nsa_cmp_v_w1[e], nsa_cmp_v_w2[e], k_gain, col)
            o_nsa = _nsa_attention(p, qnt, kcg, vct, ksg, vst, kwg, vwt, col)
            w_out = ab_w_out[e].astype(BF16)
            half = GLA_HEADS * GLA_DV
            x = _out_proj([o_gla, o_nsa], [w_out[:half], w_out[half:]], x, gate)
        else:
            o = layer // 2
            w_in, col = _ml_layout(ml_w_in[o])
            p = _norm_proj(x, norm_mix_g[layer], scale, shift, w_in, tm=512, tn=ML_TILE)
            bias = jnp.concatenate([ml_b_i[o], ml_b_f[o]])
            bias_col = _pad_cols(bias.reshape(1, -1), LANES)
            bias_row = jnp.broadcast_to(bias.reshape(-1, 1), (2 * ML_HEADS, LANES))
            hh = _mlstm(p, ml_conv_w[o], bias_col, bias_row, ml_norm_g[o], col)
            x = _out_proj([hh], [ml_w_out[o].astype(BF16)], x, gate)
        shift, scale, gate = split_mod(mod_mlp[layer])
        x = _mlp_sublayer(x, norm_mlp_g[layer], scale, shift, gate, mlp_w1_bf16, mlp_w2_bf16, layer)
    return x
```

```python
import functools

import numpy as np
import jax
import jax.numpy as jnp
from jax import lax
from jax.experimental import pallas as pl
from jax.experimental.pallas import tpu as pltpu

F32 = jnp.float32
BF16 = jnp.bfloat16
HI = lax.Precision.HIGHEST

EPS = 1e-6
NEG = -1e30
LOG2E = 1.4426950408889634

VMEM_LIMIT_BYTES = 56 * 1024 * 1024
LANES = 128

GLA_HEADS = 4
GLA_DK = 128
GLA_DV = 256
GLA_RANK = 16
GLA_GATE_TAU = 16.0
CHUNK = 64
ML_CHUNK = 128
REC_BLOCK = 256

NSA_HEADS = 16
NSA_GROUPS = 4
NSA_HPG = 4
NSA_DH = 64
CMP_LEN = 32
CMP_STRIDE = 16
SEL_BLOCK = 64
N_SELECT = 16
WINDOW = 512
NSA_TQ = 256
NSA_TK = 256
NSA_VPAD = 80

ML_HEADS = 4
ML_DQK = 256
ML_DV = 512
ML_CONV = 4


def _cparams(sem):
    return pltpu.CompilerParams(dimension_semantics=sem, vmem_limit_bytes=VMEM_LIMIT_BYTES)


def _sigmoid(x):
    return 1.0 / (1.0 + jnp.exp(-x))


def _silu(x):
    return x * _sigmoid(x)


def _log_sigmoid(x):
    return jnp.minimum(x, 0.0) - jnp.log(1.0 + jnp.exp(-jnp.abs(x)))


def _dot(a, b, precision=None):
    return jnp.dot(a, b, preferred_element_type=F32, precision=precision)


def _dot_nt(a, b, precision=None):
    return lax.dot_general(a, b, (((1,), (1,)), ((), ())), preferred_element_type=F32, precision=precision)


def _split_bf16(x, parts=2):
    out = []
    for _ in range(parts - 1):
        piece = x.astype(BF16)
        out.append(piece)
        x = x - piece.astype(F32)
    return out + [x.astype(BF16)]


def _rms(x):
    return x * lax.rsqrt(jnp.mean(x * x, axis=-1, keepdims=True) + EPS)


def _mod_kernel(c_ref, w_ref, b_ref, o_ref):
    a = _silu(c_ref[...]).astype(BF16)
    o_ref[...] = _dot(a, w_ref[...].astype(BF16)) + b_ref[...]


def _ada_mod(c, w, b):
    nl, d, d3 = w.shape
    bsz = c.shape[0]
    tn = 768
    return pl.pallas_call(
        _mod_kernel,
        grid=(nl, d3 // tn),
        in_specs=[pl.BlockSpec((bsz, d), lambda l, j: (0, 0)),
                  pl.BlockSpec((None, d, tn), lambda l, j: (l, 0, j)),
                  pl.BlockSpec((None, 1, tn), lambda l, j: (l, 0, j))],
        out_specs=pl.BlockSpec((None, bsz, tn), lambda l, j: (l, 0, j)),
        out_shape=jax.ShapeDtypeStruct((nl, bsz, d3), F32),
        compiler_params=_cparams(("parallel", "parallel")),
        name="ada_mod",
    )(c, w, b.reshape(nl, 1, d3))


def _modulated_norm(x_ref, g_ref, sc_ref, sh_ref):
    x = x_ref[...]
    gain = g_ref[...] * (1.0 + sc_ref[...])
    inv = lax.rsqrt(jnp.mean(x * x, axis=-1, keepdims=True) + EPS)
    return ((x * inv) * gain + sh_ref[...]).astype(BF16)


def _proj_kernel(x_ref, g_ref, sc_ref, sh_ref, w_ref, o_ref, h_ref):
    i = pl.program_id(2)

    @pl.when(pl.program_id(1) == 0)
    def _():
        h_ref[i] = _modulated_norm(x_ref, g_ref, sc_ref, sh_ref)

    o_ref[...] = _dot(h_ref[i], w_ref[...])


def _norm_proj(x, g, scale, shift, w, tm, tn):
    bsz, s, d = x.shape
    n = w.shape[1]
    rows = s // tm
    x_spec = pl.BlockSpec((None, tm, d), lambda b, j, i: (b, jnp.where(j == 0, i, rows - 1), 0))
    return pl.pallas_call(
        _proj_kernel,
        grid=(bsz, n // tn, rows),
        in_specs=[x_spec,
                  pl.BlockSpec((1, d), lambda b, j, i: (0, 0)),
                  pl.BlockSpec((None, 1, d), lambda b, j, i: (b, 0, 0)),
                  pl.BlockSpec((None, 1, d), lambda b, j, i: (b, 0, 0)),
                  pl.BlockSpec((d, tn), lambda b, j, i: (0, j))],
        out_specs=pl.BlockSpec((None, tm, tn), lambda b, j, i: (b, i, j)),
        out_shape=jax.ShapeDtypeStruct((bsz, s, n), F32),
        scratch_shapes=[pltpu.VMEM((rows, tm, d), BF16)],
        compiler_params=_cparams(("parallel", "arbitrary", "arbitrary")),
        name="norm_proj",
    )(x, g.reshape(1, d), scale, shift, w)


def _mlp_kernel(x_ref, g_ref, sc_ref, sh_ref, gate_ref, w1_ref, w2_ref, o_ref, h_ref, acc_ref):
    f = pl.program_id(2)

    @pl.when(f == 0)
    def _():
        h_ref[...] = _modulated_norm(x_ref, g_ref, sc_ref, sh_ref)
        acc_ref[...] = jnp.zeros_like(acc_ref)

    u = jnp.maximum(_dot(h_ref[...], w1_ref[...]), 0.0)
    acc_ref[...] += _dot((u * u).astype(BF16), w2_ref[...])

    @pl.when(f == pl.num_programs(2) - 1)
    def _():
        o_ref[...] = x_ref[...] + gate_ref[...] * acc_ref[...]


def _mlp_sublayer(x, g, scale, shift, gate, w1, w2, layer, tm=512, tf=1024):
    bsz, s, d = x.shape
    ff = w1.shape[2]
    vec = pl.BlockSpec((None, 1, d), lambda b, i, f: (b, 0, 0))
    return pl.pallas_call(
        _mlp_kernel,
        grid=(bsz, s // tm, ff // tf),
        in_specs=[pl.BlockSpec((None, tm, d), lambda b, i, f: (b, i, 0)),
                  pl.BlockSpec((1, d), lambda b, i, f: (0, 0)),
                  vec, vec, vec,
                  pl.BlockSpec((None, d, tf), lambda b, i, f: (layer, 0, f)),
                  pl.BlockSpec((None, tf, d), lambda b, i, f: (layer, f, 0))],
        out_specs=pl.BlockSpec((None, tm, d), lambda b, i, f: (b, i, 0)),
        out_shape=jax.ShapeDtypeStruct((bsz, s, d), F32),
        scratch_shapes=[pltpu.VMEM((tm, d), BF16), pltpu.VMEM((tm, d), F32)],
        compiler_params=_cparams(("parallel", "parallel", "arbitrary")),
        name="mlp_sublayer",
    )(x, g.reshape(1, d), scale, shift, gate, w1, w2)


def _outproj_kernel(n_in, *refs):
    a_refs, w_refs = refs[:n_in], refs[n_in:2 * n_in]
    x_ref, gate_ref, o_ref = refs[2 * n_in:]
    y = _dot(a_refs[0][...], w_refs[0][...])
    for a_ref, w_ref in zip(a_refs[1:], w_refs[1:]):
        y += _dot(a_ref[...], w_ref[...])
    o_ref[...] = x_ref[...] + gate_ref[...] * y


def _out_proj(acts, ws, x, gate, tm=512):
    bsz, s, d = x.shape
    n_in = len(acts)
    in_specs = [pl.BlockSpec((None, tm, a.shape[-1]), lambda b, i: (b, i, 0)) for a in acts]
    in_specs += [pl.BlockSpec(w.shape, lambda b, i: (0, 0)) for w in ws]
    in_specs += [pl.BlockSpec((None, tm, d), lambda b, i: (b, i, 0)),
                 pl.BlockSpec((None, 1, d), lambda b, i: (b, 0, 0))]
    return pl.pallas_call(
        functools.partial(_outproj_kernel, n_in),
        grid=(bsz, s // tm),
        in_specs=in_specs,
        out_specs=pl.BlockSpec((None, tm, d), lambda b, i: (b, i, 0)),
        out_shape=jax.ShapeDtypeStruct((bsz, s, d), F32),
        compiler_params=_cparams(("parallel", "parallel")),
        name="out_proj",
    )(*acts, *ws, x, gate)


def _tri(n, upper=False):
    r = lax.broadcasted_iota(jnp.int32, (n, n), 0)
    c = lax.broadcasted_iota(jnp.int32, (n, n), 1)
    return (c >= r) if upper else (c <= r)


def _gla_kernel(q_ref, k_ref, v_ref, r_ref, zg_ref, wg_ref, bg_ref, ng_ref, o_ref, st_ref):
    @pl.when(pl.program_id(1) == 0)
    def _():
        st_ref[...] = jnp.zeros_like(st_ref)

    tril = _tri(CHUNK)
    tril_b = tril.astype(BF16)
    for c in range(REC_BLOCK // CHUNK):
        rows = slice(c * CHUNK, (c + 1) * CHUNK)
        pre = _dot(zg_ref[rows, :].astype(BF16), wg_ref[...]) + bg_ref[...]
        log_a = _log_sigmoid(pre) / GLA_GATE_TAU
        cum = sum(_dot(tril_b, part) for part in _split_bf16(log_a, 3))
        total = cum[CHUNK - 1:CHUNK, :]
        q_dec = (q_ref[rows, :] * GLA_DK ** -0.5) * jnp.exp(cum)
        kk = k_ref[rows, :]
        k_dec = kk * jnp.exp(-cum)
        k_end = kk * jnp.exp(total - cum)
        decay = jnp.exp(total)
        for h in range(GLA_HEADS):
            ks = slice(h * GLA_DK, (h + 1) * GLA_DK)
            vs = slice(h * GLA_DV, (h + 1) * GLA_DV)
            qh = q_dec[:, ks].astype(BF16)
            vh = v_ref[rows, vs]
            att = jnp.where(tril, _dot_nt(qh, k_dec[:, ks].astype(BF16)), 0.0)
            st = st_ref[h]
            o = _dot(att.astype(BF16), vh.astype(BF16)) + _dot_nt(qh, st.astype(BF16))
            st_ref[h] = st * decay[:, ks] + _dot(vh.T.astype(BF16), k_end[:, ks].astype(BF16))
            rh = r_ref[rows, vs]
            o_ref[rows, vs] = ((_rms(o) * ng_ref[...]) * _silu(rh)).astype(o_ref.dtype)


def _gla(p, w_gate, b_gate, norm_g, col):
    bsz, s, _ = p.shape
    t = REC_BLOCK

    def piece(name):
        width, idx = col[name]
        return pl.BlockSpec((None, t, width), lambda b, i, idx=idx: (b, i, idx))

    const = lambda shape: pl.BlockSpec(shape, lambda b, i: (0,) * len(shape))
    hk = GLA_HEADS * GLA_DK
    return pl.pallas_call(
        _gla_kernel,
        grid=(bsz, s // t),
        in_specs=[piece("gla_q"), piece("gla_k"), piece("gla_v"), piece("gla_r"), piece("gates"),
                  const((LANES, hk)), const((1, hk)), const((1, GLA_DV))],
        out_specs=pl.BlockSpec((None, t, GLA_HEADS * GLA_DV), lambda b, i: (b, i, 0)),
        out_shape=jax.ShapeDtypeStruct((bsz, s, GLA_HEADS * GLA_DV), BF16),
        scratch_shapes=[pltpu.VMEM((GLA_HEADS, GLA_DV, GLA_DK), F32)],
        compiler_params=_cparams(("parallel", "arbitrary")),
        name="gla",
    )(p, p, p, p, p, w_gate, b_gate.reshape(1, hk), norm_g.reshape(1, GLA_DV))


def _mlstm_kernel(q_ref, k_ref, v_ref, og_ref, gc_ref, wq_ref, wk_ref, bc_ref, br_ref, ng_ref,
                  o_ref, c_ref, n_ref, m_ref, qx_ref, kx_ref):
    t = REC_BLOCK

    @pl.when(pl.program_id(1) == 0)
    def _():
        c_ref[...] = jnp.zeros_like(c_ref)
        n_ref[...] = jnp.zeros_like(n_ref)
        m_ref[...] = jnp.zeros_like(m_ref)
        qx_ref[0:8, :] = jnp.zeros((8, qx_ref.shape[1]), F32)
        kx_ref[0:8, :] = jnp.zeros((8, kx_ref.shape[1]), F32)

    def conv_silu(x_ref, xx_ref, w_ref):
        xx_ref[8:8 + t, :] = x_ref[...]
        y = xx_ref[8:8 + t, :] * w_ref[ML_CONV - 1:ML_CONV, :]
        for back in range(1, ML_CONV):
            y += xx_ref[8 - back:8 - back + t, :] * w_ref[ML_CONV - 1 - back:ML_CONV - back, :]
        xx_ref[0:8, :] = xx_ref[t:t + 8, :]
        return _silu(y)

    q_all = conv_silu(q_ref, qx_ref, wq_ref) * ML_DQK ** -0.5
    k_all = conv_silu(k_ref, kx_ref, wk_ref)

    tril = _tri(ML_CHUNK)
    tril_b = tril.astype(BF16)
    triu_b = _tri(ML_CHUNK, upper=True).astype(BF16)
    lane = lax.broadcasted_iota(jnp.int32, (ML_CHUNK, LANES), 1)
    is_f_col = (lane >= ML_HEADS) & (lane < 2 * ML_HEADS)
    sub = lax.broadcasted_iota(jnp.int32, (8, ML_CHUNK), 0)
    is_f_row = sub >= ML_HEADS
    gates_by_row = gc_ref[...].T[0:2 * ML_HEADS, :]

    for c in range(t // ML_CHUNK):
        rows = slice(c * ML_CHUNK, (c + 1) * ML_CHUNK)
        gcol = gc_ref[rows, :] + bc_ref[...]
        gcol = jnp.where(is_f_col, _log_sigmoid(gcol), gcol)
        bcol = sum(_dot(tril_b, part) for part in _split_bf16(gcol, 3))
        grow = gates_by_row[:, rows] + br_ref[:, 0:1]
        grow = jnp.where(is_f_row, _log_sigmoid(grow), grow)
        brow = sum(_dot(part, triu_b) for part in _split_bf16(grow, 3))
        for h in range(ML_HEADS):
            qs = slice(h * ML_DQK, (h + 1) * ML_DQK)
            vs = slice(h * ML_DV, (h + 1) * ML_DV)
            b_col = bcol[:, ML_HEADS + h:ML_HEADS + h + 1]
            b_row = brow[ML_HEADS + h:ML_HEADS + h + 1, :]
            li_col = gcol[:, h:h + 1]
            li_row = grow[h:h + 1, :]
            m_prev = m_ref[h:h + 1, 0:1]
            log_d = jnp.where(tril, b_col - b_row + li_row, -jnp.inf)
            log_inter = b_col + m_prev
            m_comb = jnp.maximum(log_inter, jnp.max(log_d, axis=-1, keepdims=True))
            d = jnp.exp(log_d - m_comb)
            w_inter = jnp.exp(log_inter - m_comb)
            qh = q_all[rows, qs]
            kh = k_all[rows, qs]
            qb = qh.astype(BF16)
            vb = v_ref[rows, vs].astype(BF16)
            sc = _dot_nt(qb, kh.astype(BF16)) * d
            cm = c_ref[h]
            nv = n_ref[h]
            num = _dot(sc.astype(BF16), vb) + w_inter * _dot(qb, cm.astype(BF16))
            den = jnp.sum(sc, axis=-1, keepdims=True) + w_inter * jnp.sum(qh * nv, axis=-1, keepdims=True)
            hh = num / jnp.maximum(jnp.abs(den), jnp.exp(-m_comb))
            total = b_col[ML_CHUNK - 1:ML_CHUNK, :]
            log_w = total - b_col + li_col
            m_new = jnp.maximum(total + m_prev, jnp.max(log_w, axis=0, keepdims=True))
            decay = jnp.exp(total + m_prev - m_new)
            kw = kh * jnp.exp(log_w - m_new)
            c_ref[h] = decay * cm + _dot(kw.T.astype(BF16), vb)
            n_ref[h] = decay * nv + jnp.sum(kw, axis=0, keepdims=True)
            m_ref[h:h + 1, :] = jnp.broadcast_to(m_new, (1, LANES))
            og = og_ref[rows, vs]
            o_ref[rows, vs] = ((_rms(hh) * ng_ref[...]) * _sigmoid(og)).astype(o_ref.dtype)


def _mlstm(p, conv_w, bias_col, bias_row, norm_g, col):
    bsz, s, _ = p.shape
    t = REC_BLOCK
    hq = ML_HEADS * ML_DQK
    hv = ML_HEADS * ML_DV

    def piece(name):
        width, idx = col[name]
        return pl.BlockSpec((None, t, width), lambda b, i, idx=idx: (b, i, idx))

    const = lambda shape: pl.BlockSpec(shape, lambda b, i: (0,) * len(shape))
    return pl.pallas_call(
        _mlstm_kernel,
        grid=(bsz, s // t),
        in_specs=[piece("ml_q"), piece("ml_k"), piece("ml_v"), piece("ml_og"), piece("gates"),
                  pl.BlockSpec((ML_CONV, hq), lambda b, i: (0, 0)),
                  pl.BlockSpec((ML_CONV, hq), lambda b, i: (0, 1)),
                  const((1, LANES)), const((8, LANES)), const((1, ML_DV))],
        out_specs=pl.BlockSpec((None, t, hv), lambda b, i: (b, i, 0)),
        out_shape=jax.ShapeDtypeStruct((bsz, s, hv), BF16),
        scratch_shapes=[pltpu.VMEM((ML_HEADS, ML_DQK, ML_DV), F32),
                        pltpu.VMEM((ML_HEADS, 1, ML_DQK), F32),
                        pltpu.VMEM((8, LANES), F32),
                        pltpu.VMEM((t + 8, hq), F32),
                        pltpu.VMEM((t + 8, hq), F32)],
        compiler_params=_cparams(("parallel", "arbitrary")),
        name="mlstm",
    )(p, p, p, p, p, conv_w, conv_w, bias_col, bias_row, norm_g.reshape(1, ML_DV))


def _seg_norm(x, seg_ref, segt_ref):
    if seg_ref.dtype == BF16:
        seg_sum = lambda v, m_ref: sum(_dot(part, m_ref[...]) for part in _split_bf16(v))
    else:
        seg_sum = lambda v, m_ref: _dot(v, m_ref[...], precision=HI)
    ss = seg_sum(x * x, seg_ref) * (1.0 / NSA_DH)
    return x * seg_sum(lax.rsqrt(ss + EPS), segt_ref)


def _nsa_prep_kernel(q_ref, ks_ref, vs_ref, kw_ref, vw_ref, qg_ref, kg_ref, segk_ref, segkt_ref,
                     qnt_ref, ksg_ref, vst_ref, kwg_ref, vwt_ref):
    t = q_ref.shape[0]
    qt = q_ref[...].T
    for h in range(NSA_HEADS):
        hs = slice(h * NSA_DH, (h + 1) * NSA_DH)
        blk = qt[hs, :]
        inv = lax.rsqrt(jnp.mean(blk * blk, axis=0, keepdims=True) + EPS)
        qnt_ref[hs, :] = ((blk * inv) * qg_ref[hs, :]).astype(BF16)
    ones = jnp.ones((NSA_VPAD - NSA_DH, t), BF16)
    for k_ref, v_ref, kg_out, vt_out in ((ks_ref, vs_ref, ksg_ref, vst_ref), (kw_ref, vw_ref, kwg_ref, vwt_ref)):
        kn = (_seg_norm(k_ref[...], segk_ref, segkt_ref) * kg_ref[...]).astype(BF16)
        vt = v_ref[...].T.astype(BF16)
        for g in range(NSA_GROUPS):
            gs = slice(g * NSA_DH, (g + 1) * NSA_DH)
            kg_out[g] = kn[:, gs]
            vt_out[g, 0:NSA_DH, :] = vt[gs, :]
            vt_out[g, NSA_DH:NSA_VPAD, :] = ones


def _seg_matrices(width, dtype):
    seg = (np.arange(width)[:, None] // NSA_DH == np.arange(LANES)[None, :]).astype(np.float32)
    return jnp.asarray(seg, dtype=dtype), jnp.asarray(seg.T, dtype=dtype)


def _nsa_prep(p, q_gain, k_gain, col, t=512):
    bsz, s, _ = p.shape
    hq = NSA_HEADS * NSA_DH
    kv = NSA_GROUPS * NSA_DH

    def piece(name):
        width, idx = col[name]
        return pl.BlockSpec((None, t, width), lambda b, i, idx=idx: (b, i, idx))

    const = lambda shape: pl.BlockSpec(shape, lambda b, i: (0,) * len(shape))
    segk, segkt = _seg_matrices(kv, BF16)
    keys = pl.BlockSpec((None, NSA_GROUPS, t, NSA_DH), lambda b, i: (b, 0, i, 0))
    vals = pl.BlockSpec((None, NSA_GROUPS, NSA_VPAD, t), lambda b, i: (b, 0, 0, i))
    shp = lambda *dims, dt=BF16: jax.ShapeDtypeStruct(dims, dt)
    keys_shape = shp(bsz, NSA_GROUPS, s, NSA_DH)
    vals_shape = shp(bsz, NSA_GROUPS, NSA_VPAD, s)
    return pl.pallas_call(
        _nsa_prep_kernel,
        grid=(bsz, s // t),
        in_specs=[piece("nsa_q"), piece("nsa_ks"), piece("nsa_vs"), piece("nsa_kw"), piece("nsa_vw"),
                  const((hq, 1)), const((1, kv)), const((kv, LANES)), const((LANES, kv))],
        out_specs=[pl.BlockSpec((None, hq, t), lambda b, i: (b, 0, i)), keys, vals, keys, vals],
        out_shape=[shp(bsz, hq, s), keys_shape, vals_shape, keys_shape, vals_shape],
        compiler_params=_cparams(("parallel", "parallel")),
        name="nsa_prep",
    )(p, p, p, p, p, q_gain, k_gain, segk, segkt)


def _compress_kernel(kc0_ref, kc1_ref, vc0_ref, vc1_ref, posk_ref, posv_ref, kw1a_ref, kw1b_ref, kw2_ref,
                     vw1a_ref, vw1b_ref, vw2_ref, kg_ref, segk_ref, segkt_ref, kcg_ref, vct_ref):
    def hidden(x_refs, pos_ref, w1a_ref, w1b_ref):
        nsub = x_refs[0].shape[0] // CMP_STRIDE
        first = second = None
        for l in range(CMP_STRIDE):
            x = jnp.concatenate([r[pl.ds(l, nsub, stride=CMP_STRIDE), :] for r in x_refs], axis=1)
            a = _dot((x + pos_ref[l:l + 1, :]).astype(BF16), w1a_ref[l])
            b = _dot((x + pos_ref[CMP_STRIDE + l:CMP_STRIDE + l + 1, :]).astype(BF16), w1b_ref[l])
            first = a if first is None else first + a
            second = b if second is None else second + b
        row = lax.broadcasted_iota(jnp.int32, second.shape, 0)
        nxt = jnp.where(row == nsub - 1, 0.0, pltpu.roll(second, nsub - 1, 0))
        return _silu(first + nxt)

    kc = _dot(hidden((kc0_ref, kc1_ref), posk_ref, kw1a_ref, kw1b_ref).astype(BF16), kw2_ref[...])
    kn = (_seg_norm(kc, segk_ref, segkt_ref) * kg_ref[...]).astype(BF16)
    vt = _dot(vw2_ref[...], hidden((vc0_ref, vc1_ref), posv_ref, vw1a_ref, vw1b_ref).T.astype(BF16)).astype(BF16)
    for g in range(NSA_GROUPS):
        gs = slice(g * NSA_DH, (g + 1) * NSA_DH)
        kcg_ref[g] = kn[:, gs]
        vct_ref[g] = vt[gs, :]


def _expand_w1(w1):
    w = w1.reshape(2, CMP_STRIDE, NSA_DH, NSA_DH)
    eye = jnp.eye(NSA_GROUPS, dtype=w1.dtype)
    big = jnp.einsum('hlde,gk->hlgdke', w, eye)
    big = big.reshape(2, CMP_STRIDE, NSA_GROUPS * NSA_DH, NSA_GROUPS * NSA_DH).astype(BF16)
    return big[0], big[1]


def _expand_pos(pos):
    return jnp.tile(pos, (1, NSA_GROUPS))


def _block_diag(w2):
    return jnp.kron(jnp.eye(NSA_GROUPS, dtype=w2.dtype), w2).astype(BF16)


def _compress(p, pos_k, pos_v, ck_w1, ck_w2, cv_w1, cv_w2, k_gain, col):
    bsz, s, _ = p.shape
    kv = NSA_GROUPS * NSA_DH
    nsub = s // CMP_STRIDE
    segk, segkt = _seg_matrices(kv, F32)
    kw1a, kw1b = _expand_w1(ck_w1)
    vw1a, vw1b = _expand_w1(cv_w1)
    const = lambda shape: pl.BlockSpec(shape, lambda b: (0,) * len(shape))

    def halves(name):
        width, idx = col[name]
        assert width == 2 * LANES
        return [pl.BlockSpec((None, s, LANES), lambda b, j=2 * idx + h: (b, 0, j)) for h in range(2)]

    w1_spec = const((CMP_STRIDE, kv, kv))
    return pl.pallas_call(
        _compress_kernel,
        grid=(bsz,),
        in_specs=[*halves("nsa_kc"), *halves("nsa_vc"), const((CMP_LEN, kv)), const((CMP_LEN, kv)),
                  w1_spec, w1_spec, const((kv, kv)),
                  w1_spec, w1_spec, const((kv, kv)),
                  const((1, kv)), const((kv, LANES)), const((LANES, kv))],
        out_specs=[pl.BlockSpec((None, NSA_GROUPS, nsub, NSA_DH), lambda b: (b, 0, 0, 0)),
                   pl.BlockSpec((None, NSA_GROUPS, NSA_DH, nsub), lambda b: (b, 0, 0, 0))],
        out_shape=[jax.ShapeDtypeStruct((bsz, NSA_GROUPS, nsub, NSA_DH), BF16),
                   jax.ShapeDtypeStruct((bsz, NSA_GROUPS, NSA_DH, nsub), BF16)],
        compiler_params=_cparams(("parallel",)),
        name="nsa_compress",
    )(p, p, p, p, _expand_pos(pos_k), _expand_pos(pos_v),
      kw1a, kw1b, _block_diag(ck_w2), vw1a, vw1b, _block_diag(cv_w2).T, k_gain, segk, segkt)


def _nsa_attn_kernel(qt_ref, gt_ref, kcg_ref, vct_ref, ksg_ref, vst_ref, kwg_ref, vwt_ref, ovt_ref,
                     o_ref, acc_ref, sel_ref, s_ref, p_ref, pv_ref):
    tq, tk = NSA_TQ, NSA_TK
    q0 = pl.program_id(1) * tq
    ncmp = kcg_ref.shape[1]
    nsel = ovt_ref.shape[0]
    cols4 = NSA_HPG * tq

    t_cmp = q0 + lax.broadcasted_iota(jnp.int32, (ncmp, cols4), 1) % tq
    n_idx = lax.broadcasted_iota(jnp.int32, (ncmp, cols4), 0)
    cmp_valid = n_idx * CMP_STRIDE + (CMP_LEN - 1) <= t_cmp

    j_idx = lax.broadcasted_iota(jnp.int32, (nsel, tq), 0)
    cur = (q0 + lax.broadcasted_iota(jnp.int32, (nsel, tq), 1)) // SEL_BLOCK
    forced = (j_idx == 0) | (j_idx == cur) | (j_idx == cur - 1)
    causal_blk = j_idx <= cur

    row_blk = lax.broadcasted_iota(jnp.int32, (SEL_BLOCK, tq), 0)
    t_row = q0 + lax.broadcasted_iota(jnp.int32, (1, tq), 1)

    gates_t = _sigmoid(gt_ref[...]).T

    def add_gated(g, branch, o4):
        for i in range(NSA_HPG):
            h = g * NSA_HPG + i
            hs = slice(h * NSA_DH, (h + 1) * NSA_DH)
            gc = GLA_RANK + 3 * h + branch
            term = gates_t[gc:gc + 1, :] * o4[:, i * tq:(i + 1) * tq]
            acc_ref[hs, :] = term if branch == 0 else acc_ref[hs, :] + term

    def q_group(g):
        return jnp.concatenate([qt_ref[(g * NSA_HPG + i) * NSA_DH:(g * NSA_HPG + i + 1) * NSA_DH, :]
                                for i in range(NSA_HPG)], axis=1)

    def attend(branch, k_ref, vt_ref, first_tile, n_tiles, mask_fn):
        last = first_tile + n_tiles - 1

        def scores(g, kt):
            return _dot(k_ref[g, pl.ds(pl.multiple_of(kt * tk, tk), tk), :], q_group(g))

        def weighted_values(g, kt, p):
            return _dot(vt_ref[g, :, pl.ds(pl.multiple_of(kt * tk, tk), tk)], p)

        def step(kt, cur, ms):
            nxt = 1 - cur
            out = []
            for g in range(NSA_GROUPS):
                m = ms[g]
                s_ref[g, nxt] = scores(g, jnp.minimum(kt + 1, last))
                acc = pv_ref[g] + weighted_values(g, jnp.maximum(kt - 1, first_tile), p_ref[g, nxt])
                ok = mask_fn(g, kt, kt * tk)
                m_new = []
                for c in range(cols4 // LANES):
                    cs = slice(c * LANES, (c + 1) * LANES)
                    qs = slice((c % (tq // LANES)) * LANES, (c % (tq // LANES) + 1) * LANES)
                    s = jnp.where(ok[:, qs], s_ref[g, cur, :, cs], NEG)
                    m_c = jnp.maximum(m[:, cs], jnp.max(s, axis=0, keepdims=True))
                    p_ref[g, cur, :, cs] = jnp.exp2(s - m_c).astype(BF16)
                    m_new.append(m_c)
                m_new = jnp.concatenate(m_new, axis=1)
                pv_ref[g] = acc * jnp.exp2(m - m_new)
                out.append(m_new)
            return tuple(out)

        for g in range(NSA_GROUPS):
            s_ref[g, 0] = scores(g, first_tile)
            p_ref[g, 1] = jnp.zeros(p_ref.shape[2:], BF16)
        pv_ref[...] = jnp.zeros_like(pv_ref)

        def body(j, ms):
            kt = first_tile + 2 * j
            return step(kt + 1, 1, step(kt, 0, ms))

        ms = lax.fori_loop(0, n_tiles // 2, body, tuple(jnp.full((1, cols4), NEG, F32) for _ in range(NSA_GROUPS)))

        @pl.when(n_tiles % 2 == 1)
        def _():
            step(last, 0, ms)

        for g in range(NSA_GROUPS):
            acc = pv_ref[g] + weighted_values(g, last, p_ref[g, (n_tiles - 1) % 2])
            add_gated(g, branch, acc[0:NSA_DH, :] / acc[NSA_DH:NSA_DH + 1, :])

    last_tile = (q0 + tq - 1) // tk
    last_blk = (q0 + tq - 1) // SEL_BLOCK

    def sel_mask(g, kt, k0):
        per_tile = tk // SEL_BLOCK
        blocks = []
        for j in range(per_tile):
            picked = sel_ref[g, pl.ds(kt * per_tile + j, 1), :] > 0.5
            limit = jnp.where(picked, t_row, -1) - (k0 + j * SEL_BLOCK)
            blocks.append(row_blk <= limit)
        return jnp.concatenate(blocks, axis=0)

    win_span = WINDOW + tk
    win_start = pl.multiple_of(jnp.maximum(last_tile - WINDOW // tk, 0) * tk, tk)
    row_w = lax.broadcasted_iota(jnp.int32, (win_span, tq), 0)
    ahead = t_row - win_start
    win_ok = (row_w <= ahead) & (row_w > ahead - WINDOW)

    def window_branch(g):
        s = _dot(kwg_ref[g, pl.ds(win_start, win_span), :], q_group(g))
        probs = []
        for c in range(cols4 // LANES):
            qs = slice((c % (tq // LANES)) * LANES, (c % (tq // LANES) + 1) * LANES)
            sc = jnp.where(win_ok[:, qs], s[:, c * LANES:(c + 1) * LANES], NEG)
            probs.append(jnp.exp2(sc - jnp.max(sc, axis=0, keepdims=True)).astype(BF16))
        acc = _dot(vwt_ref[g, :, pl.ds(win_start, win_span)], jnp.concatenate(probs, axis=1))
        add_gated(g, 2, acc[0:NSA_DH, :] / acc[NSA_DH:NSA_DH + 1, :])

    importance = []
    for g in range(NSA_GROUPS):
        lg = jnp.where(cmp_valid, _dot(kcg_ref[g], q_group(g)), NEG)
        ex = jnp.exp2(lg - jnp.max(lg, axis=0, keepdims=True))
        p_cmp = jnp.where(cmp_valid, ex / jnp.sum(ex, axis=0, keepdims=True), 0.0)
        add_gated(g, 0, _dot(vct_ref[g], p_cmp.astype(BF16)))
        p_grp = p_cmp[:, 0:tq] + p_cmp[:, tq:2 * tq] + p_cmp[:, 2 * tq:3 * tq] + p_cmp[:, 3 * tq:4 * tq]
        importance.append(sum(_dot(ovt_ref[...], part) for part in _split_bf16(p_grp)))
        window_branch(g)

    @pl.when(last_blk < N_SELECT)
    def _():
        for g in range(NSA_GROUPS):
            sel_ref[g] = causal_blk.astype(F32)

    @pl.when(last_blk >= N_SELECT)
    def _():
        for g in range(NSA_GROUPS):
            val = jnp.where(forced, jnp.inf, jnp.where(causal_blk, importance[g], -jnp.inf))
            rank = jnp.zeros((nsel, tq), F32)
            for i in range(nsel):
                vi = val[i:i + 1, :]
                rank += ((vi > val) | ((vi == val) & (j_idx > i))).astype(F32)
            sel_ref[g] = (rank < N_SELECT).astype(F32)

    attend(1, ksg_ref, vst_ref, 0, last_tile + 1, sel_mask)
    o_ref[...] = acc_ref[...].T.astype(o_ref.dtype)


def _overlap_t(s):
    ncmp = s // CMP_STRIDE
    nsel = s // SEL_BLOCK
    c_start = np.arange(ncmp) * CMP_STRIDE
    s_start = np.arange(nsel) * SEL_BLOCK
    overlap_t = ((c_start[None, :] <= s_start[:, None] + SEL_BLOCK - 1)
                 & (c_start[None, :] + CMP_LEN - 1 >= s_start[:, None])).astype(np.float32)
    overlap_t[:, ncmp - 1] = 0.0
    return jnp.asarray(overlap_t, dtype=BF16)


def _nsa_attention(p, qnt, kcg, vct, ksg, vst, kwg, vwt, col):
    bsz, hq, s = qnt.shape
    ncmp = s // CMP_STRIDE
    nsel = s // SEL_BLOCK
    gw, gidx = col["gates"]
    per_row = lambda shape: pl.BlockSpec((None,) + shape, lambda b, i: (b,) + (0,) * len(shape))
    keys = per_row((NSA_GROUPS, s, NSA_DH))
    vals = per_row((NSA_GROUPS, NSA_VPAD, s))
    return pl.pallas_call(
        _nsa_attn_kernel,
        grid=(bsz, s // NSA_TQ),
        in_specs=[pl.BlockSpec((None, hq, NSA_TQ), lambda b, i: (b, 0, i)),
                  pl.BlockSpec((None, NSA_TQ, gw), lambda b, i: (b, i, gidx)),
                  per_row((NSA_GROUPS, ncmp, NSA_DH)), per_row((NSA_GROUPS, NSA_DH, ncmp)),
                  keys, vals, keys, vals,
                  pl.BlockSpec((nsel, ncmp), lambda b, i: (0, 0))],
        out_specs=pl.BlockSpec((None, NSA_TQ, hq), lambda b, i: (b, i, 0)),
        out_shape=jax.ShapeDtypeStruct((bsz, s, hq), BF16),
        scratch_shapes=[pltpu.VMEM((hq, NSA_TQ), F32), pltpu.VMEM((NSA_GROUPS, nsel, NSA_TQ), F32),
                        pltpu.VMEM((NSA_GROUPS, 2, NSA_TK, NSA_HPG * NSA_TQ), F32),
                        pltpu.VMEM((NSA_GROUPS, 2, NSA_TK, NSA_HPG * NSA_TQ), BF16),
                        pltpu.VMEM((NSA_GROUPS, NSA_VPAD, NSA_HPG * NSA_TQ), F32)],
        compiler_params=_cparams(("parallel", "arbitrary")),
        name="nsa_attention",
    )(qnt, p, kcg, vct, ksg, vst, kwg, vwt, _overlap_t(s))


def _layout(pieces, tile):
    col, off, mats = {}, 0, []
    for name, w in pieces:
        width = w.shape[1]
        assert off % width == 0, (name, off, width)
        col[name] = (width, off // width)
        mats.append(w)
        off += width
    pad = -off % tile
    if pad:
        mats.append(jnp.zeros((mats[0].shape[0], pad), mats[0].dtype))
    return jnp.concatenate(mats, axis=1).astype(BF16), col


def _pad_cols(w, width):
    return jnp.pad(w, ((0, 0), (0, width - w.shape[1])))


AB_TILE = 1920
ML_TILE = 1280


def _ab_layout(w_in):
    hk = GLA_HEADS * GLA_DK
    hv = GLA_HEADS * GLA_DV
    kv = NSA_GROUPS * NSA_DH
    sizes = [hk, hk, hv, hv, GLA_RANK, NSA_HEADS * NSA_DH] + [kv] * 6 + [3 * NSA_HEADS]
    cuts = np.cumsum(sizes)[:-1].tolist()
    gq, gk, gv, gr, gz, nq, kc, vc, ks, vs, kw, vw, gt = jnp.split(w_in, cuts, axis=1)
    gates = _pad_cols(jnp.concatenate([gz, gt], axis=1), LANES)
    return _layout([("gla_q", gq), ("gla_k", gk), ("gla_v", gv), ("gla_r", gr), ("nsa_q", nq),
                    ("nsa_kc", kc), ("nsa_vc", vc), ("nsa_ks", ks), ("nsa_vs", vs), ("nsa_kw", kw),
                    ("nsa_vw", vw), ("gates", gates)], AB_TILE)


def _ml_layout(w_in):
    hq = ML_HEADS * ML_DQK
    hv = ML_HEADS * ML_DV
    wide = 2 * hq + 2 * hv
    col = {"ml_q": (hq, 0), "ml_k": (hq, 1), "ml_v": (hv, 1), "ml_og": (hv, 2), "gates": (LANES, wide // LANES)}
    total = -(-(wide + LANES) // ML_TILE) * ML_TILE
    return jnp.pad(w_in, ((0, 0), (0, total - w_in.shape[1]))).astype(BF16), col


def kernel(x, c, norm_mix_g, mod_mix_w, mod_mix_b, norm_mlp_g, mod_mlp_w, mod_mlp_b, mlp_w1, mlp_w2, ab_w_in, ab_w_out, gla_w_gate, gla_b_gate, gla_norm_g, nsa_q_norm_g, nsa_k_norm_g, nsa_cmp_pos_k, nsa_cmp_pos_v, nsa_cmp_k_w1, nsa_cmp_k_w2, nsa_cmp_v_w1, nsa_cmp_v_w2, ml_w_in, ml_w_out, ml_conv_w, ml_b_i, ml_b_f, ml_norm_g):
    bsz, s, d = x.shape
    depth = norm_mix_g.shape[0]
    mod_mix = _ada_mod(c, mod_mix_w, mod_mix_b)
    mod_mlp = _ada_mod(c, mod_mlp_w, mod_mlp_b)
    mlp_w1_bf16, mlp_w2_bf16 = mlp_w1.astype(BF16), mlp_w2.astype(BF16)

    def split_mod(m):
        return [m[:, None, i * d:(i + 1) * d] for i in range(3)]

    for layer in range(depth):
        shift, scale, gate = split_mod(mod_mix[layer])
        if layer % 2 == 0:
            e = layer // 2
            w_in, col = _ab_layout(ab_w_in[e])
            p = _norm_proj(x, norm_mix_g[layer], scale, shift, w_in, tm=512, tn=AB_TILE)
            w_gate = jnp.pad(gla_w_gate[e], ((0, LANES - GLA_RANK), (0, 0))).astype(BF16)
            o_gla = _gla(p, w_gate, gla_b_gate[e], gla_norm_g[e], col)
            q_gain = (jnp.tile(nsa_q_norm_g[e], NSA_HEADS) * (NSA_DH ** -0.5 * LOG2E)).reshape(-1, 1)
            k_gain = jnp.tile(nsa_k_norm_g[e], NSA_GROUPS).reshape(1, -1)
            qnt, ksg, vst, kwg, vwt = _nsa_prep(p, q_gain, k_gain, col)
            kcg, vct = _compress(p, nsa_cmp_pos_k[e], nsa_cmp_pos_v[e], nsa_cmp_k_w1[e],
                                 nsa_cmp_k_w2[e], nsa_cmp_v_w1[e], nsa_cmp_v_w2[e], k_gain, col)
            o_nsa = _nsa_attention(p, qnt, kcg, vct, ksg, vst, kwg, vwt, col)
            w_out = ab_w_out[e].astype(BF16)
            half = GLA_HEADS * GLA_DV
            x = _out_proj([o_gla, o_nsa], [w_out[:half], w_out[half:]], x, gate)
        else:
            o = layer // 2
            w_in, col = _ml_layout(ml_w_in[o])
            p = _norm_proj(x, norm_mix_g[layer], scale, shift, w_in, tm=1024, tn=ML_TILE)
            bias = jnp.concatenate([ml_b_i[o], ml_b_f[o]])
            bias_col = _pad_cols(bias.reshape(1, -1), LANES)
            bias_row = jnp.broadcast_to(bias.reshape(-1, 1), (2 * ML_HEADS, LANES))
            hh = _mlstm(p, ml_conv_w[o], bias_col, bias_row, ml_norm_g[o], col)
            x = _out_proj([hh], [ml_w_out[o].astype(BF16)], x, gate)
        shift, scale, gate = split_mod(mod_mlp[layer])
        x = _mlp_sublayer(x, norm_mlp_g[layer], scale, shift, gate, mlp_w1_bf16, mlp_w2_bf16, layer)
    return x
```

```python
import functools

import numpy as np
import jax
import jax.numpy as jnp
from jax import lax
from jax.experimental import pallas as pl
from jax.experimental.pallas import tpu as pltpu

F32 = jnp.float32
BF16 = jnp.bfloat16
HI = lax.Precision.HIGHEST

EPS = 1e-6
NEG = -1e30
LOG2E = 1.4426950408889634

VMEM_LIMIT_BYTES = 56 * 1024 * 1024
LANES = 128

GLA_HEADS = 4
GLA_DK = 128
GLA_DV = 256
GLA_RANK = 16
GLA_GATE_TAU = 16.0
CHUNK = 64
ML_CHUNK = 128
REC_BLOCK = 256

NSA_HEADS = 16
NSA_GROUPS = 4
NSA_HPG = 4
NSA_DH = 64
CMP_LEN = 32
CMP_STRIDE = 16
SEL_BLOCK = 64
N_SELECT = 16
WINDOW = 512
NSA_TQ = 256
NSA_TK = 256
NSA_VPAD = 80

ML_HEADS = 4
ML_DQK = 256
ML_DV = 512
ML_CONV = 4


def _cparams(sem):
    return pltpu.CompilerParams(dimension_semantics=sem, vmem_limit_bytes=VMEM_LIMIT_BYTES)


def _sigmoid(x):
    return 1.0 / (1.0 + jnp.exp(-x))


def _silu(x):
    return x * _sigmoid(x)


def _log_sigmoid(x):
    return jnp.minimum(x, 0.0) - jnp.log(1.0 + jnp.exp(-jnp.abs(x)))


def _dot(a, b, precision=None):
    return jnp.dot(a, b, preferred_element_type=F32, precision=precision)


def _dot_nt(a, b, precision=None):
    return lax.dot_general(a, b, (((1,), (1,)), ((), ())), preferred_element_type=F32, precision=precision)


def _split_bf16(x, parts=2):
    out = []
    for _ in range(parts - 1):
        piece = x.astype(BF16)
        out.append(piece)
        x = x - piece.astype(F32)
    return out + [x.astype(BF16)]


def _rms(x):
    return x * lax.rsqrt(jnp.mean(x * x, axis=-1, keepdims=True) + EPS)


def _mod_kernel(c_ref, w_ref, b_ref, o_ref):
    a = _silu(c_ref[...]).astype(BF16)
    o_ref[...] = _dot(a, w_ref[...].astype(BF16)) + b_ref[...]


def _ada_mod(c, w, b):
    nl, d, d3 = w.shape
    bsz = c.shape[0]
    tn = 768
    return pl.pallas_call(
        _mod_kernel,
        grid=(nl, d3 // tn),
        in_specs=[pl.BlockSpec((bsz, d), lambda l, j: (0, 0)),
                  pl.BlockSpec((None, d, tn), lambda l, j: (l, 0, j)),
                  pl.BlockSpec((None, 1, tn), lambda l, j: (l, 0, j))],
        out_specs=pl.BlockSpec((None, bsz, tn), lambda l, j: (l, 0, j)),
        out_shape=jax.ShapeDtypeStruct((nl, bsz, d3), F32),
        compiler_params=_cparams(("parallel", "parallel")),
        name="ada_mod",
    )(c, w, b.reshape(nl, 1, d3))


def _modulated_norm(x_ref, g_ref, sc_ref, sh_ref):
    x = x_ref[...]
    gain = g_ref[...] * (1.0 + sc_ref[...])
    inv = lax.rsqrt(jnp.mean(x * x, axis=-1, keepdims=True) + EPS)
    return ((x * inv) * gain + sh_ref[...]).astype(BF16)


def _proj_kernel(x_ref, g_ref, sc_ref, sh_ref, w_ref, o_ref, h_ref):
    i = pl.program_id(2)

    @pl.when(pl.program_id(1) == 0)
    def _():
        h_ref[i] = _modulated_norm(x_ref, g_ref, sc_ref, sh_ref)

    o_ref[...] = _dot(h_ref[i], w_ref[...])


def _norm_proj(x, g, scale, shift, w, tm, tn):
    bsz, s, d = x.shape
    n = w.shape[1]
    rows = s // tm
    x_spec = pl.BlockSpec((None, tm, d), lambda b, j, i: (b, jnp.where(j == 0, i, rows - 1), 0))
    return pl.pallas_call(
        _proj_kernel,
        grid=(bsz, n // tn, rows),
        in_specs=[x_spec,
                  pl.BlockSpec((1, d), lambda b, j, i: (0, 0)),
                  pl.BlockSpec((None, 1, d), lambda b, j, i: (b, 0, 0)),
                  pl.BlockSpec((None, 1, d), lambda b, j, i: (b, 0, 0)),
                  pl.BlockSpec((d, tn), lambda b, j, i: (0, j))],
        out_specs=pl.BlockSpec((None, tm, tn), lambda b, j, i: (b, i, j)),
        out_shape=jax.ShapeDtypeStruct((bsz, s, n), F32),
        scratch_shapes=[pltpu.VMEM((rows, tm, d), BF16)],
        compiler_params=_cparams(("parallel", "arbitrary", "arbitrary")),
        name="norm_proj",
    )(x, g.reshape(1, d), scale, shift, w)


def _mlp_kernel(x_ref, g_ref, sc_ref, sh_ref, gate_ref, w1_ref, w2_ref, o_ref, h_ref, acc_ref):
    f = pl.program_id(2)

    @pl.when(f == 0)
    def _():
        h_ref[...] = _modulated_norm(x_ref, g_ref, sc_ref, sh_ref)
        acc_ref[...] = jnp.zeros_like(acc_ref)

    u = jnp.maximum(_dot(h_ref[...], w1_ref[...]), 0.0)
    acc_ref[...] += _dot((u * u).astype(BF16), w2_ref[...])

    @pl.when(f == pl.num_programs(2) - 1)
    def _():
        o_ref[...] = x_ref[...] + gate_ref[...] * acc_ref[...]


def _mlp_sublayer(x, g, scale, shift, gate, w1, w2, layer, tm=512, tf=1024):
    bsz, s, d = x.shape
    ff = w1.shape[2]
    vec = pl.BlockSpec((None, 1, d), lambda b, i, f: (b, 0, 0))
    return pl.pallas_call(
        _mlp_kernel,
        grid=(bsz, s // tm, ff // tf),
        in_specs=[pl.BlockSpec((None, tm, d), lambda b, i, f: (b, i, 0)),
                  pl.BlockSpec((1, d), lambda b, i, f: (0, 0)),
                  vec, vec, vec,
                  pl.BlockSpec((None, d, tf), lambda b, i, f: (layer, 0, f)),
                  pl.BlockSpec((None, tf, d), lambda b, i, f: (layer, f, 0))],
        out_specs=pl.BlockSpec((None, tm, d), lambda b, i, f: (b, i, 0)),
        out_shape=jax.ShapeDtypeStruct((bsz, s, d), F32),
        scratch_shapes=[pltpu.VMEM((tm, d), BF16), pltpu.VMEM((tm, d), F32)],
        compiler_params=_cparams(("parallel", "parallel", "arbitrary")),
        name="mlp_sublayer",
    )(x, g.reshape(1, d), scale, shift, gate, w1, w2)


def _outproj_kernel(n_in, *refs):
    a_refs, w_refs = refs[:n_in], refs[n_in:2 * n_in]
    x_ref, gate_ref, o_ref = refs[2 * n_in:]
    y = _dot(a_refs[0][...], w_refs[0][...])
    for a_ref, w_ref in zip(a_refs[1:], w_refs[1:]):
        y += _dot(a_ref[...], w_ref[...])
    o_ref[...] = x_ref[...] + gate_ref[...] * y


def _out_proj(acts, ws, x, gate, tm=512):
    bsz, s, d = x.shape
    n_in = len(acts)
    in_specs = [pl.BlockSpec((None, tm, a.shape[-1]), lambda b, i: (b, i, 0)) for a in acts]
    in_specs += [pl.BlockSpec(w.shape, lambda b, i: (0, 0)) for w in ws]
    in_specs += [pl.BlockSpec((None, tm, d), lambda b, i: (b, i, 0)),
                 pl.BlockSpec((None, 1, d), lambda b, i: (b, 0, 0))]
    return pl.pallas_call(
        functools.partial(_outproj_kernel, n_in),
        grid=(bsz, s // tm),
        in_specs=in_specs,
        out_specs=pl.BlockSpec((None, tm, d), lambda b, i: (b, i, 0)),
        out_shape=jax.ShapeDtypeStruct((bsz, s, d), F32),
        compiler_params=_cparams(("parallel", "parallel")),
        name="out_proj",
    )(*acts, *ws, x, gate)


def _tri(n, upper=False):
    r = lax.broadcasted_iota(jnp.int32, (n, n), 0)
    c = lax.broadcasted_iota(jnp.int32, (n, n), 1)
    return (c >= r) if upper else (c <= r)


def _gla_kernel(q_ref, k_ref, v_ref, r_ref, zg_ref, wg_ref, bg_ref, ng_ref, o_ref, st_ref):
    @pl.when(pl.program_id(1) == 0)
    def _():
        st_ref[...] = jnp.zeros_like(st_ref)

    tril = _tri(CHUNK)
    tril_b = tril.astype(BF16)
    for c in range(REC_BLOCK // CHUNK):
        rows = slice(c * CHUNK, (c + 1) * CHUNK)
        pre = _dot(zg_ref[rows, :].astype(BF16), wg_ref[...]) + bg_ref[...]
        log_a = _log_sigmoid(pre) / GLA_GATE_TAU
        cum = sum(_dot(tril_b, part) for part in _split_bf16(log_a, 3))
        total = cum[CHUNK - 1:CHUNK, :]
        q_dec = (q_ref[rows, :] * GLA_DK ** -0.5) * jnp.exp(cum)
        kk = k_ref[rows, :]
        k_dec = kk * jnp.exp(-cum)
        k_end = kk * jnp.exp(total - cum)
        decay = jnp.exp(total)
        for h in range(GLA_HEADS):
            ks = slice(h * GLA_DK, (h + 1) * GLA_DK)
            vs = slice(h * GLA_DV, (h + 1) * GLA_DV)
            qh = q_dec[:, ks].astype(BF16)
            vh = v_ref[rows, vs]
            att = jnp.where(tril, _dot_nt(qh, k_dec[:, ks].astype(BF16)), 0.0)
            st = st_ref[h]
            o = _dot(att.astype(BF16), vh.astype(BF16)) + _dot_nt(qh, st.astype(BF16))
            st_ref[h] = st * decay[:, ks] + _dot(vh.T.astype(BF16), k_end[:, ks].astype(BF16))
            rh = r_ref[rows, vs]
            o_ref[rows, vs] = ((_rms(o) * ng_ref[...]) * _silu(rh)).astype(o_ref.dtype)


def _gla(p, w_gate, b_gate, norm_g, col):
    bsz, s, _ = p.shape
    t = REC_BLOCK

    def piece(name):
        width, idx = col[name]
        return pl.BlockSpec((None, t, width), lambda b, i, idx=idx: (b, i, idx))

    const = lambda shape: pl.BlockSpec(shape, lambda b, i: (0,) * len(shape))
    hk = GLA_HEADS * GLA_DK
    return pl.pallas_call(
        _gla_kernel,
        grid=(bsz, s // t),
        in_specs=[piece("gla_q"), piece("gla_k"), piece("gla_v"), piece("gla_r"), piece("gates"),
                  const((LANES, hk)), const((1, hk)), const((1, GLA_DV))],
        out_specs=pl.BlockSpec((None, t, GLA_HEADS * GLA_DV), lambda b, i: (b, i, 0)),
        out_shape=jax.ShapeDtypeStruct((bsz, s, GLA_HEADS * GLA_DV), BF16),
        scratch_shapes=[pltpu.VMEM((GLA_HEADS, GLA_DV, GLA_DK), F32)],
        compiler_params=_cparams(("parallel", "arbitrary")),
        name="gla",
    )(p, p, p, p, p, w_gate, b_gate.reshape(1, hk), norm_g.reshape(1, GLA_DV))


def _mlstm_kernel(q_ref, k_ref, v_ref, og_ref, gc_ref, wq_ref, wk_ref, bc_ref, br_ref, ng_ref,
                  o_ref, c_ref, n_ref, m_ref, qx_ref, kx_ref):
    t = REC_BLOCK

    @pl.when(pl.program_id(1) == 0)
    def _():
        c_ref[...] = jnp.zeros_like(c_ref)
        n_ref[...] = jnp.zeros_like(n_ref)
        m_ref[...] = jnp.zeros_like(m_ref)
        qx_ref[0:8, :] = jnp.zeros((8, qx_ref.shape[1]), F32)
        kx_ref[0:8, :] = jnp.zeros((8, kx_ref.shape[1]), F32)

    def conv_silu(x_ref, xx_ref, w_ref):
        xx_ref[8:8 + t, :] = x_ref[...]
        y = xx_ref[8:8 + t, :] * w_ref[ML_CONV - 1:ML_CONV, :]
        for back in range(1, ML_CONV):
            y += xx_ref[8 - back:8 - back + t, :] * w_ref[ML_CONV - 1 - back:ML_CONV - back, :]
        xx_ref[0:8, :] = xx_ref[t:t + 8, :]
        return _silu(y)

    q_all = conv_silu(q_ref, qx_ref, wq_ref) * ML_DQK ** -0.5
    k_all = conv_silu(k_ref, kx_ref, wk_ref)

    tril = _tri(ML_CHUNK)
    tril_b = tril.astype(BF16)
    triu_b = _tri(ML_CHUNK, upper=True).astype(BF16)
    lane = lax.broadcasted_iota(jnp.int32, (ML_CHUNK, LANES), 1)
    is_f_col = (lane >= ML_HEADS) & (lane < 2 * ML_HEADS)
    sub = lax.broadcasted_iota(jnp.int32, (8, ML_CHUNK), 0)
    is_f_row = sub >= ML_HEADS
    gates_by_row = gc_ref[...].T[0:2 * ML_HEADS, :]

    for c in range(t // ML_CHUNK):
        rows = slice(c * ML_CHUNK, (c + 1) * ML_CHUNK)
        gcol = gc_ref[rows, :] + bc_ref[...]
        gcol = jnp.where(is_f_col, _log_sigmoid(gcol), gcol)
        bcol = sum(_dot(tril_b, part) for part in _split_bf16(gcol, 3))
        grow = gates_by_row[:, rows] + br_ref[:, 0:1]
        grow = jnp.where(is_f_row, _log_sigmoid(grow), grow)
        brow = sum(_dot(part, triu_b) for part in _split_bf16(grow, 3))
        for h in range(ML_HEADS):
            qs = slice(h * ML_DQK, (h + 1) * ML_DQK)
            vs = slice(h * ML_DV, (h + 1) * ML_DV)
            b_col = bcol[:, ML_HEADS + h:ML_HEADS + h + 1]
            b_row = brow[ML_HEADS + h:ML_HEADS + h + 1, :]
            li_col = gcol[:, h:h + 1]
            li_row = grow[h:h + 1, :]
            m_prev = m_ref[h:h + 1, 0:1]
            log_d = jnp.where(tril, b_col - b_row + li_row, -jnp.inf)
            log_inter = b_col + m_prev
            m_comb = jnp.maximum(log_inter, jnp.max(log_d, axis=-1, keepdims=True))
            d = jnp.exp(log_d - m_comb)
            w_inter = jnp.exp(log_inter - m_comb)
            qh = q_all[rows, qs]
            kh = k_all[rows, qs]
            qb = qh.astype(BF16)
            vb = v_ref[rows, vs].astype(BF16)
            sc = _dot_nt(qb, kh.astype(BF16)) * d
            cm = c_ref[h]
            nv = n_ref[h]
            num = _dot(sc.astype(BF16), vb) + w_inter * _dot(qb, cm.astype(BF16))
            den = jnp.sum(sc, axis=-1, keepdims=True) + w_inter * jnp.sum(qh * nv, axis=-1, keepdims=True)
            hh = num / jnp.maximum(jnp.abs(den), jnp.exp(-m_comb))
            total = b_col[ML_CHUNK - 1:ML_CHUNK, :]
            log_w = total - b_col + li_col
            m_new = jnp.maximum(total + m_prev, jnp.max(log_w, axis=0, keepdims=True))
            decay = jnp.exp(total + m_prev - m_new)
            kw = kh * jnp.exp(log_w - m_new)
            c_ref[h] = decay * cm + _dot(kw.T.astype(BF16), vb)
            n_ref[h] = decay * nv + jnp.sum(kw, axis=0, keepdims=True)
            m_ref[h:h + 1, :] = jnp.broadcast_to(m_new, (1, LANES))
            og = og_ref[rows, vs]
            o_ref[rows, vs] = ((_rms(hh) * ng_ref[...]) * _sigmoid(og)).astype(o_ref.dtype)


def _mlstm(p, conv_w, bias_col, bias_row, norm_g, col):
    bsz, s, _ = p.shape
    t = REC_BLOCK
    hq = ML_HEADS * ML_DQK
    hv = ML_HEADS * ML_DV

    def piece(name):
        width, idx = col[name]
        return pl.BlockSpec((None, t, width), lambda b, i, idx=idx: (b, i, idx))

    const = lambda shape: pl.BlockSpec(shape, lambda b, i: (0,) * len(shape))
    return pl.pallas_call(
        _mlstm_kernel,
        grid=(bsz, s // t),
        in_specs=[piece("ml_q"), piece("ml_k"), piece("ml_v"), piece("ml_og"), piece("gates"),
                  pl.BlockSpec((ML_CONV, hq), lambda b, i: (0, 0)),
                  pl.BlockSpec((ML_CONV, hq), lambda b, i: (0, 1)),
                  const((1, LANES)), const((8, LANES)), const((1, ML_DV))],
        out_specs=pl.BlockSpec((None, t, hv), lambda b, i: (b, i, 0)),
        out_shape=jax.ShapeDtypeStruct((bsz, s, hv), BF16),
        scratch_shapes=[pltpu.VMEM((ML_HEADS, ML_DQK, ML_DV), F32),
                        pltpu.VMEM((ML_HEADS, 1, ML_DQK), F32),
                        pltpu.VMEM((8, LANES), F32),
                        pltpu.VMEM((t + 8, hq), F32),
                        pltpu.VMEM((t + 8, hq), F32)],
        compiler_params=_cparams(("parallel", "arbitrary")),
        name="mlstm",
    )(p, p, p, p, p, conv_w, conv_w, bias_col, bias_row, norm_g.reshape(1, ML_DV))


def _seg_norm(x, seg_ref, segt_ref):
    if seg_ref.dtype == BF16:
        seg_sum = lambda v, m_ref: sum(_dot(part, m_ref[...]) for part in _split_bf16(v))
    else:
        seg_sum = lambda v, m_ref: _dot(v, m_ref[...], precision=HI)
    ss = seg_sum(x * x, seg_ref) * (1.0 / NSA_DH)
    return x * seg_sum(lax.rsqrt(ss + EPS), segt_ref)


def _nsa_prep_kernel(q_ref, ks_ref, vs_ref, kw_ref, vw_ref, qg_ref, kg_ref, segk_ref, segkt_ref,
                     qnt_ref, ksg_ref, vst_ref, kwg_ref, vwt_ref):
    t = q_ref.shape[0]
    qt = q_ref[...].T
    for h in range(NSA_HEADS):
        hs = slice(h * NSA_DH, (h + 1) * NSA_DH)
        blk = qt[hs, :]
        inv = lax.rsqrt(jnp.mean(blk * blk, axis=0, keepdims=True) + EPS)
        qnt_ref[hs, :] = ((blk * inv) * qg_ref[hs, :]).astype(BF16)
    ones = jnp.ones((NSA_VPAD - NSA_DH, t), BF16)
    for k_ref, v_ref, kg_out, vt_out in ((ks_ref, vs_ref, ksg_ref, vst_ref), (kw_ref, vw_ref, kwg_ref, vwt_ref)):
        kn = (_seg_norm(k_ref[...], segk_ref, segkt_ref) * kg_ref[...]).astype(BF16)
        vt = v_ref[...].T.astype(BF16)
        for g in range(NSA_GROUPS):
            gs = slice(g * NSA_DH, (g + 1) * NSA_DH)
            kg_out[g] = kn[:, gs]
            vt_out[g, 0:NSA_DH, :] = vt[gs, :]
            vt_out[g, NSA_DH:NSA_VPAD, :] = ones


def _seg_matrices(width, dtype):
    seg = (np.arange(width)[:, None] // NSA_DH == np.arange(LANES)[None, :]).astype(np.float32)
    return jnp.asarray(seg, dtype=dtype), jnp.asarray(seg.T, dtype=dtype)


def _nsa_prep(p, q_gain, k_gain, col, t=512):
    bsz, s, _ = p.shape
    hq = NSA_HEADS * NSA_DH
    kv = NSA_GROUPS * NSA_DH

    def piece(name):
        width, idx = col[name]
        return pl.BlockSpec((None, t, width), lambda b, i, idx=idx: (b, i, idx))

    const = lambda shape: pl.BlockSpec(shape, lambda b, i: (0,) * len(shape))
    segk, segkt = _seg_matrices(kv, BF16)
    keys = pl.BlockSpec((None, NSA_GROUPS, t, NSA_DH), lambda b, i: (b, 0, i, 0))
    vals = pl.BlockSpec((None, NSA_GROUPS, NSA_VPAD, t), lambda b, i: (b, 0, 0, i))
    shp = lambda *dims, dt=BF16: jax.ShapeDtypeStruct(dims, dt)
    keys_shape = shp(bsz, NSA_GROUPS, s, NSA_DH)
    vals_shape = shp(bsz, NSA_GROUPS, NSA_VPAD, s)
    return pl.pallas_call(
        _nsa_prep_kernel,
        grid=(bsz, s // t),
        in_specs=[piece("nsa_q"), piece("nsa_ks"), piece("nsa_vs"), piece("nsa_kw"), piece("nsa_vw"),
                  const((hq, 1)), const((1, kv)), const((kv, LANES)), const((LANES, kv))],
        out_specs=[pl.BlockSpec((None, hq, t), lambda b, i: (b, 0, i)), keys, vals, keys, vals],
        out_shape=[shp(bsz, hq, s), keys_shape, vals_shape, keys_shape, vals_shape],
        compiler_params=_cparams(("parallel", "parallel")),
        name="nsa_prep",
    )(p, p, p, p, p, q_gain, k_gain, segk, segkt)


def _compress_kernel(kc0_ref, kc1_ref, vc0_ref, vc1_ref, posk_ref, posv_ref, kw1a_ref, kw1b_ref, kw2_ref,
                     vw1a_ref, vw1b_ref, vw2_ref, kg_ref, segk_ref, segkt_ref, kcg_ref, vct_ref):
    def hidden(x_refs, pos_ref, w1a_ref, w1b_ref):
        nsub = x_refs[0].shape[0] // CMP_STRIDE
        first = second = None
        for l in range(CMP_STRIDE):
            x = jnp.concatenate([r[pl.ds(l, nsub, stride=CMP_STRIDE), :] for r in x_refs], axis=1)
            a = _dot((x + pos_ref[l:l + 1, :]).astype(BF16), w1a_ref[l])
            b = _dot((x + pos_ref[CMP_STRIDE + l:CMP_STRIDE + l + 1, :]).astype(BF16), w1b_ref[l])
            first = a if first is None else first + a
            second = b if second is None else second + b
        row = lax.broadcasted_iota(jnp.int32, second.shape, 0)
        nxt = jnp.where(row == nsub - 1, 0.0, pltpu.roll(second, nsub - 1, 0))
        return _silu(first + nxt)

    kc = _dot(hidden((kc0_ref, kc1_ref), posk_ref, kw1a_ref, kw1b_ref).astype(BF16), kw2_ref[...])
    kn = (_seg_norm(kc, segk_ref, segkt_ref) * kg_ref[...]).astype(BF16)
    vt = _dot(vw2_ref[...], hidden((vc0_ref, vc1_ref), posv_ref, vw1a_ref, vw1b_ref).T.astype(BF16)).astype(BF16)
    for g in range(NSA_GROUPS):
        gs = slice(g * NSA_DH, (g + 1) * NSA_DH)
        kcg_ref[g] = kn[:, gs]
        vct_ref[g] = vt[gs, :]


def _expand_w1(w1):
    w = w1.reshape(2, CMP_STRIDE, NSA_DH, NSA_DH)
    eye = jnp.eye(NSA_GROUPS, dtype=w1.dtype)
    big = jnp.einsum('hlde,gk->hlgdke', w, eye)
    big = big.reshape(2, CMP_STRIDE, NSA_GROUPS * NSA_DH, NSA_GROUPS * NSA_DH).astype(BF16)
    return big[0], big[1]


def _expand_pos(pos):
    return jnp.tile(pos, (1, NSA_GROUPS))


def _block_diag(w2):
    return jnp.kron(jnp.eye(NSA_GROUPS, dtype=w2.dtype), w2).astype(BF16)


def _compress(p, pos_k, pos_v, ck_w1, ck_w2, cv_w1, cv_w2, k_gain, col):
    bsz, s, _ = p.shape
    kv = NSA_GROUPS * NSA_DH
    nsub = s // CMP_STRIDE
    segk, segkt = _seg_matrices(kv, F32)
    kw1a, kw1b = _expand_w1(ck_w1)
    vw1a, vw1b = _expand_w1(cv_w1)
    const = lambda shape: pl.BlockSpec(shape, lambda b: (0,) * len(shape))

    def halves(name):
        width, idx = col[name]
        assert width == 2 * LANES
        return [pl.BlockSpec((None, s, LANES), lambda b, j=2 * idx + h: (b, 0, j)) for h in range(2)]

    w1_spec = const((CMP_STRIDE, kv, kv))
    return pl.pallas_call(
        _compress_kernel,
        grid=(bsz,),
        in_specs=[*halves("nsa_kc"), *halves("nsa_vc"), const((CMP_LEN, kv)), const((CMP_LEN, kv)),
                  w1_spec, w1_spec, const((kv, kv)),
                  w1_spec, w1_spec, const((kv, kv)),
                  const((1, kv)), const((kv, LANES)), const((LANES, kv))],
        out_specs=[pl.BlockSpec((None, NSA_GROUPS, nsub, NSA_DH), lambda b: (b, 0, 0, 0)),
                   pl.BlockSpec((None, NSA_GROUPS, NSA_DH, nsub), lambda b: (b, 0, 0, 0))],
        out_shape=[jax.ShapeDtypeStruct((bsz, NSA_GROUPS, nsub, NSA_DH), BF16),
                   jax.ShapeDtypeStruct((bsz, NSA_GROUPS, NSA_DH, nsub), BF16)],
        compiler_params=_cparams(("parallel",)),
        name="nsa_compress",
    )(p, p, p, p, _expand_pos(pos_k), _expand_pos(pos_v),
      kw1a, kw1b, _block_diag(ck_w2), vw1a, vw1b, _block_diag(cv_w2).T, k_gain, segk, segkt)


def _nsa_attn_kernel(qt_ref, gt_ref, kcg_ref, vct_ref, ksg_ref, vst_ref, kwg_ref, vwt_ref, ovt_ref,
                     o_ref, acc_ref, sel_ref, s_ref, p_ref, pv_ref):
    tq, tk = NSA_TQ, NSA_TK
    q0 = pl.program_id(1) * tq
    ncmp = kcg_ref.shape[1]
    nsel = ovt_ref.shape[0]
    cols4 = NSA_HPG * tq

    t_cmp = q0 + lax.broadcasted_iota(jnp.int32, (ncmp, cols4), 1) % tq
    n_idx = lax.broadcasted_iota(jnp.int32, (ncmp, cols4), 0)
    cmp_valid = n_idx * CMP_STRIDE + (CMP_LEN - 1) <= t_cmp

    j_idx = lax.broadcasted_iota(jnp.int32, (nsel, tq), 0)
    cur = (q0 + lax.broadcasted_iota(jnp.int32, (nsel, tq), 1)) // SEL_BLOCK
    forced = (j_idx == 0) | (j_idx == cur) | (j_idx == cur - 1)
    causal_blk = j_idx <= cur

    row_blk = lax.broadcasted_iota(jnp.int32, (SEL_BLOCK, tq), 0)
    t_row = q0 + lax.broadcasted_iota(jnp.int32, (1, tq), 1)

    gates_t = _sigmoid(gt_ref[...]).T

    def add_gated(g, branch, o4):
        for i in range(NSA_HPG):
            h = g * NSA_HPG + i
            hs = slice(h * NSA_DH, (h + 1) * NSA_DH)
            gc = GLA_RANK + 3 * h + branch
            term = gates_t[gc:gc + 1, :] * o4[:, i * tq:(i + 1) * tq]
            acc_ref[hs, :] = term if branch == 0 else acc_ref[hs, :] + term

    def q_group(g):
        return jnp.concatenate([qt_ref[(g * NSA_HPG + i) * NSA_DH:(g * NSA_HPG + i + 1) * NSA_DH, :]
                                for i in range(NSA_HPG)], axis=1)

    def attend(branch, k_ref, vt_ref, first_tile, n_tiles, mask_fn):
        last = first_tile + n_tiles - 1

        def scores(g, kt):
            return _dot(k_ref[g, pl.ds(pl.multiple_of(kt * tk, tk), tk), :], q_group(g))

        def weighted_values(g, kt, p):
            return _dot(vt_ref[g, :, pl.ds(pl.multiple_of(kt * tk, tk), tk)], p)

        def step(kt, cur, ms):
            nxt = 1 - cur
            out = []
            for g in range(NSA_GROUPS):
                m = ms[g]
                s_ref[g, nxt] = scores(g, jnp.minimum(kt + 1, last))
                acc = pv_ref[g] + weighted_values(g, jnp.maximum(kt - 1, first_tile), p_ref[g, nxt])
                ok = mask_fn(g, kt, kt * tk)
                m_new = []
                for c in range(cols4 // LANES):
                    cs = slice(c * LANES, (c + 1) * LANES)
                    qs = slice((c % (tq // LANES)) * LANES, (c % (tq // LANES) + 1) * LANES)
                    s = jnp.where(ok[:, qs], s_ref[g, cur, :, cs], NEG)
                    m_c = jnp.maximum(m[:, cs], jnp.max(s, axis=0, keepdims=True))
                    p_ref[g, cur, :, cs] = jnp.exp2(s - m_c).astype(BF16)
                    m_new.append(m_c)
                m_new = jnp.concatenate(m_new, axis=1)
                pv_ref[g] = acc * jnp.exp2(m - m_new)
                out.append(m_new)
            return tuple(out)

        for g in range(NSA_GROUPS):
            s_ref[g, 0] = scores(g, first_tile)
            p_ref[g, 1] = jnp.zeros(p_ref.shape[2:], BF16)
        pv_ref[...] = jnp.zeros_like(pv_ref)

        def body(j, ms):
            kt = first_tile + 2 * j
            return step(kt + 1, 1, step(kt, 0, ms))

        ms = lax.fori_loop(0, n_tiles // 2, body, tuple(jnp.full((1, cols4), NEG, F32) for _ in range(NSA_GROUPS)))

        @pl.when(n_tiles % 2 == 1)
        def _():
            step(last, 0, ms)

        for g in range(NSA_GROUPS):
            acc = pv_ref[g] + weighted_values(g, last, p_ref[g, (n_tiles - 1) % 2])
            add_gated(g, branch, acc[0:NSA_DH, :] / acc[NSA_DH:NSA_DH + 1, :])

    last_tile = (q0 + tq - 1) // tk
    last_blk = (q0 + tq - 1) // SEL_BLOCK

    def sel_mask(g, kt, k0):
        per_tile = tk // SEL_BLOCK
        blocks = []
        for j in range(per_tile):
            picked = sel_ref[g, pl.ds(kt * per_tile + j, 1), :] > 0.5
            limit = jnp.where(picked, t_row, -1) - (k0 + j * SEL_BLOCK)
            blocks.append(row_blk <= limit)
        return jnp.concatenate(blocks, axis=0)

    win_span = WINDOW + tk
    win_start = pl.multiple_of(jnp.maximum(last_tile - WINDOW // tk, 0) * tk, tk)
    row_w = lax.broadcasted_iota(jnp.int32, (win_span, tq), 0)
    ahead = t_row - win_start
    win_ok = (row_w <= ahead) & (row_w > ahead - WINDOW)

    def window_branch(g):
        s = _dot(kwg_ref[g, pl.ds(win_start, win_span), :], q_group(g))
        probs = []
        for c in range(cols4 // LANES):
            qs = slice((c % (tq // LANES)) * LANES, (c % (tq // LANES) + 1) * LANES)
            sc = jnp.where(win_ok[:, qs], s[:, c * LANES:(c + 1) * LANES], NEG)
            probs.append(jnp.exp2(sc - jnp.max(sc, axis=0, keepdims=True)).astype(BF16))
        acc = _dot(vwt_ref[g, :, pl.ds(win_start, win_span)], jnp.concatenate(probs, axis=1))
        add_gated(g, 2, acc[0:NSA_DH, :] / acc[NSA_DH:NSA_DH + 1, :])

    importance = []
    for g in range(NSA_GROUPS):
        lg = jnp.where(cmp_valid, _dot(kcg_ref[g], q_group(g)), NEG)
        ex = jnp.exp2(lg - jnp.max(lg, axis=0, keepdims=True))
        p_cmp = jnp.where(cmp_valid, ex / jnp.sum(ex, axis=0, keepdims=True), 0.0)
        add_gated(g, 0, _dot(vct_ref[g], p_cmp.astype(BF16)))
        p_grp = p_cmp[:, 0:tq] + p_cmp[:, tq:2 * tq] + p_cmp[:, 2 * tq:3 * tq] + p_cmp[:, 3 * tq:4 * tq]
        importance.append(sum(_dot(ovt_ref[...], part) for part in _split_bf16(p_grp)))
        window_branch(g)

    @pl.when(last_blk < N_SELECT)
    def _():
        for g in range(NSA_GROUPS):
            sel_ref[g] = causal_blk.astype(F32)

    @pl.when(last_blk >= N_SELECT)
    def _():
        for g in range(NSA_GROUPS):
            val = jnp.where(forced, jnp.inf, jnp.where(causal_blk, importance[g], -jnp.inf))
            rank = jnp.zeros((nsel, tq), F32)
            for i in range(nsel):
                vi = val[i:i + 1, :]
                rank += ((vi > val) | ((vi == val) & (j_idx > i))).astype(F32)
            sel_ref[g] = (rank < N_SELECT).astype(F32)

    attend(1, ksg_ref, vst_ref, 0, last_tile + 1, sel_mask)
    o_ref[...] = acc_ref[...].T.astype(o_ref.dtype)


def _overlap_t(s):
    ncmp = s // CMP_STRIDE
    nsel = s // SEL_BLOCK
    c_start = np.arange(ncmp) * CMP_STRIDE
    s_start = np.arange(nsel) * SEL_BLOCK
    overlap_t = ((c_start[None, :] <= s_start[:, None] + SEL_BLOCK - 1)
                 & (c_start[None, :] + CMP_LEN - 1 >= s_start[:, None])).astype(np.float32)
    overlap_t[:, ncmp - 1] = 0.0
    return jnp.asarray(overlap_t, dtype=BF16)


def _nsa_attention(p, qnt, kcg, vct, ksg, vst, kwg, vwt, col):
    bsz, hq, s = qnt.shape
    ncmp = s // CMP_STRIDE
    nsel = s // SEL_BLOCK
    gw, gidx = col["gates"]
    per_row = lambda shape: pl.BlockSpec((None,) + shape, lambda b, i: (b,) + (0,) * len(shape))
    keys = per_row((NSA_GROUPS, s, NSA_DH))
    vals = per_row((NSA_GROUPS, NSA_VPAD, s))
    return pl.pallas_call(
        _nsa_attn_kernel,
        grid=(bsz, s // NSA_TQ),
        in_specs=[pl.BlockSpec((None, hq, NSA_TQ), lambda b, i: (b, 0, i)),
                  pl.BlockSpec((None, NSA_TQ, gw), lambda b, i: (b, i, gidx)),
                  per_row((NSA_GROUPS, ncmp, NSA_DH)), per_row((NSA_GROUPS, NSA_DH, ncmp)),
                  keys, vals, keys, vals,
                  pl.BlockSpec((nsel, ncmp), lambda b, i: (0, 0))],
        out_specs=pl.BlockSpec((None, NSA_TQ, hq), lambda b, i: (b, i, 0)),
        out_shape=jax.ShapeDtypeStruct((bsz, s, hq), BF16),
        scratch_shapes=[pltpu.VMEM((hq, NSA_TQ), F32), pltpu.VMEM((NSA_GROUPS, nsel, NSA_TQ), F32),
                        pltpu.VMEM((NSA_GROUPS, 2, NSA_TK, NSA_HPG * NSA_TQ), F32),
                        pltpu.VMEM((NSA_GROUPS, 2, NSA_TK, NSA_HPG * NSA_TQ), BF16),
                        pltpu.VMEM((NSA_GROUPS, NSA_VPAD, NSA_HPG * NSA_TQ), F32)],
        compiler_params=_cparams(("parallel", "arbitrary")),
        name="nsa_attention",
    )(qnt, p, kcg, vct, ksg, vst, kwg, vwt, _overlap_t(s))


def _layout(pieces, tile):
    col, off, mats = {}, 0, []
    for name, w in pieces:
        width = w.shape[1]
        assert off % width == 0, (name, off, width)
        col[name] = (width, off // width)
        mats.append(w)
        off += width
    pad = -off % tile
    if pad:
        mats.append(jnp.zeros((mats[0].shape[0], pad), mats[0].dtype))
    return jnp.concatenate(mats, axis=1).astype(BF16), col


def _pad_cols(w, width):
    return jnp.pad(w, ((0, 0), (0, width - w.shape[1])))


AB_TILE = 1536
ML_TILE = 1280
PROJ_ROWS = 1024


def _ab_layout(w_in):
    hk = GLA_HEADS * GLA_DK
    hv = GLA_HEADS * GLA_DV
    kv = NSA_GROUPS * NSA_DH
    sizes = [hk, hk, hv, hv, GLA_RANK, NSA_HEADS * NSA_DH] + [kv] * 6 + [3 * NSA_HEADS]
    cuts = np.cumsum(sizes)[:-1].tolist()
    gq, gk, gv, gr, gz, nq, kc, vc, ks, vs, kw, vw, gt = jnp.split(w_in, cuts, axis=1)
    gates = _pad_cols(jnp.concatenate([gz, gt], axis=1), LANES)
    return _layout([("gla_q", gq), ("gla_k", gk), ("gla_v", gv), ("gla_r", gr), ("nsa_q", nq),
                    ("nsa_kc", kc), ("nsa_vc", vc), ("nsa_ks", ks), ("nsa_vs", vs), ("nsa_kw", kw),
                    ("nsa_vw", vw), ("gates", gates)], AB_TILE)


def _ml_layout(w_in):
    hq = ML_HEADS * ML_DQK
    hv = ML_HEADS * ML_DV
    wide = 2 * hq + 2 * hv
    col = {"ml_q": (hq, 0), "ml_k": (hq, 1), "ml_v": (hv, 1), "ml_og": (hv, 2), "gates": (LANES, wide // LANES)}
    total = -(-(wide + LANES) // ML_TILE) * ML_TILE
    return jnp.pad(w_in, ((0, 0), (0, total - w_in.shape[1]))).astype(BF16), col


def kernel(x, c, norm_mix_g, mod_mix_w, mod_mix_b, norm_mlp_g, mod_mlp_w, mod_mlp_b, mlp_w1, mlp_w2, ab_w_in, ab_w_out, gla_w_gate, gla_b_gate, gla_norm_g, nsa_q_norm_g, nsa_k_norm_g, nsa_cmp_pos_k, nsa_cmp_pos_v, nsa_cmp_k_w1, nsa_cmp_k_w2, nsa_cmp_v_w1, nsa_cmp_v_w2, ml_w_in, ml_w_out, ml_conv_w, ml_b_i, ml_b_f, ml_norm_g):
    bsz, s, d = x.shape
    depth = norm_mix_g.shape[0]
    mod_mix = _ada_mod(c, mod_mix_w, mod_mix_b)
    mod_mlp = _ada_mod(c, mod_mlp_w, mod_mlp_b)
    mlp_w1_bf16, mlp_w2_bf16 = mlp_w1.astype(BF16), mlp_w2.astype(BF16)

    def split_mod(m):
        return [m[:, None, i * d:(i + 1) * d] for i in range(3)]

    for layer in range(depth):
        shift, scale, gate = split_mod(mod_mix[layer])
        if layer % 2 == 0:
            e = layer // 2
            w_in, col = _ab_layout(ab_w_in[e])
            p = _norm_proj(x, norm_mix_g[layer], scale, shift, w_in, tm=PROJ_ROWS, tn=AB_TILE)
            w_gate = jnp.pad(gla_w_gate[e], ((0, LANES - GLA_RANK), (0, 0))).astype(BF16)
            o_gla = _gla(p, w_gate, gla_b_gate[e], gla_norm_g[e], col)
            q_gain = (jnp.tile(nsa_q_norm_g[e], NSA_HEADS) * (NSA_DH ** -0.5 * LOG2E)).reshape(-1, 1)
            k_gain = jnp.tile(nsa_k_norm_g[e], NSA_GROUPS).reshape(1, -1)
            qnt, ksg, vst, kwg, vwt = _nsa_prep(p, q_gain, k_gain, col)
            kcg, vct = _compress(p, nsa_cmp_pos_k[e], nsa_cmp_pos_v[e], nsa_cmp_k_w1[e],
                                 nsa_cmp_k_w2[e], nsa_cmp_v_w1[e], nsa_cmp_v_w2[e], k_gain, col)
            o_nsa = _nsa_attention(p, qnt, kcg, vct, ksg, vst, kwg, vwt, col)
            w_out = ab_w_out[e].astype(BF16)
            half = GLA_HEADS * GLA_DV
            x = _out_proj([o_gla, o_nsa], [w_out[:half], w_out[half:]], x, gate)
        else:
            o = layer // 2
            w_in, col = _ml_layout(ml_w_in[o])
            p = _norm_proj(x, norm_mix_g[layer], scale, shift, w_in, tm=PROJ_ROWS, tn=ML_TILE)
            bias = jnp.concatenate([ml_b_i[o], ml_b_f[o]])
            bias_col = _pad_cols(bias.reshape(1, -1), LANES)
            bias_row = jnp.broadcast_to(bias.reshape(-1, 1), (2 * ML_HEADS, LANES))
            hh = _mlstm(p, ml_conv_w[o], bias_col, bias_row, ml_norm_g[o], col)
            x = _out_proj([hh], [ml_w_out[o].astype(BF16)], x, gate)
        shift, scale, gate = split_mod(mod_mlp[layer])
        x = _mlp_sublayer(x, norm_mlp_g[layer], scale, shift, gate, mlp_w1_bf16, mlp_w2_bf16, layer)
    return x
```

```python
import functools

import numpy as np
import jax
import jax.numpy as jnp
from jax import lax
from jax.experimental import pallas as pl
from jax.experimental.pallas import tpu as pltpu

F32 = jnp.float32
BF16 = jnp.bfloat16
HI = lax.Precision.HIGHEST

EPS = 1e-6
NEG = -1e30
LOG2E = 1.4426950408889634

VMEM_LIMIT_BYTES = 56 * 1024 * 1024
LANES = 128

GLA_HEADS = 4
GLA_DK = 128
GLA_DV = 256
GLA_RANK = 16
GLA_GATE_TAU = 16.0
CHUNK = 64
ML_CHUNK = 128
REC_BLOCK = 512

NSA_HEADS = 16
NSA_GROUPS = 4
NSA_HPG = 4
NSA_DH = 64
CMP_LEN = 32
CMP_STRIDE = 16
SEL_BLOCK = 64
N_SELECT = 16
WINDOW = 512
NSA_TQ = 256
NSA_TK = 256
NSA_VPAD = 80

ML_HEADS = 4
ML_DQK = 256
ML_DV = 512
ML_CONV = 4


def _cparams(sem):
    return pltpu.CompilerParams(dimension_semantics=sem, vmem_limit_bytes=VMEM_LIMIT_BYTES)


def _sigmoid(x):
    return 1.0 / (1.0 + jnp.exp(-x))


def _silu(x):
    return x * _sigmoid(x)


def _log_sigmoid(x):
    return jnp.minimum(x, 0.0) - jnp.log(1.0 + jnp.exp(-jnp.abs(x)))


def _dot(a, b, precision=None):
    return jnp.dot(a, b, preferred_element_type=F32, precision=precision)


def _dot_nt(a, b, precision=None):
    return lax.dot_general(a, b, (((1,), (1,)), ((), ())), preferred_element_type=F32, precision=precision)


def _split_bf16(x, parts=2):
    out = []
    for _ in range(parts - 1):
        piece = x.astype(BF16)
        out.append(piece)
        x = x - piece.astype(F32)
    return out + [x.astype(BF16)]


def _rms(x):
    return x * lax.rsqrt(jnp.mean(x * x, axis=-1, keepdims=True) + EPS)


def _mod_kernel(c_ref, w_ref, b_ref, o_ref):
    a = _silu(c_ref[...]).astype(BF16)
    o_ref[...] = _dot(a, w_ref[...].astype(BF16)) + b_ref[...]


def _ada_mod(c, w, b):
    nl, d, d3 = w.shape
    bsz = c.shape[0]
    tn = 768
    return pl.pallas_call(
        _mod_kernel,
        grid=(nl, d3 // tn),
        in_specs=[pl.BlockSpec((bsz, d), lambda l, j: (0, 0)),
                  pl.BlockSpec((None, d, tn), lambda l, j: (l, 0, j)),
                  pl.BlockSpec((None, 1, tn), lambda l, j: (l, 0, j))],
        out_specs=pl.BlockSpec((None, bsz, tn), lambda l, j: (l, 0, j)),
        out_shape=jax.ShapeDtypeStruct((nl, bsz, d3), F32),
        compiler_params=_cparams(("parallel", "parallel")),
        name="ada_mod",
    )(c, w, b.reshape(nl, 1, d3))


def _modulated_norm(x_ref, g_ref, sc_ref, sh_ref):
    x = x_ref[...]
    gain = g_ref[...] * (1.0 + sc_ref[...])
    inv = lax.rsqrt(jnp.mean(x * x, axis=-1, keepdims=True) + EPS)
    return ((x * inv) * gain + sh_ref[...]).astype(BF16)


def _proj_kernel(x_ref, g_ref, sc_ref, sh_ref, w_ref, o_ref, h_ref):
    i = pl.program_id(2)

    @pl.when(pl.program_id(1) == 0)
    def _():
        h_ref[i] = _modulated_norm(x_ref, g_ref, sc_ref, sh_ref)

    o_ref[...] = _dot(h_ref[i], w_ref[...])


def _norm_proj(x, g, scale, shift, w, tm, tn):
    bsz, s, d = x.shape
    n = w.shape[1]
    rows = s // tm
    x_spec = pl.BlockSpec((None, tm, d), lambda b, j, i: (b, jnp.where(j == 0, i, rows - 1), 0))
    return pl.pallas_call(
        _proj_kernel,
        grid=(bsz, n // tn, rows),
        in_specs=[x_spec,
                  pl.BlockSpec((1, d), lambda b, j, i: (0, 0)),
                  pl.BlockSpec((None, 1, d), lambda b, j, i: (b, 0, 0)),
                  pl.BlockSpec((None, 1, d), lambda b, j, i: (b, 0, 0)),
                  pl.BlockSpec((d, tn), lambda b, j, i: (0, j))],
        out_specs=pl.BlockSpec((None, tm, tn), lambda b, j, i: (b, i, j)),
        out_shape=jax.ShapeDtypeStruct((bsz, s, n), F32),
        scratch_shapes=[pltpu.VMEM((rows, tm, d), BF16)],
        compiler_params=_cparams(("parallel", "arbitrary", "arbitrary")),
        name="norm_proj",
    )(x, g.reshape(1, d), scale, shift, w)


def _mlp_kernel(x_ref, g_ref, sc_ref, sh_ref, gate_ref, w1_ref, w2_ref, o_ref, h_ref, acc_ref):
    f = pl.program_id(2)

    @pl.when(f == 0)
    def _():
        h_ref[...] = _modulated_norm(x_ref, g_ref, sc_ref, sh_ref)
        acc_ref[...] = jnp.zeros_like(acc_ref)

    u = jnp.maximum(_dot(h_ref[...], w1_ref[...]), 0.0)
    acc_ref[...] += _dot((u * u).astype(BF16), w2_ref[...])

    @pl.when(f == pl.num_programs(2) - 1)
    def _():
        o_ref[...] = x_ref[...] + gate_ref[...] * acc_ref[...]


def _mlp_sublayer(x, g, scale, shift, gate, w1, w2, layer, tm=512, tf=1024):
    bsz, s, d = x.shape
    ff = w1.shape[2]
    vec = pl.BlockSpec((None, 1, d), lambda b, i, f: (b, 0, 0))
    return pl.pallas_call(
        _mlp_kernel,
        grid=(bsz, s // tm, ff // tf),
        in_specs=[pl.BlockSpec((None, tm, d), lambda b, i, f: (b, i, 0)),
                  pl.BlockSpec((1, d), lambda b, i, f: (0, 0)),
                  vec, vec, vec,
                  pl.BlockSpec((None, d, tf), lambda b, i, f: (layer, 0, f)),
                  pl.BlockSpec((None, tf, d), lambda b, i, f: (layer, f, 0))],
        out_specs=pl.BlockSpec((None, tm, d), lambda b, i, f: (b, i, 0)),
        out_shape=jax.ShapeDtypeStruct((bsz, s, d), F32),
        scratch_shapes=[pltpu.VMEM((tm, d), BF16), pltpu.VMEM((tm, d), F32)],
        compiler_params=_cparams(("parallel", "parallel", "arbitrary")),
        name="mlp_sublayer",
    )(x, g.reshape(1, d), scale, shift, gate, w1, w2)


def _outproj_kernel(n_in, *refs):
    a_refs, w_refs = refs[:n_in], refs[n_in:2 * n_in]
    x_ref, gate_ref, o_ref = refs[2 * n_in:]
    y = _dot(a_refs[0][...], w_refs[0][...])
    for a_ref, w_ref in zip(a_refs[1:], w_refs[1:]):
        y += _dot(a_ref[...], w_ref[...])
    o_ref[...] = x_ref[...] + gate_ref[...] * y


def _out_proj(acts, ws, x, gate, tm=512):
    bsz, s, d = x.shape
    n_in = len(acts)
    in_specs = [pl.BlockSpec((None, tm, a.shape[-1]), lambda b, i: (b, i, 0)) for a in acts]
    in_specs += [pl.BlockSpec(w.shape, lambda b, i: (0, 0)) for w in ws]
    in_specs += [pl.BlockSpec((None, tm, d), lambda b, i: (b, i, 0)),
                 pl.BlockSpec((None, 1, d), lambda b, i: (b, 0, 0))]
    return pl.pallas_call(
        functools.partial(_outproj_kernel, n_in),
        grid=(bsz, s // tm),
        in_specs=in_specs,
        out_specs=pl.BlockSpec((None, tm, d), lambda b, i: (b, i, 0)),
        out_shape=jax.ShapeDtypeStruct((bsz, s, d), F32),
        compiler_params=_cparams(("parallel", "parallel")),
        name="out_proj",
    )(*acts, *ws, x, gate)


def _tri(n, upper=False):
    r = lax.broadcasted_iota(jnp.int32, (n, n), 0)
    c = lax.broadcasted_iota(jnp.int32, (n, n), 1)
    return (c >= r) if upper else (c <= r)


def _gla_kernel(q_ref, k_ref, v_ref, r_ref, zg_ref, wg_ref, bg_ref, ng_ref, o_ref, st_ref):
    @pl.when(pl.program_id(1) == 0)
    def _():
        st_ref[...] = jnp.zeros_like(st_ref)

    tril = _tri(CHUNK)
    tril_b = tril.astype(BF16)
    pre = _dot(zg_ref[...].astype(BF16), wg_ref[...]) + bg_ref[...]
    parts = _split_bf16(_log_sigmoid(pre) / GLA_GATE_TAU, 3)
    chunks = [slice(c * CHUNK, (c + 1) * CHUNK) for c in range(REC_BLOCK // CHUNK)]
    cum_all = jnp.concatenate([sum(_dot(tril_b, part[rows, :]) for part in parts) for rows in chunks], axis=0)
    q_dec_all = (q_ref[...] * GLA_DK ** -0.5) * jnp.exp(cum_all)
    k_dec_all = k_ref[...] * jnp.exp(-cum_all)
    qb_all = q_dec_all.astype(BF16)
    kb_all = k_dec_all.astype(BF16)
    vb_all = v_ref[...].astype(BF16)
    gate_all = _silu(r_ref[...])
    intra = {}
    for c, rows in enumerate(chunks):
        for h in range(GLA_HEADS):
            ks = slice(h * GLA_DK, (h + 1) * GLA_DK)
            vs = slice(h * GLA_DV, (h + 1) * GLA_DV)
            att = jnp.where(tril, _dot_nt(qb_all[rows, ks], kb_all[rows, ks]), 0.0)
            intra[c, h] = _dot(att.astype(BF16), vb_all[rows, vs])
    for c, rows in enumerate(chunks):
        cum = cum_all[rows, :]
        total = cum[CHUNK - 1:CHUNK, :]
        k_end = (k_ref[rows, :] * jnp.exp(total - cum)).astype(BF16)
        decay = jnp.exp(total)
        for h in range(GLA_HEADS):
            ks = slice(h * GLA_DK, (h + 1) * GLA_DK)
            vs = slice(h * GLA_DV, (h + 1) * GLA_DV)
            st = st_ref[h]
            o = intra[c, h] + _dot_nt(qb_all[rows, ks], st.astype(BF16))
            st_ref[h] = st * decay[:, ks] + _dot(v_ref[rows, vs].T.astype(BF16), k_end[:, ks])
            o_ref[rows, vs] = ((_rms(o) * ng_ref[...]) * gate_all[rows, vs]).astype(o_ref.dtype)


def _gla(p, w_gate, b_gate, norm_g, col):
    bsz, s, _ = p.shape
    t = REC_BLOCK

    def piece(name):
        width, idx = col[name]
        return pl.BlockSpec((None, t, width), lambda b, i, idx=idx: (b, i, idx))

    const = lambda shape: pl.BlockSpec(shape, lambda b, i: (0,) * len(shape))
    hk = GLA_HEADS * GLA_DK
    return pl.pallas_call(
        _gla_kernel,
        grid=(bsz, s // t),
        in_specs=[piece("gla_q"), piece("gla_k"), piece("gla_v"), piece("gla_r"), piece("gates"),
                  const((LANES, hk)), const((1, hk)), const((1, GLA_DV))],
        out_specs=pl.BlockSpec((None, t, GLA_HEADS * GLA_DV), lambda b, i: (b, i, 0)),
        out_shape=jax.ShapeDtypeStruct((bsz, s, GLA_HEADS * GLA_DV), BF16),
        scratch_shapes=[pltpu.VMEM((GLA_HEADS, GLA_DV, GLA_DK), F32)],
        compiler_params=_cparams(("parallel", "arbitrary")),
        name="gla",
    )(p, p, p, p, p, w_gate, b_gate.reshape(1, hk), norm_g.reshape(1, GLA_DV))


def _mlstm_kernel(q_ref, k_ref, v_ref, og_ref, gc_ref, wq_ref, wk_ref, bc_ref, br_ref, ng_ref,
                  o_ref, c_ref, n_ref, m_ref, qx_ref, kx_ref):
    t = REC_BLOCK

    @pl.when(pl.program_id(1) == 0)
    def _():
        c_ref[...] = jnp.zeros_like(c_ref)
        n_ref[...] = jnp.zeros_like(n_ref)
        m_ref[...] = jnp.zeros_like(m_ref)
        qx_ref[0:8, :] = jnp.zeros((8, qx_ref.shape[1]), F32)
        kx_ref[0:8, :] = jnp.zeros((8, kx_ref.shape[1]), F32)

    def conv_silu(x_ref, xx_ref, w_ref):
        xx_ref[8:8 + t, :] = x_ref[...]
        y = xx_ref[8:8 + t, :] * w_ref[ML_CONV - 1:ML_CONV, :]
        for back in range(1, ML_CONV):
            y += xx_ref[8 - back:8 - back + t, :] * w_ref[ML_CONV - 1 - back:ML_CONV - back, :]
        xx_ref[0:8, :] = xx_ref[t:t + 8, :]
        return _silu(y)

    q_all = conv_silu(q_ref, qx_ref, wq_ref) * ML_DQK ** -0.5
    k_all = conv_silu(k_ref, kx_ref, wk_ref)

    tril = _tri(ML_CHUNK)
    tril_b = tril.astype(BF16)
    triu_b = _tri(ML_CHUNK, upper=True).astype(BF16)
    lane = lax.broadcasted_iota(jnp.int32, (ML_CHUNK, LANES), 1)
    is_f_col = (lane >= ML_HEADS) & (lane < 2 * ML_HEADS)
    sub = lax.broadcasted_iota(jnp.int32, (8, ML_CHUNK), 0)
    is_f_row = sub >= ML_HEADS
    gates_by_row = gc_ref[...].T[0:2 * ML_HEADS, :]

    for c in range(t // ML_CHUNK):
        rows = slice(c * ML_CHUNK, (c + 1) * ML_CHUNK)
        gcol = gc_ref[rows, :] + bc_ref[...]
        gcol = jnp.where(is_f_col, _log_sigmoid(gcol), gcol)
        bcol = sum(_dot(tril_b, part) for part in _split_bf16(gcol, 3))
        grow = gates_by_row[:, rows] + br_ref[:, 0:1]
        grow = jnp.where(is_f_row, _log_sigmoid(grow), grow)
        brow = sum(_dot(part, triu_b) for part in _split_bf16(grow, 3))
        for h in range(ML_HEADS):
            qs = slice(h * ML_DQK, (h + 1) * ML_DQK)
            vs = slice(h * ML_DV, (h + 1) * ML_DV)
            b_col = bcol[:, ML_HEADS + h:ML_HEADS + h + 1]
            b_row = brow[ML_HEADS + h:ML_HEADS + h + 1, :]
            li_col = gcol[:, h:h + 1]
            li_row = grow[h:h + 1, :]
            m_prev = m_ref[h:h + 1, 0:1]
            log_d = jnp.where(tril, b_col - b_row + li_row, -jnp.inf)
            log_inter = b_col + m_prev
            m_comb = jnp.maximum(log_inter, jnp.max(log_d, axis=-1, keepdims=True))
            d = jnp.exp(log_d - m_comb)
            w_inter = jnp.exp(log_inter - m_comb)
            qh = q_all[rows, qs]
            kh = k_all[rows, qs]
            qb = qh.astype(BF16)
            vb = v_ref[rows, vs].astype(BF16)
            sc = _dot_nt(qb, kh.astype(BF16)) * d
            cm = c_ref[h]
            nv = n_ref[h]
            num = _dot(sc.astype(BF16), vb) + w_inter * _dot(qb, cm.astype(BF16))
            den = jnp.sum(sc, axis=-1, keepdims=True) + w_inter * jnp.sum(qh * nv, axis=-1, keepdims=True)
            hh = num / jnp.maximum(jnp.abs(den), jnp.exp(-m_comb))
            total = b_col[ML_CHUNK - 1:ML_CHUNK, :]
            log_w = total - b_col + li_col
            m_new = jnp.maximum(total + m_prev, jnp.max(log_w, axis=0, keepdims=True))
            decay = jnp.exp(total + m_prev - m_new)
            kw = kh * jnp.exp(log_w - m_new)
            c_ref[h] = decay * cm + _dot(kw.T.astype(BF16), vb)
            n_ref[h] = decay * nv + jnp.sum(kw, axis=0, keepdims=True)
            m_ref[h:h + 1, :] = jnp.broadcast_to(m_new, (1, LANES))
            og = og_ref[rows, vs]
            o_ref[rows, vs] = ((_rms(hh) * ng_ref[...]) * _sigmoid(og)).astype(o_ref.dtype)


def _mlstm(p, conv_w, bias_col, bias_row, norm_g, col):
    bsz, s, _ = p.shape
    t = REC_BLOCK
    hq = ML_HEADS * ML_DQK
    hv = ML_HEADS * ML_DV

    def piece(name):
        width, idx = col[name]
        return pl.BlockSpec((None, t, width), lambda b, i, idx=idx: (b, i, idx))

    const = lambda shape: pl.BlockSpec(shape, lambda b, i: (0,) * len(shape))
    return pl.pallas_call(
        _mlstm_kernel,
        grid=(bsz, s // t),
        in_specs=[piece("ml_q"), piece("ml_k"), piece("ml_v"), piece("ml_og"), piece("gates"),
                  pl.BlockSpec((ML_CONV, hq), lambda b, i: (0, 0)),
                  pl.BlockSpec((ML_CONV, hq), lambda b, i: (0, 1)),
                  const((1, LANES)), const((8, LANES)), const((1, ML_DV))],
        out_specs=pl.BlockSpec((None, t, hv), lambda b, i: (b, i, 0)),
        out_shape=jax.ShapeDtypeStruct((bsz, s, hv), BF16),
        scratch_shapes=[pltpu.VMEM((ML_HEADS, ML_DQK, ML_DV), F32),
                        pltpu.VMEM((ML_HEADS, 1, ML_DQK), F32),
                        pltpu.VMEM((8, LANES), F32),
                        pltpu.VMEM((t + 8, hq), F32),
                        pltpu.VMEM((t + 8, hq), F32)],
        compiler_params=_cparams(("parallel", "arbitrary")),
        name="mlstm",
    )(p, p, p, p, p, conv_w, conv_w, bias_col, bias_row, norm_g.reshape(1, ML_DV))


def _seg_norm(x, seg_ref, segt_ref):
    if seg_ref.dtype == BF16:
        seg_sum = lambda v, m_ref: sum(_dot(part, m_ref[...]) for part in _split_bf16(v))
    else:
        seg_sum = lambda v, m_ref: _dot(v, m_ref[...], precision=HI)
    ss = seg_sum(x * x, seg_ref) * (1.0 / NSA_DH)
    return x * seg_sum(lax.rsqrt(ss + EPS), segt_ref)


def _nsa_prep_kernel(q_ref, ks_ref, vs_ref, kw_ref, vw_ref, qg_ref, kg_ref, segk_ref, segkt_ref,
                     qnt_ref, ksg_ref, vst_ref, kwg_ref, vwt_ref):
    t = q_ref.shape[0]
    qt = q_ref[...].T
    for h in range(NSA_HEADS):
        hs = slice(h * NSA_DH, (h + 1) * NSA_DH)
        blk = qt[hs, :]
        inv = lax.rsqrt(jnp.mean(blk * blk, axis=0, keepdims=True) + EPS)
        qnt_ref[hs, :] = ((blk * inv) * qg_ref[hs, :]).astype(BF16)
    ones = jnp.ones((NSA_VPAD - NSA_DH, t), BF16)
    for k_ref, v_ref, kg_out, vt_out in ((ks_ref, vs_ref, ksg_ref, vst_ref), (kw_ref, vw_ref, kwg_ref, vwt_ref)):
        kn = (_seg_norm(k_ref[...], segk_ref, segkt_ref) * kg_ref[...]).astype(BF16)
        vt = v_ref[...].T.astype(BF16)
        for g in range(NSA_GROUPS):
            gs = slice(g * NSA_DH, (g + 1) * NSA_DH)
            kg_out[g] = kn[:, gs]
            vt_out[g, 0:NSA_DH, :] = vt[gs, :]
            vt_out[g, NSA_DH:NSA_VPAD, :] = ones


def _seg_matrices(width, dtype):
    seg = (np.arange(width)[:, None] // NSA_DH == np.arange(LANES)[None, :]).astype(np.float32)
    return jnp.asarray(seg, dtype=dtype), jnp.asarray(seg.T, dtype=dtype)


def _nsa_prep(p, q_gain, k_gain, col, t=512):
    bsz, s, _ = p.shape
    hq = NSA_HEADS * NSA_DH
    kv = NSA_GROUPS * NSA_DH

    def piece(name):
        width, idx = col[name]
        return pl.BlockSpec((None, t, width), lambda b, i, idx=idx: (b, i, idx))

    const = lambda shape: pl.BlockSpec(shape, lambda b, i: (0,) * len(shape))
    segk, segkt = _seg_matrices(kv, BF16)
    keys = pl.BlockSpec((None, NSA_GROUPS, t, NSA_DH), lambda b, i: (b, 0, i, 0))
    vals = pl.BlockSpec((None, NSA_GROUPS, NSA_VPAD, t), lambda b, i: (b, 0, 0, i))
    shp = lambda *dims, dt=BF16: jax.ShapeDtypeStruct(dims, dt)
    keys_shape = shp(bsz, NSA_GROUPS, s, NSA_DH)
    vals_shape = shp(bsz, NSA_GROUPS, NSA_VPAD, s)
    return pl.pallas_call(
        _nsa_prep_kernel,
        grid=(bsz, s // t),
        in_specs=[piece("nsa_q"), piece("nsa_ks"), piece("nsa_vs"), piece("nsa_kw"), piece("nsa_vw"),
                  const((hq, 1)), const((1, kv)), const((kv, LANES)), const((LANES, kv))],
        out_specs=[pl.BlockSpec((None, hq, t), lambda b, i: (b, 0, i)), keys, vals, keys, vals],
        out_shape=[shp(bsz, hq, s), keys_shape, vals_shape, keys_shape, vals_shape],
        compiler_params=_cparams(("parallel", "parallel")),
        name="nsa_prep",
    )(p, p, p, p, p, q_gain, k_gain, segk, segkt)


def _compress_kernel(kc0_ref, kc1_ref, vc0_ref, vc1_ref, posk_ref, posv_ref, kw1a_ref, kw1b_ref, kw2_ref,
                     vw1a_ref, vw1b_ref, vw2_ref, kg_ref, segk_ref, segkt_ref, kcg_ref, vct_ref):
    def hidden(x_refs, pos_ref, w1a_ref, w1b_ref):
        nsub = x_refs[0].shape[0] // CMP_STRIDE
        first = second = None
        for l in range(CMP_STRIDE):
            x = jnp.concatenate([r[pl.ds(l, nsub, stride=CMP_STRIDE), :] for r in x_refs], axis=1)
            a = _dot((x + pos_ref[l:l + 1, :]).astype(BF16), w1a_ref[l])
            b = _dot((x + pos_ref[CMP_STRIDE + l:CMP_STRIDE + l + 1, :]).astype(BF16), w1b_ref[l])
            first = a if first is None else first + a
            second = b if second is None else second + b
        row = lax.broadcasted_iota(jnp.int32, second.shape, 0)
        nxt = jnp.where(row == nsub - 1, 0.0, pltpu.roll(second, nsub - 1, 0))
        return _silu(first + nxt)

    kc = _dot(hidden((kc0_ref, kc1_ref), posk_ref, kw1a_ref, kw1b_ref).astype(BF16), kw2_ref[...])
    kn = (_seg_norm(kc, segk_ref, segkt_ref) * kg_ref[...]).astype(BF16)
    vt = _dot(vw2_ref[...], hidden((vc0_ref, vc1_ref), posv_ref, vw1a_ref, vw1b_ref).T.astype(BF16)).astype(BF16)
    for g in range(NSA_GROUPS):
        gs = slice(g * NSA_DH, (g + 1) * NSA_DH)
        kcg_ref[g] = kn[:, gs]
        vct_ref[g] = vt[gs, :]


def _expand_w1(w1):
    w = w1.reshape(2, CMP_STRIDE, NSA_DH, NSA_DH)
    eye = jnp.eye(NSA_GROUPS, dtype=w1.dtype)
    big = jnp.einsum('hlde,gk->hlgdke', w, eye)
    big = big.reshape(2, CMP_STRIDE, NSA_GROUPS * NSA_DH, NSA_GROUPS * NSA_DH).astype(BF16)
    return big[0], big[1]


def _expand_pos(pos):
    return jnp.tile(pos, (1, NSA_GROUPS))


def _block_diag(w2):
    return jnp.kron(jnp.eye(NSA_GROUPS, dtype=w2.dtype), w2).astype(BF16)


def _compress(p, pos_k, pos_v, ck_w1, ck_w2, cv_w1, cv_w2, k_gain, col):
    bsz, s, _ = p.shape
    kv = NSA_GROUPS * NSA_DH
    nsub = s // CMP_STRIDE
    segk, segkt = _seg_matrices(kv, F32)
    kw1a, kw1b = _expand_w1(ck_w1)
    vw1a, vw1b = _expand_w1(cv_w1)
    const = lambda shape: pl.BlockSpec(shape, lambda b: (0,) * len(shape))

    def halves(name):
        width, idx = col[name]
        assert width == 2 * LANES
        return [pl.BlockSpec((None, s, LANES), lambda b, j=2 * idx + h: (b, 0, j)) for h in range(2)]

    w1_spec = const((CMP_STRIDE, kv, kv))
    return pl.pallas_call(
        _compress_kernel,
        grid=(bsz,),
        in_specs=[*halves("nsa_kc"), *halves("nsa_vc"), const((CMP_LEN, kv)), const((CMP_LEN, kv)),
                  w1_spec, w1_spec, const((kv, kv)),
                  w1_spec, w1_spec, const((kv, kv)),
                  const((1, kv)), const((kv, LANES)), const((LANES, kv))],
        out_specs=[pl.BlockSpec((None, NSA_GROUPS, nsub, NSA_DH), lambda b: (b, 0, 0, 0)),
                   pl.BlockSpec((None, NSA_GROUPS, NSA_DH, nsub), lambda b: (b, 0, 0, 0))],
        out_shape=[jax.ShapeDtypeStruct((bsz, NSA_GROUPS, nsub, NSA_DH), BF16),
                   jax.ShapeDtypeStruct((bsz, NSA_GROUPS, NSA_DH, nsub), BF16)],
        compiler_params=_cparams(("parallel",)),
        name="nsa_compress",
    )(p, p, p, p, _expand_pos(pos_k), _expand_pos(pos_v),
      kw1a, kw1b, _block_diag(ck_w2), vw1a, vw1b, _block_diag(cv_w2).T, k_gain, segk, segkt)


def _nsa_attn_kernel(qt_ref, gt_ref, kcg_ref, vct_ref, ksg_ref, vst_ref, kwg_ref, vwt_ref, ovt_ref,
                     o_ref, acc_ref, sel_ref, s_ref, p_ref, pv_ref):
    tq, tk = NSA_TQ, NSA_TK
    q0 = pl.program_id(1) * tq
    ncmp = kcg_ref.shape[1]
    nsel = ovt_ref.shape[0]
    cols4 = NSA_HPG * tq

    t_cmp = q0 + lax.broadcasted_iota(jnp.int32, (ncmp, cols4), 1) % tq
    n_idx = lax.broadcasted_iota(jnp.int32, (ncmp, cols4), 0)
    cmp_valid = n_idx * CMP_STRIDE + (CMP_LEN - 1) <= t_cmp

    j_idx = lax.broadcasted_iota(jnp.int32, (nsel, tq), 0)
    cur = (q0 + lax.broadcasted_iota(jnp.int32, (nsel, tq), 1)) // SEL_BLOCK
    forced = (j_idx == 0) | (j_idx == cur) | (j_idx == cur - 1)
    causal_blk = j_idx <= cur

    row_blk = lax.broadcasted_iota(jnp.int32, (SEL_BLOCK, tq), 0)
    t_row = q0 + lax.broadcasted_iota(jnp.int32, (1, tq), 1)

    gates_t = _sigmoid(gt_ref[...]).T

    def add_gated(g, branch, o4):
        for i in range(NSA_HPG):
            h = g * NSA_HPG + i
            hs = slice(h * NSA_DH, (h + 1) * NSA_DH)
            gc = GLA_RANK + 3 * h + branch
            term = gates_t[gc:gc + 1, :] * o4[:, i * tq:(i + 1) * tq]
            acc_ref[hs, :] = term if branch == 0 else acc_ref[hs, :] + term

    def q_group(g):
        return jnp.concatenate([qt_ref[(g * NSA_HPG + i) * NSA_DH:(g * NSA_HPG + i + 1) * NSA_DH, :]
                                for i in range(NSA_HPG)], axis=1)

    def attend(branch, k_ref, vt_ref, first_tile, n_tiles, mask_fn):
        last = first_tile + n_tiles - 1

        def scores(g, kt):
            return _dot(k_ref[g, pl.ds(pl.multiple_of(kt * tk, tk), tk), :], q_group(g))

        def weighted_values(g, kt, p):
            return _dot(vt_ref[g, :, pl.ds(pl.multiple_of(kt * tk, tk), tk)], p)

        def step(kt, cur, ms):
            nxt = 1 - cur
            out = []
            for g in range(NSA_GROUPS):
                m = ms[g]
                s_ref[g, nxt] = scores(g, jnp.minimum(kt + 1, last))
                acc = pv_ref[g] + weighted_values(g, jnp.maximum(kt - 1, first_tile), p_ref[g, nxt])
                ok = mask_fn(g, kt, kt * tk)
                m_new = []
                for c in range(cols4 // LANES):
                    cs = slice(c * LANES, (c + 1) * LANES)
                    qs = slice((c % (tq // LANES)) * LANES, (c % (tq // LANES) + 1) * LANES)
                    s = jnp.where(ok[:, qs], s_ref[g, cur, :, cs], NEG)
                    m_c = jnp.maximum(m[:, cs], jnp.max(s, axis=0, keepdims=True))
                    p_ref[g, cur, :, cs] = jnp.exp2(s - m_c).astype(BF16)
                    m_new.append(m_c)
                m_new = jnp.concatenate(m_new, axis=1)
                pv_ref[g] = acc * jnp.exp2(m - m_new)
                out.append(m_new)
            return tuple(out)

        for g in range(NSA_GROUPS):
            s_ref[g, 0] = scores(g, first_tile)
            p_ref[g, 1] = jnp.zeros(p_ref.shape[2:], BF16)
        pv_ref[...] = jnp.zeros_like(pv_ref)

        def body(j, ms):
            kt = first_tile + 2 * j
            return step(kt + 1, 1, step(kt, 0, ms))

        ms = lax.fori_loop(0, n_tiles // 2, body, tuple(jnp.full((1, cols4), NEG, F32) for _ in range(NSA_GROUPS)))

        @pl.when(n_tiles % 2 == 1)
        def _():
            step(last, 0, ms)

        for g in range(NSA_GROUPS):
            acc = pv_ref[g] + weighted_values(g, last, p_ref[g, (n_tiles - 1) % 2])
            add_gated(g, branch, acc[0:NSA_DH, :] / acc[NSA_DH:NSA_DH + 1, :])

    last_tile = (q0 + tq - 1) // tk
    last_blk = (q0 + tq - 1) // SEL_BLOCK

    def sel_mask(g, kt, k0):
        per_tile = tk // SEL_BLOCK
        blocks = []
        for j in range(per_tile):
            picked = sel_ref[g, pl.ds(kt * per_tile + j, 1), :] > 0.5
            limit = jnp.where(picked, t_row, -1) - (k0 + j * SEL_BLOCK)
            blocks.append(row_blk <= limit)
        return jnp.concatenate(blocks, axis=0)

    win_span = WINDOW + tk
    win_start = pl.multiple_of(jnp.maximum(last_tile - WINDOW // tk, 0) * tk, tk)
    row_w = lax.broadcasted_iota(jnp.int32, (win_span, tq), 0)
    ahead = t_row - win_start
    win_ok = (row_w <= ahead) & (row_w > ahead - WINDOW)

    def window_branch(g):
        s = _dot(kwg_ref[g, pl.ds(win_start, win_span), :], q_group(g))
        probs = []
        for c in range(cols4 // LANES):
            qs = slice((c % (tq // LANES)) * LANES, (c % (tq // LANES) + 1) * LANES)
            sc = jnp.where(win_ok[:, qs], s[:, c * LANES:(c + 1) * LANES], NEG)
            probs.append(jnp.exp2(sc - jnp.max(sc, axis=0, keepdims=True)).astype(BF16))
        acc = _dot(vwt_ref[g, :, pl.ds(win_start, win_span)], jnp.concatenate(probs, axis=1))
        add_gated(g, 2, acc[0:NSA_DH, :] / acc[NSA_DH:NSA_DH + 1, :])

    importance = []
    for g in range(NSA_GROUPS):
        lg = jnp.where(cmp_valid, _dot(kcg_ref[g], q_group(g)), NEG)
        ex = jnp.exp2(lg - jnp.max(lg, axis=0, keepdims=True))
        p_cmp = jnp.where(cmp_valid, ex / jnp.sum(ex, axis=0, keepdims=True), 0.0)
        add_gated(g, 0, _dot(vct_ref[g], p_cmp.astype(BF16)))
        p_grp = p_cmp[:, 0:tq] + p_cmp[:, tq:2 * tq] + p_cmp[:, 2 * tq:3 * tq] + p_cmp[:, 3 * tq:4 * tq]
        importance.append(sum(_dot(ovt_ref[...], part) for part in _split_bf16(p_grp)))
        window_branch(g)

    @pl.when(last_blk < N_SELECT)
    def _():
        for g in range(NSA_GROUPS):
            sel_ref[g] = causal_blk.astype(F32)

    @pl.when(last_blk >= N_SELECT)
    def _():
        for g in range(NSA_GROUPS):
            val = jnp.where(forced, jnp.inf, jnp.where(causal_blk, importance[g], -jnp.inf))
            rank = jnp.zeros((nsel, tq), F32)
            for i in range(nsel):
                vi = val[i:i + 1, :]
                rank += ((vi > val) | ((vi == val) & (j_idx > i))).astype(F32)
            sel_ref[g] = (rank < N_SELECT).astype(F32)

    attend(1, ksg_ref, vst_ref, 0, last_tile + 1, sel_mask)
    o_ref[...] = acc_ref[...].T.astype(o_ref.dtype)


def _overlap_t(s):
    ncmp = s // CMP_STRIDE
    nsel = s // SEL_BLOCK
    c_start = np.arange(ncmp) * CMP_STRIDE
    s_start = np.arange(nsel) * SEL_BLOCK
    overlap_t = ((c_start[None, :] <= s_start[:, None] + SEL_BLOCK - 1)
                 & (c_start[None, :] + CMP_LEN - 1 >= s_start[:, None])).astype(np.float32)
    overlap_t[:, ncmp - 1] = 0.0
    return jnp.asarray(overlap_t, dtype=BF16)


def _nsa_attention(p, qnt, kcg, vct, ksg, vst, kwg, vwt, col):
    bsz, hq, s = qnt.shape
    ncmp = s // CMP_STRIDE
    nsel = s // SEL_BLOCK
    gw, gidx = col["gates"]
    per_row = lambda shape: pl.BlockSpec((None,) + shape, lambda b, i: (b,) + (0,) * len(shape))
    keys = per_row((NSA_GROUPS, s, NSA_DH))
    vals = per_row((NSA_GROUPS, NSA_VPAD, s))
    return pl.pallas_call(
        _nsa_attn_kernel,
        grid=(bsz, s // NSA_TQ),
        in_specs=[pl.BlockSpec((None, hq, NSA_TQ), lambda b, i: (b, 0, i)),
                  pl.BlockSpec((None, NSA_TQ, gw), lambda b, i: (b, i, gidx)),
                  per_row((NSA_GROUPS, ncmp, NSA_DH)), per_row((NSA_GROUPS, NSA_DH, ncmp)),
                  keys, vals, keys, vals,
                  pl.BlockSpec((nsel, ncmp), lambda b, i: (0, 0))],
        out_specs=pl.BlockSpec((None, NSA_TQ, hq), lambda b, i: (b, i, 0)),
        out_shape=jax.ShapeDtypeStruct((bsz, s, hq), BF16),
        scratch_shapes=[pltpu.VMEM((hq, NSA_TQ), F32), pltpu.VMEM((NSA_GROUPS, nsel, NSA_TQ), F32),
                        pltpu.VMEM((NSA_GROUPS, 2, NSA_TK, NSA_HPG * NSA_TQ), F32),
                        pltpu.VMEM((NSA_GROUPS, 2, NSA_TK, NSA_HPG * NSA_TQ), BF16),
                        pltpu.VMEM((NSA_GROUPS, NSA_VPAD, NSA_HPG * NSA_TQ), F32)],
        compiler_params=_cparams(("parallel", "arbitrary")),
        name="nsa_attention",
    )(qnt, p, kcg, vct, ksg, vst, kwg, vwt, _overlap_t(s))


def _layout(pieces, tile):
    col, off, mats = {}, 0, []
    for name, w in pieces:
        width = w.shape[1]
        assert off % width == 0, (name, off, width)
        col[name] = (width, off // width)
        mats.append(w)
        off += width
    pad = -off % tile
    if pad:
        mats.append(jnp.zeros((mats[0].shape[0], pad), mats[0].dtype))
    return jnp.concatenate(mats, axis=1).astype(BF16), col


def _pad_cols(w, width):
    return jnp.pad(w, ((0, 0), (0, width - w.shape[1])))


AB_TILE = 1536
ML_TILE = 1280
PROJ_ROWS = 1024


def _ab_layout(w_in):
    hk = GLA_HEADS * GLA_DK
    hv = GLA_HEADS * GLA_DV
    kv = NSA_GROUPS * NSA_DH
    sizes = [hk, hk, hv, hv, GLA_RANK, NSA_HEADS * NSA_DH] + [kv] * 6 + [3 * NSA_HEADS]
    cuts = np.cumsum(sizes)[:-1].tolist()
    gq, gk, gv, gr, gz, nq, kc, vc, ks, vs, kw, vw, gt = jnp.split(w_in, cuts, axis=1)
    gates = _pad_cols(jnp.concatenate([gz, gt], axis=1), LANES)
    return _layout([("gla_q", gq), ("gla_k", gk), ("gla_v", gv), ("gla_r", gr), ("nsa_q", nq),
                    ("nsa_kc", kc), ("nsa_vc", vc), ("nsa_ks", ks), ("nsa_vs", vs), ("nsa_kw", kw),
                    ("nsa_vw", vw), ("gates", gates)], AB_TILE)


def _ml_layout(w_in):
    hq = ML_HEADS * ML_DQK
    hv = ML_HEADS * ML_DV
    wide = 2 * hq + 2 * hv
    col = {"ml_q": (hq, 0), "ml_k": (hq, 1), "ml_v": (hv, 1), "ml_og": (hv, 2), "gates": (LANES, wide // LANES)}
    total = -(-(wide + LANES) // ML_TILE) * ML_TILE
    return jnp.pad(w_in, ((0, 0), (0, total - w_in.shape[1]))).astype(BF16), col


def kernel(x, c, norm_mix_g, mod_mix_w, mod_mix_b, norm_mlp_g, mod_mlp_w, mod_mlp_b, mlp_w1, mlp_w2, ab_w_in, ab_w_out, gla_w_gate, gla_b_gate, gla_norm_g, nsa_q_norm_g, nsa_k_norm_g, nsa_cmp_pos_k, nsa_cmp_pos_v, nsa_cmp_k_w1, nsa_cmp_k_w2, nsa_cmp_v_w1, nsa_cmp_v_w2, ml_w_in, ml_w_out, ml_conv_w, ml_b_i, ml_b_f, ml_norm_g):
    bsz, s, d = x.shape
    depth = norm_mix_g.shape[0]
    mod_mix = _ada_mod(c, mod_mix_w, mod_mix_b)
    mod_mlp = _ada_mod(c, mod_mlp_w, mod_mlp_b)
    mlp_w1_bf16, mlp_w2_bf16 = mlp_w1.astype(BF16), mlp_w2.astype(BF16)

    def split_mod(m):
        return [m[:, None, i * d:(i + 1) * d] for i in range(3)]

    for layer in range(depth):
        shift, scale, gate = split_mod(mod_mix[layer])
        if layer % 2 == 0:
            e = layer // 2
            w_in, col = _ab_layout(ab_w_in[e])
            p = _norm_proj(x, norm_mix_g[layer], scale, shift, w_in, tm=PROJ_ROWS, tn=AB_TILE)
            w_gate = jnp.pad(gla_w_gate[e], ((0, LANES - GLA_RANK), (0, 0))).astype(BF16)
            o_gla = _gla(p, w_gate, gla_b_gate[e], gla_norm_g[e], col)
            q_gain = (jnp.tile(nsa_q_norm_g[e], NSA_HEADS) * (NSA_DH ** -0.5 * LOG2E)).reshape(-1, 1)
            k_gain = jnp.tile(nsa_k_norm_g[e], NSA_GROUPS).reshape(1, -1)
            qnt, ksg, vst, kwg, vwt = _nsa_prep(p, q_gain, k_gain, col)
            kcg, vct = _compress(p, nsa_cmp_pos_k[e], nsa_cmp_pos_v[e], nsa_cmp_k_w1[e],
                                 nsa_cmp_k_w2[e], nsa_cmp_v_w1[e], nsa_cmp_v_w2[e], k_gain, col)
            o_nsa = _nsa_attention(p, qnt, kcg, vct, ksg, vst, kwg, vwt, col)
            w_out = ab_w_out[e].astype(BF16)
            half = GLA_HEADS * GLA_DV
            x = _out_proj([o_gla, o_nsa], [w_out[:half], w_out[half:]], x, gate)
        else:
            o = layer // 2
            w_in, col = _ml_layout(ml_w_in[o])
            p = _norm_proj(x, norm_mix_g[layer], scale, shift, w_in, tm=PROJ_ROWS, tn=ML_TILE)
            bias = jnp.concatenate([ml_b_i[o], ml_b_f[o]])
            bias_col = _pad_cols(bias.reshape(1, -1), LANES)
            bias_row = jnp.broadcast_to(bias.reshape(-1, 1), (2 * ML_HEADS, LANES))
            hh = _mlstm(p, ml_conv_w[o], bias_col, bias_row, ml_norm_g[o], col)
            x = _out_proj([hh], [ml_w_out[o].astype(BF16)], x, gate)
        shift, scale, gate = split_mod(mod_mlp[layer])
        x = _mlp_sublayer(x, norm_mlp_g[layer], scale, shift, gate, mlp_w1_bf16, mlp_w2_bf16, layer)
    return x
```

```python
import functools

import numpy as np
import jax
import jax.numpy as jnp
from jax import lax
from jax.experimental import pallas as pl
from jax.experimental.pallas import tpu as pltpu

F32 = jnp.float32
BF16 = jnp.bfloat16
HI = lax.Precision.HIGHEST

EPS = 1e-6
NEG = -1e30
LOG2E = 1.4426950408889634

VMEM_LIMIT_BYTES = 56 * 1024 * 1024
LANES = 128

GLA_HEADS = 4
GLA_DK = 128
GLA_DV = 256
GLA_RANK = 16
GLA_GATE_TAU = 16.0
CHUNK = 64
ML_CHUNK = 128
REC_BLOCK = 512

NSA_HEADS = 16
NSA_GROUPS = 4
NSA_HPG = 4
NSA_DH = 64
CMP_LEN = 32
CMP_STRIDE = 16
SEL_BLOCK = 64
N_SELECT = 16
WINDOW = 512
NSA_TQ = 256
NSA_TK = 256
NSA_VPAD = 80

ML_HEADS = 4
ML_DQK = 256
ML_DV = 512
ML_CONV = 4


def _cparams(sem):
    return pltpu.CompilerParams(dimension_semantics=sem, vmem_limit_bytes=VMEM_LIMIT_BYTES)


def _sigmoid(x):
    return 1.0 / (1.0 + jnp.exp(-x))


def _silu(x):
    return x * _sigmoid(x)


def _log_sigmoid(x):
    return jnp.minimum(x, 0.0) - jnp.log(1.0 + jnp.exp(-jnp.abs(x)))


def _dot(a, b, precision=None):
    return jnp.dot(a, b, preferred_element_type=F32, precision=precision)


def _dot_nt(a, b, precision=None):
    return lax.dot_general(a, b, (((1,), (1,)), ((), ())), preferred_element_type=F32, precision=precision)


def _split_bf16(x, parts=2):
    out = []
    for _ in range(parts - 1):
        piece = x.astype(BF16)
        out.append(piece)
        x = x - piece.astype(F32)
    return out + [x.astype(BF16)]


def _rms(x):
    return x * lax.rsqrt(jnp.mean(x * x, axis=-1, keepdims=True) + EPS)


def _mod_kernel(c_ref, w_ref, b_ref, o_ref):
    a = _silu(c_ref[...]).astype(BF16)
    o_ref[...] = _dot(a, w_ref[...].astype(BF16)) + b_ref[...]


def _ada_mod(c, w, b):
    nl, d, d3 = w.shape
    bsz = c.shape[0]
    tn = 768
    return pl.pallas_call(
        _mod_kernel,
        grid=(nl, d3 // tn),
        in_specs=[pl.BlockSpec((bsz, d), lambda l, j: (0, 0)),
                  pl.BlockSpec((None, d, tn), lambda l, j: (l, 0, j)),
                  pl.BlockSpec((None, 1, tn), lambda l, j: (l, 0, j))],
        out_specs=pl.BlockSpec((None, bsz, tn), lambda l, j: (l, 0, j)),
        out_shape=jax.ShapeDtypeStruct((nl, bsz, d3), F32),
        compiler_params=_cparams(("parallel", "parallel")),
        name="ada_mod",
    )(c, w, b.reshape(nl, 1, d3))


def _modulated_norm(x_ref, g_ref, sc_ref, sh_ref):
    x = x_ref[...]
    gain = g_ref[...] * (1.0 + sc_ref[...])
    inv = lax.rsqrt(jnp.mean(x * x, axis=-1, keepdims=True) + EPS)
    return ((x * inv) * gain + sh_ref[...]).astype(BF16)


def _proj_kernel(x_ref, g_ref, sc_ref, sh_ref, w_ref, o_ref, h_ref):
    i = pl.program_id(2)

    @pl.when(pl.program_id(1) == 0)
    def _():
        h_ref[i] = _modulated_norm(x_ref, g_ref, sc_ref, sh_ref)

    o_ref[...] = _dot(h_ref[i], w_ref[...])


def _norm_proj(x, g, scale, shift, w, tm, tn):
    bsz, s, d = x.shape
    n = w.shape[1]
    rows = s // tm
    x_spec = pl.BlockSpec((None, tm, d), lambda b, j, i: (b, jnp.where(j == 0, i, rows - 1), 0))
    return pl.pallas_call(
        _proj_kernel,
        grid=(bsz, n // tn, rows),
        in_specs=[x_spec,
                  pl.BlockSpec((1, d), lambda b, j, i: (0, 0)),
                  pl.BlockSpec((None, 1, d), lambda b, j, i: (b, 0, 0)),
                  pl.BlockSpec((None, 1, d), lambda b, j, i: (b, 0, 0)),
                  pl.BlockSpec((d, tn), lambda b, j, i: (0, j))],
        out_specs=pl.BlockSpec((None, tm, tn), lambda b, j, i: (b, i, j)),
        out_shape=jax.ShapeDtypeStruct((bsz, s, n), F32),
        scratch_shapes=[pltpu.VMEM((rows, tm, d), BF16)],
        compiler_params=_cparams(("parallel", "arbitrary", "arbitrary")),
        name="norm_proj",
    )(x, g.reshape(1, d), scale, shift, w)


def _mlp_kernel(x_ref, g_ref, sc_ref, sh_ref, gate_ref, w1_ref, w2_ref, o_ref, h_ref):
    @pl.when(pl.program_id(2) == 0)
    def _():
        h_ref[...] = _modulated_norm(x_ref, g_ref, sc_ref, sh_ref)
        o_ref[...] = x_ref[...]

    u = jnp.maximum(_dot(h_ref[...], w1_ref[...]), 0.0)
    o_ref[...] += gate_ref[...] * _dot((u * u).astype(BF16), w2_ref[...])


def _mlp_sublayer(x, g, scale, shift, gate, w1, w2, layer, tm=512, tf=1024):
    bsz, s, d = x.shape
    ff = w1.shape[2]
    vec = pl.BlockSpec((None, 1, d), lambda b, i, f: (b, 0, 0))
    return pl.pallas_call(
        _mlp_kernel,
        grid=(bsz, s // tm, ff // tf),
        in_specs=[pl.BlockSpec((None, tm, d), lambda b, i, f: (b, i, 0)),
                  pl.BlockSpec((1, d), lambda b, i, f: (0, 0)),
                  vec, vec, vec,
                  pl.BlockSpec((None, d, tf), lambda b, i, f: (layer, 0, f)),
                  pl.BlockSpec((None, tf, d), lambda b, i, f: (layer, f, 0))],
        out_specs=pl.BlockSpec((None, tm, d), lambda b, i, f: (b, i, 0)),
        out_shape=jax.ShapeDtypeStruct((bsz, s, d), F32),
        scratch_shapes=[pltpu.VMEM((tm, d), BF16)],
        compiler_params=_cparams(("parallel", "parallel", "arbitrary")),
        name="mlp_sublayer",
    )(x, g.reshape(1, d), scale, shift, gate, w1, w2)


def _outproj_kernel(n_in, *refs):
    a_refs, w_refs = refs[:n_in], refs[n_in:2 * n_in]
    x_ref, gate_ref, o_ref = refs[2 * n_in:]
    y = _dot(a_refs[0][...], w_refs[0][...])
    for a_ref, w_ref in zip(a_refs[1:], w_refs[1:]):
        y += _dot(a_ref[...], w_ref[...])
    o_ref[...] = x_ref[...] + gate_ref[...] * y


def _out_proj(acts, ws, x, gate, tm=512):
    bsz, s, d = x.shape
    n_in = len(acts)
    in_specs = [pl.BlockSpec((None, tm, a.shape[-1]), lambda b, i: (b, i, 0)) for a in acts]
    in_specs += [pl.BlockSpec(w.shape, lambda b, i: (0, 0)) for w in ws]
    in_specs += [pl.BlockSpec((None, tm, d), lambda b, i: (b, i, 0)),
                 pl.BlockSpec((None, 1, d), lambda b, i: (b, 0, 0))]
    return pl.pallas_call(
        functools.partial(_outproj_kernel, n_in),
        grid=(bsz, s // tm),
        in_specs=in_specs,
        out_specs=pl.BlockSpec((None, tm, d), lambda b, i: (b, i, 0)),
        out_shape=jax.ShapeDtypeStruct((bsz, s, d), F32),
        compiler_params=_cparams(("parallel", "parallel")),
        name="out_proj",
    )(*acts, *ws, x, gate)


def _tri(n, upper=False):
    r = lax.broadcasted_iota(jnp.int32, (n, n), 0)
    c = lax.broadcasted_iota(jnp.int32, (n, n), 1)
    return (c >= r) if upper else (c <= r)


def _gla_kernel(q_ref, k_ref, v_ref, r_ref, zg_ref, wg_ref, bg_ref, ng_ref, o_ref, st_ref):
    @pl.when(pl.program_id(1) == 0)
    def _():
        st_ref[...] = jnp.zeros_like(st_ref)

    tril = _tri(CHUNK)
    tril_b = tril.astype(BF16)
    pre = _dot(zg_ref[...].astype(BF16), wg_ref[...]) + bg_ref[...]
    parts = _split_bf16(_log_sigmoid(pre) / GLA_GATE_TAU, 3)
    chunks = [slice(c * CHUNK, (c + 1) * CHUNK) for c in range(REC_BLOCK // CHUNK)]
    cum_all = jnp.concatenate([sum(_dot(tril_b, part[rows, :]) for part in parts) for rows in chunks], axis=0)
    q_dec_all = (q_ref[...] * GLA_DK ** -0.5) * jnp.exp(cum_all)
    k_dec_all = k_ref[...] * jnp.exp(-cum_all)
    qb_all = q_dec_all.astype(BF16)
    kb_all = k_dec_all.astype(BF16)
    vb_all = v_ref[...].astype(BF16)
    gate_all = _silu(r_ref[...])
    intra = {}
    for c, rows in enumerate(chunks):
        for h in range(GLA_HEADS):
            ks = slice(h * GLA_DK, (h + 1) * GLA_DK)
            vs = slice(h * GLA_DV, (h + 1) * GLA_DV)
            att = jnp.where(tril, _dot_nt(qb_all[rows, ks], kb_all[rows, ks]), 0.0)
            intra[c, h] = _dot(att.astype(BF16), vb_all[rows, vs])
    for c, rows in enumerate(chunks):
        cum = cum_all[rows, :]
        total = cum[CHUNK - 1:CHUNK, :]
        k_end = (k_ref[rows, :] * jnp.exp(total - cum)).astype(BF16)
        decay = jnp.exp(total)
        for h in range(GLA_HEADS):
            ks = slice(h * GLA_DK, (h + 1) * GLA_DK)
            vs = slice(h * GLA_DV, (h + 1) * GLA_DV)
            st = st_ref[h]
            o = intra[c, h] + _dot_nt(qb_all[rows, ks], st.astype(BF16))
            st_ref[h] = st * decay[:, ks] + _dot(v_ref[rows, vs].T.astype(BF16), k_end[:, ks])
            o_ref[rows, vs] = ((_rms(o) * ng_ref[...]) * gate_all[rows, vs]).astype(o_ref.dtype)


def _gla(p, w_gate, b_gate, norm_g, col):
    bsz, s, _ = p.shape
    t = REC_BLOCK

    def piece(name):
        width, idx = col[name]
        return pl.BlockSpec((None, t, width), lambda b, i, idx=idx: (b, i, idx))

    const = lambda shape: pl.BlockSpec(shape, lambda b, i: (0,) * len(shape))
    hk = GLA_HEADS * GLA_DK
    return pl.pallas_call(
        _gla_kernel,
        grid=(bsz, s // t),
        in_specs=[piece("gla_q"), piece("gla_k"), piece("gla_v"), piece("gla_r"), piece("gates"),
                  const((LANES, hk)), const((1, hk)), const((1, GLA_DV))],
        out_specs=pl.BlockSpec((None, t, GLA_HEADS * GLA_DV), lambda b, i: (b, i, 0)),
        out_shape=jax.ShapeDtypeStruct((bsz, s, GLA_HEADS * GLA_DV), BF16),
        scratch_shapes=[pltpu.VMEM((GLA_HEADS, GLA_DV, GLA_DK), F32)],
        compiler_params=_cparams(("parallel", "arbitrary")),
        name="gla",
    )(p, p, p, p, p, w_gate, b_gate.reshape(1, hk), norm_g.reshape(1, GLA_DV))


def _mlstm_kernel(q_ref, k_ref, v_ref, og_ref, gc_ref, wq_ref, wk_ref, bc_ref, br_ref, ng_ref,
                  o_ref, c_ref, n_ref, m_ref, qx_ref, kx_ref):
    t = REC_BLOCK

    @pl.when(pl.program_id(1) == 0)
    def _():
        c_ref[...] = jnp.zeros_like(c_ref)
        n_ref[...] = jnp.zeros_like(n_ref)
        m_ref[...] = jnp.zeros_like(m_ref)
        qx_ref[0:8, :] = jnp.zeros((8, qx_ref.shape[1]), F32)
        kx_ref[0:8, :] = jnp.zeros((8, kx_ref.shape[1]), F32)

    def conv_silu(x_ref, xx_ref, w_ref):
        xx_ref[8:8 + t, :] = x_ref[...]
        y = xx_ref[8:8 + t, :] * w_ref[ML_CONV - 1:ML_CONV, :]
        for back in range(1, ML_CONV):
            y += xx_ref[8 - back:8 - back + t, :] * w_ref[ML_CONV - 1 - back:ML_CONV - back, :]
        xx_ref[0:8, :] = xx_ref[t:t + 8, :]
        return _silu(y)

    q_all = conv_silu(q_ref, qx_ref, wq_ref) * ML_DQK ** -0.5
    k_all = conv_silu(k_ref, kx_ref, wk_ref)

    tril = _tri(ML_CHUNK)
    tril_b = tril.astype(BF16)
    triu_b = _tri(ML_CHUNK, upper=True).astype(BF16)
    lane = lax.broadcasted_iota(jnp.int32, (ML_CHUNK, LANES), 1)
    is_f_col = (lane >= ML_HEADS) & (lane < 2 * ML_HEADS)
    sub = lax.broadcasted_iota(jnp.int32, (8, ML_CHUNK), 0)
    is_f_row = sub >= ML_HEADS
    gates_by_row = gc_ref[...].T[0:2 * ML_HEADS, :]

    for c in range(t // ML_CHUNK):
        rows = slice(c * ML_CHUNK, (c + 1) * ML_CHUNK)
        gcol = gc_ref[rows, :] + bc_ref[...]
        gcol = jnp.where(is_f_col, _log_sigmoid(gcol), gcol)
        bcol = sum(_dot(tril_b, part) for part in _split_bf16(gcol, 3))
        grow = gates_by_row[:, rows] + br_ref[:, 0:1]
        grow = jnp.where(is_f_row, _log_sigmoid(grow), grow)
        brow = sum(_dot(part, triu_b) for part in _split_bf16(grow, 3))
        for h in range(ML_HEADS):
            qs = slice(h * ML_DQK, (h + 1) * ML_DQK)
            vs = slice(h * ML_DV, (h + 1) * ML_DV)
            b_col = bcol[:, ML_HEADS + h:ML_HEADS + h + 1]
            b_row = brow[ML_HEADS + h:ML_HEADS + h + 1, :]
            li_col = gcol[:, h:h + 1]
            li_row = grow[h:h + 1, :]
            m_prev = m_ref[h:h + 1, 0:1]
            log_d = jnp.where(tril, b_col - b_row + li_row, -jnp.inf)
            log_inter = b_col + m_prev
            m_comb = jnp.maximum(log_inter, jnp.max(log_d, axis=-1, keepdims=True))
            d = jnp.exp(log_d - m_comb)
            w_inter = jnp.exp(log_inter - m_comb)
            qh = q_all[rows, qs]
            kh = k_all[rows, qs]
            qb = qh.astype(BF16)
            vb = v_ref[rows, vs].astype(BF16)
            sc = _dot_nt(qb, kh.astype(BF16)) * d
            cm = c_ref[h]
            nv = n_ref[h]
            num = _dot(sc.astype(BF16), vb) + w_inter * _dot(qb, cm.astype(BF16))
            den = jnp.sum(sc, axis=-1, keepdims=True) + w_inter * jnp.sum(qh * nv, axis=-1, keepdims=True)
            hh = num / jnp.maximum(jnp.abs(den), jnp.exp(-m_comb))
            total = b_col[ML_CHUNK - 1:ML_CHUNK, :]
            log_w = total - b_col + li_col
            m_new = jnp.maximum(total + m_prev, jnp.max(log_w, axis=0, keepdims=True))
            decay = jnp.exp(total + m_prev - m_new)
            kw = kh * jnp.exp(log_w - m_new)
            c_ref[h] = decay * cm + _dot(kw.T.astype(BF16), vb)
            n_ref[h] = decay * nv + jnp.sum(kw, axis=0, keepdims=True)
            m_ref[h:h + 1, :] = jnp.broadcast_to(m_new, (1, LANES))
            og = og_ref[rows, vs]
            o_ref[rows, vs] = ((_rms(hh) * ng_ref[...]) * _sigmoid(og)).astype(o_ref.dtype)


def _mlstm(p, conv_w, bias_col, bias_row, norm_g, col):
    bsz, s, _ = p.shape
    t = REC_BLOCK
    hq = ML_HEADS * ML_DQK
    hv = ML_HEADS * ML_DV

    def piece(name):
        width, idx = col[name]
        return pl.BlockSpec((None, t, width), lambda b, i, idx=idx: (b, i, idx))

    const = lambda shape: pl.BlockSpec(shape, lambda b, i: (0,) * len(shape))
    return pl.pallas_call(
        _mlstm_kernel,
        grid=(bsz, s // t),
        in_specs=[piece("ml_q"), piece("ml_k"), piece("ml_v"), piece("ml_og"), piece("gates"),
                  pl.BlockSpec((ML_CONV, hq), lambda b, i: (0, 0)),
                  pl.BlockSpec((ML_CONV, hq), lambda b, i: (0, 1)),
                  const((1, LANES)), const((8, LANES)), const((1, ML_DV))],
        out_specs=pl.BlockSpec((None, t, hv), lambda b, i: (b, i, 0)),
        out_shape=jax.ShapeDtypeStruct((bsz, s, hv), BF16),
        scratch_shapes=[pltpu.VMEM((ML_HEADS, ML_DQK, ML_DV), F32),
                        pltpu.VMEM((ML_HEADS, 1, ML_DQK), F32),
                        pltpu.VMEM((8, LANES), F32),
                        pltpu.VMEM((t + 8, hq), F32),
                        pltpu.VMEM((t + 8, hq), F32)],
        compiler_params=_cparams(("parallel", "arbitrary")),
        name="mlstm",
    )(p, p, p, p, p, conv_w, conv_w, bias_col, bias_row, norm_g.reshape(1, ML_DV))


def _seg_norm(x, seg_ref, segt_ref):
    if seg_ref.dtype == BF16:
        seg_sum = lambda v, m_ref: sum(_dot(part, m_ref[...]) for part in _split_bf16(v))
    else:
        seg_sum = lambda v, m_ref: _dot(v, m_ref[...], precision=HI)
    ss = seg_sum(x * x, seg_ref) * (1.0 / NSA_DH)
    return x * seg_sum(lax.rsqrt(ss + EPS), segt_ref)


def _nsa_prep_kernel(q_ref, ks_ref, vs_ref, kw_ref, vw_ref, qg_ref, kg_ref, segk_ref, segkt_ref,
                     qnt_ref, ksg_ref, vst_ref, kwg_ref, vwt_ref):
    t = q_ref.shape[0]
    qt = q_ref[...].T
    for h in range(NSA_HEADS):
        hs = slice(h * NSA_DH, (h + 1) * NSA_DH)
        blk = qt[hs, :]
        inv = lax.rsqrt(jnp.mean(blk * blk, axis=0, keepdims=True) + EPS)
        qnt_ref[hs, :] = ((blk * inv) * qg_ref[hs, :]).astype(BF16)
    ones = jnp.ones((NSA_VPAD - NSA_DH, t), BF16)
    for k_ref, v_ref, kg_out, vt_out in ((ks_ref, vs_ref, ksg_ref, vst_ref), (kw_ref, vw_ref, kwg_ref, vwt_ref)):
        kn = (_seg_norm(k_ref[...], segk_ref, segkt_ref) * kg_ref[...]).astype(BF16)
        vt = v_ref[...].T.astype(BF16)
        for g in range(NSA_GROUPS):
            gs = slice(g * NSA_DH, (g + 1) * NSA_DH)
            kg_out[g] = kn[:, gs]
            vt_out[g, 0:NSA_DH, :] = vt[gs, :]
            vt_out[g, NSA_DH:NSA_VPAD, :] = ones


def _seg_matrices(width, dtype):
    seg = (np.arange(width)[:, None] // NSA_DH == np.arange(LANES)[None, :]).astype(np.float32)
    return jnp.asarray(seg, dtype=dtype), jnp.asarray(seg.T, dtype=dtype)


def _nsa_prep(p, q_gain, k_gain, col, t=512):
    bsz, s, _ = p.shape
    hq = NSA_HEADS * NSA_DH
    kv = NSA_GROUPS * NSA_DH

    def piece(name):
        width, idx = col[name]
        return pl.BlockSpec((None, t, width), lambda b, i, idx=idx: (b, i, idx))

    const = lambda shape: pl.BlockSpec(shape, lambda b, i: (0,) * len(shape))
    segk, segkt = _seg_matrices(kv, BF16)
    keys = pl.BlockSpec((None, NSA_GROUPS, t, NSA_DH), lambda b, i: (b, 0, i, 0))
    vals = pl.BlockSpec((None, NSA_GROUPS, NSA_VPAD, t), lambda b, i: (b, 0, 0, i))
    shp = lambda *dims, dt=BF16: jax.ShapeDtypeStruct(dims, dt)
    keys_shape = shp(bsz, NSA_GROUPS, s, NSA_DH)
    vals_shape = shp(bsz, NSA_GROUPS, NSA_VPAD, s)
    return pl.pallas_call(
        _nsa_prep_kernel,
        grid=(bsz, s // t),
        in_specs=[piece("nsa_q"), piece("nsa_ks"), piece("nsa_vs"), piece("nsa_kw"), piece("nsa_vw"),
                  const((hq, 1)), const((1, kv)), const((kv, LANES)), const((LANES, kv))],
        out_specs=[pl.BlockSpec((None, hq, t), lambda b, i: (b, 0, i)), keys, vals, keys, vals],
        out_shape=[shp(bsz, hq, s), keys_shape, vals_shape, keys_shape, vals_shape],
        compiler_params=_cparams(("parallel", "parallel")),
        name="nsa_prep",
    )(p, p, p, p, p, q_gain, k_gain, segk, segkt)


def _compress_kernel(kc0_ref, kc1_ref, vc0_ref, vc1_ref, posk_ref, posv_ref, kw1a_ref, kw1b_ref, kw2_ref,
                     vw1a_ref, vw1b_ref, vw2_ref, kg_ref, segk_ref, segkt_ref, kcg_ref, vct_ref):
    def hidden(x_refs, pos_ref, w1a_ref, w1b_ref):
        nsub = x_refs[0].shape[0] // CMP_STRIDE
        first = second = None
        for l in range(CMP_STRIDE):
            x = jnp.concatenate([r[pl.ds(l, nsub, stride=CMP_STRIDE), :] for r in x_refs], axis=1)
            a = _dot((x + pos_ref[l:l + 1, :]).astype(BF16), w1a_ref[l])
            b = _dot((x + pos_ref[CMP_STRIDE + l:CMP_STRIDE + l + 1, :]).astype(BF16), w1b_ref[l])
            first = a if first is None else first + a
            second = b if second is None else second + b
        row = lax.broadcasted_iota(jnp.int32, second.shape, 0)
        nxt = jnp.where(row == nsub - 1, 0.0, pltpu.roll(second, nsub - 1, 0))
        return _silu(first + nxt)

    kc = _dot(hidden((kc0_ref, kc1_ref), posk_ref, kw1a_ref, kw1b_ref).astype(BF16), kw2_ref[...])
    kn = (_seg_norm(kc, segk_ref, segkt_ref) * kg_ref[...]).astype(BF16)
    vt = _dot(vw2_ref[...], hidden((vc0_ref, vc1_ref), posv_ref, vw1a_ref, vw1b_ref).T.astype(BF16)).astype(BF16)
    for g in range(NSA_GROUPS):
        gs = slice(g * NSA_DH, (g + 1) * NSA_DH)
        kcg_ref[g] = kn[:, gs]
        vct_ref[g] = vt[gs, :]


def _expand_w1(w1):
    w = w1.reshape(2, CMP_STRIDE, NSA_DH, NSA_DH)
    eye = jnp.eye(NSA_GROUPS, dtype=w1.dtype)
    big = jnp.einsum('hlde,gk->hlgdke', w, eye)
    big = big.reshape(2, CMP_STRIDE, NSA_GROUPS * NSA_DH, NSA_GROUPS * NSA_DH).astype(BF16)
    return big[0], big[1]


def _expand_pos(pos):
    return jnp.tile(pos, (1, NSA_GROUPS))


def _block_diag(w2):
    return jnp.kron(jnp.eye(NSA_GROUPS, dtype=w2.dtype), w2).astype(BF16)


def _compress(p, pos_k, pos_v, ck_w1, ck_w2, cv_w1, cv_w2, k_gain, col):
    bsz, s, _ = p.shape
    kv = NSA_GROUPS * NSA_DH
    nsub = s // CMP_STRIDE
    segk, segkt = _seg_matrices(kv, F32)
    kw1a, kw1b = _expand_w1(ck_w1)
    vw1a, vw1b = _expand_w1(cv_w1)
    const = lambda shape: pl.BlockSpec(shape, lambda b: (0,) * len(shape))

    def halves(name):
        width, idx = col[name]
        assert width == 2 * LANES
        return [pl.BlockSpec((None, s, LANES), lambda b, j=2 * idx + h: (b, 0, j)) for h in range(2)]

    w1_spec = const((CMP_STRIDE, kv, kv))
    return pl.pallas_call(
        _compress_kernel,
        grid=(bsz,),
        in_specs=[*halves("nsa_kc"), *halves("nsa_vc"), const((CMP_LEN, kv)), const((CMP_LEN, kv)),
                  w1_spec, w1_spec, const((kv, kv)),
                  w1_spec, w1_spec, const((kv, kv)),
                  const((1, kv)), const((kv, LANES)), const((LANES, kv))],
        out_specs=[pl.BlockSpec((None, NSA_GROUPS, nsub, NSA_DH), lambda b: (b, 0, 0, 0)),
                   pl.BlockSpec((None, NSA_GROUPS, NSA_DH, nsub), lambda b: (b, 0, 0, 0))],
        out_shape=[jax.ShapeDtypeStruct((bsz, NSA_GROUPS, nsub, NSA_DH), BF16),
                   jax.ShapeDtypeStruct((bsz, NSA_GROUPS, NSA_DH, nsub), BF16)],
        compiler_params=_cparams(("parallel",)),
        name="nsa_compress",
    )(p, p, p, p, _expand_pos(pos_k), _expand_pos(pos_v),
      kw1a, kw1b, _block_diag(ck_w2), vw1a, vw1b, _block_diag(cv_w2).T, k_gain, segk, segkt)


def _nsa_attn_kernel(qt_ref, gt_ref, kcg_ref, vct_ref, ksg_ref, vst_ref, kwg_ref, vwt_ref, ovt_ref,
                     o_ref, acc_ref, sel_ref, s_ref, p_ref, pv_ref):
    tq, tk = NSA_TQ, NSA_TK
    q0 = pl.program_id(1) * tq
    ncmp = kcg_ref.shape[1]
    nsel = ovt_ref.shape[0]
    cols4 = NSA_HPG * tq

    t_cmp = q0 + lax.broadcasted_iota(jnp.int32, (ncmp, cols4), 1) % tq
    n_idx = lax.broadcasted_iota(jnp.int32, (ncmp, cols4), 0)
    cmp_valid = n_idx * CMP_STRIDE + (CMP_LEN - 1) <= t_cmp

    j_idx = lax.broadcasted_iota(jnp.int32, (nsel, tq), 0)
    cur = (q0 + lax.broadcasted_iota(jnp.int32, (nsel, tq), 1)) // SEL_BLOCK
    forced = (j_idx == 0) | (j_idx == cur) | (j_idx == cur - 1)
    causal_blk = j_idx <= cur

    row_blk = lax.broadcasted_iota(jnp.int32, (SEL_BLOCK, tq), 0)
    t_row = q0 + lax.broadcasted_iota(jnp.int32, (1, tq), 1)

    gates_t = _sigmoid(gt_ref[...]).T

    def add_gated(g, branch, o4):
        for i in range(NSA_HPG):
            h = g * NSA_HPG + i
            hs = slice(h * NSA_DH, (h + 1) * NSA_DH)
            gc = GLA_RANK + 3 * h + branch
            term = gates_t[gc:gc + 1, :] * o4[:, i * tq:(i + 1) * tq]
            acc_ref[hs, :] = term if branch == 0 else acc_ref[hs, :] + term

    def q_group(g):
        return jnp.concatenate([qt_ref[(g * NSA_HPG + i) * NSA_DH:(g * NSA_HPG + i + 1) * NSA_DH, :]
                                for i in range(NSA_HPG)], axis=1)

    def attend(branch, k_ref, vt_ref, first_tile, n_tiles, mask_fn):
        last = first_tile + n_tiles - 1

        def scores(g, kt):
            return _dot(k_ref[g, pl.ds(pl.multiple_of(kt * tk, tk), tk), :], q_group(g))

        def weighted_values(g, kt, p):
            return _dot(vt_ref[g, :, pl.ds(pl.multiple_of(kt * tk, tk), tk)], p)

        def step(kt, cur, ms):
            nxt = 1 - cur
            out = []
            for g in range(NSA_GROUPS):
                m = ms[g]
                s_ref[g, nxt] = scores(g, jnp.minimum(kt + 1, last))
                acc = pv_ref[g] + weighted_values(g, jnp.maximum(kt - 1, first_tile), p_ref[g, nxt])
                ok = mask_fn(g, kt, kt * tk)
                m_new = []
                for c in range(cols4 // LANES):
                    cs = slice(c * LANES, (c + 1) * LANES)
                    qs = slice((c % (tq // LANES)) * LANES, (c % (tq // LANES) + 1) * LANES)
                    s = jnp.where(ok[:, qs], s_ref[g, cur, :, cs], NEG)
                    m_c = jnp.maximum(m[:, cs], jnp.max(s, axis=0, keepdims=True))
                    p_ref[g, cur, :, cs] = jnp.exp2(s - m_c).astype(BF16)
                    m_new.append(m_c)
                m_new = jnp.concatenate(m_new, axis=1)
                pv_ref[g] = acc * jnp.exp2(m - m_new)
                out.append(m_new)
            return tuple(out)

        for g in range(NSA_GROUPS):
            s_ref[g, 0] = scores(g, first_tile)
            p_ref[g, 1] = jnp.zeros(p_ref.shape[2:], BF16)
        pv_ref[...] = jnp.zeros_like(pv_ref)

        def body(j, ms):
            kt = first_tile + 2 * j
            return step(kt + 1, 1, step(kt, 0, ms))

        ms = lax.fori_loop(0, n_tiles // 2, body, tuple(jnp.full((1, cols4), NEG, F32) for _ in range(NSA_GROUPS)))

        @pl.when(n_tiles % 2 == 1)
        def _():
            step(last, 0, ms)

        for g in range(NSA_GROUPS):
            acc = pv_ref[g] + weighted_values(g, last, p_ref[g, (n_tiles - 1) % 2])
            add_gated(g, branch, acc[0:NSA_DH, :] / acc[NSA_DH:NSA_DH + 1, :])

    last_tile = (q0 + tq - 1) // tk
    last_blk = (q0 + tq - 1) // SEL_BLOCK

    def sel_mask(g, kt, k0):
        per_tile = tk // SEL_BLOCK
        blocks = []
        for j in range(per_tile):
            picked = sel_ref[g, pl.ds(kt * per_tile + j, 1), :] > 0.5
            limit = jnp.where(picked, t_row, -1) - (k0 + j * SEL_BLOCK)
            blocks.append(row_blk <= limit)
        return jnp.concatenate(blocks, axis=0)

    win_span = WINDOW + tk
    win_start = pl.multiple_of(jnp.maximum(last_tile - WINDOW // tk, 0) * tk, tk)
    row_w = lax.broadcasted_iota(jnp.int32, (win_span, tq), 0)
    ahead = t_row - win_start
    win_ok = (row_w <= ahead) & (row_w > ahead - WINDOW)

    def window_branch(g):
        s = _dot(kwg_ref[g, pl.ds(win_start, win_span), :], q_group(g))
        probs = []
        for c in range(cols4 // LANES):
            qs = slice((c % (tq // LANES)) * LANES, (c % (tq // LANES) + 1) * LANES)
            sc = jnp.where(win_ok[:, qs], s[:, c * LANES:(c + 1) * LANES], NEG)
            probs.append(jnp.exp2(sc - jnp.max(sc, axis=0, keepdims=True)).astype(BF16))
        acc = _dot(vwt_ref[g, :, pl.ds(win_start, win_span)], jnp.concatenate(probs, axis=1))
        add_gated(g, 2, acc[0:NSA_DH, :] / acc[NSA_DH:NSA_DH + 1, :])

    importance = []
    for g in range(NSA_GROUPS):
        lg = jnp.where(cmp_valid, _dot(kcg_ref[g], q_group(g)), NEG)
        ex = jnp.exp2(lg - jnp.max(lg, axis=0, keepdims=True))
        p_cmp = jnp.where(cmp_valid, ex / jnp.sum(ex, axis=0, keepdims=True), 0.0)
        add_gated(g, 0, _dot(vct_ref[g], p_cmp.astype(BF16)))
        p_grp = p_cmp[:, 0:tq] + p_cmp[:, tq:2 * tq] + p_cmp[:, 2 * tq:3 * tq] + p_cmp[:, 3 * tq:4 * tq]
        importance.append(sum(_dot(ovt_ref[...], part) for part in _split_bf16(p_grp)))
        window_branch(g)

    @pl.when(last_blk < N_SELECT)
    def _():
        for g in range(NSA_GROUPS):
            sel_ref[g] = causal_blk.astype(F32)

    @pl.when(last_blk >= N_SELECT)
    def _():
        for g in range(NSA_GROUPS):
            val = jnp.where(forced, jnp.inf, jnp.where(causal_blk, importance[g], -jnp.inf))
            rank = jnp.zeros((nsel, tq), F32)
            for i in range(nsel):
                vi = val[i:i + 1, :]
                rank += ((vi > val) | ((vi == val) & (j_idx > i))).astype(F32)
            sel_ref[g] = (rank < N_SELECT).astype(F32)

    attend(1, ksg_ref, vst_ref, 0, last_tile + 1, sel_mask)
    o_ref[...] = acc_ref[...].T.astype(o_ref.dtype)


def _overlap_t(s):
    ncmp = s // CMP_STRIDE
    nsel = s // SEL_BLOCK
    c_start = np.arange(ncmp) * CMP_STRIDE
    s_start = np.arange(nsel) * SEL_BLOCK
    overlap_t = ((c_start[None, :] <= s_start[:, None] + SEL_BLOCK - 1)
                 & (c_start[None, :] + CMP_LEN - 1 >= s_start[:, None])).astype(np.float32)
    overlap_t[:, ncmp - 1] = 0.0
    return jnp.asarray(overlap_t, dtype=BF16)


def _nsa_attention(p, qnt, kcg, vct, ksg, vst, kwg, vwt, col):
    bsz, hq, s = qnt.shape
    ncmp = s // CMP_STRIDE
    nsel = s // SEL_BLOCK
    gw, gidx = col["gates"]
    per_row = lambda shape: pl.BlockSpec((None,) + shape, lambda b, i: (b,) + (0,) * len(shape))
    keys = per_row((NSA_GROUPS, s, NSA_DH))
    vals = per_row((NSA_GROUPS, NSA_VPAD, s))
    return pl.pallas_call(
        _nsa_attn_kernel,
        grid=(bsz, s // NSA_TQ),
        in_specs=[pl.BlockSpec((None, hq, NSA_TQ), lambda b, i: (b, 0, i)),
                  pl.BlockSpec((None, NSA_TQ, gw), lambda b, i: (b, i, gidx)),
                  per_row((NSA_GROUPS, ncmp, NSA_DH)), per_row((NSA_GROUPS, NSA_DH, ncmp)),
                  keys, vals, keys, vals,
                  pl.BlockSpec((nsel, ncmp), lambda b, i: (0, 0))],
        out_specs=pl.BlockSpec((None, NSA_TQ, hq), lambda b, i: (b, i, 0)),
        out_shape=jax.ShapeDtypeStruct((bsz, s, hq), BF16),
        scratch_shapes=[pltpu.VMEM((hq, NSA_TQ), F32), pltpu.VMEM((NSA_GROUPS, nsel, NSA_TQ), F32),
                        pltpu.VMEM((NSA_GROUPS, 2, NSA_TK, NSA_HPG * NSA_TQ), F32),
                        pltpu.VMEM((NSA_GROUPS, 2, NSA_TK, NSA_HPG * NSA_TQ), BF16),
                        pltpu.VMEM((NSA_GROUPS, NSA_VPAD, NSA_HPG * NSA_TQ), F32)],
        compiler_params=_cparams(("parallel", "arbitrary")),
        name="nsa_attention",
    )(qnt, p, kcg, vct, ksg, vst, kwg, vwt, _overlap_t(s))


def _layout(pieces, tile):
    col, off, mats = {}, 0, []
    for name, w in pieces:
        width = w.shape[1]
        assert off % width == 0, (name, off, width)
        col[name] = (width, off // width)
        mats.append(w)
        off += width
    pad = -off % tile
    if pad:
        mats.append(jnp.zeros((mats[0].shape[0], pad), mats[0].dtype))
    return jnp.concatenate(mats, axis=1).astype(BF16), col


def _pad_cols(w, width):
    return jnp.pad(w, ((0, 0), (0, width - w.shape[1])))


AB_TILE = 1536
ML_TILE = 1280
PROJ_ROWS = 1024


def _ab_layout(w_in):
    hk = GLA_HEADS * GLA_DK
    hv = GLA_HEADS * GLA_DV
    kv = NSA_GROUPS * NSA_DH
    sizes = [hk, hk, hv, hv, GLA_RANK, NSA_HEADS * NSA_DH] + [kv] * 6 + [3 * NSA_HEADS]
    cuts = np.cumsum(sizes)[:-1].tolist()
    gq, gk, gv, gr, gz, nq, kc, vc, ks, vs, kw, vw, gt = jnp.split(w_in, cuts, axis=1)
    gates = _pad_cols(jnp.concatenate([gz, gt], axis=1), LANES)
    return _layout([("gla_q", gq), ("gla_k", gk), ("gla_v", gv), ("gla_r", gr), ("nsa_q", nq),
                    ("nsa_kc", kc), ("nsa_vc", vc), ("nsa_ks", ks), ("nsa_vs", vs), ("nsa_kw", kw),
                    ("nsa_vw", vw), ("gates", gates)], AB_TILE)


def _ml_layout(w_in):
    hq = ML_HEADS * ML_DQK
    hv = ML_HEADS * ML_DV
    wide = 2 * hq + 2 * hv
    col = {"ml_q": (hq, 0), "ml_k": (hq, 1), "ml_v": (hv, 1), "ml_og": (hv, 2), "gates": (LANES, wide // LANES)}
    total = -(-(wide + LANES) // ML_TILE) * ML_TILE
    return jnp.pad(w_in, ((0, 0), (0, total - w_in.shape[1]))).astype(BF16), col


def kernel(x, c, norm_mix_g, mod_mix_w, mod_mix_b, norm_mlp_g, mod_mlp_w, mod_mlp_b, mlp_w1, mlp_w2, ab_w_in, ab_w_out, gla_w_gate, gla_b_gate, gla_norm_g, nsa_q_norm_g, nsa_k_norm_g, nsa_cmp_pos_k, nsa_cmp_pos_v, nsa_cmp_k_w1, nsa_cmp_k_w2, nsa_cmp_v_w1, nsa_cmp_v_w2, ml_w_in, ml_w_out, ml_conv_w, ml_b_i, ml_b_f, ml_norm_g):
    bsz, s, d = x.shape
    depth = norm_mix_g.shape[0]
    mod_mix = _ada_mod(c, mod_mix_w, mod_mix_b)
    mod_mlp = _ada_mod(c, mod_mlp_w, mod_mlp_b)
    mlp_w1_bf16, mlp_w2_bf16 = mlp_w1.astype(BF16), mlp_w2.astype(BF16)

    def split_mod(m):
        return [m[:, None, i * d:(i + 1) * d] for i in range(3)]

    for layer in range(depth):
        shift, scale, gate = split_mod(mod_mix[layer])
        if layer % 2 == 0:
            e = layer // 2
            w_in, col = _ab_layout(ab_w_in[e])
            p = _norm_proj(x, norm_mix_g[layer], scale, shift, w_in, tm=PROJ_ROWS, tn=AB_TILE)
            w_gate = jnp.pad(gla_w_gate[e], ((0, LANES - GLA_RANK), (0, 0))).astype(BF16)
            o_gla = _gla(p, w_gate, gla_b_gate[e], gla_norm_g[e], col)
            q_gain = (jnp.tile(nsa_q_norm_g[e], NSA_HEADS) * (NSA_DH ** -0.5 * LOG2E)).reshape(-1, 1)
            k_gain = jnp.tile(nsa_k_norm_g[e], NSA_GROUPS).reshape(1, -1)
            qnt, ksg, vst, kwg, vwt = _nsa_prep(p, q_gain, k_gain, col)
            kcg, vct = _compress(p, nsa_cmp_pos_k[e], nsa_cmp_pos_v[e], nsa_cmp_k_w1[e],
                                 nsa_cmp_k_w2[e], nsa_cmp_v_w1[e], nsa_cmp_v_w2[e], k_gain, col)
            o_nsa = _nsa_attention(p, qnt, kcg, vct, ksg, vst, kwg, vwt, col)
            w_out = ab_w_out[e].astype(BF16)
            half = GLA_HEADS * GLA_DV
            x = _out_proj([o_gla, o_nsa], [w_out[:half], w_out[half:]], x, gate)
        else:
            o = layer // 2
            w_in, col = _ml_layout(ml_w_in[o])
            p = _norm_proj(x, norm_mix_g[layer], scale, shift, w_in, tm=PROJ_ROWS, tn=ML_TILE)
            bias = jnp.concatenate([ml_b_i[o], ml_b_f[o]])
            bias_col = _pad_cols(bias.reshape(1, -1), LANES)
            bias_row = jnp.broadcast_to(bias.reshape(-1, 1), (2 * ML_HEADS, LANES))
            hh = _mlstm(p, ml_conv_w[o], bias_col, bias_row, ml_norm_g[o], col)
            x = _out_proj([hh], [ml_w_out[o].astype(BF16)], x, gate)
        shift, scale, gate = split_mod(mod_mlp[layer])
        x = _mlp_sublayer(x, norm_mlp_g[layer], scale, shift, gate, mlp_w1_bf16, mlp_w2_bf16, layer)
    return x
```

```python
import functools

import numpy as np
import jax
import jax.numpy as jnp
from jax import lax
from jax.experimental import pallas as pl
from jax.experimental.pallas import tpu as pltpu

F32 = jnp.float32
BF16 = jnp.bfloat16
HI = lax.Precision.HIGHEST

EPS = 1e-6
NEG = -1e30
LOG2E = 1.4426950408889634

VMEM_LIMIT_BYTES = 56 * 1024 * 1024
LANES = 128

PROJ_ROWS = 1024
AB_TILE = 1536
ML_TILE = 1280
MLP_ROWS = 512
MLP_HIDDEN_TILE = 1024
OUT_ROWS = 512
MOD_TILE = 768
PREP_ROWS = 512

GLA_HEADS = 4
GLA_DK = 128
GLA_DV = 256
GLA_RANK = 16
GLA_GATE_TAU = 16.0
CHUNK = 64
ML_CHUNK = 128
REC_BLOCK = 512

NSA_HEADS = 16
NSA_GROUPS = 4
NSA_HPG = 4
NSA_DH = 64
CMP_LEN = 32
CMP_STRIDE = 16
SEL_BLOCK = 64
N_SELECT = 16
WINDOW = 512
NSA_TQ = 256
NSA_TK = 256
NSA_VPAD = 80

ML_HEADS = 4
ML_DQK = 256
ML_DV = 512
ML_CONV = 4


def _cparams(sem):
    return pltpu.CompilerParams(dimension_semantics=sem, vmem_limit_bytes=VMEM_LIMIT_BYTES)


def _sigmoid(x):
    return 1.0 / (1.0 + jnp.exp(-x))


def _silu(x):
    return x * _sigmoid(x)


def _log_sigmoid(x):
    return jnp.minimum(x, 0.0) - jnp.log(1.0 + jnp.exp(-jnp.abs(x)))


def _dot(a, b, precision=None):
    return jnp.dot(a, b, preferred_element_type=F32, precision=precision)


def _dot_nt(a, b, precision=None):
    return lax.dot_general(a, b, (((1,), (1,)), ((), ())), preferred_element_type=F32, precision=precision)


def _split_bf16(x, parts=2):
    out = []
    for _ in range(parts - 1):
        piece = x.astype(BF16)
        out.append(piece)
        x = x - piece.astype(F32)
    return out + [x.astype(BF16)]


def _rms(x):
    return x * lax.rsqrt(jnp.mean(x * x, axis=-1, keepdims=True) + EPS)


def _mod_kernel(c_ref, w_ref, b_ref, o_ref):
    a = _silu(c_ref[...]).astype(BF16)
    o_ref[...] = _dot(a, w_ref[...].astype(BF16)) + b_ref[...]


def _ada_mod(c, w, b):
    nl, d, d3 = w.shape
    bsz = c.shape[0]
    tn = MOD_TILE
    return pl.pallas_call(
        _mod_kernel,
        grid=(nl, d3 // tn),
        in_specs=[pl.BlockSpec((bsz, d), lambda l, j: (0, 0)),
                  pl.BlockSpec((None, d, tn), lambda l, j: (l, 0, j)),
                  pl.BlockSpec((None, 1, tn), lambda l, j: (l, 0, j))],
        out_specs=pl.BlockSpec((None, bsz, tn), lambda l, j: (l, 0, j)),
        out_shape=jax.ShapeDtypeStruct((nl, bsz, d3), F32),
        compiler_params=_cparams(("parallel", "parallel")),
        name="ada_mod",
    )(c, w, b.reshape(nl, 1, d3))


def _modulated_norm(x_ref, g_ref, sc_ref, sh_ref):
    x = x_ref[...]
    gain = g_ref[...] * (1.0 + sc_ref[...])
    inv = lax.rsqrt(jnp.mean(x * x, axis=-1, keepdims=True) + EPS)
    return ((x * inv) * gain + sh_ref[...]).astype(BF16)


def _proj_kernel(x_ref, g_ref, sc_ref, sh_ref, w_ref, o_ref, h_ref):
    i = pl.program_id(2)

    @pl.when(pl.program_id(1) == 0)
    def _():
        h_ref[i] = _modulated_norm(x_ref, g_ref, sc_ref, sh_ref)

    o_ref[...] = _dot(h_ref[i], w_ref[...])


def _norm_proj(x, g, scale, shift, w, tm, tn):
    bsz, s, d = x.shape
    n = w.shape[1]
    rows = s // tm
    x_spec = pl.BlockSpec((None, tm, d), lambda b, j, i: (b, jnp.where(j == 0, i, rows - 1), 0))
    return pl.pallas_call(
        _proj_kernel,
        grid=(bsz, n // tn, rows),
        in_specs=[x_spec,
                  pl.BlockSpec((1, d), lambda b, j, i: (0, 0)),
                  pl.BlockSpec((None, 1, d), lambda b, j, i: (b, 0, 0)),
                  pl.BlockSpec((None, 1, d), lambda b, j, i: (b, 0, 0)),
                  pl.BlockSpec((d, tn), lambda b, j, i: (0, j))],
        out_specs=pl.BlockSpec((None, tm, tn), lambda b, j, i: (b, i, j)),
        out_shape=jax.ShapeDtypeStruct((bsz, s, n), F32),
        scratch_shapes=[pltpu.VMEM((rows, tm, d), BF16)],
        compiler_params=_cparams(("parallel", "arbitrary", "arbitrary")),
        name="norm_proj",
    )(x, g.reshape(1, d), scale, shift, w)


def _mlp_kernel(x_ref, g_ref, sc_ref, sh_ref, gate_ref, w1_ref, w2_ref, o_ref, h_ref):
    @pl.when(pl.program_id(2) == 0)
    def _():
        h_ref[...] = _modulated_norm(x_ref, g_ref, sc_ref, sh_ref)
        o_ref[...] = x_ref[...]

    u = jnp.maximum(_dot(h_ref[...], w1_ref[...]), 0.0)
    o_ref[...] += gate_ref[...] * _dot((u * u).astype(BF16), w2_ref[...])


def _mlp_sublayer(x, g, scale, shift, gate, w1, w2, layer, tm=MLP_ROWS, tf=MLP_HIDDEN_TILE):
    bsz, s, d = x.shape
    ff = w1.shape[2]
    vec = pl.BlockSpec((None, 1, d), lambda b, i, f: (b, 0, 0))
    return pl.pallas_call(
        _mlp_kernel,
        grid=(bsz, s // tm, ff // tf),
        in_specs=[pl.BlockSpec((None, tm, d), lambda b, i, f: (b, i, 0)),
                  pl.BlockSpec((1, d), lambda b, i, f: (0, 0)),
                  vec, vec, vec,
                  pl.BlockSpec((None, d, tf), lambda b, i, f: (layer, 0, f)),
                  pl.BlockSpec((None, tf, d), lambda b, i, f: (layer, f, 0))],
        out_specs=pl.BlockSpec((None, tm, d), lambda b, i, f: (b, i, 0)),
        out_shape=jax.ShapeDtypeStruct((bsz, s, d), F32),
        scratch_shapes=[pltpu.VMEM((tm, d), BF16)],
        compiler_params=_cparams(("parallel", "parallel", "arbitrary")),
        name="mlp_sublayer",
    )(x, g.reshape(1, d), scale, shift, gate, w1, w2)


def _outproj_kernel(n_in, *refs):
    a_refs, w_refs = refs[:n_in], refs[n_in:2 * n_in]
    x_ref, gate_ref, o_ref = refs[2 * n_in:]
    y = _dot(a_refs[0][...], w_refs[0][...])
    for a_ref, w_ref in zip(a_refs[1:], w_refs[1:]):
        y += _dot(a_ref[...], w_ref[...])
    o_ref[...] = x_ref[...] + gate_ref[...] * y


def _out_proj(acts, ws, x, gate, tm=OUT_ROWS):
    bsz, s, d = x.shape
    n_in = len(acts)
    in_specs = [pl.BlockSpec((None, tm, a.shape[-1]), lambda b, i: (b, i, 0)) for a in acts]
    in_specs += [pl.BlockSpec(w.shape, lambda b, i: (0, 0)) for w in ws]
    in_specs += [pl.BlockSpec((None, tm, d), lambda b, i: (b, i, 0)),
                 pl.BlockSpec((None, 1, d), lambda b, i: (b, 0, 0))]
    return pl.pallas_call(
        functools.partial(_outproj_kernel, n_in),
        grid=(bsz, s // tm),
        in_specs=in_specs,
        out_specs=pl.BlockSpec((None, tm, d), lambda b, i: (b, i, 0)),
        out_shape=jax.ShapeDtypeStruct((bsz, s, d), F32),
        compiler_params=_cparams(("parallel", "parallel")),
        name="out_proj",
    )(*acts, *ws, x, gate)


def _tri(n, upper=False):
    r = lax.broadcasted_iota(jnp.int32, (n, n), 0)
    c = lax.broadcasted_iota(jnp.int32, (n, n), 1)
    return (c >= r) if upper else (c <= r)


def _gla_kernel(q_ref, k_ref, v_ref, r_ref, zg_ref, wg_ref, bg_ref, ng_ref, o_ref, st_ref):
    @pl.when(pl.program_id(1) == 0)
    def _():
        st_ref[...] = jnp.zeros_like(st_ref)

    tril = _tri(CHUNK)
    tril_b = tril.astype(BF16)
    pre = _dot(zg_ref[...].astype(BF16), wg_ref[...]) + bg_ref[...]
    parts = _split_bf16(_log_sigmoid(pre) / GLA_GATE_TAU, 3)
    chunks = [slice(c * CHUNK, (c + 1) * CHUNK) for c in range(REC_BLOCK // CHUNK)]
    cum_all = jnp.concatenate([sum(_dot(tril_b, part[rows, :]) for part in parts) for rows in chunks], axis=0)
    q_dec_all = (q_ref[...] * GLA_DK ** -0.5) * jnp.exp(cum_all)
    k_dec_all = k_ref[...] * jnp.exp(-cum_all)
    qb_all = q_dec_all.astype(BF16)
    kb_all = k_dec_all.astype(BF16)
    vb_all = v_ref[...].astype(BF16)
    gate_all = _silu(r_ref[...])
    intra = {}
    for c, rows in enumerate(chunks):
        for h in range(GLA_HEADS):
            ks = slice(h * GLA_DK, (h + 1) * GLA_DK)
            vs = slice(h * GLA_DV, (h + 1) * GLA_DV)
            att = jnp.where(tril, _dot_nt(qb_all[rows, ks], kb_all[rows, ks]), 0.0)
            intra[c, h] = _dot(att.astype(BF16), vb_all[rows, vs])
    for c, rows in enumerate(chunks):
        cum = cum_all[rows, :]
        total = cum[CHUNK - 1:CHUNK, :]
        k_end = (k_ref[rows, :] * jnp.exp(total - cum)).astype(BF16)
        decay = jnp.exp(total)
        for h in range(GLA_HEADS):
            ks = slice(h * GLA_DK, (h + 1) * GLA_DK)
            vs = slice(h * GLA_DV, (h + 1) * GLA_DV)
            st = st_ref[h]
            o = intra[c, h] + _dot_nt(qb_all[rows, ks], st.astype(BF16))
            st_ref[h] = st * decay[:, ks] + _dot(v_ref[rows, vs].T.astype(BF16), k_end[:, ks])
            o_ref[rows, vs] = ((_rms(o) * ng_ref[...]) * gate_all[rows, vs]).astype(o_ref.dtype)


def _gla(p, w_gate, b_gate, norm_g, col):
    bsz, s, _ = p.shape
    t = REC_BLOCK

    def piece(name):
        width, idx = col[name]
        return pl.BlockSpec((None, t, width), lambda b, i, idx=idx: (b, i, idx))

    const = lambda shape: pl.BlockSpec(shape, lambda b, i: (0,) * len(shape))
    hk = GLA_HEADS * GLA_DK
    return pl.pallas_call(
        _gla_kernel,
        grid=(bsz, s // t),
        in_specs=[piece("gla_q"), piece("gla_k"), piece("gla_v"), piece("gla_r"), piece("gates"),
                  const((LANES, hk)), const((1, hk)), const((1, GLA_DV))],
        out_specs=pl.BlockSpec((None, t, GLA_HEADS * GLA_DV), lambda b, i: (b, i, 0)),
        out_shape=jax.ShapeDtypeStruct((bsz, s, GLA_HEADS * GLA_DV), BF16),
        scratch_shapes=[pltpu.VMEM((GLA_HEADS, GLA_DV, GLA_DK), F32)],
        compiler_params=_cparams(("parallel", "arbitrary")),
        name="gla",
    )(p, p, p, p, p, w_gate, b_gate.reshape(1, hk), norm_g.reshape(1, GLA_DV))


def _mlstm_kernel(q_ref, k_ref, v_ref, og_ref, gc_ref, wq_ref, wk_ref, bc_ref, br_ref, ng_ref,
                  o_ref, c_ref, n_ref, m_ref, qx_ref, kx_ref):
    t = REC_BLOCK

    @pl.when(pl.program_id(1) == 0)
    def _():
        c_ref[...] = jnp.zeros_like(c_ref)
        n_ref[...] = jnp.zeros_like(n_ref)
        m_ref[...] = jnp.zeros_like(m_ref)
        qx_ref[0:8, :] = jnp.zeros((8, qx_ref.shape[1]), F32)
        kx_ref[0:8, :] = jnp.zeros((8, kx_ref.shape[1]), F32)

    def conv_silu(x_ref, xx_ref, w_ref):
        xx_ref[8:8 + t, :] = x_ref[...]
        y = xx_ref[8:8 + t, :] * w_ref[ML_CONV - 1:ML_CONV, :]
        for back in range(1, ML_CONV):
            y += xx_ref[8 - back:8 - back + t, :] * w_ref[ML_CONV - 1 - back:ML_CONV - back, :]
        xx_ref[0:8, :] = xx_ref[t:t + 8, :]
        return _silu(y)

    q_all = conv_silu(q_ref, qx_ref, wq_ref) * ML_DQK ** -0.5
    k_all = conv_silu(k_ref, kx_ref, wk_ref)

    tril = _tri(ML_CHUNK)
    tril_b = tril.astype(BF16)
    triu_b = _tri(ML_CHUNK, upper=True).astype(BF16)
    lane = lax.broadcasted_iota(jnp.int32, (ML_CHUNK, LANES), 1)
    is_f_col = (lane >= ML_HEADS) & (lane < 2 * ML_HEADS)
    sub = lax.broadcasted_iota(jnp.int32, (8, ML_CHUNK), 0)
    is_f_row = sub >= ML_HEADS
    gates_by_row = gc_ref[...].T[0:2 * ML_HEADS, :]

    for c in range(t // ML_CHUNK):
        rows = slice(c * ML_CHUNK, (c + 1) * ML_CHUNK)
        gcol = gc_ref[rows, :] + bc_ref[...]
        gcol = jnp.where(is_f_col, _log_sigmoid(gcol), gcol)
        bcol = sum(_dot(tril_b, part) for part in _split_bf16(gcol, 3))
        grow = gates_by_row[:, rows] + br_ref[:, 0:1]
        grow = jnp.where(is_f_row, _log_sigmoid(grow), grow)
        brow = sum(_dot(part, triu_b) for part in _split_bf16(grow, 3))
        for h in range(ML_HEADS):
            qs = slice(h * ML_DQK, (h + 1) * ML_DQK)
            vs = slice(h * ML_DV, (h + 1) * ML_DV)
            b_col = bcol[:, ML_HEADS + h:ML_HEADS + h + 1]
            b_row = brow[ML_HEADS + h:ML_HEADS + h + 1, :]
            li_col = gcol[:, h:h + 1]
            li_row = grow[h:h + 1, :]
            m_prev = m_ref[h:h + 1, 0:1]
            log_d = jnp.where(tril, b_col - b_row + li_row, -jnp.inf)
            log_inter = b_col + m_prev
            m_comb = jnp.maximum(log_inter, jnp.max(log_d, axis=-1, keepdims=True))
            d = jnp.exp(log_d - m_comb)
            w_inter = jnp.exp(log_inter - m_comb)
            qh = q_all[rows, qs]
            kh = k_all[rows, qs]
            qb = qh.astype(BF16)
            vb = v_ref[rows, vs].astype(BF16)
            sc = _dot_nt(qb, kh.astype(BF16)) * d
            cm = c_ref[h]
            nv = n_ref[h]
            num = _dot(sc.astype(BF16), vb) + w_inter * _dot(qb, cm.astype(BF16))
            den = jnp.sum(sc, axis=-1, keepdims=True) + w_inter * jnp.sum(qh * nv, axis=-1, keepdims=True)
            hh = num / jnp.maximum(jnp.abs(den), jnp.exp(-m_comb))
            total = b_col[ML_CHUNK - 1:ML_CHUNK, :]
            log_w = total - b_col + li_col
            m_new = jnp.maximum(total + m_prev, jnp.max(log_w, axis=0, keepdims=True))
            decay = jnp.exp(total + m_prev - m_new)
            kw = kh * jnp.exp(log_w - m_new)
            c_ref[h] = decay * cm + _dot(kw.T.astype(BF16), vb)
            n_ref[h] = decay * nv + jnp.sum(kw, axis=0, keepdims=True)
            m_ref[h:h + 1, :] = jnp.broadcast_to(m_new, (1, LANES))
            og = og_ref[rows, vs]
            o_ref[rows, vs] = ((_rms(hh) * ng_ref[...]) * _sigmoid(og)).astype(o_ref.dtype)


def _mlstm(p, conv_w, bias_col, bias_row, norm_g, col):
    bsz, s, _ = p.shape
    t = REC_BLOCK
    hq = ML_HEADS * ML_DQK
    hv = ML_HEADS * ML_DV

    def piece(name):
        width, idx = col[name]
        return pl.BlockSpec((None, t, width), lambda b, i, idx=idx: (b, i, idx))

    const = lambda shape: pl.BlockSpec(shape, lambda b, i: (0,) * len(shape))
    return pl.pallas_call(
        _mlstm_kernel,
        grid=(bsz, s // t),
        in_specs=[piece("ml_q"), piece("ml_k"), piece("ml_v"), piece("ml_og"), piece("gates"),
                  pl.BlockSpec((ML_CONV, hq), lambda b, i: (0, 0)),
                  pl.BlockSpec((ML_CONV, hq), lambda b, i: (0, 1)),
                  const((1, LANES)), const((8, LANES)), const((1, ML_DV))],
        out_specs=pl.BlockSpec((None, t, hv), lambda b, i: (b, i, 0)),
        out_shape=jax.ShapeDtypeStruct((bsz, s, hv), BF16),
        scratch_shapes=[pltpu.VMEM((ML_HEADS, ML_DQK, ML_DV), F32),
                        pltpu.VMEM((ML_HEADS, 1, ML_DQK), F32),
                        pltpu.VMEM((8, LANES), F32),
                        pltpu.VMEM((t + 8, hq), F32),
                        pltpu.VMEM((t + 8, hq), F32)],
        compiler_params=_cparams(("parallel", "arbitrary")),
        name="mlstm",
    )(p, p, p, p, p, conv_w, conv_w, bias_col, bias_row, norm_g.reshape(1, ML_DV))


def _seg_norm(x, seg_ref, segt_ref):
    if seg_ref.dtype == BF16:
        seg_sum = lambda v, m_ref: sum(_dot(part, m_ref[...]) for part in _split_bf16(v))
    else:
        seg_sum = lambda v, m_ref: _dot(v, m_ref[...], precision=HI)
    ss = seg_sum(x * x, seg_ref) * (1.0 / NSA_DH)
    return x * seg_sum(lax.rsqrt(ss + EPS), segt_ref)


def _nsa_prep_kernel(q_ref, ks_ref, vs_ref, kw_ref, vw_ref, qg_ref, kg_ref, segk_ref, segkt_ref,
                     qnt_ref, ksg_ref, vst_ref, kwg_ref, vwt_ref):
    t = q_ref.shape[0]
    qt = q_ref[...].T
    for h in range(NSA_HEADS):
        hs = slice(h * NSA_DH, (h + 1) * NSA_DH)
        blk = qt[hs, :]
        inv = lax.rsqrt(jnp.mean(blk * blk, axis=0, keepdims=True) + EPS)
        qnt_ref[hs, :] = ((blk * inv) * qg_ref[hs, :]).astype(BF16)
    ones = jnp.ones((NSA_VPAD - NSA_DH, t), BF16)
    for k_ref, v_ref, kg_out, vt_out in ((ks_ref, vs_ref, ksg_ref, vst_ref), (kw_ref, vw_ref, kwg_ref, vwt_ref)):
        kn = (_seg_norm(k_ref[...], segk_ref, segkt_ref) * kg_ref[...]).astype(BF16)
        vt = v_ref[...].T.astype(BF16)
        for g in range(NSA_GROUPS):
            gs = slice(g * NSA_DH, (g + 1) * NSA_DH)
            kg_out[g] = kn[:, gs]
            vt_out[g, 0:NSA_DH, :] = vt[gs, :]
            vt_out[g, NSA_DH:NSA_VPAD, :] = ones


def _seg_matrices(width, dtype):
    seg = (np.arange(width)[:, None] // NSA_DH == np.arange(LANES)[None, :]).astype(np.float32)
    return jnp.asarray(seg, dtype=dtype), jnp.asarray(seg.T, dtype=dtype)


def _nsa_prep(p, q_gain, k_gain, col, t=PREP_ROWS):
    bsz, s, _ = p.shape
    hq = NSA_HEADS * NSA_DH
    kv = NSA_GROUPS * NSA_DH

    def piece(name):
        width, idx = col[name]
        return pl.BlockSpec((None, t, width), lambda b, i, idx=idx: (b, i, idx))

    const = lambda shape: pl.BlockSpec(shape, lambda b, i: (0,) * len(shape))
    segk, segkt = _seg_matrices(kv, BF16)
    keys = pl.BlockSpec((None, NSA_GROUPS, t, NSA_DH), lambda b, i: (b, 0, i, 0))
    vals = pl.BlockSpec((None, NSA_GROUPS, NSA_VPAD, t), lambda b, i: (b, 0, 0, i))
    shp = lambda *dims, dt=BF16: jax.ShapeDtypeStruct(dims, dt)
    keys_shape = shp(bsz, NSA_GROUPS, s, NSA_DH)
    vals_shape = shp(bsz, NSA_GROUPS, NSA_VPAD, s)
    return pl.pallas_call(
        _nsa_prep_kernel,
        grid=(bsz, s // t),
        in_specs=[piece("nsa_q"), piece("nsa_ks"), piece("nsa_vs"), piece("nsa_kw"), piece("nsa_vw"),
                  const((hq, 1)), const((1, kv)), const((kv, LANES)), const((LANES, kv))],
        out_specs=[pl.BlockSpec((None, hq, t), lambda b, i: (b, 0, i)), keys, vals, keys, vals],
        out_shape=[shp(bsz, hq, s), keys_shape, vals_shape, keys_shape, vals_shape],
        compiler_params=_cparams(("parallel", "parallel")),
        name="nsa_prep",
    )(p, p, p, p, p, q_gain, k_gain, segk, segkt)


def _compress_kernel(kc0_ref, kc1_ref, vc0_ref, vc1_ref, posk_ref, posv_ref, kw1a_ref, kw1b_ref, kw2_ref,
                     vw1a_ref, vw1b_ref, vw2_ref, kg_ref, segk_ref, segkt_ref, kcg_ref, vct_ref):
    def hidden(x_refs, pos_ref, w1a_ref, w1b_ref):
        nsub = x_refs[0].shape[0] // CMP_STRIDE
        first = second = None
        for l in range(CMP_STRIDE):
            x = jnp.concatenate([r[pl.ds(l, nsub, stride=CMP_STRIDE), :] for r in x_refs], axis=1)
            a = _dot((x + pos_ref[l:l + 1, :]).astype(BF16), w1a_ref[l])
            b = _dot((x + pos_ref[CMP_STRIDE + l:CMP_STRIDE + l + 1, :]).astype(BF16), w1b_ref[l])
            first = a if first is None else first + a
            second = b if second is None else second + b
        row = lax.broadcasted_iota(jnp.int32, second.shape, 0)
        nxt = jnp.where(row == nsub - 1, 0.0, pltpu.roll(second, nsub - 1, 0))
        return _silu(first + nxt)

    kc = _dot(hidden((kc0_ref, kc1_ref), posk_ref, kw1a_ref, kw1b_ref).astype(BF16), kw2_ref[...])
    kn = (_seg_norm(kc, segk_ref, segkt_ref) * kg_ref[...]).astype(BF16)
    vt = _dot(vw2_ref[...], hidden((vc0_ref, vc1_ref), posv_ref, vw1a_ref, vw1b_ref).T.astype(BF16)).astype(BF16)
    for g in range(NSA_GROUPS):
        gs = slice(g * NSA_DH, (g + 1) * NSA_DH)
        kcg_ref[g] = kn[:, gs]
        vct_ref[g] = vt[gs, :]


def _expand_w1(w1):
    w = w1.reshape(2, CMP_STRIDE, NSA_DH, NSA_DH)
    eye = jnp.eye(NSA_GROUPS, dtype=w1.dtype)
    big = jnp.einsum('hlde,gk->hlgdke', w, eye)
    big = big.reshape(2, CMP_STRIDE, NSA_GROUPS * NSA_DH, NSA_GROUPS * NSA_DH).astype(BF16)
    return big[0], big[1]


def _expand_pos(pos):
    return jnp.tile(pos, (1, NSA_GROUPS))


def _block_diag(w2):
    return jnp.kron(jnp.eye(NSA_GROUPS, dtype=w2.dtype), w2).astype(BF16)


def _compress(p, pos_k, pos_v, ck_w1, ck_w2, cv_w1, cv_w2, k_gain, col):
    bsz, s, _ = p.shape
    kv = NSA_GROUPS * NSA_DH
    nsub = s // CMP_STRIDE
    segk, segkt = _seg_matrices(kv, F32)
    kw1a, kw1b = _expand_w1(ck_w1)
    vw1a, vw1b = _expand_w1(cv_w1)
    const = lambda shape: pl.BlockSpec(shape, lambda b: (0,) * len(shape))

    def halves(name):
        width, idx = col[name]
        assert width == 2 * LANES
        return [pl.BlockSpec((None, s, LANES), lambda b, j=2 * idx + h: (b, 0, j)) for h in range(2)]

    w1_spec = const((CMP_STRIDE, kv, kv))
    return pl.pallas_call(
        _compress_kernel,
        grid=(bsz,),
        in_specs=[*halves("nsa_kc"), *halves("nsa_vc"), const((CMP_LEN, kv)), const((CMP_LEN, kv)),
                  w1_spec, w1_spec, const((kv, kv)),
                  w1_spec, w1_spec, const((kv, kv)),
                  const((1, kv)), const((kv, LANES)), const((LANES, kv))],
        out_specs=[pl.BlockSpec((None, NSA_GROUPS, nsub, NSA_DH), lambda b: (b, 0, 0, 0)),
                   pl.BlockSpec((None, NSA_GROUPS, NSA_DH, nsub), lambda b: (b, 0, 0, 0))],
        out_shape=[jax.ShapeDtypeStruct((bsz, NSA_GROUPS, nsub, NSA_DH), BF16),
                   jax.ShapeDtypeStruct((bsz, NSA_GROUPS, NSA_DH, nsub), BF16)],
        compiler_params=_cparams(("parallel",)),
        name="nsa_compress",
    )(p, p, p, p, _expand_pos(pos_k), _expand_pos(pos_v),
      kw1a, kw1b, _block_diag(ck_w2), vw1a, vw1b, _block_diag(cv_w2).T, k_gain, segk, segkt)


def _nsa_attn_kernel(qt_ref, gt_ref, kcg_ref, vct_ref, ksg_ref, vst_ref, kwg_ref, vwt_ref, ovt_ref,
                     o_ref, acc_ref, sel_ref, s_ref, p_ref, pv_ref):
    tq, tk = NSA_TQ, NSA_TK
    q0 = pl.program_id(1) * tq
    ncmp = kcg_ref.shape[1]
    nsel = ovt_ref.shape[0]
    cols4 = NSA_HPG * tq

    t_cmp = q0 + lax.broadcasted_iota(jnp.int32, (ncmp, cols4), 1) % tq
    n_idx = lax.broadcasted_iota(jnp.int32, (ncmp, cols4), 0)
    cmp_valid = n_idx * CMP_STRIDE + (CMP_LEN - 1) <= t_cmp

    j_idx = lax.broadcasted_iota(jnp.int32, (nsel, tq), 0)
    cur = (q0 + lax.broadcasted_iota(jnp.int32, (nsel, tq), 1)) // SEL_BLOCK
    forced = (j_idx == 0) | (j_idx == cur) | (j_idx == cur - 1)
    causal_blk = j_idx <= cur

    row_blk = lax.broadcasted_iota(jnp.int32, (SEL_BLOCK, tq), 0)
    t_row = q0 + lax.broadcasted_iota(jnp.int32, (1, tq), 1)

    gates_t = _sigmoid(gt_ref[...]).T

    def add_gated(g, branch, o4):
        for i in range(NSA_HPG):
            h = g * NSA_HPG + i
            hs = slice(h * NSA_DH, (h + 1) * NSA_DH)
            gc = GLA_RANK + 3 * h + branch
            term = gates_t[gc:gc + 1, :] * o4[:, i * tq:(i + 1) * tq]
            acc_ref[hs, :] = term if branch == 0 else acc_ref[hs, :] + term

    def q_group(g):
        return jnp.concatenate([qt_ref[(g * NSA_HPG + i) * NSA_DH:(g * NSA_HPG + i + 1) * NSA_DH, :]
                                for i in range(NSA_HPG)], axis=1)

    def attend(branch, k_ref, vt_ref, first_tile, n_tiles, mask_fn):
        last = first_tile + n_tiles - 1

        def scores(g, kt):
            return _dot(k_ref[g, pl.ds(pl.multiple_of(kt * tk, tk), tk), :], q_group(g))

        def weighted_values(g, kt, p):
            return _dot(vt_ref[g, :, pl.ds(pl.multiple_of(kt * tk, tk), tk)], p)

        def step(kt, cur, ms):
            nxt = 1 - cur
            out = []
            for g in range(NSA_GROUPS):
                m = ms[g]
                s_ref[g, nxt] = scores(g, jnp.minimum(kt + 1, last))
                acc = pv_ref[g] + weighted_values(g, jnp.maximum(kt - 1, first_tile), p_ref[g, nxt])
                ok = mask_fn(g, kt, kt * tk)
                m_new = []
                for c in range(cols4 // LANES):
                    cs = slice(c * LANES, (c + 1) * LANES)
                    qs = slice((c % (tq // LANES)) * LANES, (c % (tq // LANES) + 1) * LANES)
                    s = jnp.where(ok[:, qs], s_ref[g, cur, :, cs], NEG)
                    m_c = jnp.maximum(m[:, cs], jnp.max(s, axis=0, keepdims=True))
                    p_ref[g, cur, :, cs] = jnp.exp2(s - m_c).astype(BF16)
                    m_new.append(m_c)
                m_new = jnp.concatenate(m_new, axis=1)
                pv_ref[g] = acc * jnp.exp2(m - m_new)
                out.append(m_new)
            return tuple(out)

        for g in range(NSA_GROUPS):
            s_ref[g, 0] = scores(g, first_tile)
            p_ref[g, 1] = jnp.zeros(p_ref.shape[2:], BF16)
        pv_ref[...] = jnp.zeros_like(pv_ref)

        def body(j, ms):
            kt = first_tile + 2 * j
            return step(kt + 1, 1, step(kt, 0, ms))

        ms = lax.fori_loop(0, n_tiles // 2, body, tuple(jnp.full((1, cols4), NEG, F32) for _ in range(NSA_GROUPS)))

        @pl.when(n_tiles % 2 == 1)
        def _():
            step(last, 0, ms)

        for g in range(NSA_GROUPS):
            acc = pv_ref[g] + weighted_values(g, last, p_ref[g, (n_tiles - 1) % 2])
            add_gated(g, branch, acc[0:NSA_DH, :] / acc[NSA_DH:NSA_DH + 1, :])

    last_tile = (q0 + tq - 1) // tk
    last_blk = (q0 + tq - 1) // SEL_BLOCK

    def sel_mask(g, kt, k0):
        per_tile = tk // SEL_BLOCK
        blocks = []
        for j in range(per_tile):
            picked = sel_ref[g, pl.ds(kt * per_tile + j, 1), :] > 0.5
            limit = jnp.where(picked, t_row, -1) - (k0 + j * SEL_BLOCK)
            blocks.append(row_blk <= limit)
        return jnp.concatenate(blocks, axis=0)

    win_span = WINDOW + tk
    win_start = pl.multiple_of(jnp.maximum(last_tile - WINDOW // tk, 0) * tk, tk)
    row_w = lax.broadcasted_iota(jnp.int32, (win_span, tq), 0)
    ahead = t_row - win_start
    win_ok = (row_w <= ahead) & (row_w > ahead - WINDOW)

    def window_branch(g):
        s = _dot(kwg_ref[g, pl.ds(win_start, win_span), :], q_group(g))
        probs = []
        for c in range(cols4 // LANES):
            qs = slice((c % (tq // LANES)) * LANES, (c % (tq // LANES) + 1) * LANES)
            sc = jnp.where(win_ok[:, qs], s[:, c * LANES:(c + 1) * LANES], NEG)
            probs.append(jnp.exp2(sc - jnp.max(sc, axis=0, keepdims=True)).astype(BF16))
        acc = _dot(vwt_ref[g, :, pl.ds(win_start, win_span)], jnp.concatenate(probs, axis=1))
        add_gated(g, 2, acc[0:NSA_DH, :] / acc[NSA_DH:NSA_DH + 1, :])

    importance = []
    for g in range(NSA_GROUPS):
        lg = jnp.where(cmp_valid, _dot(kcg_ref[g], q_group(g)), NEG)
        ex = jnp.exp2(lg - jnp.max(lg, axis=0, keepdims=True))
        p_cmp = jnp.where(cmp_valid, ex / jnp.sum(ex, axis=0, keepdims=True), 0.0)
        add_gated(g, 0, _dot(vct_ref[g], p_cmp.astype(BF16)))
        p_grp = p_cmp[:, 0:tq] + p_cmp[:, tq:2 * tq] + p_cmp[:, 2 * tq:3 * tq] + p_cmp[:, 3 * tq:4 * tq]
        importance.append(sum(_dot(ovt_ref[...], part) for part in _split_bf16(p_grp)))
        window_branch(g)

    @pl.when(last_blk < N_SELECT)
    def _():
        for g in range(NSA_GROUPS):
            sel_ref[g] = causal_blk.astype(F32)

    @pl.when(last_blk >= N_SELECT)
    def _():
        for g in range(NSA_GROUPS):
            val = jnp.where(forced, jnp.inf, jnp.where(causal_blk, importance[g], -jnp.inf))
            rank = jnp.zeros((nsel, tq), F32)
            for i in range(nsel):
                vi = val[i:i + 1, :]
                rank += ((vi > val) | ((vi == val) & (j_idx > i))).astype(F32)
            sel_ref[g] = (rank < N_SELECT).astype(F32)

    attend(1, ksg_ref, vst_ref, 0, last_tile + 1, sel_mask)
    o_ref[...] = acc_ref[...].T.astype(o_ref.dtype)


def _overlap_t(s):
    ncmp = s // CMP_STRIDE
    nsel = s // SEL_BLOCK
    c_start = np.arange(ncmp) * CMP_STRIDE
    s_start = np.arange(nsel) * SEL_BLOCK
    overlap_t = ((c_start[None, :] <= s_start[:, None] + SEL_BLOCK - 1)
                 & (c_start[None, :] + CMP_LEN - 1 >= s_start[:, None])).astype(np.float32)
    overlap_t[:, ncmp - 1] = 0.0
    return jnp.asarray(overlap_t, dtype=BF16)


def _nsa_attention(p, qnt, kcg, vct, ksg, vst, kwg, vwt, col):
    bsz, hq, s = qnt.shape
    ncmp = s // CMP_STRIDE
    nsel = s // SEL_BLOCK
    gw, gidx = col["gates"]
    per_row = lambda shape: pl.BlockSpec((None,) + shape, lambda b, i: (b,) + (0,) * len(shape))
    keys = per_row((NSA_GROUPS, s, NSA_DH))
    vals = per_row((NSA_GROUPS, NSA_VPAD, s))
    return pl.pallas_call(
        _nsa_attn_kernel,
        grid=(bsz, s // NSA_TQ),
        in_specs=[pl.BlockSpec((None, hq, NSA_TQ), lambda b, i: (b, 0, i)),
                  pl.BlockSpec((None, NSA_TQ, gw), lambda b, i: (b, i, gidx)),
                  per_row((NSA_GROUPS, ncmp, NSA_DH)), per_row((NSA_GROUPS, NSA_DH, ncmp)),
                  keys, vals, keys, vals,
                  pl.BlockSpec((nsel, ncmp), lambda b, i: (0, 0))],
        out_specs=pl.BlockSpec((None, NSA_TQ, hq), lambda b, i: (b, i, 0)),
        out_shape=jax.ShapeDtypeStruct((bsz, s, hq), BF16),
        scratch_shapes=[pltpu.VMEM((hq, NSA_TQ), F32), pltpu.VMEM((NSA_GROUPS, nsel, NSA_TQ), F32),
                        pltpu.VMEM((NSA_GROUPS, 2, NSA_TK, NSA_HPG * NSA_TQ), F32),
                        pltpu.VMEM((NSA_GROUPS, 2, NSA_TK, NSA_HPG * NSA_TQ), BF16),
                        pltpu.VMEM((NSA_GROUPS, NSA_VPAD, NSA_HPG * NSA_TQ), F32)],
        compiler_params=_cparams(("parallel", "arbitrary")),
        name="nsa_attention",
    )(qnt, p, kcg, vct, ksg, vst, kwg, vwt, _overlap_t(s))


def _layout(pieces, tile):
    col, off, mats = {}, 0, []
    for name, w in pieces:
        width = w.shape[1]
        assert off % width == 0, (name, off, width)
        col[name] = (width, off // width)
        mats.append(w)
        off += width
    pad = -off % tile
    if pad:
        mats.append(jnp.zeros((mats[0].shape[0], pad), mats[0].dtype))
    return jnp.concatenate(mats, axis=1).astype(BF16), col


def _pad_cols(w, width):
    return jnp.pad(w, ((0, 0), (0, width - w.shape[1])))


def _ab_layout(w_in):
    hk = GLA_HEADS * GLA_DK
    hv = GLA_HEADS * GLA_DV
    kv = NSA_GROUPS * NSA_DH
    sizes = [hk, hk, hv, hv, GLA_RANK, NSA_HEADS * NSA_DH] + [kv] * 6 + [3 * NSA_HEADS]
    cuts = np.cumsum(sizes)[:-1].tolist()
    gq, gk, gv, gr, gz, nq, kc, vc, ks, vs, kw, vw, gt = jnp.split(w_in, cuts, axis=1)
    gates = _pad_cols(jnp.concatenate([gz, gt], axis=1), LANES)
    return _layout([("gla_q", gq), ("gla_k", gk), ("gla_v", gv), ("gla_r", gr), ("nsa_q", nq),
                    ("nsa_kc", kc), ("nsa_vc", vc), ("nsa_ks", ks), ("nsa_vs", vs), ("nsa_kw", kw),
                    ("nsa_vw", vw), ("gates", gates)], AB_TILE)


def _ml_layout(w_in):
    hq = ML_HEADS * ML_DQK
    hv = ML_HEADS * ML_DV
    wide = 2 * hq + 2 * hv
    col = {"ml_q": (hq, 0), "ml_k": (hq, 1), "ml_v": (hv, 1), "ml_og": (hv, 2), "gates": (LANES, wide // LANES)}
    total = -(-(wide + LANES) // ML_TILE) * ML_TILE
    return jnp.pad(w_in, ((0, 0), (0, total - w_in.shape[1]))).astype(BF16), col


def kernel(x, c, norm_mix_g, mod_mix_w, mod_mix_b, norm_mlp_g, mod_mlp_w, mod_mlp_b, mlp_w1, mlp_w2, ab_w_in, ab_w_out, gla_w_gate, gla_b_gate, gla_norm_g, nsa_q_norm_g, nsa_k_norm_g, nsa_cmp_pos_k, nsa_cmp_pos_v, nsa_cmp_k_w1, nsa_cmp_k_w2, nsa_cmp_v_w1, nsa_cmp_v_w2, ml_w_in, ml_w_out, ml_conv_w, ml_b_i, ml_b_f, ml_norm_g):
    bsz, s, d = x.shape
    depth = norm_mix_g.shape[0]
    mod_mix = _ada_mod(c, mod_mix_w, mod_mix_b)
    mod_mlp = _ada_mod(c, mod_mlp_w, mod_mlp_b)
    mlp_w1_bf16, mlp_w2_bf16 = mlp_w1.astype(BF16), mlp_w2.astype(BF16)

    def split_mod(m):
        return [m[:, None, i * d:(i + 1) * d] for i in range(3)]

    for layer in range(depth):
        shift, scale, gate = split_mod(mod_mix[layer])
        if layer % 2 == 0:
            e = layer // 2
            w_in, col = _ab_layout(ab_w_in[e])
            p = _norm_proj(x, norm_mix_g[layer], scale, shift, w_in, tm=PROJ_ROWS, tn=AB_TILE)
            w_gate = jnp.pad(gla_w_gate[e], ((0, LANES - GLA_RANK), (0, 0))).astype(BF16)
            o_gla = _gla(p, w_gate, gla_b_gate[e], gla_norm_g[e], col)
            q_gain = (jnp.tile(nsa_q_norm_g[e], NSA_HEADS) * (NSA_DH ** -0.5 * LOG2E)).reshape(-1, 1)
            k_gain = jnp.tile(nsa_k_norm_g[e], NSA_GROUPS).reshape(1, -1)
            qnt, ksg, vst, kwg, vwt = _nsa_prep(p, q_gain, k_gain, col)
            kcg, vct = _compress(p, nsa_cmp_pos_k[e], nsa_cmp_pos_v[e], nsa_cmp_k_w1[e],
                                 nsa_cmp_k_w2[e], nsa_cmp_v_w1[e], nsa_cmp_v_w2[e], k_gain, col)
            o_nsa = _nsa_attention(p, qnt, kcg, vct, ksg, vst, kwg, vwt, col)
            w_out = ab_w_out[e].astype(BF16)
            half = GLA_HEADS * GLA_DV
            x = _out_proj([o_gla, o_nsa], [w_out[:half], w_out[half:]], x, gate)
        else:
            o = layer // 2
            w_in, col = _ml_layout(ml_w_in[o])
            p = _norm_proj(x, norm_mix_g[layer], scale, shift, w_in, tm=PROJ_ROWS, tn=ML_TILE)
            bias = jnp.concatenate([ml_b_i[o], ml_b_f[o]])
            bias_col = _pad_cols(bias.reshape(1, -1), LANES)
            bias_row = jnp.broadcast_to(bias.reshape(-1, 1), (2 * ML_HEADS, LANES))
            hh = _mlstm(p, ml_conv_w[o], bias_col, bias_row, ml_norm_g[o], col)
            x = _out_proj([hh], [ml_w_out[o].astype(BF16)], x, gate)
        shift, scale, gate = split_mod(mod_mlp[layer])
        x = _mlp_sublayer(x, norm_mlp_g[layer], scale, shift, gate, mlp_w1_bf16, mlp_w2_bf16, layer)
    return x
```

```python
import functools

import numpy as np
import jax
import jax.numpy as jnp
from jax import lax
from jax.experimental import pallas as pl
from jax.experimental.pallas import tpu as pltpu

F32 = jnp.float32
BF16 = jnp.bfloat16
HI = lax.Precision.HIGHEST

EPS = 1e-6
NEG = -1e30
LOG2E = 1.4426950408889634

VMEM_LIMIT_BYTES = 56 * 1024 * 1024
LANES = 128

PROJ_ROWS = 1024
AB_TILE = 1536
ML_TILE = 1280
MLP_ROWS = 512
MLP_HIDDEN_TILE = 2048
MLP_VMEM_LIMIT_BYTES = 60 * 1024 * 1024
OUT_ROWS = 512
MOD_TILE = 768
PREP_ROWS = 512

GLA_HEADS = 4
GLA_DK = 128
GLA_DV = 256
GLA_RANK = 16
GLA_GATE_TAU = 16.0
CHUNK = 64
ML_CHUNK = 128
REC_BLOCK = 512

NSA_HEADS = 16
NSA_GROUPS = 4
NSA_HPG = 4
NSA_DH = 64
CMP_LEN = 32
CMP_STRIDE = 16
SEL_BLOCK = 64
N_SELECT = 16
WINDOW = 512
NSA_TQ = 256
NSA_TK = 256
NSA_VPAD = 80

ML_HEADS = 4
ML_DQK = 256
ML_DV = 512
ML_CONV = 4


def _cparams(sem, vmem_limit_bytes=VMEM_LIMIT_BYTES):
    return pltpu.CompilerParams(dimension_semantics=sem, vmem_limit_bytes=vmem_limit_bytes)


def _sigmoid(x):
    return 1.0 / (1.0 + jnp.exp(-x))


def _silu(x):
    return x * _sigmoid(x)


def _log_sigmoid(x):
    return jnp.minimum(x, 0.0) - jnp.log(1.0 + jnp.exp(-jnp.abs(x)))


def _dot(a, b, precision=None):
    return jnp.dot(a, b, preferred_element_type=F32, precision=precision)


def _dot_nt(a, b, precision=None):
    return lax.dot_general(a, b, (((1,), (1,)), ((), ())), preferred_element_type=F32, precision=precision)


def _split_bf16(x, parts=2):
    out = []
    for _ in range(parts - 1):
        piece = x.astype(BF16)
        out.append(piece)
        x = x - piece.astype(F32)
    return out + [x.astype(BF16)]


def _rms(x):
    return x * lax.rsqrt(jnp.mean(x * x, axis=-1, keepdims=True) + EPS)


def _mod_kernel(c_ref, w_ref, b_ref, o_ref):
    a = _silu(c_ref[...]).astype(BF16)
    o_ref[...] = _dot(a, w_ref[...].astype(BF16)) + b_ref[...]


def _ada_mod(c, w, b):
    nl, d, d3 = w.shape
    bsz = c.shape[0]
    tn = MOD_TILE
    return pl.pallas_call(
        _mod_kernel,
        grid=(nl, d3 // tn),
        in_specs=[pl.BlockSpec((bsz, d), lambda l, j: (0, 0)),
                  pl.BlockSpec((None, d, tn), lambda l, j: (l, 0, j)),
                  pl.BlockSpec((None, 1, tn), lambda l, j: (l, 0, j))],
        out_specs=pl.BlockSpec((None, bsz, tn), lambda l, j: (l, 0, j)),
        out_shape=jax.ShapeDtypeStruct((nl, bsz, d3), F32),
        compiler_params=_cparams(("parallel", "parallel")),
        name="ada_mod",
    )(c, w, b.reshape(nl, 1, d3))


def _modulated_norm(x_ref, g_ref, sc_ref, sh_ref):
    x = x_ref[...]
    gain = g_ref[...] * (1.0 + sc_ref[...])
    inv = lax.rsqrt(jnp.mean(x * x, axis=-1, keepdims=True) + EPS)
    return ((x * inv) * gain + sh_ref[...]).astype(BF16)


def _proj_kernel(x_ref, g_ref, sc_ref, sh_ref, w_ref, o_ref, h_ref):
    i = pl.program_id(2)

    @pl.when(pl.program_id(1) == 0)
    def _():
        h_ref[i] = _modulated_norm(x_ref, g_ref, sc_ref, sh_ref)

    o_ref[...] = _dot(h_ref[i], w_ref[...])


def _norm_proj(x, g, scale, shift, w, tm, tn):
    bsz, s, d = x.shape
    n = w.shape[1]
    rows = s // tm
    x_spec = pl.BlockSpec((None, tm, d), lambda b, j, i: (b, jnp.where(j == 0, i, rows - 1), 0))
    return pl.pallas_call(
        _proj_kernel,
        grid=(bsz, n // tn, rows),
        in_specs=[x_spec,
                  pl.BlockSpec((1, d), lambda b, j, i: (0, 0)),
                  pl.BlockSpec((None, 1, d), lambda b, j, i: (b, 0, 0)),
                  pl.BlockSpec((None, 1, d), lambda b, j, i: (b, 0, 0)),
                  pl.BlockSpec((d, tn), lambda b, j, i: (0, j))],
        out_specs=pl.BlockSpec((None, tm, tn), lambda b, j, i: (b, i, j)),
        out_shape=jax.ShapeDtypeStruct((bsz, s, n), F32),
        scratch_shapes=[pltpu.VMEM((rows, tm, d), BF16)],
        compiler_params=_cparams(("parallel", "arbitrary", "arbitrary")),
        name="norm_proj",
    )(x, g.reshape(1, d), scale, shift, w)


def _mlp_kernel(x_ref, g_ref, sc_ref, sh_ref, gate_ref, w1_ref, w2_ref, o_ref, h_ref):
    @pl.when(pl.program_id(2) == 0)
    def _():
        h_ref[...] = _modulated_norm(x_ref, g_ref, sc_ref, sh_ref)
        o_ref[...] = x_ref[...]

    sub = w1_ref.shape[1] // 2
    for kk in range(2):
        cols = slice(kk * sub, (kk + 1) * sub)
        u = jnp.maximum(_dot(h_ref[...], w1_ref[:, cols]), 0.0)
        o_ref[...] += gate_ref[...] * _dot((u * u).astype(BF16), w2_ref[cols, :])


def _mlp_sublayer(x, g, scale, shift, gate, w1, w2, layer, tm=MLP_ROWS, tf=MLP_HIDDEN_TILE):
    bsz, s, d = x.shape
    ff = w1.shape[2]
    vec = pl.BlockSpec((None, 1, d), lambda b, i, f: (b, 0, 0))
    return pl.pallas_call(
        _mlp_kernel,
        grid=(bsz, s // tm, ff // tf),
        in_specs=[pl.BlockSpec((None, tm, d), lambda b, i, f: (b, i, 0)),
                  pl.BlockSpec((1, d), lambda b, i, f: (0, 0)),
                  vec, vec, vec,
                  pl.BlockSpec((None, d, tf), lambda b, i, f: (layer, 0, f)),
                  pl.BlockSpec((None, tf, d), lambda b, i, f: (layer, f, 0))],
        out_specs=pl.BlockSpec((None, tm, d), lambda b, i, f: (b, i, 0)),
        out_shape=jax.ShapeDtypeStruct((bsz, s, d), F32),
        scratch_shapes=[pltpu.VMEM((tm, d), BF16)],
        compiler_params=_cparams(("parallel", "parallel", "arbitrary"), MLP_VMEM_LIMIT_BYTES),
        name="mlp_sublayer",
    )(x, g.reshape(1, d), scale, shift, gate, w1, w2)


def _outproj_kernel(n_in, *refs):
    a_refs, w_refs = refs[:n_in], refs[n_in:2 * n_in]
    x_ref, gate_ref, o_ref = refs[2 * n_in:]
    y = _dot(a_refs[0][...], w_refs[0][...])
    for a_ref, w_ref in zip(a_refs[1:], w_refs[1:]):
        y += _dot(a_ref[...], w_ref[...])
    o_ref[...] = x_ref[...] + gate_ref[...] * y


def _out_proj(acts, ws, x, gate, tm=OUT_ROWS):
    bsz, s, d = x.shape
    n_in = len(acts)
    in_specs = [pl.BlockSpec((None, tm, a.shape[-1]), lambda b, i: (b, i, 0)) for a in acts]
    in_specs += [pl.BlockSpec(w.shape, lambda b, i: (0, 0)) for w in ws]
    in_specs += [pl.BlockSpec((None, tm, d), lambda b, i: (b, i, 0)),
                 pl.BlockSpec((None, 1, d), lambda b, i: (b, 0, 0))]
    return pl.pallas_call(
        functools.partial(_outproj_kernel, n_in),
        grid=(bsz, s // tm),
        in_specs=in_specs,
        out_specs=pl.BlockSpec((None, tm, d), lambda b, i: (b, i, 0)),
        out_shape=jax.ShapeDtypeStruct((bsz, s, d), F32),
        compiler_params=_cparams(("parallel", "parallel")),
        name="out_proj",
    )(*acts, *ws, x, gate)


def _tri(n, upper=False):
    r = lax.broadcasted_iota(jnp.int32, (n, n), 0)
    c = lax.broadcasted_iota(jnp.int32, (n, n), 1)
    return (c >= r) if upper else (c <= r)


def _gla_kernel(q_ref, k_ref, v_ref, r_ref, zg_ref, wg_ref, bg_ref, ng_ref, o_ref, st_ref):
    @pl.when(pl.program_id(1) == 0)
    def _():
        st_ref[...] = jnp.zeros_like(st_ref)

    tril = _tri(CHUNK)
    tril_b = tril.astype(BF16)
    pre = _dot(zg_ref[...].astype(BF16), wg_ref[...]) + bg_ref[...]
    parts = _split_bf16(_log_sigmoid(pre) / GLA_GATE_TAU, 3)
    chunks = [slice(c * CHUNK, (c + 1) * CHUNK) for c in range(REC_BLOCK // CHUNK)]
    cum_all = jnp.concatenate([sum(_dot(tril_b, part[rows, :]) for part in parts) for rows in chunks], axis=0)
    q_dec_all = (q_ref[...] * GLA_DK ** -0.5) * jnp.exp(cum_all)
    k_dec_all = k_ref[...] * jnp.exp(-cum_all)
    qb_all = q_dec_all.astype(BF16)
    kb_all = k_dec_all.astype(BF16)
    vb_all = v_ref[...].astype(BF16)
    gate_all = _silu(r_ref[...])
    intra = {}
    for c, rows in enumerate(chunks):
        for h in range(GLA_HEADS):
            ks = slice(h * GLA_DK, (h + 1) * GLA_DK)
            vs = slice(h * GLA_DV, (h + 1) * GLA_DV)
            att = jnp.where(tril, _dot_nt(qb_all[rows, ks], kb_all[rows, ks]), 0.0)
            intra[c, h] = _dot(att.astype(BF16), vb_all[rows, vs])
    for c, rows in enumerate(chunks):
        cum = cum_all[rows, :]
        total = cum[CHUNK - 1:CHUNK, :]
        k_end = (k_ref[rows, :] * jnp.exp(total - cum)).astype(BF16)
        decay = jnp.exp(total)
        for h in range(GLA_HEADS):
            ks = slice(h * GLA_DK, (h + 1) * GLA_DK)
            vs = slice(h * GLA_DV, (h + 1) * GLA_DV)
            st = st_ref[h]
            o = intra[c, h] + _dot_nt(qb_all[rows, ks], st.astype(BF16))
            st_ref[h] = st * decay[:, ks] + _dot(v_ref[rows, vs].T.astype(BF16), k_end[:, ks])
            o_ref[rows, vs] = ((_rms(o) * ng_ref[...]) * gate_all[rows, vs]).astype(o_ref.dtype)


def _gla(p, w_gate, b_gate, norm_g, col):
    bsz, s, _ = p.shape
    t = REC_BLOCK

    def piece(name):
        width, idx = col[name]
        return pl.BlockSpec((None, t, width), lambda b, i, idx=idx: (b, i, idx))

    const = lambda shape: pl.BlockSpec(shape, lambda b, i: (0,) * len(shape))
    hk = GLA_HEADS * GLA_DK
    return pl.pallas_call(
        _gla_kernel,
        grid=(bsz, s // t),
        in_specs=[piece("gla_q"), piece("gla_k"), piece("gla_v"), piece("gla_r"), piece("gates"),
                  const((LANES, hk)), const((1, hk)), const((1, GLA_DV))],
        out_specs=pl.BlockSpec((None, t, GLA_HEADS * GLA_DV), lambda b, i: (b, i, 0)),
        out_shape=jax.ShapeDtypeStruct((bsz, s, GLA_HEADS * GLA_DV), BF16),
        scratch_shapes=[pltpu.VMEM((GLA_HEADS, GLA_DV, GLA_DK), F32)],
        compiler_params=_cparams(("parallel", "arbitrary")),
        name="gla",
    )(p, p, p, p, p, w_gate, b_gate.reshape(1, hk), norm_g.reshape(1, GLA_DV))


def _mlstm_kernel(q_ref, k_ref, v_ref, og_ref, gc_ref, wq_ref, wk_ref, bc_ref, br_ref, ng_ref,
                  o_ref, c_ref, n_ref, m_ref, qx_ref, kx_ref):
    t = REC_BLOCK

    @pl.when(pl.program_id(1) == 0)
    def _():
        c_ref[...] = jnp.zeros_like(c_ref)
        n_ref[...] = jnp.zeros_like(n_ref)
        m_ref[...] = jnp.zeros_like(m_ref)
        qx_ref[0:8, :] = jnp.zeros((8, qx_ref.shape[1]), F32)
        kx_ref[0:8, :] = jnp.zeros((8, kx_ref.shape[1]), F32)

    def conv_silu(x_ref, xx_ref, w_ref):
        xx_ref[8:8 + t, :] = x_ref[...]
        y = xx_ref[8:8 + t, :] * w_ref[ML_CONV - 1:ML_CONV, :]
        for back in range(1, ML_CONV):
            y += xx_ref[8 - back:8 - back + t, :] * w_ref[ML_CONV - 1 - back:ML_CONV - back, :]
        xx_ref[0:8, :] = xx_ref[t:t + 8, :]
        return _silu(y)

    q_all = conv_silu(q_ref, qx_ref, wq_ref) * ML_DQK ** -0.5
    k_all = conv_silu(k_ref, kx_ref, wk_ref)

    tril = _tri(ML_CHUNK)
    tril_b = tril.astype(BF16)
    triu_b = _tri(ML_CHUNK, upper=True).astype(BF16)
    lane = lax.broadcasted_iota(jnp.int32, (ML_CHUNK, LANES), 1)
    is_f_col = (lane >= ML_HEADS) & (lane < 2 * ML_HEADS)
    sub = lax.broadcasted_iota(jnp.int32, (8, ML_CHUNK), 0)
    is_f_row = sub >= ML_HEADS
    gates_by_row = gc_ref[...].T[0:2 * ML_HEADS, :]

    for c in range(t // ML_CHUNK):
        rows = slice(c * ML_CHUNK, (c + 1) * ML_CHUNK)
        gcol = gc_ref[rows, :] + bc_ref[...]
        gcol = jnp.where(is_f_col, _log_sigmoid(gcol), gcol)
        bcol = sum(_dot(tril_b, part) for part in _split_bf16(gcol, 3))
        grow = gates_by_row[:, rows] + br_ref[:, 0:1]
        grow = jnp.where(is_f_row, _log_sigmoid(grow), grow)
        brow = sum(_dot(part, triu_b) for part in _split_bf16(grow, 3))
        for h in range(ML_HEADS):
            qs = slice(h * ML_DQK, (h + 1) * ML_DQK)
            vs = slice(h * ML_DV, (h + 1) * ML_DV)
            b_col = bcol[:, ML_HEADS + h:ML_HEADS + h + 1]
            b_row = brow[ML_HEADS + h:ML_HEADS + h + 1, :]
            li_col = gcol[:, h:h + 1]
            li_row = grow[h:h + 1, :]
            m_prev = m_ref[h:h + 1, 0:1]
            log_d = jnp.where(tril, b_col - b_row + li_row, -jnp.inf)
            log_inter = b_col + m_prev
            m_comb = jnp.maximum(log_inter, jnp.max(log_d, axis=-1, keepdims=True))
            d = jnp.exp(log_d - m_comb)
            w_inter = jnp.exp(log_inter - m_comb)
            qh = q_all[rows, qs]
            kh = k_all[rows, qs]
            qb = qh.astype(BF16)
            vb = v_ref[rows, vs].astype(BF16)
            sc = _dot_nt(qb, kh.astype(BF16)) * d
            cm = c_ref[h]
            nv = n_ref[h]
            num = _dot(sc.astype(BF16), vb) + w_inter * _dot(qb, cm.astype(BF16))
            den = jnp.sum(sc, axis=-1, keepdims=True) + w_inter * jnp.sum(qh * nv, axis=-1, keepdims=True)
            hh = num / jnp.maximum(jnp.abs(den), jnp.exp(-m_comb))
            total = b_col[ML_CHUNK - 1:ML_CHUNK, :]
            log_w = total - b_col + li_col
            m_new = jnp.maximum(total + m_prev, jnp.max(log_w, axis=0, keepdims=True))
            decay = jnp.exp(total + m_prev - m_new)
            kw = kh * jnp.exp(log_w - m_new)
            c_ref[h] = decay * cm + _dot(kw.T.astype(BF16), vb)
            n_ref[h] = decay * nv + jnp.sum(kw, axis=0, keepdims=True)
            m_ref[h:h + 1, :] = jnp.broadcast_to(m_new, (1, LANES))
            og = og_ref[rows, vs]
            o_ref[rows, vs] = ((_rms(hh) * ng_ref[...]) * _sigmoid(og)).astype(o_ref.dtype)


def _mlstm(p, conv_w, bias_col, bias_row, norm_g, col):
    bsz, s, _ = p.shape
    t = REC_BLOCK
    hq = ML_HEADS * ML_DQK
    hv = ML_HEADS * ML_DV

    def piece(name):
        width, idx = col[name]
        return pl.BlockSpec((None, t, width), lambda b, i, idx=idx: (b, i, idx))

    const = lambda shape: pl.BlockSpec(shape, lambda b, i: (0,) * len(shape))
    return pl.pallas_call(
        _mlstm_kernel,
        grid=(bsz, s // t),
        in_specs=[piece("ml_q"), piece("ml_k"), piece("ml_v"), piece("ml_og"), piece("gates"),
                  pl.BlockSpec((ML_CONV, hq), lambda b, i: (0, 0)),
                  pl.BlockSpec((ML_CONV, hq), lambda b, i: (0, 1)),
                  const((1, LANES)), const((8, LANES)), const((1, ML_DV))],
        out_specs=pl.BlockSpec((None, t, hv), lambda b, i: (b, i, 0)),
        out_shape=jax.ShapeDtypeStruct((bsz, s, hv), BF16),
        scratch_shapes=[pltpu.VMEM((ML_HEADS, ML_DQK, ML_DV), F32),
                        pltpu.VMEM((ML_HEADS, 1, ML_DQK), F32),
                        pltpu.VMEM((8, LANES), F32),
                        pltpu.VMEM((t + 8, hq), F32),
                        pltpu.VMEM((t + 8, hq), F32)],
        compiler_params=_cparams(("parallel", "arbitrary")),
        name="mlstm",
    )(p, p, p, p, p, conv_w, conv_w, bias_col, bias_row, norm_g.reshape(1, ML_DV))


def _seg_norm(x, seg_ref, segt_ref):
    if seg_ref.dtype == BF16:
        seg_sum = lambda v, m_ref: sum(_dot(part, m_ref[...]) for part in _split_bf16(v))
    else:
        seg_sum = lambda v, m_ref: _dot(v, m_ref[...], precision=HI)
    ss = seg_sum(x * x, seg_ref) * (1.0 / NSA_DH)
    return x * seg_sum(lax.rsqrt(ss + EPS), segt_ref)


def _nsa_prep_kernel(q_ref, ks_ref, vs_ref, kw_ref, vw_ref, qg_ref, kg_ref, segk_ref, segkt_ref,
                     qnt_ref, ksg_ref, vst_ref, kwg_ref, vwt_ref):
    t = q_ref.shape[0]
    qt = q_ref[...].T
    for h in range(NSA_HEADS):
        hs = slice(h * NSA_DH, (h + 1) * NSA_DH)
        blk = qt[hs, :]
        inv = lax.rsqrt(jnp.mean(blk * blk, axis=0, keepdims=True) + EPS)
        qnt_ref[hs, :] = ((blk * inv) * qg_ref[hs, :]).astype(BF16)
    ones = jnp.ones((NSA_VPAD - NSA_DH, t), BF16)
    for k_ref, v_ref, kg_out, vt_out in ((ks_ref, vs_ref, ksg_ref, vst_ref), (kw_ref, vw_ref, kwg_ref, vwt_ref)):
        kn = (_seg_norm(k_ref[...], segk_ref, segkt_ref) * kg_ref[...]).astype(BF16)
        vt = v_ref[...].T.astype(BF16)
        for g in range(NSA_GROUPS):
            gs = slice(g * NSA_DH, (g + 1) * NSA_DH)
            kg_out[g] = kn[:, gs]
            vt_out[g, 0:NSA_DH, :] = vt[gs, :]
            vt_out[g, NSA_DH:NSA_VPAD, :] = ones


def _seg_matrices(width, dtype):
    seg = (np.arange(width)[:, None] // NSA_DH == np.arange(LANES)[None, :]).astype(np.float32)
    return jnp.asarray(seg, dtype=dtype), jnp.asarray(seg.T, dtype=dtype)


def _nsa_prep(p, q_gain, k_gain, col, t=PREP_ROWS):
    bsz, s, _ = p.shape
    hq = NSA_HEADS * NSA_DH
    kv = NSA_GROUPS * NSA_DH

    def piece(name):
        width, idx = col[name]
        return pl.BlockSpec((None, t, width), lambda b, i, idx=idx: (b, i, idx))

    const = lambda shape: pl.BlockSpec(shape, lambda b, i: (0,) * len(shape))
    segk, segkt = _seg_matrices(kv, BF16)
    keys = pl.BlockSpec((None, NSA_GROUPS, t, NSA_DH), lambda b, i: (b, 0, i, 0))
    vals = pl.BlockSpec((None, NSA_GROUPS, NSA_VPAD, t), lambda b, i: (b, 0, 0, i))
    shp = lambda *dims, dt=BF16: jax.ShapeDtypeStruct(dims, dt)
    keys_shape = shp(bsz, NSA_GROUPS, s, NSA_DH)
    vals_shape = shp(bsz, NSA_GROUPS, NSA_VPAD, s)
    return pl.pallas_call(
        _nsa_prep_kernel,
        grid=(bsz, s // t),
        in_specs=[piece("nsa_q"), piece("nsa_ks"), piece("nsa_vs"), piece("nsa_kw"), piece("nsa_vw"),
                  const((hq, 1)), const((1, kv)), const((kv, LANES)), const((LANES, kv))],
        out_specs=[pl.BlockSpec((None, hq, t), lambda b, i: (b, 0, i)), keys, vals, keys, vals],
        out_shape=[shp(bsz, hq, s), keys_shape, vals_shape, keys_shape, vals_shape],
        compiler_params=_cparams(("parallel", "parallel")),
        name="nsa_prep",
    )(p, p, p, p, p, q_gain, k_gain, segk, segkt)


def _compress_kernel(kc0_ref, kc1_ref, vc0_ref, vc1_ref, posk_ref, posv_ref, kw1a_ref, kw1b_ref, kw2_ref,
                     vw1a_ref, vw1b_ref, vw2_ref, kg_ref, segk_ref, segkt_ref, kcg_ref, vct_ref):
    def hidden(x_refs, pos_ref, w1a_ref, w1b_ref):
        nsub = x_refs[0].shape[0] // CMP_STRIDE
        first = second = None
        for l in range(CMP_STRIDE):
            x = jnp.concatenate([r[pl.ds(l, nsub, stride=CMP_STRIDE), :] for r in x_refs], axis=1)
            a = _dot((x + pos_ref[l:l + 1, :]).astype(BF16), w1a_ref[l])
            b = _dot((x + pos_ref[CMP_STRIDE + l:CMP_STRIDE + l + 1, :]).astype(BF16), w1b_ref[l])
            first = a if first is None else first + a
            second = b if second is None else second + b
        row = lax.broadcasted_iota(jnp.int32, second.shape, 0)
        nxt = jnp.where(row == nsub - 1, 0.0, pltpu.roll(second, nsub - 1, 0))
        return _silu(first + nxt)

    kc = _dot(hidden((kc0_ref, kc1_ref), posk_ref, kw1a_ref, kw1b_ref).astype(BF16), kw2_ref[...])
    kn = (_seg_norm(kc, segk_ref, segkt_ref) * kg_ref[...]).astype(BF16)
    vt = _dot(vw2_ref[...], hidden((vc0_ref, vc1_ref), posv_ref, vw1a_ref, vw1b_ref).T.astype(BF16)).astype(BF16)
    for g in range(NSA_GROUPS):
        gs = slice(g * NSA_DH, (g + 1) * NSA_DH)
        kcg_ref[g] = kn[:, gs]
        vct_ref[g] = vt[gs, :]


def _expand_w1(w1):
    w = w1.reshape(2, CMP_STRIDE, NSA_DH, NSA_DH)
    eye = jnp.eye(NSA_GROUPS, dtype=w1.dtype)
    big = jnp.einsum('hlde,gk->hlgdke', w, eye)
    big = big.reshape(2, CMP_STRIDE, NSA_GROUPS * NSA_DH, NSA_GROUPS * NSA_DH).astype(BF16)
    return big[0], big[1]


def _expand_pos(pos):
    return jnp.tile(pos, (1, NSA_GROUPS))


def _block_diag(w2):
    return jnp.kron(jnp.eye(NSA_GROUPS, dtype=w2.dtype), w2).astype(BF16)


def _compress(p, pos_k, pos_v, ck_w1, ck_w2, cv_w1, cv_w2, k_gain, col):
    bsz, s, _ = p.shape
    kv = NSA_GROUPS * NSA_DH
    nsub = s // CMP_STRIDE
    segk, segkt = _seg_matrices(kv, F32)
    kw1a, kw1b = _expand_w1(ck_w1)
    vw1a, vw1b = _expand_w1(cv_w1)
    const = lambda shape: pl.BlockSpec(shape, lambda b: (0,) * len(shape))

    def halves(name):
        width, idx = col[name]
        assert width == 2 * LANES
        return [pl.BlockSpec((None, s, LANES), lambda b, j=2 * idx + h: (b, 0, j)) for h in range(2)]

    w1_spec = const((CMP_STRIDE, kv, kv))
    return pl.pallas_call(
        _compress_kernel,
        grid=(bsz,),
        in_specs=[*halves("nsa_kc"), *halves("nsa_vc"), const((CMP_LEN, kv)), const((CMP_LEN, kv)),
                  w1_spec, w1_spec, const((kv, kv)),
                  w1_spec, w1_spec, const((kv, kv)),
                  const((1, kv)), const((kv, LANES)), const((LANES, kv))],
        out_specs=[pl.BlockSpec((None, NSA_GROUPS, nsub, NSA_DH), lambda b: (b, 0, 0, 0)),
                   pl.BlockSpec((None, NSA_GROUPS, NSA_DH, nsub), lambda b: (b, 0, 0, 0))],
        out_shape=[jax.ShapeDtypeStruct((bsz, NSA_GROUPS, nsub, NSA_DH), BF16),
                   jax.ShapeDtypeStruct((bsz, NSA_GROUPS, NSA_DH, nsub), BF16)],
        compiler_params=_cparams(("parallel",)),
        name="nsa_compress",
    )(p, p, p, p, _expand_pos(pos_k), _expand_pos(pos_v),
      kw1a, kw1b, _block_diag(ck_w2), vw1a, vw1b, _block_diag(cv_w2).T, k_gain, segk, segkt)


def _nsa_attn_kernel(qt_ref, gt_ref, kcg_ref, vct_ref, ksg_ref, vst_ref, kwg_ref, vwt_ref, ovt_ref,
                     o_ref, acc_ref, sel_ref, s_ref, p_ref, pv_ref):
    tq, tk = NSA_TQ, NSA_TK
    q0 = pl.program_id(1) * tq
    ncmp = kcg_ref.shape[1]
    nsel = ovt_ref.shape[0]
    cols4 = NSA_HPG * tq

    t_cmp = q0 + lax.broadcasted_iota(jnp.int32, (ncmp, cols4), 1) % tq
    n_idx = lax.broadcasted_iota(jnp.int32, (ncmp, cols4), 0)
    cmp_valid = n_idx * CMP_STRIDE + (CMP_LEN - 1) <= t_cmp

    j_idx = lax.broadcasted_iota(jnp.int32, (nsel, tq), 0)
    cur = (q0 + lax.broadcasted_iota(jnp.int32, (nsel, tq), 1)) // SEL_BLOCK
    forced = (j_idx == 0) | (j_idx == cur) | (j_idx == cur - 1)
    causal_blk = j_idx <= cur

    row_blk = lax.broadcasted_iota(jnp.int32, (SEL_BLOCK, tq), 0)
    t_row = q0 + lax.broadcasted_iota(jnp.int32, (1, tq), 1)

    gates_t = _sigmoid(gt_ref[...]).T

    def add_gated(g, branch, o4):
        for i in range(NSA_HPG):
            h = g * NSA_HPG + i
            hs = slice(h * NSA_DH, (h + 1) * NSA_DH)
            gc = GLA_RANK + 3 * h + branch
            term = gates_t[gc:gc + 1, :] * o4[:, i * tq:(i + 1) * tq]
            acc_ref[hs, :] = term if branch == 0 else acc_ref[hs, :] + term

    def q_group(g):
        return jnp.concatenate([qt_ref[(g * NSA_HPG + i) * NSA_DH:(g * NSA_HPG + i + 1) * NSA_DH, :]
                                for i in range(NSA_HPG)], axis=1)

    def attend(branch, k_ref, vt_ref, first_tile, n_tiles, mask_fn):
        last = first_tile + n_tiles - 1

        def scores(g, kt):
            return _dot(k_ref[g, pl.ds(pl.multiple_of(kt * tk, tk), tk), :], q_group(g))

        def weighted_values(g, kt, p):
            return _dot(vt_ref[g, :, pl.ds(pl.multiple_of(kt * tk, tk), tk)], p)

        def step(kt, cur, ms):
            nxt = 1 - cur
            out = []
            for g in range(NSA_GROUPS):
                m = ms[g]
                s_ref[g, nxt] = scores(g, jnp.minimum(kt + 1, last))
                acc = pv_ref[g] + weighted_values(g, jnp.maximum(kt - 1, first_tile), p_ref[g, nxt])
                ok = mask_fn(g, kt, kt * tk)
                m_new = []
                for c in range(cols4 // LANES):
                    cs = slice(c * LANES, (c + 1) * LANES)
                    qs = slice((c % (tq // LANES)) * LANES, (c % (tq // LANES) + 1) * LANES)
                    s = jnp.where(ok[:, qs], s_ref[g, cur, :, cs], NEG)
                    m_c = jnp.maximum(m[:, cs], jnp.max(s, axis=0, keepdims=True))
                    p_ref[g, cur, :, cs] = jnp.exp2(s - m_c).astype(BF16)
                    m_new.append(m_c)
                m_new = jnp.concatenate(m_new, axis=1)
                pv_ref[g] = acc * jnp.exp2(m - m_new)
                out.append(m_new)
            return tuple(out)

        for g in range(NSA_GROUPS):
            s_ref[g, 0] = scores(g, first_tile)
            p_ref[g, 1] = jnp.zeros(p_ref.shape[2:], BF16)
        pv_ref[...] = jnp.zeros_like(pv_ref)

        def body(j, ms):
            kt = first_tile + 2 * j
            return step(kt + 1, 1, step(kt, 0, ms))

        ms = lax.fori_loop(0, n_tiles // 2, body, tuple(jnp.full((1, cols4), NEG, F32) for _ in range(NSA_GROUPS)))

        @pl.when(n_tiles % 2 == 1)
        def _():
            step(last, 0, ms)

        for g in range(NSA_GROUPS):
            acc = pv_ref[g] + weighted_values(g, last, p_ref[g, (n_tiles - 1) % 2])
            add_gated(g, branch, acc[0:NSA_DH, :] / acc[NSA_DH:NSA_DH + 1, :])

    last_tile = (q0 + tq - 1) // tk
    last_blk = (q0 + tq - 1) // SEL_BLOCK

    def sel_mask(g, kt, k0):
        per_tile = tk // SEL_BLOCK
        blocks = []
        for j in range(per_tile):
            picked = sel_ref[g, pl.ds(kt * per_tile + j, 1), :] > 0.5
            limit = jnp.where(picked, t_row, -1) - (k0 + j * SEL_BLOCK)
            blocks.append(row_blk <= limit)
        return jnp.concatenate(blocks, axis=0)

    win_span = WINDOW + tk
    win_start = pl.multiple_of(jnp.maximum(last_tile - WINDOW // tk, 0) * tk, tk)
    row_w = lax.broadcasted_iota(jnp.int32, (win_span, tq), 0)
    ahead = t_row - win_start
    win_ok = (row_w <= ahead) & (row_w > ahead - WINDOW)

    def window_branch(g):
        s = _dot(kwg_ref[g, pl.ds(win_start, win_span), :], q_group(g))
        probs = []
        for c in range(cols4 // LANES):
            qs = slice((c % (tq // LANES)) * LANES, (c % (tq // LANES) + 1) * LANES)
            sc = jnp.where(win_ok[:, qs], s[:, c * LANES:(c + 1) * LANES], NEG)
            probs.append(jnp.exp2(sc - jnp.max(sc, axis=0, keepdims=True)).astype(BF16))
        acc = _dot(vwt_ref[g, :, pl.ds(win_start, win_span)], jnp.concatenate(probs, axis=1))
        add_gated(g, 2, acc[0:NSA_DH, :] / acc[NSA_DH:NSA_DH + 1, :])

    importance = []
    for g in range(NSA_GROUPS):
        lg = jnp.where(cmp_valid, _dot(kcg_ref[g], q_group(g)), NEG)
        ex = jnp.exp2(lg - jnp.max(lg, axis=0, keepdims=True))
        p_cmp = jnp.where(cmp_valid, ex / jnp.sum(ex, axis=0, keepdims=True), 0.0)
        add_gated(g, 0, _dot(vct_ref[g], p_cmp.astype(BF16)))
        p_grp = p_cmp[:, 0:tq] + p_cmp[:, tq:2 * tq] + p_cmp[:, 2 * tq:3 * tq] + p_cmp[:, 3 * tq:4 * tq]
        importance.append(sum(_dot(ovt_ref[...], part) for part in _split_bf16(p_grp)))
        window_branch(g)

    @pl.when(last_blk < N_SELECT)
    def _():
        for g in range(NSA_GROUPS):
            sel_ref[g] = causal_blk.astype(F32)

    @pl.when(last_blk >= N_SELECT)
    def _():
        for g in range(NSA_GROUPS):
            val = jnp.where(forced, jnp.inf, jnp.where(causal_blk, importance[g], -jnp.inf))
            rank = jnp.zeros((nsel, tq), F32)
            for i in range(nsel):
                vi = val[i:i + 1, :]
                rank += ((vi > val) | ((vi == val) & (j_idx > i))).astype(F32)
            sel_ref[g] = (rank < N_SELECT).astype(F32)

    attend(1, ksg_ref, vst_ref, 0, last_tile + 1, sel_mask)
    o_ref[...] = acc_ref[...].T.astype(o_ref.dtype)


def _overlap_t(s):
    ncmp = s // CMP_STRIDE
    nsel = s // SEL_BLOCK
    c_start = np.arange(ncmp) * CMP_STRIDE
    s_start = np.arange(nsel) * SEL_BLOCK
    overlap_t = ((c_start[None, :] <= s_start[:, None] + SEL_BLOCK - 1)
                 & (c_start[None, :] + CMP_LEN - 1 >= s_start[:, None])).astype(np.float32)
    overlap_t[:, ncmp - 1] = 0.0
    return jnp.asarray(overlap_t, dtype=BF16)


def _nsa_attention(p, qnt, kcg, vct, ksg, vst, kwg, vwt, col):
    bsz, hq, s = qnt.shape
    ncmp = s // CMP_STRIDE
    nsel = s // SEL_BLOCK
    gw, gidx = col["gates"]
    per_row = lambda shape: pl.BlockSpec((None,) + shape, lambda b, i: (b,) + (0,) * len(shape))
    keys = per_row((NSA_GROUPS, s, NSA_DH))
    vals = per_row((NSA_GROUPS, NSA_VPAD, s))
    return pl.pallas_call(
        _nsa_attn_kernel,
        grid=(bsz, s // NSA_TQ),
        in_specs=[pl.BlockSpec((None, hq, NSA_TQ), lambda b, i: (b, 0, i)),
                  pl.BlockSpec((None, NSA_TQ, gw), lambda b, i: (b, i, gidx)),
                  per_row((NSA_GROUPS, ncmp, NSA_DH)), per_row((NSA_GROUPS, NSA_DH, ncmp)),
                  keys, vals, keys, vals,
                  pl.BlockSpec((nsel, ncmp), lambda b, i: (0, 0))],
        out_specs=pl.BlockSpec((None, NSA_TQ, hq), lambda b, i: (b, i, 0)),
        out_shape=jax.ShapeDtypeStruct((bsz, s, hq), BF16),
        scratch_shapes=[pltpu.VMEM((hq, NSA_TQ), F32), pltpu.VMEM((NSA_GROUPS, nsel, NSA_TQ), F32),
                        pltpu.VMEM((NSA_GROUPS, 2, NSA_TK, NSA_HPG * NSA_TQ), F32),
                        pltpu.VMEM((NSA_GROUPS, 2, NSA_TK, NSA_HPG * NSA_TQ), BF16),
                        pltpu.VMEM((NSA_GROUPS, NSA_VPAD, NSA_HPG * NSA_TQ), F32)],
        compiler_params=_cparams(("parallel", "arbitrary")),
        name="nsa_attention",
    )(qnt, p, kcg, vct, ksg, vst, kwg, vwt, _overlap_t(s))


def _layout(pieces, tile):
    col, off, mats = {}, 0, []
    for name, w in pieces:
        width = w.shape[1]
        assert off % width == 0, (name, off, width)
        col[name] = (width, off // width)
        mats.append(w)
        off += width
    pad = -off % tile
    if pad:
        mats.append(jnp.zeros((mats[0].shape[0], pad), mats[0].dtype))
    return jnp.concatenate(mats, axis=1).astype(BF16), col


def _pad_cols(w, width):
    return jnp.pad(w, ((0, 0), (0, width - w.shape[1])))


def _ab_layout(w_in):
    hk = GLA_HEADS * GLA_DK
    hv = GLA_HEADS * GLA_DV
    kv = NSA_GROUPS * NSA_DH
    sizes = [hk, hk, hv, hv, GLA_RANK, NSA_HEADS * NSA_DH] + [kv] * 6 + [3 * NSA_HEADS]
    cuts = np.cumsum(sizes)[:-1].tolist()
    gq, gk, gv, gr, gz, nq, kc, vc, ks, vs, kw, vw, gt = jnp.split(w_in, cuts, axis=1)
    gates = _pad_cols(jnp.concatenate([gz, gt], axis=1), LANES)
    return _layout([("gla_q", gq), ("gla_k", gk), ("gla_v", gv), ("gla_r", gr), ("nsa_q", nq),
                    ("nsa_kc", kc), ("nsa_vc", vc), ("nsa_ks", ks), ("nsa_vs", vs), ("nsa_kw", kw),
                    ("nsa_vw", vw), ("gates", gates)], AB_TILE)


def _ml_layout(w_in):
    hq = ML_HEADS * ML_DQK
    hv = ML_HEADS * ML_DV
    wide = 2 * hq + 2 * hv
    col = {"ml_q": (hq, 0), "ml_k": (hq, 1), "ml_v": (hv, 1), "ml_og": (hv, 2), "gates": (LANES, wide // LANES)}
    total = -(-(wide + LANES) // ML_TILE) * ML_TILE
    return jnp.pad(w_in, ((0, 0), (0, total - w_in.shape[1]))).astype(BF16), col


def kernel(x, c, norm_mix_g, mod_mix_w, mod_mix_b, norm_mlp_g, mod_mlp_w, mod_mlp_b, mlp_w1, mlp_w2, ab_w_in, ab_w_out, gla_w_gate, gla_b_gate, gla_norm_g, nsa_q_norm_g, nsa_k_norm_g, nsa_cmp_pos_k, nsa_cmp_pos_v, nsa_cmp_k_w1, nsa_cmp_k_w2, nsa_cmp_v_w1, nsa_cmp_v_w2, ml_w_in, ml_w_out, ml_conv_w, ml_b_i, ml_b_f, ml_norm_g):
    bsz, s, d = x.shape
    depth = norm_mix_g.shape[0]
    mod_mix = _ada_mod(c, mod_mix_w, mod_mix_b)
    mod_mlp = _ada_mod(c, mod_mlp_w, mod_mlp_b)
    mlp_w1_bf16, mlp_w2_bf16 = mlp_w1.astype(BF16), mlp_w2.astype(BF16)

    def split_mod(m):
        return [m[:, None, i * d:(i + 1) * d] for i in range(3)]

    for layer in range(depth):
        shift, scale, gate = split_mod(mod_mix[layer])
        if layer % 2 == 0:
            e = layer // 2
            w_in, col = _ab_layout(ab_w_in[e])
            p = _norm_proj(x, norm_mix_g[layer], scale, shift, w_in, tm=PROJ_ROWS, tn=AB_TILE)
            w_gate = jnp.pad(gla_w_gate[e], ((0, LANES - GLA_RANK), (0, 0))).astype(BF16)
            o_gla = _gla(p, w_gate, gla_b_gate[e], gla_norm_g[e], col)
            q_gain = (jnp.tile(nsa_q_norm_g[e], NSA_HEADS) * (NSA_DH ** -0.5 * LOG2E)).reshape(-1, 1)
            k_gain = jnp.tile(nsa_k_norm_g[e], NSA_GROUPS).reshape(1, -1)
            qnt, ksg, vst, kwg, vwt = _nsa_prep(p, q_gain, k_gain, col)
            kcg, vct = _compress(p, nsa_cmp_pos_k[e], nsa_cmp_pos_v[e], nsa_cmp_k_w1[e],
                                 nsa_cmp_k_w2[e], nsa_cmp_v_w1[e], nsa_cmp_v_w2[e], k_gain, col)
            o_nsa = _nsa_attention(p, qnt, kcg, vct, ksg, vst, kwg, vwt, col)
            w_out = ab_w_out[e].astype(BF16)
            half = GLA_HEADS * GLA_DV
            x = _out_proj([o_gla, o_nsa], [w_out[:half], w_out[half:]], x, gate)
        else:
            o = layer // 2
            w_in, col = _ml_layout(ml_w_in[o])
            p = _norm_proj(x, norm_mix_g[layer], scale, shift, w_in, tm=PROJ_ROWS, tn=ML_TILE)
            bias = jnp.concatenate([ml_b_i[o], ml_b_f[o]])
            bias_col = _pad_cols(bias.reshape(1, -1), LANES)
            bias_row = jnp.broadcast_to(bias.reshape(-1, 1), (2 * ML_HEADS, LANES))
            hh = _mlstm(p, ml_conv_w[o], bias_col, bias_row, ml_norm_g[o], col)
            x = _out_proj([hh], [ml_w_out[o].astype(BF16)], x, gate)
        shift, scale, gate = split_mod(mod_mlp[layer])
        x = _mlp_sublayer(x, norm_mlp_g[layer], scale, shift, gate, mlp_w1_bf16, mlp_w2_bf16, layer)
    return x
```

```python
import functools

import numpy as np
import jax
import jax.numpy as jnp
from jax import lax
from jax.experimental import pallas as pl
from jax.experimental.pallas import tpu as pltpu

F32 = jnp.float32
BF16 = jnp.bfloat16
HI = lax.Precision.HIGHEST

EPS = 1e-6
NEG = -1e30
LOG2E = 1.4426950408889634

VMEM_LIMIT_BYTES = 56 * 1024 * 1024
LANES = 128

PROJ_ROWS = 1024
AB_TILE = 1536
ML_TILE = 1280
MLP_ROWS = 512
MLP_HIDDEN_TILE = 2048
MLP_VMEM_LIMIT_BYTES = 60 * 1024 * 1024
OUT_ROWS = 512
MOD_TILE = 768
PREP_ROWS = 512

GLA_HEADS = 4
GLA_DK = 128
GLA_DV = 256
GLA_RANK = 16
GLA_GATE_TAU = 16.0
CHUNK = 64
ML_CHUNK = 128
REC_BLOCK = 512

NSA_HEADS = 16
NSA_GROUPS = 4
NSA_HPG = 4
NSA_DH = 64
CMP_LEN = 32
CMP_STRIDE = 16
SEL_BLOCK = 64
N_SELECT = 16
WINDOW = 512
NSA_TQ = 256
NSA_TK = 256
NSA_VPAD = 80

ML_HEADS = 4
ML_DQK = 256
ML_DV = 512
ML_CONV = 4


def _cparams(sem, vmem_limit_bytes=VMEM_LIMIT_BYTES):
    return pltpu.CompilerParams(dimension_semantics=sem, vmem_limit_bytes=vmem_limit_bytes)


def _sigmoid(x):
    return 1.0 / (1.0 + jnp.exp(-x))


def _silu(x):
    return x * _sigmoid(x)


def _log_sigmoid(x):
    return jnp.minimum(x, 0.0) - jnp.log(1.0 + jnp.exp(-jnp.abs(x)))


def _dot(a, b, precision=None):
    return jnp.dot(a, b, preferred_element_type=F32, precision=precision)


def _dot_nt(a, b, precision=None):
    return lax.dot_general(a, b, (((1,), (1,)), ((), ())), preferred_element_type=F32, precision=precision)


def _split_bf16(x, parts=2):
    out = []
    for _ in range(parts - 1):
        piece = x.astype(BF16)
        out.append(piece)
        x = x - piece.astype(F32)
    return out + [x.astype(BF16)]


def _rms(x):
    return x * lax.rsqrt(jnp.mean(x * x, axis=-1, keepdims=True) + EPS)


def _mod_kernel(c_ref, w_ref, b_ref, o_ref):
    a = _silu(c_ref[...]).astype(BF16)
    o_ref[...] = _dot(a, w_ref[...].astype(BF16)) + b_ref[...]


def _ada_mod(c, w, b):
    nl, d, d3 = w.shape
    bsz = c.shape[0]
    tn = MOD_TILE
    return pl.pallas_call(
        _mod_kernel,
        grid=(nl, d3 // tn),
        in_specs=[pl.BlockSpec((bsz, d), lambda l, j: (0, 0)),
                  pl.BlockSpec((None, d, tn), lambda l, j: (l, 0, j)),
                  pl.BlockSpec((None, 1, tn), lambda l, j: (l, 0, j))],
        out_specs=pl.BlockSpec((None, bsz, tn), lambda l, j: (l, 0, j)),
        out_shape=jax.ShapeDtypeStruct((nl, bsz, d3), F32),
        compiler_params=_cparams(("parallel", "parallel")),
        name="ada_mod",
    )(c, w, b.reshape(nl, 1, d3))


def _modulated_norm(x_ref, g_ref, sc_ref, sh_ref):
    x = x_ref[...]
    gain = g_ref[...] * (1.0 + sc_ref[...])
    inv = lax.rsqrt(jnp.mean(x * x, axis=-1, keepdims=True) + EPS)
    return ((x * inv) * gain + sh_ref[...]).astype(BF16)


def _proj_kernel(x_ref, g_ref, sc_ref, sh_ref, w_ref, o_ref, h_ref):
    i = pl.program_id(2)
    half = x_ref.shape[0] // 2

    @pl.when(pl.program_id(1) == 0)
    def _():
        for r in range(2):
            rows = slice(r * half, (r + 1) * half)
            h_ref[i, rows, :] = _modulated_norm(x_ref.at[rows, :], g_ref, sc_ref, sh_ref)
            o_ref[rows, :] = _dot(h_ref[i, rows, :], w_ref[...])

    @pl.when(pl.program_id(1) > 0)
    def _():
        o_ref[...] = _dot(h_ref[i], w_ref[...])


def _norm_proj(x, g, scale, shift, w, tm, tn):
    bsz, s, d = x.shape
    n = w.shape[1]
    rows = s // tm
    x_spec = pl.BlockSpec((None, tm, d), lambda b, j, i: (b, jnp.where(j == 0, i, rows - 1), 0))
    return pl.pallas_call(
        _proj_kernel,
        grid=(bsz, n // tn, rows),
        in_specs=[x_spec,
                  pl.BlockSpec((1, d), lambda b, j, i: (0, 0)),
                  pl.BlockSpec((None, 1, d), lambda b, j, i: (b, 0, 0)),
                  pl.BlockSpec((None, 1, d), lambda b, j, i: (b, 0, 0)),
                  pl.BlockSpec((d, tn), lambda b, j, i: (0, j))],
        out_specs=pl.BlockSpec((None, tm, tn), lambda b, j, i: (b, i, j)),
        out_shape=jax.ShapeDtypeStruct((bsz, s, n), F32),
        scratch_shapes=[pltpu.VMEM((rows, tm, d), BF16)],
        compiler_params=_cparams(("parallel", "arbitrary", "arbitrary")),
        name="norm_proj",
    )(x, g.reshape(1, d), scale, shift, w)


def _mlp_kernel(x_ref, g_ref, sc_ref, sh_ref, gate_ref, w1_ref, w2_ref, o_ref, h_ref):
    sub = w1_ref.shape[1] // 2

    def accumulate(rows):
        for kk in range(2):
            cols = slice(kk * sub, (kk + 1) * sub)
            u = jnp.maximum(_dot(h_ref[rows, :], w1_ref[:, cols]), 0.0)
            o_ref[rows, :] += gate_ref[...] * _dot((u * u).astype(BF16), w2_ref[cols, :])

    @pl.when(pl.program_id(2) == 0)
    def _():
        half = x_ref.shape[0] // 2
        for r in range(2):
            rows = slice(r * half, (r + 1) * half)
            h_ref[rows, :] = _modulated_norm(x_ref.at[rows, :], g_ref, sc_ref, sh_ref)
            o_ref[rows, :] = x_ref[rows, :]
            accumulate(rows)

    @pl.when(pl.program_id(2) > 0)
    def _():
        accumulate(slice(None))


def _mlp_sublayer(x, g, scale, shift, gate, w1, w2, layer, tm=MLP_ROWS, tf=MLP_HIDDEN_TILE):
    bsz, s, d = x.shape
    ff = w1.shape[2]
    vec = pl.BlockSpec((None, 1, d), lambda b, i, f: (b, 0, 0))
    return pl.pallas_call(
        _mlp_kernel,
        grid=(bsz, s // tm, ff // tf),
        in_specs=[pl.BlockSpec((None, tm, d), lambda b, i, f: (b, i, 0)),
                  pl.BlockSpec((1, d), lambda b, i, f: (0, 0)),
                  vec, vec, vec,
                  pl.BlockSpec((None, d, tf), lambda b, i, f: (layer, 0, f)),
                  pl.BlockSpec((None, tf, d), lambda b, i, f: (layer, f, 0))],
        out_specs=pl.BlockSpec((None, tm, d), lambda b, i, f: (b, i, 0)),
        out_shape=jax.ShapeDtypeStruct((bsz, s, d), F32),
        scratch_shapes=[pltpu.VMEM((tm, d), BF16)],
        compiler_params=_cparams(("parallel", "parallel", "arbitrary"), MLP_VMEM_LIMIT_BYTES),
        name="mlp_sublayer",
    )(x, g.reshape(1, d), scale, shift, gate, w1, w2)


def _outproj_kernel(n_in, *refs):
    a_refs, w_refs = refs[:n_in], refs[n_in:2 * n_in]
    x_ref, gate_ref, o_ref = refs[2 * n_in:]
    y = _dot(a_refs[0][...], w_refs[0][...])
    for a_ref, w_ref in zip(a_refs[1:], w_refs[1:]):
        y += _dot(a_ref[...], w_ref[...])
    o_ref[...] = x_ref[...] + gate_ref[...] * y


def _out_proj(acts, ws, x, gate, tm=OUT_ROWS):
    bsz, s, d = x.shape
    n_in = len(acts)
    in_specs = [pl.BlockSpec((None, tm, a.shape[-1]), lambda b, i: (b, i, 0)) for a in acts]
    in_specs += [pl.BlockSpec(w.shape, lambda b, i: (0, 0)) for w in ws]
    in_specs += [pl.BlockSpec((None, tm, d), lambda b, i: (b, i, 0)),
                 pl.BlockSpec((None, 1, d), lambda b, i: (b, 0, 0))]
    return pl.pallas_call(
        functools.partial(_outproj_kernel, n_in),
        grid=(bsz, s // tm),
        in_specs=in_specs,
        out_specs=pl.BlockSpec((None, tm, d), lambda b, i: (b, i, 0)),
        out_shape=jax.ShapeDtypeStruct((bsz, s, d), F32),
        compiler_params=_cparams(("parallel", "parallel")),
        name="out_proj",
    )(*acts, *ws, x, gate)


def _tri(n, upper=False):
    r = lax.broadcasted_iota(jnp.int32, (n, n), 0)
    c = lax.broadcasted_iota(jnp.int32, (n, n), 1)
    return (c >= r) if upper else (c <= r)


def _gla_kernel(q_ref, k_ref, v_ref, r_ref, zg_ref, wg_ref, bg_ref, ng_ref, o_ref, st_ref):
    @pl.when(pl.program_id(1) == 0)
    def _():
        st_ref[...] = jnp.zeros_like(st_ref)

    tril = _tri(CHUNK)
    tril_b = tril.astype(BF16)
    pre = _dot(zg_ref[...].astype(BF16), wg_ref[...]) + bg_ref[...]
    parts = _split_bf16(_log_sigmoid(pre) / GLA_GATE_TAU, 3)
    chunks = [slice(c * CHUNK, (c + 1) * CHUNK) for c in range(REC_BLOCK // CHUNK)]
    cum_all = jnp.concatenate([sum(_dot(tril_b, part[rows, :]) for part in parts) for rows in chunks], axis=0)
    q_dec_all = (q_ref[...] * GLA_DK ** -0.5) * jnp.exp(cum_all)
    k_dec_all = k_ref[...] * jnp.exp(-cum_all)
    qb_all = q_dec_all.astype(BF16)
    kb_all = k_dec_all.astype(BF16)
    vb_all = v_ref[...].astype(BF16)
    gate_all = _silu(r_ref[...])
    intra = {}
    for c, rows in enumerate(chunks):
        for h in range(GLA_HEADS):
            ks = slice(h * GLA_DK, (h + 1) * GLA_DK)
            vs = slice(h * GLA_DV, (h + 1) * GLA_DV)
            att = jnp.where(tril, _dot_nt(qb_all[rows, ks], kb_all[rows, ks]), 0.0)
            intra[c, h] = _dot(att.astype(BF16), vb_all[rows, vs])
    for c, rows in enumerate(chunks):
        cum = cum_all[rows, :]
        total = cum[CHUNK - 1:CHUNK, :]
        k_end = (k_ref[rows, :] * jnp.exp(total - cum)).astype(BF16)
        decay = jnp.exp(total)
        for h in range(GLA_HEADS):
            ks = slice(h * GLA_DK, (h + 1) * GLA_DK)
            vs = slice(h * GLA_DV, (h + 1) * GLA_DV)
            st = st_ref[h]
            o = intra[c, h] + _dot_nt(qb_all[rows, ks], st.astype(BF16))
            st_ref[h] = st * decay[:, ks] + _dot(v_ref[rows, vs].T.astype(BF16), k_end[:, ks])
            o_ref[rows, vs] = ((_rms(o) * ng_ref[...]) * gate_all[rows, vs]).astype(o_ref.dtype)


def _gla(p, w_gate, b_gate, norm_g, col):
    bsz, s, _ = p.shape
    t = REC_BLOCK

    def piece(name):
        width, idx = col[name]
        return pl.BlockSpec((None, t, width), lambda b, i, idx=idx: (b, i, idx))

    const = lambda shape: pl.BlockSpec(shape, lambda b, i: (0,) * len(shape))
    hk = GLA_HEADS * GLA_DK
    return pl.pallas_call(
        _gla_kernel,
        grid=(bsz, s // t),
        in_specs=[piece("gla_q"), piece("gla_k"), piece("gla_v"), piece("gla_r"), piece("gates"),
                  const((LANES, hk)), const((1, hk)), const((1, GLA_DV))],
        out_specs=pl.BlockSpec((None, t, GLA_HEADS * GLA_DV), lambda b, i: (b, i, 0)),
        out_shape=jax.ShapeDtypeStruct((bsz, s, GLA_HEADS * GLA_DV), BF16),
        scratch_shapes=[pltpu.VMEM((GLA_HEADS, GLA_DV, GLA_DK), F32)],
        compiler_params=_cparams(("parallel", "arbitrary")),
        name="gla",
    )(p, p, p, p, p, w_gate, b_gate.reshape(1, hk), norm_g.reshape(1, GLA_DV))


def _mlstm_kernel(q_ref, k_ref, v_ref, og_ref, gc_ref, wq_ref, wk_ref, bc_ref, br_ref, ng_ref,
                  o_ref, c_ref, n_ref, m_ref, qx_ref, kx_ref):
    t = REC_BLOCK

    @pl.when(pl.program_id(1) == 0)
    def _():
        c_ref[...] = jnp.zeros_like(c_ref)
        n_ref[...] = jnp.zeros_like(n_ref)
        m_ref[...] = jnp.zeros_like(m_ref)
        qx_ref[0:8, :] = jnp.zeros((8, qx_ref.shape[1]), F32)
        kx_ref[0:8, :] = jnp.zeros((8, kx_ref.shape[1]), F32)

    def conv_silu(x_ref, xx_ref, w_ref):
        xx_ref[8:8 + t, :] = x_ref[...]
        y = xx_ref[8:8 + t, :] * w_ref[ML_CONV - 1:ML_CONV, :]
        for back in range(1, ML_CONV):
            y += xx_ref[8 - back:8 - back + t, :] * w_ref[ML_CONV - 1 - back:ML_CONV - back, :]
        xx_ref[0:8, :] = xx_ref[t:t + 8, :]
        return _silu(y)

    q_all = conv_silu(q_ref, qx_ref, wq_ref) * ML_DQK ** -0.5
    k_all = conv_silu(k_ref, kx_ref, wk_ref)

    tril = _tri(ML_CHUNK)
    tril_b = tril.astype(BF16)
    triu_b = _tri(ML_CHUNK, upper=True).astype(BF16)
    lane = lax.broadcasted_iota(jnp.int32, (ML_CHUNK, LANES), 1)
    is_f_col = (lane >= ML_HEADS) & (lane < 2 * ML_HEADS)
    sub = lax.broadcasted_iota(jnp.int32, (8, ML_CHUNK), 0)
    is_f_row = sub >= ML_HEADS
    gates_by_row = gc_ref[...].T[0:2 * ML_HEADS, :]

    for c in range(t // ML_CHUNK):
        rows = slice(c * ML_CHUNK, (c + 1) * ML_CHUNK)
        gcol = gc_ref[rows, :] + bc_ref[...]
        gcol = jnp.where(is_f_col, _log_sigmoid(gcol), gcol)
        bcol = sum(_dot(tril_b, part) for part in _split_bf16(gcol, 3))
        grow = gates_by_row[:, rows] + br_ref[:, 0:1]
        grow = jnp.where(is_f_row, _log_sigmoid(grow), grow)
        brow = sum(_dot(part, triu_b) for part in _split_bf16(grow, 3))
        for h in range(ML_HEADS):
            qs = slice(h * ML_DQK, (h + 1) * ML_DQK)
            vs = slice(h * ML_DV, (h + 1) * ML_DV)
            b_col = bcol[:, ML_HEADS + h:ML_HEADS + h + 1]
            b_row = brow[ML_HEADS + h:ML_HEADS + h + 1, :]
            li_col = gcol[:, h:h + 1]
            li_row = grow[h:h + 1, :]
            m_prev = m_ref[h:h + 1, 0:1]
            log_d = jnp.where(tril, b_col - b_row + li_row, -jnp.inf)
            log_inter = b_col + m_prev
            m_comb = jnp.maximum(log_inter, jnp.max(log_d, axis=-1, keepdims=True))
            d = jnp.exp(log_d - m_comb)
            w_inter = jnp.exp(log_inter - m_comb)
            qh = q_all[rows, qs]
            kh = k_all[rows, qs]
            qb = qh.astype(BF16)
            vb = v_ref[rows, vs].astype(BF16)
            sc = _dot_nt(qb, kh.astype(BF16)) * d
            cm = c_ref[h]
            nv = n_ref[h]
            num = _dot(sc.astype(BF16), vb) + w_inter * _dot(qb, cm.astype(BF16))
            den = jnp.sum(sc, axis=-1, keepdims=True) + w_inter * jnp.sum(qh * nv, axis=-1, keepdims=True)
            hh = num / jnp.maximum(jnp.abs(den), jnp.exp(-m_comb))
            total = b_col[ML_CHUNK - 1:ML_CHUNK, :]
            log_w = total - b_col + li_col
            m_new = jnp.maximum(total + m_prev, jnp.max(log_w, axis=0, keepdims=True))
            decay = jnp.exp(total + m_prev - m_new)
            kw = kh * jnp.exp(log_w - m_new)
            c_ref[h] = decay * cm + _dot(kw.T.astype(BF16), vb)
            n_ref[h] = decay * nv + jnp.sum(kw, axis=0, keepdims=True)
            m_ref[h:h + 1, :] = jnp.broadcast_to(m_new, (1, LANES))
            og = og_ref[rows, vs]
            o_ref[rows, vs] = ((_rms(hh) * ng_ref[...]) * _sigmoid(og)).astype(o_ref.dtype)


def _mlstm(p, conv_w, bias_col, bias_row, norm_g, col):
    bsz, s, _ = p.shape
    t = REC_BLOCK
    hq = ML_HEADS * ML_DQK
    hv = ML_HEADS * ML_DV

    def piece(name):
        width, idx = col[name]
        return pl.BlockSpec((None, t, width), lambda b, i, idx=idx: (b, i, idx))

    const = lambda shape: pl.BlockSpec(shape, lambda b, i: (0,) * len(shape))
    return pl.pallas_call(
        _mlstm_kernel,
        grid=(bsz, s // t),
        in_specs=[piece("ml_q"), piece("ml_k"), piece("ml_v"), piece("ml_og"), piece("gates"),
                  pl.BlockSpec((ML_CONV, hq), lambda b, i: (0, 0)),
                  pl.BlockSpec((ML_CONV, hq), lambda b, i: (0, 1)),
                  const((1, LANES)), const((8, LANES)), const((1, ML_DV))],
        out_specs=pl.BlockSpec((None, t, hv), lambda b, i: (b, i, 0)),
        out_shape=jax.ShapeDtypeStruct((bsz, s, hv), BF16),
        scratch_shapes=[pltpu.VMEM((ML_HEADS, ML_DQK, ML_DV), F32),
                        pltpu.VMEM((ML_HEADS, 1, ML_DQK), F32),
                        pltpu.VMEM((8, LANES), F32),
                        pltpu.VMEM((t + 8, hq), F32),
                        pltpu.VMEM((t + 8, hq), F32)],
        compiler_params=_cparams(("parallel", "arbitrary")),
        name="mlstm",
    )(p, p, p, p, p, conv_w, conv_w, bias_col, bias_row, norm_g.reshape(1, ML_DV))


def _seg_norm(x, seg_ref, segt_ref):
    if seg_ref.dtype == BF16:
        seg_sum = lambda v, m_ref: sum(_dot(part, m_ref[...]) for part in _split_bf16(v))
    else:
        seg_sum = lambda v, m_ref: _dot(v, m_ref[...], precision=HI)
    ss = seg_sum(x * x, seg_ref) * (1.0 / NSA_DH)
    return x * seg_sum(lax.rsqrt(ss + EPS), segt_ref)


def _nsa_prep_kernel(q_ref, ks_ref, vs_ref, kw_ref, vw_ref, qg_ref, kg_ref, segk_ref, segkt_ref,
                     qnt_ref, ksg_ref, vst_ref, kwg_ref, vwt_ref):
    t = q_ref.shape[0]
    qt = q_ref[...].T
    for h in range(NSA_HEADS):
        hs = slice(h * NSA_DH, (h + 1) * NSA_DH)
        blk = qt[hs, :]
        inv = lax.rsqrt(jnp.mean(blk * blk, axis=0, keepdims=True) + EPS)
        qnt_ref[hs, :] = ((blk * inv) * qg_ref[hs, :]).astype(BF16)
    ones = jnp.ones((NSA_VPAD - NSA_DH, t), BF16)
    for k_ref, v_ref, kg_out, vt_out in ((ks_ref, vs_ref, ksg_ref, vst_ref), (kw_ref, vw_ref, kwg_ref, vwt_ref)):
        kn = (_seg_norm(k_ref[...], segk_ref, segkt_ref) * kg_ref[...]).astype(BF16)
        vt = v_ref[...].T.astype(BF16)
        for g in range(NSA_GROUPS):
            gs = slice(g * NSA_DH, (g + 1) * NSA_DH)
            kg_out[g] = kn[:, gs]
            vt_out[g, 0:NSA_DH, :] = vt[gs, :]
            vt_out[g, NSA_DH:NSA_VPAD, :] = ones


def _seg_matrices(width, dtype):
    seg = (np.arange(width)[:, None] // NSA_DH == np.arange(LANES)[None, :]).astype(np.float32)
    return jnp.asarray(seg, dtype=dtype), jnp.asarray(seg.T, dtype=dtype)


def _nsa_prep(p, q_gain, k_gain, col, t=PREP_ROWS):
    bsz, s, _ = p.shape
    hq = NSA_HEADS * NSA_DH
    kv = NSA_GROUPS * NSA_DH

    def piece(name):
        width, idx = col[name]
        return pl.BlockSpec((None, t, width), lambda b, i, idx=idx: (b, i, idx))

    const = lambda shape: pl.BlockSpec(shape, lambda b, i: (0,) * len(shape))
    segk, segkt = _seg_matrices(kv, BF16)
    keys = pl.BlockSpec((None, NSA_GROUPS, t, NSA_DH), lambda b, i: (b, 0, i, 0))
    vals = pl.BlockSpec((None, NSA_GROUPS, NSA_VPAD, t), lambda b, i: (b, 0, 0, i))
    shp = lambda *dims, dt=BF16: jax.ShapeDtypeStruct(dims, dt)
    keys_shape = shp(bsz, NSA_GROUPS, s, NSA_DH)
    vals_shape = shp(bsz, NSA_GROUPS, NSA_VPAD, s)
    return pl.pallas_call(
        _nsa_prep_kernel,
        grid=(bsz, s // t),
        in_specs=[piece("nsa_q"), piece("nsa_ks"), piece("nsa_vs"), piece("nsa_kw"), piece("nsa_vw"),
                  const((hq, 1)), const((1, kv)), const((kv, LANES)), const((LANES, kv))],
        out_specs=[pl.BlockSpec((None, hq, t), lambda b, i: (b, 0, i)), keys, vals, keys, vals],
        out_shape=[shp(bsz, hq, s), keys_shape, vals_shape, keys_shape, vals_shape],
        compiler_params=_cparams(("parallel", "parallel")),
        name="nsa_prep",
    )(p, p, p, p, p, q_gain, k_gain, segk, segkt)


def _compress_kernel(kc0_ref, kc1_ref, vc0_ref, vc1_ref, posk_ref, posv_ref, kw1a_ref, kw1b_ref, kw2_ref,
                     vw1a_ref, vw1b_ref, vw2_ref, kg_ref, segk_ref, segkt_ref, kcg_ref, vct_ref):
    def hidden(x_refs, pos_ref, w1a_ref, w1b_ref):
        nsub = x_refs[0].shape[0] // CMP_STRIDE
        first = second = None
        for l in range(CMP_STRIDE):
            x = jnp.concatenate([r[pl.ds(l, nsub, stride=CMP_STRIDE), :] for r in x_refs], axis=1)
            a = _dot((x + pos_ref[l:l + 1, :]).astype(BF16), w1a_ref[l])
            b = _dot((x + pos_ref[CMP_STRIDE + l:CMP_STRIDE + l + 1, :]).astype(BF16), w1b_ref[l])
            first = a if first is None else first + a
            second = b if second is None else second + b
        row = lax.broadcasted_iota(jnp.int32, second.shape, 0)
        nxt = jnp.where(row == nsub - 1, 0.0, pltpu.roll(second, nsub - 1, 0))
        return _silu(first + nxt)

    kc = _dot(hidden((kc0_ref, kc1_ref), posk_ref, kw1a_ref, kw1b_ref).astype(BF16), kw2_ref[...])
    kn = (_seg_norm(kc, segk_ref, segkt_ref) * kg_ref[...]).astype(BF16)
    vt = _dot(vw2_ref[...], hidden((vc0_ref, vc1_ref), posv_ref, vw1a_ref, vw1b_ref).T.astype(BF16)).astype(BF16)
    for g in range(NSA_GROUPS):
        gs = slice(g * NSA_DH, (g + 1) * NSA_DH)
        kcg_ref[g] = kn[:, gs]
        vct_ref[g] = vt[gs, :]


def _expand_w1(w1):
    w = w1.reshape(2, CMP_STRIDE, NSA_DH, NSA_DH)
    eye = jnp.eye(NSA_GROUPS, dtype=w1.dtype)
    big = jnp.einsum('hlde,gk->hlgdke', w, eye)
    big = big.reshape(2, CMP_STRIDE, NSA_GROUPS * NSA_DH, NSA_GROUPS * NSA_DH).astype(BF16)
    return big[0], big[1]


def _expand_pos(pos):
    return jnp.tile(pos, (1, NSA_GROUPS))


def _block_diag(w2):
    return jnp.kron(jnp.eye(NSA_GROUPS, dtype=w2.dtype), w2).astype(BF16)


def _compress(p, pos_k, pos_v, ck_w1, ck_w2, cv_w1, cv_w2, k_gain, col):
    bsz, s, _ = p.shape
    kv = NSA_GROUPS * NSA_DH
    nsub = s // CMP_STRIDE
    segk, segkt = _seg_matrices(kv, F32)
    kw1a, kw1b = _expand_w1(ck_w1)
    vw1a, vw1b = _expand_w1(cv_w1)
    const = lambda shape: pl.BlockSpec(shape, lambda b: (0,) * len(shape))

    def halves(name):
        width, idx = col[name]
        assert width == 2 * LANES
        return [pl.BlockSpec((None, s, LANES), lambda b, j=2 * idx + h: (b, 0, j)) for h in range(2)]

    w1_spec = const((CMP_STRIDE, kv, kv))
    return pl.pallas_call(
        _compress_kernel,
        grid=(bsz,),
        in_specs=[*halves("nsa_kc"), *halves("nsa_vc"), const((CMP_LEN, kv)), const((CMP_LEN, kv)),
                  w1_spec, w1_spec, const((kv, kv)),
                  w1_spec, w1_spec, const((kv, kv)),
                  const((1, kv)), const((kv, LANES)), const((LANES, kv))],
        out_specs=[pl.BlockSpec((None, NSA_GROUPS, nsub, NSA_DH), lambda b: (b, 0, 0, 0)),
                   pl.BlockSpec((None, NSA_GROUPS, NSA_DH, nsub), lambda b: (b, 0, 0, 0))],
        out_shape=[jax.ShapeDtypeStruct((bsz, NSA_GROUPS, nsub, NSA_DH), BF16),
                   jax.ShapeDtypeStruct((bsz, NSA_GROUPS, NSA_DH, nsub), BF16)],
        compiler_params=_cparams(("parallel",)),
        name="nsa_compress",
    )(p, p, p, p, _expand_pos(pos_k), _expand_pos(pos_v),
      kw1a, kw1b, _block_diag(ck_w2), vw1a, vw1b, _block_diag(cv_w2).T, k_gain, segk, segkt)


def _nsa_attn_kernel(qt_ref, gt_ref, kcg_ref, vct_ref, ksg_ref, vst_ref, kwg_ref, vwt_ref, ovt_ref,
                     o_ref, acc_ref, sel_ref, s_ref, p_ref, pv_ref):
    tq, tk = NSA_TQ, NSA_TK
    q0 = pl.program_id(1) * tq
    ncmp = kcg_ref.shape[1]
    nsel = ovt_ref.shape[0]
    cols4 = NSA_HPG * tq

    t_cmp = q0 + lax.broadcasted_iota(jnp.int32, (ncmp, cols4), 1) % tq
    n_idx = lax.broadcasted_iota(jnp.int32, (ncmp, cols4), 0)
    cmp_valid = n_idx * CMP_STRIDE + (CMP_LEN - 1) <= t_cmp

    j_idx = lax.broadcasted_iota(jnp.int32, (nsel, tq), 0)
    cur = (q0 + lax.broadcasted_iota(jnp.int32, (nsel, tq), 1)) // SEL_BLOCK
    forced = (j_idx == 0) | (j_idx == cur) | (j_idx == cur - 1)
    causal_blk = j_idx <= cur

    row_blk = lax.broadcasted_iota(jnp.int32, (SEL_BLOCK, tq), 0)
    t_row = q0 + lax.broadcasted_iota(jnp.int32, (1, tq), 1)

    gates_t = _sigmoid(gt_ref[...]).T

    def add_gated(g, branch, o4):
        for i in range(NSA_HPG):
            h = g * NSA_HPG + i
            hs = slice(h * NSA_DH, (h + 1) * NSA_DH)
            gc = GLA_RANK + 3 * h + branch
            term = gates_t[gc:gc + 1, :] * o4[:, i * tq:(i + 1) * tq]
            acc_ref[hs, :] = term if branch == 0 else acc_ref[hs, :] + term

    def q_group(g):
        return jnp.concatenate([qt_ref[(g * NSA_HPG + i) * NSA_DH:(g * NSA_HPG + i + 1) * NSA_DH, :]
                                for i in range(NSA_HPG)], axis=1)

    def attend(branch, k_ref, vt_ref, first_tile, n_tiles, mask_fn):
        last = first_tile + n_tiles - 1

        def scores(g, kt):
            return _dot(k_ref[g, pl.ds(pl.multiple_of(kt * tk, tk), tk), :], q_group(g))

        def weighted_values(g, kt, p):
            return _dot(vt_ref[g, :, pl.ds(pl.multiple_of(kt * tk, tk), tk)], p)

        def step(kt, cur, ms):
            nxt = 1 - cur
            out = []
            for g in range(NSA_GROUPS):
                m = ms[g]
                s_ref[g, nxt] = scores(g, jnp.minimum(kt + 1, last))
                acc = pv_ref[g] + weighted_values(g, jnp.maximum(kt - 1, first_tile), p_ref[g, nxt])
                ok = mask_fn(g, kt, kt * tk)
                m_new = []
                for c in range(cols4 // LANES):
                    cs = slice(c * LANES, (c + 1) * LANES)
                    qs = slice((c % (tq // LANES)) * LANES, (c % (tq // LANES) + 1) * LANES)
                    s = jnp.where(ok[:, qs], s_ref[g, cur, :, cs], NEG)
                    m_c = jnp.maximum(m[:, cs], jnp.max(s, axis=0, keepdims=True))
                    p_ref[g, cur, :, cs] = jnp.exp2(s - m_c).astype(BF16)
                    m_new.append(m_c)
                m_new = jnp.concatenate(m_new, axis=1)
                pv_ref[g] = acc * jnp.exp2(m - m_new)
                out.append(m_new)
            return tuple(out)

        for g in range(NSA_GROUPS):
            s_ref[g, 0] = scores(g, first_tile)
            p_ref[g, 1] = jnp.zeros(p_ref.shape[2:], BF16)
        pv_ref[...] = jnp.zeros_like(pv_ref)

        def body(j, ms):
            kt = first_tile + 2 * j
            return step(kt + 1, 1, step(kt, 0, ms))

        ms = lax.fori_loop(0, n_tiles // 2, body, tuple(jnp.full((1, cols4), NEG, F32) for _ in range(NSA_GROUPS)))

        @pl.when(n_tiles % 2 == 1)
        def _():
            step(last, 0, ms)

        for g in range(NSA_GROUPS):
            acc = pv_ref[g] + weighted_values(g, last, p_ref[g, (n_tiles - 1) % 2])
            add_gated(g, branch, acc[0:NSA_DH, :] / acc[NSA_DH:NSA_DH + 1, :])

    last_tile = (q0 + tq - 1) // tk
    last_blk = (q0 + tq - 1) // SEL_BLOCK

    def sel_mask(g, kt, k0):
        per_tile = tk // SEL_BLOCK
        blocks = []
        for j in range(per_tile):
            picked = sel_ref[g, pl.ds(kt * per_tile + j, 1), :] > 0.5
            limit = jnp.where(picked, t_row, -1) - (k0 + j * SEL_BLOCK)
            blocks.append(row_blk <= limit)
        return jnp.concatenate(blocks, axis=0)

    win_span = WINDOW + tk
    win_start = pl.multiple_of(jnp.maximum(last_tile - WINDOW // tk, 0) * tk, tk)
    row_w = lax.broadcasted_iota(jnp.int32, (win_span, tq), 0)
    ahead = t_row - win_start
    win_ok = (row_w <= ahead) & (row_w > ahead - WINDOW)

    def window_branch(g):
        s = _dot(kwg_ref[g, pl.ds(win_start, win_span), :], q_group(g))
        probs = []
        for c in range(cols4 // LANES):
            qs = slice((c % (tq // LANES)) * LANES, (c % (tq // LANES) + 1) * LANES)
            sc = jnp.where(win_ok[:, qs], s[:, c * LANES:(c + 1) * LANES], NEG)
            probs.append(jnp.exp2(sc - jnp.max(sc, axis=0, keepdims=True)).astype(BF16))
        acc = _dot(vwt_ref[g, :, pl.ds(win_start, win_span)], jnp.concatenate(probs, axis=1))
        add_gated(g, 2, acc[0:NSA_DH, :] / acc[NSA_DH:NSA_DH + 1, :])

    importance = []
    for g in range(NSA_GROUPS):
        lg = jnp.where(cmp_valid, _dot(kcg_ref[g], q_group(g)), NEG)
        ex = jnp.exp2(lg - jnp.max(lg, axis=0, keepdims=True))
        p_cmp = jnp.where(cmp_valid, ex / jnp.sum(ex, axis=0, keepdims=True), 0.0)
        add_gated(g, 0, _dot(vct_ref[g], p_cmp.astype(BF16)))
        p_grp = p_cmp[:, 0:tq] + p_cmp[:, tq:2 * tq] + p_cmp[:, 2 * tq:3 * tq] + p_cmp[:, 3 * tq:4 * tq]
        importance.append(sum(_dot(ovt_ref[...], part) for part in _split_bf16(p_grp)))
        window_branch(g)

    @pl.when(last_blk < N_SELECT)
    def _():
        for g in range(NSA_GROUPS):
            sel_ref[g] = causal_blk.astype(F32)

    @pl.when(last_blk >= N_SELECT)
    def _():
        for g in range(NSA_GROUPS):
            val = jnp.where(forced, jnp.inf, jnp.where(causal_blk, importance[g], -jnp.inf))
            rank = jnp.zeros((nsel, tq), F32)
            for i in range(nsel):
                vi = val[i:i + 1, :]
                rank += ((vi > val) | ((vi == val) & (j_idx > i))).astype(F32)
            sel_ref[g] = (rank < N_SELECT).astype(F32)

    attend(1, ksg_ref, vst_ref, 0, last_tile + 1, sel_mask)
    o_ref[...] = acc_ref[...].T.astype(o_ref.dtype)


def _overlap_t(s):
    ncmp = s // CMP_STRIDE
    nsel = s // SEL_BLOCK
    c_start = np.arange(ncmp) * CMP_STRIDE
    s_start = np.arange(nsel) * SEL_BLOCK
    overlap_t = ((c_start[None, :] <= s_start[:, None] + SEL_BLOCK - 1)
                 & (c_start[None, :] + CMP_LEN - 1 >= s_start[:, None])).astype(np.float32)
    overlap_t[:, ncmp - 1] = 0.0
    return jnp.asarray(overlap_t, dtype=BF16)


def _nsa_attention(p, qnt, kcg, vct, ksg, vst, kwg, vwt, col):
    bsz, hq, s = qnt.shape
    ncmp = s // CMP_STRIDE
    nsel = s // SEL_BLOCK
    gw, gidx = col["gates"]
    per_row = lambda shape: pl.BlockSpec((None,) + shape, lambda b, i: (b,) + (0,) * len(shape))
    keys = per_row((NSA_GROUPS, s, NSA_DH))
    vals = per_row((NSA_GROUPS, NSA_VPAD, s))
    return pl.pallas_call(
        _nsa_attn_kernel,
        grid=(bsz, s // NSA_TQ),
        in_specs=[pl.BlockSpec((None, hq, NSA_TQ), lambda b, i: (b, 0, i)),
                  pl.BlockSpec((None, NSA_TQ, gw), lambda b, i: (b, i, gidx)),
                  per_row((NSA_GROUPS, ncmp, NSA_DH)), per_row((NSA_GROUPS, NSA_DH, ncmp)),
                  keys, vals, keys, vals,
                  pl.BlockSpec((nsel, ncmp), lambda b, i: (0, 0))],
        out_specs=pl.BlockSpec((None, NSA_TQ, hq), lambda b, i: (b, i, 0)),
        out_shape=jax.ShapeDtypeStruct((bsz, s, hq), BF16),
        scratch_shapes=[pltpu.VMEM((hq, NSA_TQ), F32), pltpu.VMEM((NSA_GROUPS, nsel, NSA_TQ), F32),
                        pltpu.VMEM((NSA_GROUPS, 2, NSA_TK, NSA_HPG * NSA_TQ), F32),
                        pltpu.VMEM((NSA_GROUPS, 2, NSA_TK, NSA_HPG * NSA_TQ), BF16),
                        pltpu.VMEM((NSA_GROUPS, NSA_VPAD, NSA_HPG * NSA_TQ), F32)],
        compiler_params=_cparams(("parallel", "arbitrary")),
        name="nsa_attention",
    )(qnt, p, kcg, vct, ksg, vst, kwg, vwt, _overlap_t(s))


def _layout(pieces, tile):
    col, off, mats = {}, 0, []
    for name, w in pieces:
        width = w.shape[1]
        assert off % width == 0, (name, off, width)
        col[name] = (width, off // width)
        mats.append(w)
        off += width
    pad = -off % tile
    if pad:
        mats.append(jnp.zeros((mats[0].shape[0], pad), mats[0].dtype))
    return jnp.concatenate(mats, axis=1).astype(BF16), col


def _pad_cols(w, width):
    return jnp.pad(w, ((0, 0), (0, width - w.shape[1])))


def _ab_layout(w_in):
    hk = GLA_HEADS * GLA_DK
    hv = GLA_HEADS * GLA_DV
    kv = NSA_GROUPS * NSA_DH
    sizes = [hk, hk, hv, hv, GLA_RANK, NSA_HEADS * NSA_DH] + [kv] * 6 + [3 * NSA_HEADS]
    cuts = np.cumsum(sizes)[:-1].tolist()
    gq, gk, gv, gr, gz, nq, kc, vc, ks, vs, kw, vw, gt = jnp.split(w_in, cuts, axis=1)
    gates = _pad_cols(jnp.concatenate([gz, gt], axis=1), LANES)
    return _layout([("gla_q", gq), ("gla_k", gk), ("gla_v", gv), ("gla_r", gr), ("nsa_q", nq),
                    ("nsa_kc", kc), ("nsa_vc", vc), ("nsa_ks", ks), ("nsa_vs", vs), ("nsa_kw", kw),
                    ("nsa_vw", vw), ("gates", gates)], AB_TILE)


def _ml_layout(w_in):
    hq = ML_HEADS * ML_DQK
    hv = ML_HEADS * ML_DV
    wide = 2 * hq + 2 * hv
    col = {"ml_q": (hq, 0), "ml_k": (hq, 1), "ml_v": (hv, 1), "ml_og": (hv, 2), "gates": (LANES, wide // LANES)}
    total = -(-(wide + LANES) // ML_TILE) * ML_TILE
    return jnp.pad(w_in, ((0, 0), (0, total - w_in.shape[1]))).astype(BF16), col


def kernel(x, c, norm_mix_g, mod_mix_w, mod_mix_b, norm_mlp_g, mod_mlp_w, mod_mlp_b, mlp_w1, mlp_w2, ab_w_in, ab_w_out, gla_w_gate, gla_b_gate, gla_norm_g, nsa_q_norm_g, nsa_k_norm_g, nsa_cmp_pos_k, nsa_cmp_pos_v, nsa_cmp_k_w1, nsa_cmp_k_w2, nsa_cmp_v_w1, nsa_cmp_v_w2, ml_w_in, ml_w_out, ml_conv_w, ml_b_i, ml_b_f, ml_norm_g):
    bsz, s, d = x.shape
    depth = norm_mix_g.shape[0]
    mod_mix = _ada_mod(c, mod_mix_w, mod_mix_b)
    mod_mlp = _ada_mod(c, mod_mlp_w, mod_mlp_b)
    mlp_w1_bf16, mlp_w2_bf16 = mlp_w1.astype(BF16), mlp_w2.astype(BF16)

    def split_mod(m):
        return [m[:, None, i * d:(i + 1) * d] for i in range(3)]

    for layer in range(depth):
        shift, scale, gate = split_mod(mod_mix[layer])
        if layer % 2 == 0:
            e = layer // 2
            w_in, col = _ab_layout(ab_w_in[e])
            p = _norm_proj(x, norm_mix_g[layer], scale, shift, w_in, tm=PROJ_ROWS, tn=AB_TILE)
            w_gate = jnp.pad(gla_w_gate[e], ((0, LANES - GLA_RANK), (0, 0))).astype(BF16)
            o_gla = _gla(p, w_gate, gla_b_gate[e], gla_norm_g[e], col)
            q_gain = (jnp.tile(nsa_q_norm_g[e], NSA_HEADS) * (NSA_DH ** -0.5 * LOG2E)).reshape(-1, 1)
            k_gain = jnp.tile(nsa_k_norm_g[e], NSA_GROUPS).reshape(1, -1)
            qnt, ksg, vst, kwg, vwt = _nsa_prep(p, q_gain, k_gain, col)
            kcg, vct = _compress(p, nsa_cmp_pos_k[e], nsa_cmp_pos_v[e], nsa_cmp_k_w1[e],
                                 nsa_cmp_k_w2[e], nsa_cmp_v_w1[e], nsa_cmp_v_w2[e], k_gain, col)
            o_nsa = _nsa_attention(p, qnt, kcg, vct, ksg, vst, kwg, vwt, col)
            w_out = ab_w_out[e].astype(BF16)
            half = GLA_HEADS * GLA_DV
            x = _out_proj([o_gla, o_nsa], [w_out[:half], w_out[half:]], x, gate)
        else:
            o = layer // 2
            w_in, col = _ml_layout(ml_w_in[o])
            p = _norm_proj(x, norm_mix_g[layer], scale, shift, w_in, tm=PROJ_ROWS, tn=ML_TILE)
            bias = jnp.concatenate([ml_b_i[o], ml_b_f[o]])
            bias_col = _pad_cols(bias.reshape(1, -1), LANES)
            bias_row = jnp.broadcast_to(bias.reshape(-1, 1), (2 * ML_HEADS, LANES))
            hh = _mlstm(p, ml_conv_w[o], bias_col, bias_row, ml_norm_g[o], col)
            x = _out_proj([hh], [ml_w_out[o].astype(BF16)], x, gate)
        shift, scale, gate = split_mod(mod_mlp[layer])
        x = _mlp_sublayer(x, norm_mlp_g[layer], scale, shift, gate, mlp_w1_bf16, mlp_w2_bf16, layer)
    return x
```

```python
import functools

import numpy as np
import jax
import jax.numpy as jnp
from jax import lax
from jax.experimental import pallas as pl
from jax.experimental.pallas import tpu as pltpu

F32 = jnp.float32
BF16 = jnp.bfloat16
HI = lax.Precision.HIGHEST

EPS = 1e-6
NEG = -1e30
LOG2E = 1.4426950408889634

VMEM_LIMIT_BYTES = 56 * 1024 * 1024
LANES = 128

PROJ_ROWS = 1024
AB_TILE = 1536
ML_TILE = 1280
MLP_ROWS = 512
MLP_HIDDEN_TILE = 2048
WIDE_VMEM_LIMIT_BYTES = 60 * 1024 * 1024
OUT_ROWS = 1024
MOD_TILE = 768
PREP_ROWS = 512

GLA_HEADS = 4
GLA_DK = 128
GLA_DV = 256
GLA_RANK = 16
GLA_GATE_TAU = 16.0
CHUNK = 64
ML_CHUNK = 128
REC_BLOCK = 512

NSA_HEADS = 16
NSA_GROUPS = 4
NSA_HPG = 4
NSA_DH = 64
CMP_LEN = 32
CMP_STRIDE = 16
SEL_BLOCK = 64
N_SELECT = 16
WINDOW = 512
NSA_TQ = 256
NSA_TK = 256
NSA_VPAD = 80

ML_HEADS = 4
ML_DQK = 256
ML_DV = 512
ML_CONV = 4


def _cparams(sem, vmem_limit_bytes=VMEM_LIMIT_BYTES):
    return pltpu.CompilerParams(dimension_semantics=sem, vmem_limit_bytes=vmem_limit_bytes)


def _sigmoid(x):
    return 1.0 / (1.0 + jnp.exp(-x))


def _silu(x):
    return x * _sigmoid(x)


def _log_sigmoid(x):
    return jnp.minimum(x, 0.0) - jnp.log(1.0 + jnp.exp(-jnp.abs(x)))


def _dot(a, b, precision=None):
    return jnp.dot(a, b, preferred_element_type=F32, precision=precision)


def _dot_nt(a, b, precision=None):
    return lax.dot_general(a, b, (((1,), (1,)), ((), ())), preferred_element_type=F32, precision=precision)


def _split_bf16(x, parts=2):
    out = []
    for _ in range(parts - 1):
        piece = x.astype(BF16)
        out.append(piece)
        x = x - piece.astype(F32)
    return out + [x.astype(BF16)]


def _rms(x):
    return x * lax.rsqrt(jnp.mean(x * x, axis=-1, keepdims=True) + EPS)


def _mod_kernel(c_ref, w_ref, b_ref, o_ref):
    a = _silu(c_ref[...]).astype(BF16)
    o_ref[...] = _dot(a, w_ref[...].astype(BF16)) + b_ref[...]


def _ada_mod(c, w, b):
    nl, d, d3 = w.shape
    bsz = c.shape[0]
    tn = MOD_TILE
    return pl.pallas_call(
        _mod_kernel,
        grid=(nl, d3 // tn),
        in_specs=[pl.BlockSpec((bsz, d), lambda l, j: (0, 0)),
                  pl.BlockSpec((None, d, tn), lambda l, j: (l, 0, j)),
                  pl.BlockSpec((None, 1, tn), lambda l, j: (l, 0, j))],
        out_specs=pl.BlockSpec((None, bsz, tn), lambda l, j: (l, 0, j)),
        out_shape=jax.ShapeDtypeStruct((nl, bsz, d3), F32),
        compiler_params=_cparams(("parallel", "parallel")),
        name="ada_mod",
    )(c, w, b.reshape(nl, 1, d3))


def _modulated_norm(x_ref, g_ref, sc_ref, sh_ref):
    x = x_ref[...]
    gain = g_ref[...] * (1.0 + sc_ref[...])
    inv = lax.rsqrt(jnp.mean(x * x, axis=-1, keepdims=True) + EPS)
    return ((x * inv) * gain + sh_ref[...]).astype(BF16)


def _proj_kernel(x_ref, g_ref, sc_ref, sh_ref, w_ref, o_ref, h_ref):
    i = pl.program_id(2)
    half = x_ref.shape[0] // 2

    @pl.when(pl.program_id(1) == 0)
    def _():
        for r in range(2):
            rows = slice(r * half, (r + 1) * half)
            h_ref[i, rows, :] = _modulated_norm(x_ref.at[rows, :], g_ref, sc_ref, sh_ref)
            o_ref[rows, :] = _dot(h_ref[i, rows, :], w_ref[...])

    @pl.when(pl.program_id(1) > 0)
    def _():
        o_ref[...] = _dot(h_ref[i], w_ref[...])


def _norm_proj(x, g, scale, shift, w, tm, tn):
    bsz, s, d = x.shape
    n = w.shape[1]
    rows = s // tm
    x_spec = pl.BlockSpec((None, tm, d), lambda b, j, i: (b, jnp.where(j == 0, i, rows - 1), 0))
    return pl.pallas_call(
        _proj_kernel,
        grid=(bsz, n // tn, rows),
        in_specs=[x_spec,
                  pl.BlockSpec((1, d), lambda b, j, i: (0, 0)),
                  pl.BlockSpec((None, 1, d), lambda b, j, i: (b, 0, 0)),
                  pl.BlockSpec((None, 1, d), lambda b, j, i: (b, 0, 0)),
                  pl.BlockSpec((d, tn), lambda b, j, i: (0, j))],
        out_specs=pl.BlockSpec((None, tm, tn), lambda b, j, i: (b, i, j)),
        out_shape=jax.ShapeDtypeStruct((bsz, s, n), F32),
        scratch_shapes=[pltpu.VMEM((rows, tm, d), BF16)],
        compiler_params=_cparams(("parallel", "arbitrary", "arbitrary")),
        name="norm_proj",
    )(x, g.reshape(1, d), scale, shift, w)


def _mlp_kernel(x_ref, g_ref, sc_ref, sh_ref, gate_ref, w1_ref, w2_ref, o_ref, h_ref):
    sub = w1_ref.shape[1] // 2

    def accumulate(rows):
        for kk in range(2):
            cols = slice(kk * sub, (kk + 1) * sub)
            u = jnp.maximum(_dot(h_ref[rows, :], w1_ref[:, cols]), 0.0)
            o_ref[rows, :] += gate_ref[...] * _dot((u * u).astype(BF16), w2_ref[cols, :])

    @pl.when(pl.program_id(2) == 0)
    def _():
        half = x_ref.shape[0] // 2
        for r in range(2):
            rows = slice(r * half, (r + 1) * half)
            h_ref[rows, :] = _modulated_norm(x_ref.at[rows, :], g_ref, sc_ref, sh_ref)
            o_ref[rows, :] = x_ref[rows, :]
            accumulate(rows)

    @pl.when(pl.program_id(2) > 0)
    def _():
        accumulate(slice(None))


def _mlp_sublayer(x, g, scale, shift, gate, w1, w2, layer, tm=MLP_ROWS, tf=MLP_HIDDEN_TILE):
    bsz, s, d = x.shape
    ff = w1.shape[2]
    vec = pl.BlockSpec((None, 1, d), lambda b, i, f: (b, 0, 0))
    return pl.pallas_call(
        _mlp_kernel,
        grid=(bsz, s // tm, ff // tf),
        in_specs=[pl.BlockSpec((None, tm, d), lambda b, i, f: (b, i, 0)),
                  pl.BlockSpec((1, d), lambda b, i, f: (0, 0)),
                  vec, vec, vec,
                  pl.BlockSpec((None, d, tf), lambda b, i, f: (layer, 0, f)),
                  pl.BlockSpec((None, tf, d), lambda b, i, f: (layer, f, 0))],
        out_specs=pl.BlockSpec((None, tm, d), lambda b, i, f: (b, i, 0)),
        out_shape=jax.ShapeDtypeStruct((bsz, s, d), F32),
        scratch_shapes=[pltpu.VMEM((tm, d), BF16)],
        compiler_params=_cparams(("parallel", "parallel", "arbitrary"), WIDE_VMEM_LIMIT_BYTES),
        name="mlp_sublayer",
    )(x, g.reshape(1, d), scale, shift, gate, w1, w2)


def _outproj_kernel(n_in, *refs):
    a_refs, w_refs = refs[:n_in], refs[n_in:2 * n_in]
    x_ref, gate_ref, o_ref = refs[2 * n_in:]
    y = _dot(a_refs[0][...], w_refs[0][...])
    for a_ref, w_ref in zip(a_refs[1:], w_refs[1:]):
        y += _dot(a_ref[...], w_ref[...])
    o_ref[...] = x_ref[...] + gate_ref[...] * y


def _out_proj(acts, ws, x, gate, tm=OUT_ROWS):
    bsz, s, d = x.shape
    n_in = len(acts)
    in_specs = [pl.BlockSpec((None, tm, a.shape[-1]), lambda b, i: (b, i, 0)) for a in acts]
    in_specs += [pl.BlockSpec(w.shape, lambda b, i: (0, 0)) for w in ws]
    in_specs += [pl.BlockSpec((None, tm, d), lambda b, i: (b, i, 0)),
                 pl.BlockSpec((None, 1, d), lambda b, i: (b, 0, 0))]
    return pl.pallas_call(
        functools.partial(_outproj_kernel, n_in),
        grid=(bsz, s // tm),
        in_specs=in_specs,
        out_specs=pl.BlockSpec((None, tm, d), lambda b, i: (b, i, 0)),
        out_shape=jax.ShapeDtypeStruct((bsz, s, d), F32),
        compiler_params=_cparams(("parallel", "parallel"), WIDE_VMEM_LIMIT_BYTES),
        name="out_proj",
    )(*acts, *ws, x, gate)


def _tri(n, upper=False):
    r = lax.broadcasted_iota(jnp.int32, (n, n), 0)
    c = lax.broadcasted_iota(jnp.int32, (n, n), 1)
    return (c >= r) if upper else (c <= r)


def _gla_kernel(q_ref, k_ref, v_ref, r_ref, zg_ref, wg_ref, bg_ref, ng_ref, o_ref, st_ref):
    @pl.when(pl.program_id(1) == 0)
    def _():
        st_ref[...] = jnp.zeros_like(st_ref)

    tril = _tri(CHUNK)
    tril_b = tril.astype(BF16)
    pre = _dot(zg_ref[...].astype(BF16), wg_ref[...]) + bg_ref[...]
    parts = _split_bf16(_log_sigmoid(pre) / GLA_GATE_TAU, 3)
    chunks = [slice(c * CHUNK, (c + 1) * CHUNK) for c in range(REC_BLOCK // CHUNK)]
    cum_all = jnp.concatenate([sum(_dot(tril_b, part[rows, :]) for part in parts) for rows in chunks], axis=0)
    q_dec_all = (q_ref[...] * GLA_DK ** -0.5) * jnp.exp(cum_all)
    k_dec_all = k_ref[...] * jnp.exp(-cum_all)
    qb_all = q_dec_all.astype(BF16)
    kb_all = k_dec_all.astype(BF16)
    vb_all = v_ref[...].astype(BF16)
    gate_all = _silu(r_ref[...])
    intra = {}
    for c, rows in enumerate(chunks):
        for h in range(GLA_HEADS):
            ks = slice(h * GLA_DK, (h + 1) * GLA_DK)
            vs = slice(h * GLA_DV, (h + 1) * GLA_DV)
            att = jnp.where(tril, _dot_nt(qb_all[rows, ks], kb_all[rows, ks]), 0.0)
            intra[c, h] = _dot(att.astype(BF16), vb_all[rows, vs])
    for c, rows in enumerate(chunks):
        cum = cum_all[rows, :]
        total = cum[CHUNK - 1:CHUNK, :]
        k_end = (k_ref[rows, :] * jnp.exp(total - cum)).astype(BF16)
        decay = jnp.exp(total)
        for h in range(GLA_HEADS):
            ks = slice(h * GLA_DK, (h + 1) * GLA_DK)
            vs = slice(h * GLA_DV, (h + 1) * GLA_DV)
            st = st_ref[h]
            o = intra[c, h] + _dot_nt(qb_all[rows, ks], st.astype(BF16))
            st_ref[h] = st * decay[:, ks] + _dot(v_ref[rows, vs].T.astype(BF16), k_end[:, ks])
            o_ref[rows, vs] = ((_rms(o) * ng_ref[...]) * gate_all[rows, vs]).astype(o_ref.dtype)


def _gla(p, w_gate, b_gate, norm_g, col):
    bsz, s, _ = p.shape
    t = REC_BLOCK

    def piece(name):
        width, idx = col[name]
        return pl.BlockSpec((None, t, width), lambda b, i, idx=idx: (b, i, idx))

    const = lambda shape: pl.BlockSpec(shape, lambda b, i: (0,) * len(shape))
    hk = GLA_HEADS * GLA_DK
    return pl.pallas_call(
        _gla_kernel,
        grid=(bsz, s // t),
        in_specs=[piece("gla_q"), piece("gla_k"), piece("gla_v"), piece("gla_r"), piece("gates"),
                  const((LANES, hk)), const((1, hk)), const((1, GLA_DV))],
        out_specs=pl.BlockSpec((None, t, GLA_HEADS * GLA_DV), lambda b, i: (b, i, 0)),
        out_shape=jax.ShapeDtypeStruct((bsz, s, GLA_HEADS * GLA_DV), BF16),
        scratch_shapes=[pltpu.VMEM((GLA_HEADS, GLA_DV, GLA_DK), F32)],
        compiler_params=_cparams(("parallel", "arbitrary")),
        name="gla",
    )(p, p, p, p, p, w_gate, b_gate.reshape(1, hk), norm_g.reshape(1, GLA_DV))


def _mlstm_kernel(q_ref, k_ref, v_ref, og_ref, gc_ref, wq_ref, wk_ref, bc_ref, br_ref, ng_ref,
                  o_ref, c_ref, n_ref, m_ref, qx_ref, kx_ref):
    t = REC_BLOCK

    @pl.when(pl.program_id(1) == 0)
    def _():
        c_ref[...] = jnp.zeros_like(c_ref)
        n_ref[...] = jnp.zeros_like(n_ref)
        m_ref[...] = jnp.zeros_like(m_ref)
        qx_ref[0:8, :] = jnp.zeros((8, qx_ref.shape[1]), F32)
        kx_ref[0:8, :] = jnp.zeros((8, kx_ref.shape[1]), F32)

    def conv_silu(x_ref, xx_ref, w_ref):
        xx_ref[8:8 + t, :] = x_ref[...]
        y = xx_ref[8:8 + t, :] * w_ref[ML_CONV - 1:ML_CONV, :]
        for back in range(1, ML_CONV):
            y += xx_ref[8 - back:8 - back + t, :] * w_ref[ML_CONV - 1 - back:ML_CONV - back, :]
        xx_ref[0:8, :] = xx_ref[t:t + 8, :]
        return _silu(y)

    q_all = conv_silu(q_ref, qx_ref, wq_ref) * ML_DQK ** -0.5
    k_all = conv_silu(k_ref, kx_ref, wk_ref)

    tril = _tri(ML_CHUNK)
    tril_b = tril.astype(BF16)
    triu_b = _tri(ML_CHUNK, upper=True).astype(BF16)
    lane = lax.broadcasted_iota(jnp.int32, (ML_CHUNK, LANES), 1)
    is_f_col = (lane >= ML_HEADS) & (lane < 2 * ML_HEADS)
    sub = lax.broadcasted_iota(jnp.int32, (8, ML_CHUNK), 0)
    is_f_row = sub >= ML_HEADS
    gates_by_row = gc_ref[...].T[0:2 * ML_HEADS, :]

    for c in range(t // ML_CHUNK):
        rows = slice(c * ML_CHUNK, (c + 1) * ML_CHUNK)
        gcol = gc_ref[rows, :] + bc_ref[...]
        gcol = jnp.where(is_f_col, _log_sigmoid(gcol), gcol)
        bcol = sum(_dot(tril_b, part) for part in _split_bf16(gcol, 3))
        grow = gates_by_row[:, rows] + br_ref[:, 0:1]
        grow = jnp.where(is_f_row, _log_sigmoid(grow), grow)
        brow = sum(_dot(part, triu_b) for part in _split_bf16(grow, 3))
        for h in range(ML_HEADS):
            qs = slice(h * ML_DQK, (h + 1) * ML_DQK)
            vs = slice(h * ML_DV, (h + 1) * ML_DV)
            b_col = bcol[:, ML_HEADS + h:ML_HEADS + h + 1]
            b_row = brow[ML_HEADS + h:ML_HEADS + h + 1, :]
            li_col = gcol[:, h:h + 1]
            li_row = grow[h:h + 1, :]
            m_prev = m_ref[h:h + 1, 0:1]
            log_d = jnp.where(tril, b_col - b_row + li_row, -jnp.inf)
            log_inter = b_col + m_prev
            m_comb = jnp.maximum(log_inter, jnp.max(log_d, axis=-1, keepdims=True))
            d = jnp.exp(log_d - m_comb)
            w_inter = jnp.exp(log_inter - m_comb)
            qh = q_all[rows, qs]
            kh = k_all[rows, qs]
            qb = qh.astype(BF16)
            vb = v_ref[rows, vs].astype(BF16)
            sc = _dot_nt(qb, kh.astype(BF16)) * d
            cm = c_ref[h]
            nv = n_ref[h]
            num = _dot(sc.astype(BF16), vb) + w_inter * _dot(qb, cm.astype(BF16))
            den = jnp.sum(sc, axis=-1, keepdims=True) + w_inter * jnp.sum(qh * nv, axis=-1, keepdims=True)
            hh = num / jnp.maximum(jnp.abs(den), jnp.exp(-m_comb))
            total = b_col[ML_CHUNK - 1:ML_CHUNK, :]
            log_w = total - b_col + li_col
            m_new = jnp.maximum(total + m_prev, jnp.max(log_w, axis=0, keepdims=True))
            decay = jnp.exp(total + m_prev - m_new)
            kw = kh * jnp.exp(log_w - m_new)
            c_ref[h] = decay * cm + _dot(kw.T.astype(BF16), vb)
            n_ref[h] = decay * nv + jnp.sum(kw, axis=0, keepdims=True)
            m_ref[h:h + 1, :] = jnp.broadcast_to(m_new, (1, LANES))
            og = og_ref[rows, vs]
            o_ref[rows, vs] = ((_rms(hh) * ng_ref[...]) * _sigmoid(og)).astype(o_ref.dtype)


def _mlstm(p, conv_w, bias_col, bias_row, norm_g, col):
    bsz, s, _ = p.shape
    t = REC_BLOCK
    hq = ML_HEADS * ML_DQK
    hv = ML_HEADS * ML_DV

    def piece(name):
        width, idx = col[name]
        return pl.BlockSpec((None, t, width), lambda b, i, idx=idx: (b, i, idx))

    const = lambda shape: pl.BlockSpec(shape, lambda b, i: (0,) * len(shape))
    return pl.pallas_call(
        _mlstm_kernel,
        grid=(bsz, s // t),
        in_specs=[piece("ml_q"), piece("ml_k"), piece("ml_v"), piece("ml_og"), piece("gates"),
                  pl.BlockSpec((ML_CONV, hq), lambda b, i: (0, 0)),
                  pl.BlockSpec((ML_CONV, hq), lambda b, i: (0, 1)),
                  const((1, LANES)), const((8, LANES)), const((1, ML_DV))],
        out_specs=pl.BlockSpec((None, t, hv), lambda b, i: (b, i, 0)),
        out_shape=jax.ShapeDtypeStruct((bsz, s, hv), BF16),
        scratch_shapes=[pltpu.VMEM((ML_HEADS, ML_DQK, ML_DV), F32),
                        pltpu.VMEM((ML_HEADS, 1, ML_DQK), F32),
                        pltpu.VMEM((8, LANES), F32),
                        pltpu.VMEM((t + 8, hq), F32),
                        pltpu.VMEM((t + 8, hq), F32)],
        compiler_params=_cparams(("parallel", "arbitrary")),
        name="mlstm",
    )(p, p, p, p, p, conv_w, conv_w, bias_col, bias_row, norm_g.reshape(1, ML_DV))


def _seg_norm(x, seg_ref, segt_ref):
    if seg_ref.dtype == BF16:
        seg_sum = lambda v, m_ref: sum(_dot(part, m_ref[...]) for part in _split_bf16(v))
    else:
        seg_sum = lambda v, m_ref: _dot(v, m_ref[...], precision=HI)
    ss = seg_sum(x * x, seg_ref) * (1.0 / NSA_DH)
    return x * seg_sum(lax.rsqrt(ss + EPS), segt_ref)


def _nsa_prep_kernel(q_ref, ks_ref, vs_ref, kw_ref, vw_ref, qg_ref, kg_ref, segk_ref, segkt_ref,
                     qnt_ref, ksg_ref, vst_ref, kwg_ref, vwt_ref):
    t = q_ref.shape[0]
    qt = q_ref[...].T
    for h in range(NSA_HEADS):
        hs = slice(h * NSA_DH, (h + 1) * NSA_DH)
        blk = qt[hs, :]
        inv = lax.rsqrt(jnp.mean(blk * blk, axis=0, keepdims=True) + EPS)
        qnt_ref[hs, :] = ((blk * inv) * qg_ref[hs, :]).astype(BF16)
    ones = jnp.ones((NSA_VPAD - NSA_DH, t), BF16)
    for k_ref, v_ref, kg_out, vt_out in ((ks_ref, vs_ref, ksg_ref, vst_ref), (kw_ref, vw_ref, kwg_ref, vwt_ref)):
        kn = (_seg_norm(k_ref[...], segk_ref, segkt_ref) * kg_ref[...]).astype(BF16)
        vt = v_ref[...].T.astype(BF16)
        for g in range(NSA_GROUPS):
            gs = slice(g * NSA_DH, (g + 1) * NSA_DH)
            kg_out[g] = kn[:, gs]
            vt_out[g, 0:NSA_DH, :] = vt[gs, :]
            vt_out[g, NSA_DH:NSA_VPAD, :] = ones


def _seg_matrices(width, dtype):
    seg = (np.arange(width)[:, None] // NSA_DH == np.arange(LANES)[None, :]).astype(np.float32)
    return jnp.asarray(seg, dtype=dtype), jnp.asarray(seg.T, dtype=dtype)


def _nsa_prep(p, q_gain, k_gain, col, t=PREP_ROWS):
    bsz, s, _ = p.shape
    hq = NSA_HEADS * NSA_DH
    kv = NSA_GROUPS * NSA_DH

    def piece(name):
        width, idx = col[name]
        return pl.BlockSpec((None, t, width), lambda b, i, idx=idx: (b, i, idx))

    const = lambda shape: pl.BlockSpec(shape, lambda b, i: (0,) * len(shape))
    segk, segkt = _seg_matrices(kv, BF16)
    keys = pl.BlockSpec((None, NSA_GROUPS, t, NSA_DH), lambda b, i: (b, 0, i, 0))
    vals = pl.BlockSpec((None, NSA_GROUPS, NSA_VPAD, t), lambda b, i: (b, 0, 0, i))
    shp = lambda *dims, dt=BF16: jax.ShapeDtypeStruct(dims, dt)
    keys_shape = shp(bsz, NSA_GROUPS, s, NSA_DH)
    vals_shape = shp(bsz, NSA_GROUPS, NSA_VPAD, s)
    return pl.pallas_call(
        _nsa_prep_kernel,
        grid=(bsz, s // t),
        in_specs=[piece("nsa_q"), piece("nsa_ks"), piece("nsa_vs"), piece("nsa_kw"), piece("nsa_vw"),
                  const((hq, 1)), const((1, kv)), const((kv, LANES)), const((LANES, kv))],
        out_specs=[pl.BlockSpec((None, hq, t), lambda b, i: (b, 0, i)), keys, vals, keys, vals],
        out_shape=[shp(bsz, hq, s), keys_shape, vals_shape, keys_shape, vals_shape],
        compiler_params=_cparams(("parallel", "parallel")),
        name="nsa_prep",
    )(p, p, p, p, p, q_gain, k_gain, segk, segkt)


def _compress_kernel(kc0_ref, kc1_ref, vc0_ref, vc1_ref, posk_ref, posv_ref, kw1a_ref, kw1b_ref, kw2_ref,
                     vw1a_ref, vw1b_ref, vw2_ref, kg_ref, segk_ref, segkt_ref, kcg_ref, vct_ref):
    def hidden(x_refs, pos_ref, w1a_ref, w1b_ref):
        nsub = x_refs[0].shape[0] // CMP_STRIDE
        first = second = None
        for l in range(CMP_STRIDE):
            x = jnp.concatenate([r[pl.ds(l, nsub, stride=CMP_STRIDE), :] for r in x_refs], axis=1)
            a = _dot((x + pos_ref[l:l + 1, :]).astype(BF16), w1a_ref[l])
            b = _dot((x + pos_ref[CMP_STRIDE + l:CMP_STRIDE + l + 1, :]).astype(BF16), w1b_ref[l])
            first = a if first is None else first + a
            second = b if second is None else second + b
        row = lax.broadcasted_iota(jnp.int32, second.shape, 0)
        nxt = jnp.where(row == nsub - 1, 0.0, pltpu.roll(second, nsub - 1, 0))
        return _silu(first + nxt)

    kc = _dot(hidden((kc0_ref, kc1_ref), posk_ref, kw1a_ref, kw1b_ref).astype(BF16), kw2_ref[...])
    kn = (_seg_norm(kc, segk_ref, segkt_ref) * kg_ref[...]).astype(BF16)
    vt = _dot(vw2_ref[...], hidden((vc0_ref, vc1_ref), posv_ref, vw1a_ref, vw1b_ref).T.astype(BF16)).astype(BF16)
    for g in range(NSA_GROUPS):
        gs = slice(g * NSA_DH, (g + 1) * NSA_DH)
        kcg_ref[g] = kn[:, gs]
        vct_ref[g] = vt[gs, :]


def _expand_w1(w1):
    w = w1.reshape(2, CMP_STRIDE, NSA_DH, NSA_DH)
    eye = jnp.eye(NSA_GROUPS, dtype=w1.dtype)
    big = jnp.einsum('hlde,gk->hlgdke', w, eye)
    big = big.reshape(2, CMP_STRIDE, NSA_GROUPS * NSA_DH, NSA_GROUPS * NSA_DH).astype(BF16)
    return big[0], big[1]


def _expand_pos(pos):
    return jnp.tile(pos, (1, NSA_GROUPS))


def _block_diag(w2):
    return jnp.kron(jnp.eye(NSA_GROUPS, dtype=w2.dtype), w2).astype(BF16)


def _compress(p, pos_k, pos_v, ck_w1, ck_w2, cv_w1, cv_w2, k_gain, col):
    bsz, s, _ = p.shape
    kv = NSA_GROUPS * NSA_DH
    nsub = s // CMP_STRIDE
    segk, segkt = _seg_matrices(kv, F32)
    kw1a, kw1b = _expand_w1(ck_w1)
    vw1a, vw1b = _expand_w1(cv_w1)
    const = lambda shape: pl.BlockSpec(shape, lambda b: (0,) * len(shape))

    def halves(name):
        width, idx = col[name]
        assert width == 2 * LANES
        return [pl.BlockSpec((None, s, LANES), lambda b, j=2 * idx + h: (b, 0, j)) for h in range(2)]

    w1_spec = const((CMP_STRIDE, kv, kv))
    return pl.pallas_call(
        _compress_kernel,
        grid=(bsz,),
        in_specs=[*halves("nsa_kc"), *halves("nsa_vc"), const((CMP_LEN, kv)), const((CMP_LEN, kv)),
                  w1_spec, w1_spec, const((kv, kv)),
                  w1_spec, w1_spec, const((kv, kv)),
                  const((1, kv)), const((kv, LANES)), const((LANES, kv))],
        out_specs=[pl.BlockSpec((None, NSA_GROUPS, nsub, NSA_DH), lambda b: (b, 0, 0, 0)),
                   pl.BlockSpec((None, NSA_GROUPS, NSA_DH, nsub), lambda b: (b, 0, 0, 0))],
        out_shape=[jax.ShapeDtypeStruct((bsz, NSA_GROUPS, nsub, NSA_DH), BF16),
                   jax.ShapeDtypeStruct((bsz, NSA_GROUPS, NSA_DH, nsub), BF16)],
        compiler_params=_cparams(("parallel",)),
        name="nsa_compress",
    )(p, p, p, p, _expand_pos(pos_k), _expand_pos(pos_v),
      kw1a, kw1b, _block_diag(ck_w2), vw1a, vw1b, _block_diag(cv_w2).T, k_gain, segk, segkt)


def _nsa_attn_kernel(qt_ref, gt_ref, kcg_ref, vct_ref, ksg_ref, vst_ref, kwg_ref, vwt_ref, ovt_ref,
                     o_ref, acc_ref, sel_ref, s_ref, p_ref, pv_ref):
    tq, tk = NSA_TQ, NSA_TK
    q0 = pl.program_id(1) * tq
    ncmp = kcg_ref.shape[1]
    nsel = ovt_ref.shape[0]
    cols4 = NSA_HPG * tq

    t_cmp = q0 + lax.broadcasted_iota(jnp.int32, (ncmp, cols4), 1) % tq
    n_idx = lax.broadcasted_iota(jnp.int32, (ncmp, cols4), 0)
    cmp_valid = n_idx * CMP_STRIDE + (CMP_LEN - 1) <= t_cmp

    j_idx = lax.broadcasted_iota(jnp.int32, (nsel, tq), 0)
    cur = (q0 + lax.broadcasted_iota(jnp.int32, (nsel, tq), 1)) // SEL_BLOCK
    forced = (j_idx == 0) | (j_idx == cur) | (j_idx == cur - 1)
    causal_blk = j_idx <= cur

    row_blk = lax.broadcasted_iota(jnp.int32, (SEL_BLOCK, tq), 0)
    t_row = q0 + lax.broadcasted_iota(jnp.int32, (1, tq), 1)

    gates_t = _sigmoid(gt_ref[...]).T

    def add_gated(g, branch, o4):
        for i in range(NSA_HPG):
            h = g * NSA_HPG + i
            hs = slice(h * NSA_DH, (h + 1) * NSA_DH)
            gc = GLA_RANK + 3 * h + branch
            term = gates_t[gc:gc + 1, :] * o4[:, i * tq:(i + 1) * tq]
            acc_ref[hs, :] = term if branch == 0 else acc_ref[hs, :] + term

    def q_group(g):
        return jnp.concatenate([qt_ref[(g * NSA_HPG + i) * NSA_DH:(g * NSA_HPG + i + 1) * NSA_DH, :]
                                for i in range(NSA_HPG)], axis=1)

    def attend_init(k_ref, first_tile):
        for g in range(NSA_GROUPS):
            s_ref[g, 0] = _dot(k_ref[g, pl.ds(pl.multiple_of(first_tile * tk, tk), tk), :], q_group(g))
            p_ref[g, 1] = jnp.zeros(p_ref.shape[2:], BF16)
        pv_ref[...] = jnp.zeros_like(pv_ref)

    def attend(branch, k_ref, vt_ref, first_tile, n_tiles, mask_fn):
        last = first_tile + n_tiles - 1

        def scores(g, kt):
            return _dot(k_ref[g, pl.ds(pl.multiple_of(kt * tk, tk), tk), :], q_group(g))

        def weighted_values(g, kt, p):
            return _dot(vt_ref[g, :, pl.ds(pl.multiple_of(kt * tk, tk), tk)], p)

        def step(kt, cur, ms):
            nxt = 1 - cur
            out = []
            for g in range(NSA_GROUPS):
                m = ms[g]
                s_ref[g, nxt] = scores(g, jnp.minimum(kt + 1, last))
                acc = pv_ref[g] + weighted_values(g, jnp.maximum(kt - 1, first_tile), p_ref[g, nxt])
                ok = mask_fn(g, kt, kt * tk)
                m_new = []
                for c in range(cols4 // LANES):
                    cs = slice(c * LANES, (c + 1) * LANES)
                    qs = slice((c % (tq // LANES)) * LANES, (c % (tq // LANES) + 1) * LANES)
                    s = jnp.where(ok[:, qs], s_ref[g, cur, :, cs], NEG)
                    m_c = jnp.maximum(m[:, cs], jnp.max(s, axis=0, keepdims=True))
                    p_ref[g, cur, :, cs] = jnp.exp2(s - m_c).astype(BF16)
                    m_new.append(m_c)
                m_new = jnp.concatenate(m_new, axis=1)
                pv_ref[g] = acc * jnp.exp2(m - m_new)
                out.append(m_new)
            return tuple(out)

        def body(j, ms):
            kt = first_tile + 2 * j
            return step(kt + 1, 1, step(kt, 0, ms))

        ms = lax.fori_loop(0, n_tiles // 2, body, tuple(jnp.full((1, cols4), NEG, F32) for _ in range(NSA_GROUPS)))

        @pl.when(n_tiles % 2 == 1)
        def _():
            step(last, 0, ms)

        for g in range(NSA_GROUPS):
            acc = pv_ref[g] + weighted_values(g, last, p_ref[g, (n_tiles - 1) % 2])
            add_gated(g, branch, acc[0:NSA_DH, :] / acc[NSA_DH:NSA_DH + 1, :])

    last_tile = (q0 + tq - 1) // tk
    last_blk = (q0 + tq - 1) // SEL_BLOCK

    def sel_mask(g, kt, k0):
        per_tile = tk // SEL_BLOCK
        blocks = []
        for j in range(per_tile):
            picked = sel_ref[g, pl.ds(kt * per_tile + j, 1), :] > 0.5
            limit = jnp.where(picked, t_row, -1) - (k0 + j * SEL_BLOCK)
            blocks.append(row_blk <= limit)
        return jnp.concatenate(blocks, axis=0)

    win_span = WINDOW + tk
    win_start = pl.multiple_of(jnp.maximum(last_tile - WINDOW // tk, 0) * tk, tk)
    row_w = lax.broadcasted_iota(jnp.int32, (win_span, tq), 0)
    ahead = t_row - win_start
    win_ok = (row_w <= ahead) & (row_w > ahead - WINDOW)

    def window_branch(g):
        s = _dot(kwg_ref[g, pl.ds(win_start, win_span), :], q_group(g))
        probs = []
        for c in range(cols4 // LANES):
            qs = slice((c % (tq // LANES)) * LANES, (c % (tq // LANES) + 1) * LANES)
            sc = jnp.where(win_ok[:, qs], s[:, c * LANES:(c + 1) * LANES], NEG)
            probs.append(jnp.exp2(sc - jnp.max(sc, axis=0, keepdims=True)).astype(BF16))
        acc = _dot(vwt_ref[g, :, pl.ds(win_start, win_span)], jnp.concatenate(probs, axis=1))
        add_gated(g, 2, acc[0:NSA_DH, :] / acc[NSA_DH:NSA_DH + 1, :])

    importance = []
    for g in range(NSA_GROUPS):
        lg = jnp.where(cmp_valid, _dot(kcg_ref[g], q_group(g)), NEG)
        ex = jnp.exp2(lg - jnp.max(lg, axis=0, keepdims=True))
        p_cmp = jnp.where(cmp_valid, ex / jnp.sum(ex, axis=0, keepdims=True), 0.0)
        add_gated(g, 0, _dot(vct_ref[g], p_cmp.astype(BF16)))
        p_grp = p_cmp[:, 0:tq] + p_cmp[:, tq:2 * tq] + p_cmp[:, 2 * tq:3 * tq] + p_cmp[:, 3 * tq:4 * tq]
        importance.append(sum(_dot(ovt_ref[...], part) for part in _split_bf16(p_grp)))
        window_branch(g)

    attend_init(ksg_ref, 0)

    @pl.when(last_blk < N_SELECT)
    def _():
        for g in range(NSA_GROUPS):
            sel_ref[g] = causal_blk.astype(F32)

    @pl.when(last_blk >= N_SELECT)
    def _():
        for g in range(NSA_GROUPS):
            val = jnp.where(forced, jnp.inf, jnp.where(causal_blk, importance[g], -jnp.inf))
            rank = jnp.zeros((nsel, tq), F32)
            for i in range(nsel):
                vi = val[i:i + 1, :]
                rank += ((vi > val) | ((vi == val) & (j_idx > i))).astype(F32)
            sel_ref[g] = (rank < N_SELECT).astype(F32)

    attend(1, ksg_ref, vst_ref, 0, last_tile + 1, sel_mask)
    o_ref[...] = acc_ref[...].T.astype(o_ref.dtype)


def _overlap_t(s):
    ncmp = s // CMP_STRIDE
    nsel = s // SEL_BLOCK
    c_start = np.arange(ncmp) * CMP_STRIDE
    s_start = np.arange(nsel) * SEL_BLOCK
    overlap_t = ((c_start[None, :] <= s_start[:, None] + SEL_BLOCK - 1)
                 & (c_start[None, :] + CMP_LEN - 1 >= s_start[:, None])).astype(np.float32)
    overlap_t[:, ncmp - 1] = 0.0
    return jnp.asarray(overlap_t, dtype=BF16)


def _nsa_attention(p, qnt, kcg, vct, ksg, vst, kwg, vwt, col):
    bsz, hq, s = qnt.shape
    ncmp = s // CMP_STRIDE
    nsel = s // SEL_BLOCK
    gw, gidx = col["gates"]
    per_row = lambda shape: pl.BlockSpec((None,) + shape, lambda b, i: (b,) + (0,) * len(shape))
    keys = per_row((NSA_GROUPS, s, NSA_DH))
    vals = per_row((NSA_GROUPS, NSA_VPAD, s))
    return pl.pallas_call(
        _nsa_attn_kernel,
        grid=(bsz, s // NSA_TQ),
        in_specs=[pl.BlockSpec((None, hq, NSA_TQ), lambda b, i: (b, 0, i)),
                  pl.BlockSpec((None, NSA_TQ, gw), lambda b, i: (b, i, gidx)),
                  per_row((NSA_GROUPS, ncmp, NSA_DH)), per_row((NSA_GROUPS, NSA_DH, ncmp)),
                  keys, vals, keys, vals,
                  pl.BlockSpec((nsel, ncmp), lambda b, i: (0, 0))],
        out_specs=pl.BlockSpec((None, NSA_TQ, hq), lambda b, i: (b, i, 0)),
        out_shape=jax.ShapeDtypeStruct((bsz, s, hq), BF16),
        scratch_shapes=[pltpu.VMEM((hq, NSA_TQ), F32), pltpu.VMEM((NSA_GROUPS, nsel, NSA_TQ), F32),
                        pltpu.VMEM((NSA_GROUPS, 2, NSA_TK, NSA_HPG * NSA_TQ), F32),
                        pltpu.VMEM((NSA_GROUPS, 2, NSA_TK, NSA_HPG * NSA_TQ), BF16),
                        pltpu.VMEM((NSA_GROUPS, NSA_VPAD, NSA_HPG * NSA_TQ), F32)],
        compiler_params=_cparams(("parallel", "arbitrary")),
        name="nsa_attention",
    )(qnt, p, kcg, vct, ksg, vst, kwg, vwt, _overlap_t(s))


def _layout(pieces, tile):
    col, off, mats = {}, 0, []
    for name, w in pieces:
        width = w.shape[1]
        assert off % width == 0, (name, off, width)
        col[name] = (width, off // width)
        mats.append(w)
        off += width
    pad = -off % tile
    if pad:
        mats.append(jnp.zeros((mats[0].shape[0], pad), mats[0].dtype))
    return jnp.concatenate(mats, axis=1).astype(BF16), col


def _pad_cols(w, width):
    return jnp.pad(w, ((0, 0), (0, width - w.shape[1])))


def _ab_layout(w_in):
    hk = GLA_HEADS * GLA_DK
    hv = GLA_HEADS * GLA_DV
    kv = NSA_GROUPS * NSA_DH
    sizes = [hk, hk, hv, hv, GLA_RANK, NSA_HEADS * NSA_DH] + [kv] * 6 + [3 * NSA_HEADS]
    cuts = np.cumsum(sizes)[:-1].tolist()
    gq, gk, gv, gr, gz, nq, kc, vc, ks, vs, kw, vw, gt = jnp.split(w_in, cuts, axis=1)
    gates = _pad_cols(jnp.concatenate([gz, gt], axis=1), LANES)
    return _layout([("gla_q", gq), ("gla_k", gk), ("gla_v", gv), ("gla_r", gr), ("nsa_q", nq),
                    ("nsa_kc", kc), ("nsa_vc", vc), ("nsa_ks", ks), ("nsa_vs", vs), ("nsa_kw", kw),
                    ("nsa_vw", vw), ("gates", gates)], AB_TILE)


def _ml_layout(w_in):
    hq = ML_HEADS * ML_DQK
    hv = ML_HEADS * ML_DV
    wide = 2 * hq + 2 * hv
    col = {"ml_q": (hq, 0), "ml_k": (hq, 1), "ml_v": (hv, 1), "ml_og": (hv, 2), "gates": (LANES, wide // LANES)}
    total = -(-(wide + LANES) // ML_TILE) * ML_TILE
    return jnp.pad(w_in, ((0, 0), (0, total - w_in.shape[1]))).astype(BF16), col


def kernel(x, c, norm_mix_g, mod_mix_w, mod_mix_b, norm_mlp_g, mod_mlp_w, mod_mlp_b, mlp_w1, mlp_w2, ab_w_in, ab_w_out, gla_w_gate, gla_b_gate, gla_norm_g, nsa_q_norm_g, nsa_k_norm_g, nsa_cmp_pos_k, nsa_cmp_pos_v, nsa_cmp_k_w1, nsa_cmp_k_w2, nsa_cmp_v_w1, nsa_cmp_v_w2, ml_w_in, ml_w_out, ml_conv_w, ml_b_i, ml_b_f, ml_norm_g):
    bsz, s, d = x.shape
    depth = norm_mix_g.shape[0]
    mod_mix = _ada_mod(c, mod_mix_w, mod_mix_b)
    mod_mlp = _ada_mod(c, mod_mlp_w, mod_mlp_b)
    mlp_w1_bf16, mlp_w2_bf16 = mlp_w1.astype(BF16), mlp_w2.astype(BF16)

    def split_mod(m):
        return [m[:, None, i * d:(i + 1) * d] for i in range(3)]

    for layer in range(depth):
        shift, scale, gate = split_mod(mod_mix[layer])
        if layer % 2 == 0:
            e = layer // 2
            w_in, col = _ab_layout(ab_w_in[e])
            p = _norm_proj(x, norm_mix_g[layer], scale, shift, w_in, tm=PROJ_ROWS, tn=AB_TILE)
            w_gate = jnp.pad(gla_w_gate[e], ((0, LANES - GLA_RANK), (0, 0))).astype(BF16)
            o_gla = _gla(p, w_gate, gla_b_gate[e], gla_norm_g[e], col)
            q_gain = (jnp.tile(nsa_q_norm_g[e], NSA_HEADS) * (NSA_DH ** -0.5 * LOG2E)).reshape(-1, 1)
            k_gain = jnp.tile(nsa_k_norm_g[e], NSA_GROUPS).reshape(1, -1)
            qnt, ksg, vst, kwg, vwt = _nsa_prep(p, q_gain, k_gain, col)
            kcg, vct = _compress(p, nsa_cmp_pos_k[e], nsa_cmp_pos_v[e], nsa_cmp_k_w1[e],
                                 nsa_cmp_k_w2[e], nsa_cmp_v_w1[e], nsa_cmp_v_w2[e], k_gain, col)
            o_nsa = _nsa_attention(p, qnt, kcg, vct, ksg, vst, kwg, vwt, col)
            w_out = ab_w_out[e].astype(BF16)
            half = GLA_HEADS * GLA_DV
            x = _out_proj([o_gla, o_nsa], [w_out[:half], w_out[half:]], x, gate)
        else:
            o = layer // 2
            w_in, col = _ml_layout(ml_w_in[o])
            p = _norm_proj(x, norm_mix_g[layer], scale, shift, w_in, tm=PROJ_ROWS, tn=ML_TILE)
            bias = jnp.concatenate([ml_b_i[o], ml_b_f[o]])
            bias_col = _pad_cols(bias.reshape(1, -1), LANES)
            bias_row = jnp.broadcast_to(bias.reshape(-1, 1), (2 * ML_HEADS, LANES))
            hh = _mlstm(p, ml_conv_w[o], bias_col, bias_row, ml_norm_g[o], col)
            x = _out_proj([hh], [ml_w_out[o].astype(BF16)], x, gate)
        shift, scale, gate = split_mod(mod_mlp[layer])
        x = _mlp_sublayer(x, norm_mlp_g[layer], scale, shift, gate, mlp_w1_bf16, mlp_w2_bf16, layer)
    return x
```

```python
import functools

import numpy as np
import jax
import jax.numpy as jnp
from jax import lax
from jax.experimental import pallas as pl
from jax.experimental.pallas import tpu as pltpu

F32 = jnp.float32
BF16 = jnp.bfloat16
HI = lax.Precision.HIGHEST

EPS = 1e-6
NEG = -1e30
LOG2E = 1.4426950408889634

VMEM_LIMIT_BYTES = 56 * 1024 * 1024
LANES = 128

PROJ_ROWS = 1024
AB_TILE = 1536
ML_TILE = 1280
MLP_ROWS = 512
MLP_HIDDEN_TILE = 2048
WIDE_VMEM_LIMIT_BYTES = 60 * 1024 * 1024
OUT_ROWS = 1024
MOD_TILE = 768
PREP_ROWS = 512

GLA_HEADS = 4
GLA_DK = 128
GLA_DV = 256
GLA_RANK = 16
GLA_GATE_TAU = 16.0
CHUNK = 64
ML_CHUNK = 128
REC_BLOCK = 512

NSA_HEADS = 16
NSA_GROUPS = 4
NSA_HPG = 4
NSA_DH = 64
CMP_LEN = 32
CMP_STRIDE = 16
SEL_BLOCK = 64
N_SELECT = 16
WINDOW = 512
NSA_TQ = 256
NSA_TK = 256
NSA_VPAD = 80

ML_HEADS = 4
ML_DQK = 256
ML_DV = 512
ML_CONV = 4


def _cparams(sem, vmem_limit_bytes=VMEM_LIMIT_BYTES):
    return pltpu.CompilerParams(dimension_semantics=sem, vmem_limit_bytes=vmem_limit_bytes)


def _sigmoid(x):
    return 1.0 / (1.0 + jnp.exp(-x))


def _silu(x):
    return x * _sigmoid(x)


def _log_sigmoid(x):
    return jnp.minimum(x, 0.0) - jnp.log(1.0 + jnp.exp(-jnp.abs(x)))


def _dot(a, b, precision=None):
    return jnp.dot(a, b, preferred_element_type=F32, precision=precision)


def _dot_nt(a, b, precision=None):
    return lax.dot_general(a, b, (((1,), (1,)), ((), ())), preferred_element_type=F32, precision=precision)


def _split_bf16(x, parts=2):
    out = []
    for _ in range(parts - 1):
        piece = x.astype(BF16)
        out.append(piece)
        x = x - piece.astype(F32)
    return out + [x.astype(BF16)]


def _rms(x):
    return x * lax.rsqrt(jnp.mean(x * x, axis=-1, keepdims=True) + EPS)


def _mod_kernel(c_ref, w_ref, b_ref, o_ref):
    a = _silu(c_ref[...]).astype(BF16)
    o_ref[...] = _dot(a, w_ref[...].astype(BF16)) + b_ref[...]


def _ada_mod(c, w, b):
    nl, d, d3 = w.shape
    bsz = c.shape[0]
    tn = MOD_TILE
    return pl.pallas_call(
        _mod_kernel,
        grid=(nl, d3 // tn),
        in_specs=[pl.BlockSpec((bsz, d), lambda l, j: (0, 0)),
                  pl.BlockSpec((None, d, tn), lambda l, j: (l, 0, j)),
                  pl.BlockSpec((None, 1, tn), lambda l, j: (l, 0, j))],
        out_specs=pl.BlockSpec((None, bsz, tn), lambda l, j: (l, 0, j)),
        out_shape=jax.ShapeDtypeStruct((nl, bsz, d3), F32),
        compiler_params=_cparams(("parallel", "parallel")),
        name="ada_mod",
    )(c, w, b.reshape(nl, 1, d3))


def _modulated_norm(x_ref, g_ref, sc_ref, sh_ref):
    x = x_ref[...]
    gain = g_ref[...] * (1.0 + sc_ref[...])
    inv = lax.rsqrt(jnp.mean(x * x, axis=-1, keepdims=True) + EPS)
    return ((x * inv) * gain + sh_ref[...]).astype(BF16)


def _proj_kernel(x_ref, g_ref, sc_ref, sh_ref, w_ref, o_ref, h_ref):
    i = pl.program_id(2)
    half = x_ref.shape[0] // 2

    @pl.when(pl.program_id(1) == 0)
    def _():
        for r in range(2):
            rows = slice(r * half, (r + 1) * half)
            h_ref[i, rows, :] = _modulated_norm(x_ref.at[rows, :], g_ref, sc_ref, sh_ref)
            o_ref[rows, :] = _dot(h_ref[i, rows, :], w_ref[...]).astype(o_ref.dtype)

    @pl.when(pl.program_id(1) > 0)
    def _():
        o_ref[...] = _dot(h_ref[i], w_ref[...]).astype(o_ref.dtype)


def _norm_proj(x, g, scale, shift, w, tm, tn):
    bsz, s, d = x.shape
    n = w.shape[1]
    rows = s // tm
    x_spec = pl.BlockSpec((None, tm, d), lambda b, j, i: (b, jnp.where(j == 0, i, rows - 1), 0))
    return pl.pallas_call(
        _proj_kernel,
        grid=(bsz, n // tn, rows),
        in_specs=[x_spec,
                  pl.BlockSpec((1, d), lambda b, j, i: (0, 0)),
                  pl.BlockSpec((None, 1, d), lambda b, j, i: (b, 0, 0)),
                  pl.BlockSpec((None, 1, d), lambda b, j, i: (b, 0, 0)),
                  pl.BlockSpec((d, tn), lambda b, j, i: (0, j))],
        out_specs=pl.BlockSpec((None, tm, tn), lambda b, j, i: (b, i, j)),
        out_shape=jax.ShapeDtypeStruct((bsz, s, n), BF16),
        scratch_shapes=[pltpu.VMEM((rows, tm, d), BF16)],
        compiler_params=_cparams(("parallel", "arbitrary", "arbitrary")),
        name="norm_proj",
    )(x, g.reshape(1, d), scale, shift, w)


def _mlp_kernel(x_ref, g_ref, sc_ref, sh_ref, gate_ref, w1_ref, w2_ref, o_ref, h_ref):
    sub = w1_ref.shape[1] // 2

    def accumulate(rows):
        for kk in range(2):
            cols = slice(kk * sub, (kk + 1) * sub)
            u = jnp.maximum(_dot(h_ref[rows, :], w1_ref[:, cols]), 0.0)
            o_ref[rows, :] += gate_ref[...] * _dot((u * u).astype(BF16), w2_ref[cols, :])

    @pl.when(pl.program_id(2) == 0)
    def _():
        half = x_ref.shape[0] // 2
        for r in range(2):
            rows = slice(r * half, (r + 1) * half)
            h_ref[rows, :] = _modulated_norm(x_ref.at[rows, :], g_ref, sc_ref, sh_ref)
            o_ref[rows, :] = x_ref[rows, :]
            accumulate(rows)

    @pl.when(pl.program_id(2) > 0)
    def _():
        accumulate(slice(None))


def _mlp_sublayer(x, g, scale, shift, gate, w1, w2, layer, tm=MLP_ROWS, tf=MLP_HIDDEN_TILE):
    bsz, s, d = x.shape
    ff = w1.shape[2]
    vec = pl.BlockSpec((None, 1, d), lambda b, i, f: (b, 0, 0))
    return pl.pallas_call(
        _mlp_kernel,
        grid=(bsz, s // tm, ff // tf),
        in_specs=[pl.BlockSpec((None, tm, d), lambda b, i, f: (b, i, 0)),
                  pl.BlockSpec((1, d), lambda b, i, f: (0, 0)),
                  vec, vec, vec,
                  pl.BlockSpec((None, d, tf), lambda b, i, f: (layer, 0, f)),
                  pl.BlockSpec((None, tf, d), lambda b, i, f: (layer, f, 0))],
        out_specs=pl.BlockSpec((None, tm, d), lambda b, i, f: (b, i, 0)),
        out_shape=jax.ShapeDtypeStruct((bsz, s, d), F32),
        scratch_shapes=[pltpu.VMEM((tm, d), BF16)],
        compiler_params=_cparams(("parallel", "parallel", "arbitrary"), WIDE_VMEM_LIMIT_BYTES),
        name="mlp_sublayer",
    )(x, g.reshape(1, d), scale, shift, gate, w1, w2)


def _outproj_kernel(n_in, *refs):
    a_refs, w_refs = refs[:n_in], refs[n_in:2 * n_in]
    x_ref, gate_ref, o_ref = refs[2 * n_in:]
    y = _dot(a_refs[0][...], w_refs[0][...])
    for a_ref, w_ref in zip(a_refs[1:], w_refs[1:]):
        y += _dot(a_ref[...], w_ref[...])
    o_ref[...] = x_ref[...] + gate_ref[...] * y


def _out_proj(acts, ws, x, gate, tm=OUT_ROWS):
    bsz, s, d = x.shape
    n_in = len(acts)
    in_specs = [pl.BlockSpec((None, tm, a.shape[-1]), lambda b, i: (b, i, 0)) for a in acts]
    in_specs += [pl.BlockSpec(w.shape, lambda b, i: (0, 0)) for w in ws]
    in_specs += [pl.BlockSpec((None, tm, d), lambda b, i: (b, i, 0)),
                 pl.BlockSpec((None, 1, d), lambda b, i: (b, 0, 0))]
    return pl.pallas_call(
        functools.partial(_outproj_kernel, n_in),
        grid=(bsz, s // tm),
        in_specs=in_specs,
        out_specs=pl.BlockSpec((None, tm, d), lambda b, i: (b, i, 0)),
        out_shape=jax.ShapeDtypeStruct((bsz, s, d), F32),
        compiler_params=_cparams(("parallel", "parallel"), WIDE_VMEM_LIMIT_BYTES),
        name="out_proj",
    )(*acts, *ws, x, gate)


def _tri(n, upper=False):
    r = lax.broadcasted_iota(jnp.int32, (n, n), 0)
    c = lax.broadcasted_iota(jnp.int32, (n, n), 1)
    return (c >= r) if upper else (c <= r)


def _gla_kernel(q_ref, k_ref, v_ref, r_ref, zg_ref, wg_ref, bg_ref, ng_ref, o_ref, st_ref):
    @pl.when(pl.program_id(1) == 0)
    def _():
        st_ref[...] = jnp.zeros_like(st_ref)

    tril = _tri(CHUNK)
    tril_b = tril.astype(BF16)
    pre = _dot(zg_ref[...].astype(BF16), wg_ref[...]) + bg_ref[...]
    parts = _split_bf16(_log_sigmoid(pre) / GLA_GATE_TAU, 3)
    chunks = [slice(c * CHUNK, (c + 1) * CHUNK) for c in range(REC_BLOCK // CHUNK)]
    cum_all = jnp.concatenate([sum(_dot(tril_b, part[rows, :]) for part in parts) for rows in chunks], axis=0)
    q_dec_all = (q_ref[...].astype(F32) * GLA_DK ** -0.5) * jnp.exp(cum_all)
    k_dec_all = k_ref[...].astype(F32) * jnp.exp(-cum_all)
    qb_all = q_dec_all.astype(BF16)
    kb_all = k_dec_all.astype(BF16)
    vb_all = v_ref[...]
    gate_all = _silu(r_ref[...].astype(F32))
    intra = {}
    for c, rows in enumerate(chunks):
        for h in range(GLA_HEADS):
            ks = slice(h * GLA_DK, (h + 1) * GLA_DK)
            vs = slice(h * GLA_DV, (h + 1) * GLA_DV)
            att = jnp.where(tril, _dot_nt(qb_all[rows, ks], kb_all[rows, ks]), 0.0)
            intra[c, h] = _dot(att.astype(BF16), vb_all[rows, vs])
    for c, rows in enumerate(chunks):
        cum = cum_all[rows, :]
        total = cum[CHUNK - 1:CHUNK, :]
        k_end = (k_ref[rows, :].astype(F32) * jnp.exp(total - cum)).astype(BF16)
        decay = jnp.exp(total)
        for h in range(GLA_HEADS):
            ks = slice(h * GLA_DK, (h + 1) * GLA_DK)
            vs = slice(h * GLA_DV, (h + 1) * GLA_DV)
            st = st_ref[h]
            o = intra[c, h] + _dot_nt(qb_all[rows, ks], st.astype(BF16))
            st_ref[h] = st * decay[:, ks] + _dot(v_ref[rows, vs].astype(F32).T.astype(BF16), k_end[:, ks])
            o_ref[rows, vs] = ((_rms(o) * ng_ref[...]) * gate_all[rows, vs]).astype(o_ref.dtype)


def _gla(p, w_gate, b_gate, norm_g, col):
    bsz, s, _ = p.shape
    t = REC_BLOCK

    def piece(name):
        width, idx = col[name]
        return pl.BlockSpec((None, t, width), lambda b, i, idx=idx: (b, i, idx))

    const = lambda shape: pl.BlockSpec(shape, lambda b, i: (0,) * len(shape))
    hk = GLA_HEADS * GLA_DK
    return pl.pallas_call(
        _gla_kernel,
        grid=(bsz, s // t),
        in_specs=[piece("gla_q"), piece("gla_k"), piece("gla_v"), piece("gla_r"), piece("gates"),
                  const((LANES, hk)), const((1, hk)), const((1, GLA_DV))],
        out_specs=pl.BlockSpec((None, t, GLA_HEADS * GLA_DV), lambda b, i: (b, i, 0)),
        out_shape=jax.ShapeDtypeStruct((bsz, s, GLA_HEADS * GLA_DV), BF16),
        scratch_shapes=[pltpu.VMEM((GLA_HEADS, GLA_DV, GLA_DK), F32)],
        compiler_params=_cparams(("parallel", "arbitrary")),
        name="gla",
    )(p, p, p, p, p, w_gate, b_gate.reshape(1, hk), norm_g.reshape(1, GLA_DV))


def _mlstm_kernel(q_ref, k_ref, v_ref, og_ref, gc_ref, wq_ref, wk_ref, bc_ref, br_ref, ng_ref,
                  o_ref, c_ref, n_ref, m_ref, qx_ref, kx_ref):
    t = REC_BLOCK

    @pl.when(pl.program_id(1) == 0)
    def _():
        c_ref[...] = jnp.zeros_like(c_ref)
        n_ref[...] = jnp.zeros_like(n_ref)
        m_ref[...] = jnp.zeros_like(m_ref)
        qx_ref[0:8, :] = jnp.zeros((8, qx_ref.shape[1]), F32)
        kx_ref[0:8, :] = jnp.zeros((8, kx_ref.shape[1]), F32)

    def conv_silu(x_ref, xx_ref, w_ref):
        xx_ref[8:8 + t, :] = x_ref[...].astype(F32)
        y = xx_ref[8:8 + t, :] * w_ref[ML_CONV - 1:ML_CONV, :]
        for back in range(1, ML_CONV):
            y += xx_ref[8 - back:8 - back + t, :] * w_ref[ML_CONV - 1 - back:ML_CONV - back, :]
        xx_ref[0:8, :] = xx_ref[t:t + 8, :]
        return _silu(y)

    q_all = conv_silu(q_ref, qx_ref, wq_ref) * ML_DQK ** -0.5
    k_all = conv_silu(k_ref, kx_ref, wk_ref)

    tril = _tri(ML_CHUNK)
    tril_b = tril.astype(BF16)
    triu_b = _tri(ML_CHUNK, upper=True).astype(BF16)
    lane = lax.broadcasted_iota(jnp.int32, (ML_CHUNK, LANES), 1)
    is_f_col = (lane >= ML_HEADS) & (lane < 2 * ML_HEADS)
    sub = lax.broadcasted_iota(jnp.int32, (8, ML_CHUNK), 0)
    is_f_row = sub >= ML_HEADS
    gates_by_row = gc_ref[...].astype(F32).T[0:2 * ML_HEADS, :]

    for c in range(t // ML_CHUNK):
        rows = slice(c * ML_CHUNK, (c + 1) * ML_CHUNK)
        gcol = gc_ref[rows, :] + bc_ref[...]
        gcol = jnp.where(is_f_col, _log_sigmoid(gcol), gcol)
        bcol = sum(_dot(tril_b, part) for part in _split_bf16(gcol, 3))
        grow = gates_by_row[:, rows] + br_ref[:, 0:1]
        grow = jnp.where(is_f_row, _log_sigmoid(grow), grow)
        brow = sum(_dot(part, triu_b) for part in _split_bf16(grow, 3))
        for h in range(ML_HEADS):
            qs = slice(h * ML_DQK, (h + 1) * ML_DQK)
            vs = slice(h * ML_DV, (h + 1) * ML_DV)
            b_col = bcol[:, ML_HEADS + h:ML_HEADS + h + 1]
            b_row = brow[ML_HEADS + h:ML_HEADS + h + 1, :]
            li_col = gcol[:, h:h + 1]
            li_row = grow[h:h + 1, :]
            m_prev = m_ref[h:h + 1, 0:1]
            log_d = jnp.where(tril, b_col - b_row + li_row, -jnp.inf)
            log_inter = b_col + m_prev
            m_comb = jnp.maximum(log_inter, jnp.max(log_d, axis=-1, keepdims=True))
            d = jnp.exp(log_d - m_comb)
            w_inter = jnp.exp(log_inter - m_comb)
            qh = q_all[rows, qs]
            kh = k_all[rows, qs]
            qb = qh.astype(BF16)
            vb = v_ref[rows, vs]
            sc = _dot_nt(qb, kh.astype(BF16)) * d
            cm = c_ref[h]
            nv = n_ref[h]
            num = _dot(sc.astype(BF16), vb) + w_inter * _dot(qb, cm.astype(BF16))
            den = jnp.sum(sc, axis=-1, keepdims=True) + w_inter * jnp.sum(qh * nv, axis=-1, keepdims=True)
            hh = num / jnp.maximum(jnp.abs(den), jnp.exp(-m_comb))
            total = b_col[ML_CHUNK - 1:ML_CHUNK, :]
            log_w = total - b_col + li_col
            m_new = jnp.maximum(total + m_prev, jnp.max(log_w, axis=0, keepdims=True))
            decay = jnp.exp(total + m_prev - m_new)
            kw = kh * jnp.exp(log_w - m_new)
            c_ref[h] = decay * cm + _dot(kw.T.astype(BF16), vb)
            n_ref[h] = decay * nv + jnp.sum(kw, axis=0, keepdims=True)
            m_ref[h:h + 1, :] = jnp.broadcast_to(m_new, (1, LANES))
            og = og_ref[rows, vs].astype(F32)
            o_ref[rows, vs] = ((_rms(hh) * ng_ref[...]) * _sigmoid(og)).astype(o_ref.dtype)


def _mlstm(p, conv_w, bias_col, bias_row, norm_g, col):
    bsz, s, _ = p.shape
    t = REC_BLOCK
    hq = ML_HEADS * ML_DQK
    hv = ML_HEADS * ML_DV

    def piece(name):
        width, idx = col[name]
        return pl.BlockSpec((None, t, width), lambda b, i, idx=idx: (b, i, idx))

    const = lambda shape: pl.BlockSpec(shape, lambda b, i: (0,) * len(shape))
    return pl.pallas_call(
        _mlstm_kernel,
        grid=(bsz, s // t),
        in_specs=[piece("ml_q"), piece("ml_k"), piece("ml_v"), piece("ml_og"), piece("gates"),
                  pl.BlockSpec((ML_CONV, hq), lambda b, i: (0, 0)),
                  pl.BlockSpec((ML_CONV, hq), lambda b, i: (0, 1)),
                  const((1, LANES)), const((8, LANES)), const((1, ML_DV))],
        out_specs=pl.BlockSpec((None, t, hv), lambda b, i: (b, i, 0)),
        out_shape=jax.ShapeDtypeStruct((bsz, s, hv), BF16),
        scratch_shapes=[pltpu.VMEM((ML_HEADS, ML_DQK, ML_DV), F32),
                        pltpu.VMEM((ML_HEADS, 1, ML_DQK), F32),
                        pltpu.VMEM((8, LANES), F32),
                        pltpu.VMEM((t + 8, hq), F32),
                        pltpu.VMEM((t + 8, hq), F32)],
        compiler_params=_cparams(("parallel", "arbitrary")),
        name="mlstm",
    )(p, p, p, p, p, conv_w, conv_w, bias_col, bias_row, norm_g.reshape(1, ML_DV))


def _seg_norm(x, seg_ref, segt_ref):
    if seg_ref.dtype == BF16:
        seg_sum = lambda v, m_ref: sum(_dot(part, m_ref[...]) for part in _split_bf16(v))
    else:
        seg_sum = lambda v, m_ref: _dot(v, m_ref[...], precision=HI)
    ss = seg_sum(x * x, seg_ref) * (1.0 / NSA_DH)
    return x * seg_sum(lax.rsqrt(ss + EPS), segt_ref)


def _nsa_prep_kernel(q_ref, ks_ref, vs_ref, kw_ref, vw_ref, qg_ref, kg_ref, segk_ref, segkt_ref,
                     qnt_ref, ksg_ref, vst_ref, kwg_ref, vwt_ref):
    t = q_ref.shape[0]
    qt = q_ref[...].astype(F32).T
    for h in range(NSA_HEADS):
        hs = slice(h * NSA_DH, (h + 1) * NSA_DH)
        blk = qt[hs, :]
        inv = lax.rsqrt(jnp.mean(blk * blk, axis=0, keepdims=True) + EPS)
        qnt_ref[hs, :] = ((blk * inv) * qg_ref[hs, :]).astype(BF16)
    ones = jnp.ones((NSA_VPAD - NSA_DH, t), BF16)
    for k_ref, v_ref, kg_out, vt_out in ((ks_ref, vs_ref, ksg_ref, vst_ref), (kw_ref, vw_ref, kwg_ref, vwt_ref)):
        kn = (_seg_norm(k_ref[...].astype(F32), segk_ref, segkt_ref) * kg_ref[...]).astype(BF16)
        vt = v_ref[...].astype(F32).T.astype(BF16)
        for g in range(NSA_GROUPS):
            gs = slice(g * NSA_DH, (g + 1) * NSA_DH)
            kg_out[g] = kn[:, gs]
            vt_out[g, 0:NSA_DH, :] = vt[gs, :]
            vt_out[g, NSA_DH:NSA_VPAD, :] = ones


def _seg_matrices(width, dtype):
    seg = (np.arange(width)[:, None] // NSA_DH == np.arange(LANES)[None, :]).astype(np.float32)
    return jnp.asarray(seg, dtype=dtype), jnp.asarray(seg.T, dtype=dtype)


def _nsa_prep(p, q_gain, k_gain, col, t=PREP_ROWS):
    bsz, s, _ = p.shape
    hq = NSA_HEADS * NSA_DH
    kv = NSA_GROUPS * NSA_DH

    def piece(name):
        width, idx = col[name]
        return pl.BlockSpec((None, t, width), lambda b, i, idx=idx: (b, i, idx))

    const = lambda shape: pl.BlockSpec(shape, lambda b, i: (0,) * len(shape))
    segk, segkt = _seg_matrices(kv, BF16)
    keys = pl.BlockSpec((None, NSA_GROUPS, t, NSA_DH), lambda b, i: (b, 0, i, 0))
    vals = pl.BlockSpec((None, NSA_GROUPS, NSA_VPAD, t), lambda b, i: (b, 0, 0, i))
    shp = lambda *dims, dt=BF16: jax.ShapeDtypeStruct(dims, dt)
    keys_shape = shp(bsz, NSA_GROUPS, s, NSA_DH)
    vals_shape = shp(bsz, NSA_GROUPS, NSA_VPAD, s)
    return pl.pallas_call(
        _nsa_prep_kernel,
        grid=(bsz, s // t),
        in_specs=[piece("nsa_q"), piece("nsa_ks"), piece("nsa_vs"), piece("nsa_kw"), piece("nsa_vw"),
                  const((hq, 1)), const((1, kv)), const((kv, LANES)), const((LANES, kv))],
        out_specs=[pl.BlockSpec((None, hq, t), lambda b, i: (b, 0, i)), keys, vals, keys, vals],
        out_shape=[shp(bsz, hq, s), keys_shape, vals_shape, keys_shape, vals_shape],
        compiler_params=_cparams(("parallel", "parallel")),
        name="nsa_prep",
    )(p, p, p, p, p, q_gain, k_gain, segk, segkt)


def _compress_kernel(kc0_ref, kc1_ref, vc0_ref, vc1_ref, posk_ref, posv_ref, kw1a_ref, kw1b_ref, kw2_ref,
                     vw1a_ref, vw1b_ref, vw2_ref, kg_ref, segk_ref, segkt_ref, kcg_ref, vct_ref, *stage_refs):
    for src, dst in zip((kc0_ref, kc1_ref, vc0_ref, vc1_ref), stage_refs):
        dst[...] = src[...].astype(F32)
    kc0_ref, kc1_ref, vc0_ref, vc1_ref = stage_refs

    def hidden(x_refs, pos_ref, w1a_ref, w1b_ref):
        nsub = x_refs[0].shape[0] // CMP_STRIDE
        first = second = None
        for l in range(CMP_STRIDE):
            x = jnp.concatenate([r[pl.ds(l, nsub, stride=CMP_STRIDE), :] for r in x_refs], axis=1)
            a = _dot((x + pos_ref[l:l + 1, :]).astype(BF16), w1a_ref[l])
            b = _dot((x + pos_ref[CMP_STRIDE + l:CMP_STRIDE + l + 1, :]).astype(BF16), w1b_ref[l])
            first = a if first is None else first + a
            second = b if second is None else second + b
        row = lax.broadcasted_iota(jnp.int32, second.shape, 0)
        nxt = jnp.where(row == nsub - 1, 0.0, pltpu.roll(second, nsub - 1, 0))
        return _silu(first + nxt)

    kc = _dot(hidden((kc0_ref, kc1_ref), posk_ref, kw1a_ref, kw1b_ref).astype(BF16), kw2_ref[...])
    kn = (_seg_norm(kc, segk_ref, segkt_ref) * kg_ref[...]).astype(BF16)
    vt = _dot(vw2_ref[...], hidden((vc0_ref, vc1_ref), posv_ref, vw1a_ref, vw1b_ref).T.astype(BF16)).astype(BF16)
    for g in range(NSA_GROUPS):
        gs = slice(g * NSA_DH, (g + 1) * NSA_DH)
        kcg_ref[g] = kn[:, gs]
        vct_ref[g] = vt[gs, :]


def _expand_w1(w1):
    w = w1.reshape(2, CMP_STRIDE, NSA_DH, NSA_DH)
    eye = jnp.eye(NSA_GROUPS, dtype=w1.dtype)
    big = jnp.einsum('hlde,gk->hlgdke', w, eye)
    big = big.reshape(2, CMP_STRIDE, NSA_GROUPS * NSA_DH, NSA_GROUPS * NSA_DH).astype(BF16)
    return big[0], big[1]


def _expand_pos(pos):
    return jnp.tile(pos, (1, NSA_GROUPS))


def _block_diag(w2):
    return jnp.kron(jnp.eye(NSA_GROUPS, dtype=w2.dtype), w2).astype(BF16)


def _compress(p, pos_k, pos_v, ck_w1, ck_w2, cv_w1, cv_w2, k_gain, col):
    bsz, s, _ = p.shape
    kv = NSA_GROUPS * NSA_DH
    nsub = s // CMP_STRIDE
    segk, segkt = _seg_matrices(kv, F32)
    kw1a, kw1b = _expand_w1(ck_w1)
    vw1a, vw1b = _expand_w1(cv_w1)
    const = lambda shape: pl.BlockSpec(shape, lambda b: (0,) * len(shape))

    def halves(name):
        width, idx = col[name]
        assert width == 2 * LANES
        return [pl.BlockSpec((None, s, LANES), lambda b, j=2 * idx + h: (b, 0, j)) for h in range(2)]

    w1_spec = const((CMP_STRIDE, kv, kv))
    return pl.pallas_call(
        _compress_kernel,
        grid=(bsz,),
        in_specs=[*halves("nsa_kc"), *halves("nsa_vc"), const((CMP_LEN, kv)), const((CMP_LEN, kv)),
                  w1_spec, w1_spec, const((kv, kv)),
                  w1_spec, w1_spec, const((kv, kv)),
                  const((1, kv)), const((kv, LANES)), const((LANES, kv))],
        out_specs=[pl.BlockSpec((None, NSA_GROUPS, nsub, NSA_DH), lambda b: (b, 0, 0, 0)),
                   pl.BlockSpec((None, NSA_GROUPS, NSA_DH, nsub), lambda b: (b, 0, 0, 0))],
        out_shape=[jax.ShapeDtypeStruct((bsz, NSA_GROUPS, nsub, NSA_DH), BF16),
                   jax.ShapeDtypeStruct((bsz, NSA_GROUPS, NSA_DH, nsub), BF16)],
        scratch_shapes=[pltpu.VMEM((s, LANES), F32)] * 4,
        compiler_params=_cparams(("parallel",)),
        name="nsa_compress",
    )(p, p, p, p, _expand_pos(pos_k), _expand_pos(pos_v),
      kw1a, kw1b, _block_diag(ck_w2), vw1a, vw1b, _block_diag(cv_w2).T, k_gain, segk, segkt)


def _nsa_attn_kernel(qt_ref, gt_ref, kcg_ref, vct_ref, ksg_ref, vst_ref, kwg_ref, vwt_ref, ovt_ref,
                     o_ref, acc_ref, sel_ref, s_ref, p_ref, pv_ref):
    tq, tk = NSA_TQ, NSA_TK
    q0 = pl.program_id(1) * tq
    ncmp = kcg_ref.shape[1]
    nsel = ovt_ref.shape[0]
    cols4 = NSA_HPG * tq

    t_cmp = q0 + lax.broadcasted_iota(jnp.int32, (ncmp, cols4), 1) % tq
    n_idx = lax.broadcasted_iota(jnp.int32, (ncmp, cols4), 0)
    cmp_valid = n_idx * CMP_STRIDE + (CMP_LEN - 1) <= t_cmp

    j_idx = lax.broadcasted_iota(jnp.int32, (nsel, tq), 0)
    cur = (q0 + lax.broadcasted_iota(jnp.int32, (nsel, tq), 1)) // SEL_BLOCK
    forced = (j_idx == 0) | (j_idx == cur) | (j_idx == cur - 1)
    causal_blk = j_idx <= cur

    row_blk = lax.broadcasted_iota(jnp.int32, (SEL_BLOCK, tq), 0)
    t_row = q0 + lax.broadcasted_iota(jnp.int32, (1, tq), 1)

    gates_t = _sigmoid(gt_ref[...].astype(F32)).T

    def add_gated(g, branch, o4):
        for i in range(NSA_HPG):
            h = g * NSA_HPG + i
            hs = slice(h * NSA_DH, (h + 1) * NSA_DH)
            gc = GLA_RANK + 3 * h + branch
            term = gates_t[gc:gc + 1, :] * o4[:, i * tq:(i + 1) * tq]
            acc_ref[hs, :] = term if branch == 0 else acc_ref[hs, :] + term

    def q_group(g):
        return jnp.concatenate([qt_ref[(g * NSA_HPG + i) * NSA_DH:(g * NSA_HPG + i + 1) * NSA_DH, :]
                                for i in range(NSA_HPG)], axis=1)

    def attend_init(k_ref, first_tile):
        for g in range(NSA_GROUPS):
            s_ref[g, 0] = _dot(k_ref[g, pl.ds(pl.multiple_of(first_tile * tk, tk), tk), :], q_group(g))
            p_ref[g, 1] = jnp.zeros(p_ref.shape[2:], BF16)
        pv_ref[...] = jnp.zeros_like(pv_ref)

    def attend(branch, k_ref, vt_ref, first_tile, n_tiles, mask_fn):
        last = first_tile + n_tiles - 1

        def scores(g, kt):
            return _dot(k_ref[g, pl.ds(pl.multiple_of(kt * tk, tk), tk), :], q_group(g))

        def weighted_values(g, kt, p):
            return _dot(vt_ref[g, :, pl.ds(pl.multiple_of(kt * tk, tk), tk)], p)

        def step(kt, cur, ms):
            nxt = 1 - cur
            out = []
            for g in range(NSA_GROUPS):
                m = ms[g]
                s_ref[g, nxt] = scores(g, jnp.minimum(kt + 1, last))
                acc = pv_ref[g] + weighted_values(g, jnp.maximum(kt - 1, first_tile), p_ref[g, nxt])
                ok = mask_fn(g, kt, kt * tk)
                m_new = []
                for c in range(cols4 // LANES):
                    cs = slice(c * LANES, (c + 1) * LANES)
                    qs = slice((c % (tq // LANES)) * LANES, (c % (tq // LANES) + 1) * LANES)
                    s = jnp.where(ok[:, qs], s_ref[g, cur, :, cs], NEG)
                    m_c = jnp.maximum(m[:, cs], jnp.max(s, axis=0, keepdims=True))
                    p_ref[g, cur, :, cs] = jnp.exp2(s - m_c).astype(BF16)
                    m_new.append(m_c)
                m_new = jnp.concatenate(m_new, axis=1)
                pv_ref[g] = acc * jnp.exp2(m - m_new)
                out.append(m_new)
            return tuple(out)

        def body(j, ms):
            kt = first_tile + 2 * j
            return step(kt + 1, 1, step(kt, 0, ms))

        ms = lax.fori_loop(0, n_tiles // 2, body, tuple(jnp.full((1, cols4), NEG, F32) for _ in range(NSA_GROUPS)))

        @pl.when(n_tiles % 2 == 1)
        def _():
            step(last, 0, ms)

        for g in range(NSA_GROUPS):
            acc = pv_ref[g] + weighted_values(g, last, p_ref[g, (n_tiles - 1) % 2])
            add_gated(g, branch, acc[0:NSA_DH, :] / acc[NSA_DH:NSA_DH + 1, :])

    last_tile = (q0 + tq - 1) // tk
    last_blk = (q0 + tq - 1) // SEL_BLOCK

    def sel_mask(g, kt, k0):
        per_tile = tk // SEL_BLOCK
        blocks = []
        for j in range(per_tile):
            picked = sel_ref[g, pl.ds(kt * per_tile + j, 1), :] > 0.5
            limit = jnp.where(picked, t_row, -1) - (k0 + j * SEL_BLOCK)
            blocks.append(row_blk <= limit)
        return jnp.concatenate(blocks, axis=0)

    win_span = WINDOW + tk
    win_start = pl.multiple_of(jnp.maximum(last_tile - WINDOW // tk, 0) * tk, tk)
    row_w = lax.broadcasted_iota(jnp.int32, (win_span, tq), 0)
    ahead = t_row - win_start
    win_ok = (row_w <= ahead) & (row_w > ahead - WINDOW)

    def window_branch(g):
        s = _dot(kwg_ref[g, pl.ds(win_start, win_span), :], q_group(g))
        probs = []
        for c in range(cols4 // LANES):
            qs = slice((c % (tq // LANES)) * LANES, (c % (tq // LANES) + 1) * LANES)
            sc = jnp.where(win_ok[:, qs], s[:, c * LANES:(c + 1) * LANES], NEG)
            probs.append(jnp.exp2(sc - jnp.max(sc, axis=0, keepdims=True)).astype(BF16))
        acc = _dot(vwt_ref[g, :, pl.ds(win_start, win_span)], jnp.concatenate(probs, axis=1))
        add_gated(g, 2, acc[0:NSA_DH, :] / acc[NSA_DH:NSA_DH + 1, :])

    importance = []
    for g in range(NSA_GROUPS):
        lg = jnp.where(cmp_valid, _dot(kcg_ref[g], q_group(g)), NEG)
        ex = jnp.exp2(lg - jnp.max(lg, axis=0, keepdims=True))
        p_cmp = jnp.where(cmp_valid, ex / jnp.sum(ex, axis=0, keepdims=True), 0.0)
        add_gated(g, 0, _dot(vct_ref[g], p_cmp.astype(BF16)))
        p_grp = p_cmp[:, 0:tq] + p_cmp[:, tq:2 * tq] + p_cmp[:, 2 * tq:3 * tq] + p_cmp[:, 3 * tq:4 * tq]
        importance.append(sum(_dot(ovt_ref[...], part) for part in _split_bf16(p_grp)))
        window_branch(g)

    attend_init(ksg_ref, 0)

    @pl.when(last_blk < N_SELECT)
    def _():
        for g in range(NSA_GROUPS):
            sel_ref[g] = causal_blk.astype(F32)

    @pl.when(last_blk >= N_SELECT)
    def _():
        for g in range(NSA_GROUPS):
            val = jnp.where(forced, jnp.inf, jnp.where(causal_blk, importance[g], -jnp.inf))
            rank = jnp.zeros((nsel, tq), F32)
            for i in range(nsel):
                vi = val[i:i + 1, :]
                rank += ((vi > val) | ((vi == val) & (j_idx > i))).astype(F32)
            sel_ref[g] = (rank < N_SELECT).astype(F32)

    attend(1, ksg_ref, vst_ref, 0, last_tile + 1, sel_mask)
    o_ref[...] = acc_ref[...].T.astype(o_ref.dtype)


def _overlap_t(s):
    ncmp = s // CMP_STRIDE
    nsel = s // SEL_BLOCK
    c_start = np.arange(ncmp) * CMP_STRIDE
    s_start = np.arange(nsel) * SEL_BLOCK
    overlap_t = ((c_start[None, :] <= s_start[:, None] + SEL_BLOCK - 1)
                 & (c_start[None, :] + CMP_LEN - 1 >= s_start[:, None])).astype(np.float32)
    overlap_t[:, ncmp - 1] = 0.0
    return jnp.asarray(overlap_t, dtype=BF16)


def _nsa_attention(p, qnt, kcg, vct, ksg, vst, kwg, vwt, col):
    bsz, hq, s = qnt.shape
    ncmp = s // CMP_STRIDE
    nsel = s // SEL_BLOCK
    gw, gidx = col["gates"]
    per_row = lambda shape: pl.BlockSpec((None,) + shape, lambda b, i: (b,) + (0,) * len(shape))
    keys = per_row((NSA_GROUPS, s, NSA_DH))
    vals = per_row((NSA_GROUPS, NSA_VPAD, s))
    return pl.pallas_call(
        _nsa_attn_kernel,
        grid=(bsz, s // NSA_TQ),
        in_specs=[pl.BlockSpec((None, hq, NSA_TQ), lambda b, i: (b, 0, i)),
                  pl.BlockSpec((None, NSA_TQ, gw), lambda b, i: (b, i, gidx)),
                  per_row((NSA_GROUPS, ncmp, NSA_DH)), per_row((NSA_GROUPS, NSA_DH, ncmp)),
                  keys, vals, keys, vals,
                  pl.BlockSpec((nsel, ncmp), lambda b, i: (0, 0))],
        out_specs=pl.BlockSpec((None, NSA_TQ, hq), lambda b, i: (b, i, 0)),
        out_shape=jax.ShapeDtypeStruct((bsz, s, hq), BF16),
        scratch_shapes=[pltpu.VMEM((hq, NSA_TQ), F32), pltpu.VMEM((NSA_GROUPS, nsel, NSA_TQ), F32),
                        pltpu.VMEM((NSA_GROUPS, 2, NSA_TK, NSA_HPG * NSA_TQ), F32),
                        pltpu.VMEM((NSA_GROUPS, 2, NSA_TK, NSA_HPG * NSA_TQ), BF16),
                        pltpu.VMEM((NSA_GROUPS, NSA_VPAD, NSA_HPG * NSA_TQ), F32)],
        compiler_params=_cparams(("parallel", "arbitrary")),
        name="nsa_attention",
    )(qnt, p, kcg, vct, ksg, vst, kwg, vwt, _overlap_t(s))


def _layout(pieces, tile):
    col, off, mats = {}, 0, []
    for name, w in pieces:
        width = w.shape[1]
        assert off % width == 0, (name, off, width)
        col[name] = (width, off // width)
        mats.append(w)
        off += width
    pad = -off % tile
    if pad:
        mats.append(jnp.zeros((mats[0].shape[0], pad), mats[0].dtype))
    return jnp.concatenate(mats, axis=1).astype(BF16), col


def _pad_cols(w, width):
    return jnp.pad(w, ((0, 0), (0, width - w.shape[1])))


def _ab_layout(w_in):
    hk = GLA_HEADS * GLA_DK
    hv = GLA_HEADS * GLA_DV
    kv = NSA_GROUPS * NSA_DH
    sizes = [hk, hk, hv, hv, GLA_RANK, NSA_HEADS * NSA_DH] + [kv] * 6 + [3 * NSA_HEADS]
    cuts = np.cumsum(sizes)[:-1].tolist()
    gq, gk, gv, gr, gz, nq, kc, vc, ks, vs, kw, vw, gt = jnp.split(w_in, cuts, axis=1)
    gates = _pad_cols(jnp.concatenate([gz, gt], axis=1), LANES)
    return _layout([("gla_q", gq), ("gla_k", gk), ("gla_v", gv), ("gla_r", gr), ("nsa_q", nq),
                    ("nsa_kc", kc), ("nsa_vc", vc), ("nsa_ks", ks), ("nsa_vs", vs), ("nsa_kw", kw),
                    ("nsa_vw", vw), ("gates", gates)], AB_TILE)


def _ml_layout(w_in):
    hq = ML_HEADS * ML_DQK
    hv = ML_HEADS * ML_DV
    wide = 2 * hq + 2 * hv
    col = {"ml_q": (hq, 0), "ml_k": (hq, 1), "ml_v": (hv, 1), "ml_og": (hv, 2), "gates": (LANES, wide // LANES)}
    total = -(-(wide + LANES) // ML_TILE) * ML_TILE
    return jnp.pad(w_in, ((0, 0), (0, total - w_in.shape[1]))).astype(BF16), col


def kernel(x, c, norm_mix_g, mod_mix_w, mod_mix_b, norm_mlp_g, mod_mlp_w, mod_mlp_b, mlp_w1, mlp_w2, ab_w_in, ab_w_out, gla_w_gate, gla_b_gate, gla_norm_g, nsa_q_norm_g, nsa_k_norm_g, nsa_cmp_pos_k, nsa_cmp_pos_v, nsa_cmp_k_w1, nsa_cmp_k_w2, nsa_cmp_v_w1, nsa_cmp_v_w2, ml_w_in, ml_w_out, ml_conv_w, ml_b_i, ml_b_f, ml_norm_g):
    bsz, s, d = x.shape
    depth = norm_mix_g.shape[0]
    mod_mix = _ada_mod(c, mod_mix_w, mod_mix_b)
    mod_mlp = _ada_mod(c, mod_mlp_w, mod_mlp_b)
    mlp_w1_bf16, mlp_w2_bf16 = mlp_w1.astype(BF16), mlp_w2.astype(BF16)

    def split_mod(m):
        return [m[:, None, i * d:(i + 1) * d] for i in range(3)]

    for layer in range(depth):
        shift, scale, gate = split_mod(mod_mix[layer])
        if layer % 2 == 0:
            e = layer // 2
            w_in, col = _ab_layout(ab_w_in[e])
            p = _norm_proj(x, norm_mix_g[layer], scale, shift, w_in, tm=PROJ_ROWS, tn=AB_TILE)
            w_gate = jnp.pad(gla_w_gate[e], ((0, LANES - GLA_RANK), (0, 0))).astype(BF16)
            o_gla = _gla(p, w_gate, gla_b_gate[e], gla_norm_g[e], col)
            q_gain = (jnp.tile(nsa_q_norm_g[e], NSA_HEADS) * (NSA_DH ** -0.5 * LOG2E)).reshape(-1, 1)
            k_gain = jnp.tile(nsa_k_norm_g[e], NSA_GROUPS).reshape(1, -1)
            qnt, ksg, vst, kwg, vwt = _nsa_prep(p, q_gain, k_gain, col)
            kcg, vct = _compress(p, nsa_cmp_pos_k[e], nsa_cmp_pos_v[e], nsa_cmp_k_w1[e],
                                 nsa_cmp_k_w2[e], nsa_cmp_v_w1[e], nsa_cmp_v_w2[e], k_gain, col)
            o_nsa = _nsa_attention(p, qnt, kcg, vct, ksg, vst, kwg, vwt, col)
            w_out = ab_w_out[e].astype(BF16)
            half = GLA_HEADS * GLA_DV
            x = _out_proj([o_gla, o_nsa], [w_out[:half], w_out[half:]], x, gate)
        else:
            o = layer // 2
            w_in, col = _ml_layout(ml_w_in[o])
            p = _norm_proj(x, norm_mix_g[layer], scale, shift, w_in, tm=PROJ_ROWS, tn=ML_TILE)
            bias = jnp.concatenate([ml_b_i[o], ml_b_f[o]])
            bias_col = _pad_cols(bias.reshape(1, -1), LANES)
            bias_row = jnp.broadcast_to(bias.reshape(-1, 1), (2 * ML_HEADS, LANES))
            hh = _mlstm(p, ml_conv_w[o], bias_col, bias_row, ml_norm_g[o], col)
            x = _out_proj([hh], [ml_w_out[o].astype(BF16)], x, gate)
        shift, scale, gate = split_mod(mod_mlp[layer])
        x = _mlp_sublayer(x, norm_mlp_g[layer], scale, shift, gate, mlp_w1_bf16, mlp_w2_bf16, layer)
    return x
```
